```python
import jax, jax.numpy as jnp
from jax import lax
import numpy as np

D_MODEL = 1024
BATCH = 2
SEQ = 8192
DEPTH = 1

GRID_W = 64
CTX_LEN = 256
D_MIX = D_MODEL
D_FOURIER = D_MIX // 2
D_LRU = D_MIX - D_FOURIER
N_FOURIER_GROUPS = 4
FOURIER_GROUP_DIM = D_FOURIER // N_FOURIER_GROUPS
N_LRU_HEADS = 4
LRU_HEAD_DIM = D_LRU // N_LRU_HEADS
CONV_WIDTH = 4
CONV_PAD = (2, 1)
LRU_C = 8.0
EPS = 1e-6

kernel_name = "hybrid_fourier_rglru_prefix_dit_layer"


def rmsnorm(x, g):
    xf = x.astype(jnp.float32)
    y = xf * lax.rsqrt(jnp.mean(xf * xf, axis=-1, keepdims=True) + EPS)
    return (y * g.astype(jnp.float32)).astype(x.dtype)


def modulate(x, g, shift, scale):
    return rmsnorm(x, g) * (1.0 + scale) + shift


def in_proj(h, w_in):
    z = h @ w_in
    return jnp.split(z, [D_FOURIER, 2 * D_FOURIER, 2 * D_FOURIER + D_LRU], axis=-1)


def fourier_branch(u, w_four):
    b_, l_, _ = u.shape
    ug = u.astype(jnp.float32).reshape(b_, l_, N_FOURIER_GROUPS, FOURIER_GROUP_DIM)
    mixed = jnp.fft.fftn(ug, axes=(1, 3), norm="ortho").real
    return mixed.reshape(b_, l_, D_FOURIER).astype(u.dtype) @ w_four


def dwconv_centred(u, w, b):
    y = lax.conv_general_dilated(u, w[:, None, :].astype(u.dtype), window_strides=(1,),
                                 padding=[CONV_PAD],
                                 dimension_numbers=("NWC", "WIO", "NWC"),
                                 feature_group_count=D_LRU)
    return y + b


def block_diag(u, w, b):
    b_, l_, _ = u.shape
    uh = u.reshape(b_, l_, N_LRU_HEADS, LRU_HEAD_DIM)
    return jnp.einsum("blhd,hde->blhe", uh, w).reshape(b_, l_, D_LRU) + b


def _combine(left, right):
    a1, b1 = left
    a2, b2 = right
    return a1 * a2, a2 * b1 + b2


def rglru_direction(u, w_rg, b_rg, w_ig, b_ig, lam, h0, reverse):
    r = jax.nn.sigmoid(block_diag(u, w_rg, b_rg).astype(jnp.float32))
    i = jax.nn.sigmoid(block_diag(u, w_ig, b_ig).astype(jnp.float32))
    log_a = -LRU_C * r * jax.nn.softplus(-lam.astype(jnp.float32))
    a = jnp.exp(log_a)
    bx = jnp.sqrt(-jnp.expm1(2.0 * log_a)) * (i * u.astype(jnp.float32))
    a_cum, b_cum = lax.associative_scan(_combine, (a, bx), axis=1, reverse=reverse)
    h = a_cum * h0[:, None, :] + b_cum
    final = h[:, 0] if reverse else h[:, -1]
    return h, final


def lru_branch(u, conv_w, conv_b, w_rg, b_rg, w_ig, b_ig, lam, h0_fwd, h0_bwd):
    v = dwconv_centred(u, conv_w, conv_b)
    h_f, fin_f = rglru_direction(v, w_rg[0], b_rg[0], w_ig[0], b_ig[0], lam[0], h0_fwd, False)
    h_b, fin_b = rglru_direction(v, w_rg[1], b_rg[1], w_ig[1], b_ig[1], lam[1], h0_bwd, True)
    return (h_f + h_b).astype(u.dtype), fin_f, fin_b


def out_proj(y_f, g_f, y_l, g_l, w_out):
    y = jnp.concatenate([y_f * jax.nn.silu(g_f), y_l * jax.nn.silu(g_l)], axis=-1)
    return y @ w_out


def setup_inputs(seed: int = 0) -> dict:
    key = jax.random.key(seed)
    ks = jax.random.split(key, 20)
    f32 = jnp.float32
    x = jax.random.normal(ks[0], (BATCH, SEQ, D_MODEL), f32)
    c = jax.random.normal(ks[1], (BATCH, D_MODEL), f32)
    ctx = jax.random.normal(ks[2], (BATCH, CTX_LEN, D_MODEL), f32)
    c_ctx = jax.random.normal(ks[3], (D_MODEL,), f32)
    w_ada = jax.random.normal(ks[4], (DEPTH, D_MODEL, 3 * D_MODEL), f32) * (0.5 * D_MODEL ** -0.5)
    b_ada = jax.random.normal(ks[5], (DEPTH, 3 * D_MODEL), f32) * 0.02
    norm_gain = 1.0 + 0.05 * jax.random.normal(ks[6], (DEPTH, D_MODEL), f32)
    w_in = jax.random.normal(ks[7], (DEPTH, D_MODEL, 2 * D_MIX), f32) * D_MODEL ** -0.5
    w_four = jax.random.normal(ks[8], (DEPTH, D_FOURIER, D_FOURIER), f32) * D_FOURIER ** -0.5
    conv_w = jax.random.normal(ks[9], (DEPTH, CONV_WIDTH, D_LRU), f32) * CONV_WIDTH ** -0.5
    conv_b = jax.random.normal(ks[10], (DEPTH, D_LRU), f32) * 0.02
    gshape = (DEPTH, 2, N_LRU_HEADS, LRU_HEAD_DIM, LRU_HEAD_DIM)
    w_rg = jax.random.normal(ks[11], gshape, f32) * LRU_HEAD_DIM ** -0.5
    b_rg = jax.random.normal(ks[12], (DEPTH, 2, D_LRU), f32) * 0.02
    w_ig = jax.random.normal(ks[13], gshape, f32) * LRU_HEAD_DIM ** -0.5
    b_ig = jax.random.normal(ks[14], (DEPTH, 2, D_LRU), f32) * 0.02
    u = jax.random.uniform(ks[15], (DEPTH, 2, D_LRU), f32, minval=0.9, maxval=0.999)
    p = u ** (1.0 / LRU_C)
    lam = jnp.log(p) - jnp.log1p(-p)
    w_out = jax.random.normal(ks[16], (DEPTH, D_MIX, D_MODEL), f32) * D_MIX ** -0.5
    final_gain = 1.0 + 0.05 * jax.random.normal(ks[17], (D_MODEL,), f32)
    return {"x": x, "c": c, "ctx": ctx, "c_ctx": c_ctx, "w_ada": w_ada, "b_ada": b_ada,
            "norm_gain": norm_gain, "w_in": w_in, "w_four": w_four, "conv_w": conv_w,
            "conv_b": conv_b, "w_rg": w_rg, "b_rg": b_rg, "w_ig": w_ig, "b_ig": b_ig,
            "lam": lam, "w_out": w_out, "final_gain": final_gain}


def reference(x, c, ctx, c_ctx, w_ada, b_ada, norm_gain, w_in, w_four, conv_w, conv_b,
              w_rg, b_rg, w_ig, b_ig, lam, w_out, final_gain):
    bsz = x.shape[0]
    for l in range(DEPTH):
        mod = jax.nn.silu(c) @ w_ada[l] + b_ada[l]
        shift, scale, gate = jnp.split(mod[:, None, :], 3, axis=-1)
        mod_c = jax.nn.silu(c_ctx) @ w_ada[l] + b_ada[l]
        shift_c, scale_c, gate_c = jnp.split(mod_c, 3)

        hx = modulate(x, norm_gain[l], shift, scale)
        hc = modulate(ctx, norm_gain[l], shift_c, scale_c)
        uf_x, gf_x, ul_x, gl_x = in_proj(hx, w_in[l])
        uf_c, gf_c, ul_c, gl_c = in_proj(hc, w_in[l])

        zeros = jnp.zeros((bsz, D_LRU), jnp.float32)
        yl_c, fin_f, fin_b = lru_branch(ul_c, conv_w[l], conv_b[l], w_rg[l], b_rg[l],
                                        w_ig[l], b_ig[l], lam[l], zeros, zeros)
        yl_x, _, _ = lru_branch(ul_x, conv_w[l], conv_b[l], w_rg[l], b_rg[l],
                                w_ig[l], b_ig[l], lam[l], fin_f, fin_b)
        yf_x = fourier_branch(uf_x, w_four[l])
        x = x + gate * out_proj(yf_x, gf_x, yl_x, gl_x, w_out[l])
        if l < DEPTH - 1:
            yf_c = fourier_branch(uf_c, w_four[l])
            ctx = ctx + gate_c * out_proj(yf_c, gf_c, yl_c, gl_c, w_out[l])
    return rmsnorm(x, final_gain)
```

```python
import functools

import numpy as np
import jax
import jax.numpy as jnp
from jax import lax
from jax.experimental import pallas as pl
from jax.experimental.pallas import tpu as pltpu

D_MODEL = 1024
SEQ = 8192
CTX_LEN = 256
D_FOURIER = 512
D_LRU = 512
N_GROUPS = 4
GROUP_DIM = 128
N_HEADS = 4
HEAD_DIM = 128
LRU_C = 8.0
EPS = 1e-6

MC = 16
CH = 64
TILE = MC * CH
N_TILES = SEQ // TILE
CH_CTX = CTX_LEN // MC

DFT_N1 = 128
DFT_N2 = 64
F1_LANES = 4096
F2_K1 = 16

VMEM_LIMIT_BYTES = 56 * 1024 * 1024

F32 = jnp.float32
BF16 = jnp.bfloat16


def _dft_constants():
    c = np.arange(GROUP_DIM)
    ang = 2.0 * np.pi * ((c[:, None] * c[None, :]) % GROUP_DIM) / GROUP_DIM
    dftc = np.concatenate([np.cos(ang), np.sin(ang)], axis=1) * 2.0 ** -3

    n = np.arange(DFT_N1)
    ang1 = 2.0 * np.pi * ((n[:, None] * n[None, :]) % DFT_N1) / DFT_N1
    c1, s1 = np.cos(ang1), np.sin(ang1)
    f1 = np.block([[c1, -s1], [s1, c1]]) * 2.0 ** -4

    k1 = np.arange(DFT_N1)[:, None, None]
    k2 = np.arange(DFT_N2)[None, :, None]
    n2 = np.arange(DFT_N2)[None, None, :]
    ang2 = 2.0 * np.pi * ((n2 * (k1 + DFT_N1 * k2)) % SEQ) / SEQ
    m2 = np.concatenate([np.cos(ang2), -np.sin(ang2)], axis=2) * 2.0 ** -3
    return jnp.asarray(dftc, dtype=F32), jnp.asarray(f1, dtype=F32), jnp.asarray(m2, dtype=F32)


def _sigmoid(x):
    return 0.5 * jnp.tanh(0.5 * x) + 0.5


def _silu(x):
    hx = 0.5 * x
    return hx * jnp.tanh(hx) + hx


def _decay_rate(lam_row):
    z = -lam_row
    return -LRU_C * (jnp.maximum(z, 0.0) + jnp.log1p(jnp.exp(-jnp.abs(z))))


def _modulate(xs, gmul, shift):
    ms = jnp.mean(xs * xs, axis=-1, keepdims=True)
    return (xs * lax.rsqrt(ms + EPS)) * gmul + shift


def _gate_ab(pre_r, pre_i, v, rate):
    r = _sigmoid(pre_r)
    i = _sigmoid(pre_i)
    a = jnp.exp(r * rate)
    om = 1.0 - a * a
    s = om * lax.rsqrt(jnp.maximum(om, 1e-30))
    return a, s * (i * v)


def _shift_down(slab, first_row):
    return jnp.concatenate([first_row, slab[:MC - 1]], axis=0)


def _shift_up(slab, last_row):
    return jnp.concatenate([slab[1:], last_row], axis=0)


def _conv_tile(u_rows, lookahead, c_prev1, c_prev2, cw_ref, cb_ref, uext_s, ch):
    n = ch * MC
    uext_s[2 * MC:2 * MC + n, :] = u_rows
    u_last = u_rows[n - MC:n]
    u_last2 = u_rows[n - 2 * MC:n - MC]
    uext_s[0:MC, :] = _shift_down(u_last2, c_prev2)
    uext_s[MC:2 * MC, :] = _shift_down(u_last, c_prev1)
    uext_s[2 * MC + n:3 * MC + n, :] = _shift_up(u_rows[0:MC], lookahead)
    return u_last[MC - 1:MC], u_last2[MC - 1:MC]


def _conv_slab(uext_s, s, cw_ref, cb_ref):
    acc = cb_ref[...] + cw_ref[0:1, :] * uext_s[s * MC:(s + 1) * MC, :]
    acc = acc + cw_ref[1:2, :] * uext_s[(s + 1) * MC:(s + 2) * MC, :]
    acc = acc + cw_ref[2:3, :] * uext_s[(s + 2) * MC:(s + 3) * MC, :]
    acc = acc + cw_ref[3:4, :] * uext_s[(s + 3) * MC:(s + 4) * MC, :]
    return acc


def _gate_matmuls(vb_s, wg_ref, g_s, n):
    for h in range(N_HEADS):
        res = jnp.dot(vb_s[0:n, h * HEAD_DIM:(h + 1) * HEAD_DIM], wg_ref[h],
                      preferred_element_type=F32)
        g_s[0:n, h * HEAD_DIM:(h + 1) * HEAD_DIM] = res[:, :HEAD_DIM]
        g_s[0:n, D_LRU + h * HEAD_DIM:D_LRU + (h + 1) * HEAD_DIM] = res[:, HEAD_DIM:]


def _local_scan(g_s, vf_s, bg_ref, rate, a_s, h_s, ch, reverse):
    b_r = bg_ref[:, :D_LRU]
    b_i = bg_ref[:, D_LRU:]

    def body(it, carry):
        acc_a, acc_h = carry
        s = (ch - 1 - it) if reverse else it
        rows = pl.ds(pl.multiple_of(s * MC, MC), MC)
        pre = g_s[rows, :]
        v = vf_s[rows, :].astype(F32)
        a, bx = _gate_ab(pre[:, :D_LRU] + b_r, pre[:, D_LRU:] + b_i, v, rate)
        acc_a = a * acc_a
        acc_h = a * acc_h + bx
        a_s[rows, :] = acc_a
        h_s[rows, :] = acc_h
        return acc_a, acc_h

    init = (jnp.ones((MC, D_LRU), F32), jnp.zeros((MC, D_LRU), F32))
    return lax.fori_loop(0, ch, body, init)


def _chunk_carries(tot_a, tot_h, carry_in, reverse):
    row_id = lax.broadcasted_iota(jnp.int32, (MC, D_LRU), 0)
    hin = jnp.zeros((MC, D_LRU), F32)
    h = carry_in
    order = range(MC - 1, -1, -1) if reverse else range(MC)
    for m in order:
        hin = jnp.where(row_id == m, h, hin)
        h = tot_a[m:m + 1] * h + tot_h[m:m + 1]
    return hin, h


def _ada_kernel(c_ref, w_ref, b_ref, o_ref):
    s = _silu(c_ref[...]).astype(BF16)
    o_ref[...] = jnp.dot(s, w_ref[...].astype(BF16), preferred_element_type=F32) + b_ref[...]


def _ada_mod(cc, w_ada, b_ada):
    nblk = 512
    return pl.pallas_call(
        _ada_kernel,
        grid=(3 * D_MODEL // nblk,),
        in_specs=[pl.BlockSpec((8, D_MODEL), lambda i: (0, 0)),
                  pl.BlockSpec((D_MODEL, nblk), lambda i: (0, i)),
                  pl.BlockSpec((1, nblk), lambda i: (0, i))],
        out_specs=pl.BlockSpec((8, nblk), lambda i: (0, i)),
        out_shape=jax.ShapeDtypeStruct((8, 3 * D_MODEL), F32),
        name="ada_mod",
    )(cc, w_ada, b_ada)


def _ctx_kernel(x_ref, mod_ref, g_ref, wlx_ref, cw_ref, cb_ref, wgf_ref, bgf_ref, wgb_ref, bgb_ref,
                lam_ref, fin_ref, lhs_s, uext_s, vf_s, vb_s, g_s, a_s, h_s):
    ch = CH_CTX
    n = ch * MC
    shift = mod_ref[0, :, 0:D_MODEL]
    scale = mod_ref[0, :, D_MODEL:2 * D_MODEL]
    gmul = g_ref[...] * (1.0 + scale)
    for s in range(ch):
        xs = x_ref[0, :, s * D_MODEL:(s + 1) * D_MODEL]
        lhs_s[s * MC:(s + 1) * MC, :] = _modulate(xs, gmul, shift).astype(BF16)
    u = jnp.dot(lhs_s[...], wlx_ref[...], preferred_element_type=F32)
    zero_row = jnp.zeros((1, D_LRU), F32)
    _conv_tile(u, zero_row, zero_row, zero_row, cw_ref, cb_ref, uext_s, ch)
    for s in range(ch):
        v = _conv_slab(uext_s, s, cw_ref, cb_ref)
        vf_s[s * MC:(s + 1) * MC, :] = v
        vb_s[s * MC:(s + 1) * MC, :] = v.astype(BF16)
    for d, (wg_ref, bg_ref) in enumerate(((wgf_ref, bgf_ref), (wgb_ref, bgb_ref))):
        _gate_matmuls(vb_s, wg_ref, g_s, n)
        rate = _decay_rate(lam_ref[d:d + 1, :])
        tot_a, tot_h = _local_scan(g_s, vf_s, bg_ref, rate, a_s, h_s, ch, reverse=(d == 1))
        _, fin = _chunk_carries(tot_a, tot_h, zero_row, reverse=(d == 1))
        fin_ref[0, d:d + 1, :] = fin


def _ctx_states(ctx3, mod3, g, wlx, cw, cb, wgf, bgf, wgb, bgb, lam):
    bsz = ctx3.shape[0]
    n = CTX_LEN
    full = lambda *shape: pl.BlockSpec(shape, lambda b: (0,) * len(shape))
    return pl.pallas_call(
        _ctx_kernel,
        grid=(bsz,),
        in_specs=[pl.BlockSpec((1, MC, CH_CTX * D_MODEL), lambda b: (b, 0, 0)),
                  pl.BlockSpec((1, 1, 3 * D_MODEL), lambda b: (bsz, 0, 0)),
                  full(1, D_MODEL), full(D_MODEL, D_LRU), full(4, D_LRU), full(1, D_LRU),
                  full(N_HEADS, HEAD_DIM, 2 * HEAD_DIM), full(1, 2 * D_LRU),
                  full(N_HEADS, HEAD_DIM, 2 * HEAD_DIM), full(1, 2 * D_LRU),
                  full(2, D_LRU)],
        out_specs=pl.BlockSpec((1, 2, D_LRU), lambda b: (b, 0, 0)),
        out_shape=jax.ShapeDtypeStruct((bsz, 2, D_LRU), F32),
        scratch_shapes=[pltpu.VMEM((n, D_MODEL), BF16),
                        pltpu.VMEM((n + 3 * MC, D_LRU), F32),
                        pltpu.VMEM((n, D_LRU), F32),
                        pltpu.VMEM((n, D_LRU), BF16),
                        pltpu.VMEM((n, 2 * D_LRU), F32),
                        pltpu.VMEM((n, D_LRU), F32),
                        pltpu.VMEM((n, D_LRU), F32)],
        compiler_params=pltpu.CompilerParams(dimension_semantics=("arbitrary",)),
        name="ctx_states",
    )(ctx3, mod3, g, wlx, cw, cb, wgf, bgf, wgb, bgb, lam)


def _fwd_kernel(x_ref, xh_ref, mod_ref, g_ref, win_ref, dftc_ref, cw_ref, cb_ref, wg_ref, bg_ref,
                lam_ref, h0_ref,
                p_ref, q_ref, gfs_ref, gls_ref, v_ref, hf_ref,
                lhs_s, uext_s, vf_s, vb_s, g_s, a_s, h_s, cc_s, hc_s):
    j = pl.program_id(1)

    @pl.when(j == 0)
    def _():
        cc_s[...] = jnp.zeros_like(cc_s)
        hc_s[...] = jnp.broadcast_to(h0_ref[0, 0:1, :], hc_s.shape)

    shift = mod_ref[0, :, 0:D_MODEL]
    scale = mod_ref[0, :, D_MODEL:2 * D_MODEL]
    gmul = g_ref[...] * (1.0 + scale)
    for s in range(CH):
        xs = x_ref[0, :, s * D_MODEL:(s + 1) * D_MODEL]
        lhs_s[s * MC:(s + 1) * MC, :] = _modulate(xs, gmul, shift).astype(BF16)
    hh = _modulate(xh_ref[0, 0], gmul, shift)
    lhs_s[TILE:TILE + MC, :] = jnp.concatenate([hh, jnp.zeros_like(hh)], axis=0).astype(BF16)

    zf = jnp.dot(lhs_s[0:TILE, :], win_ref[:, 0:D_FOURIER], preferred_element_type=F32).astype(BF16)
    dftc = dftc_ref[...].astype(BF16)
    for g in range(N_GROUPS):
        pq = jnp.dot(zf[:, g * GROUP_DIM:(g + 1) * GROUP_DIM], dftc,
                     preferred_element_type=F32).astype(BF16)
        for s in range(CH):
            lo = s * D_FOURIER + g * GROUP_DIM
            p_ref[0, 0, :, lo:lo + GROUP_DIM] = pq[s * MC:(s + 1) * MC, 0:GROUP_DIM]
            q_ref[0, 0, :, lo:lo + GROUP_DIM] = pq[s * MC:(s + 1) * MC, GROUP_DIM:]

    for cols, o_ref in ((D_FOURIER, gfs_ref), (2 * D_FOURIER + D_LRU, gls_ref)):
        zg = _silu(jnp.dot(lhs_s[0:TILE, :], win_ref[:, cols:cols + D_LRU],
                           preferred_element_type=F32)).astype(BF16)
        for s in range(CH):
            o_ref[0, 0, :, s * D_LRU:(s + 1) * D_LRU] = zg[s * MC:(s + 1) * MC, :]

    u = jnp.dot(lhs_s[...], win_ref[:, 2 * D_FOURIER:2 * D_FOURIER + D_LRU],
                preferred_element_type=F32)
    look = jnp.where(j == N_TILES - 1, 0.0, u[TILE:TILE + 1])
    c1, c2 = _conv_tile(u[0:TILE], look, cc_s[0:1, :], cc_s[1:2, :], cw_ref, cb_ref, uext_s, CH)
    cc_s[0:1, :] = c1
    cc_s[1:2, :] = c2
    for s in range(CH):
        v = _conv_slab(uext_s, s, cw_ref, cb_ref)
        vb = v.astype(BF16)
        vf_s[s * MC:(s + 1) * MC, :] = v
        vb_s[s * MC:(s + 1) * MC, :] = vb
        v_ref[0, 0, :, s * D_LRU:(s + 1) * D_LRU] = vb

    _gate_matmuls(vb_s, wg_ref, g_s, TILE)
    rate = _decay_rate(lam_ref[0:1, :])
    tot_a, tot_h = _local_scan(g_s, vf_s, bg_ref, rate, a_s, h_s, CH, reverse=False)
    hin, h_out = _chunk_carries(tot_a, tot_h, hc_s[0:1, :], reverse=False)
    hc_s[...] = jnp.broadcast_to(h_out, hc_s.shape)
    for s in range(CH):
        h = h_s[s * MC:(s + 1) * MC, :] + a_s[s * MC:(s + 1) * MC, :] * hin
        hf_ref[0, 0, :, s * D_LRU:(s + 1) * D_LRU] = h.astype(BF16)


def _fwd_pass(x3, x4, mod3, g, win, dftc, cw, cb, wgf, bgf, lam, fin):
    bsz = x3.shape[0]
    full = lambda *shape: pl.BlockSpec(shape, lambda b, j: (0,) * len(shape))
    fold = lambda c: pl.BlockSpec((1, 1, MC, CH * c), lambda b, j: (b, j, 0, 0))
    fold_shape = lambda c: jax.ShapeDtypeStruct((bsz, N_TILES, MC, CH * c), BF16)
    rows8 = TILE // 8
    return pl.pallas_call(
        _fwd_kernel,
        grid=(bsz, N_TILES),
        in_specs=[pl.BlockSpec((1, MC, CH * D_MODEL), lambda b, j: (b, j, 0)),
                  pl.BlockSpec((1, 1, 8, D_MODEL),
                               lambda b, j: (b, jnp.minimum((j + 1) * rows8, SEQ // 8 - 1), 0, 0)),
                  pl.BlockSpec((1, 1, 3 * D_MODEL), lambda b, j: (b, 0, 0)),
                  full(1, D_MODEL), full(D_MODEL, 2 * D_MODEL), full(GROUP_DIM, 2 * GROUP_DIM),
                  full(4, D_LRU), full(1, D_LRU),
                  full(N_HEADS, HEAD_DIM, 2 * HEAD_DIM), full(1, 2 * D_LRU),
                  full(2, D_LRU),
                  pl.BlockSpec((1, 2, D_LRU), lambda b, j: (b, 0, 0))],
        out_specs=[fold(D_FOURIER), fold(D_FOURIER), fold(D_FOURIER), fold(D_LRU), fold(D_LRU),
                   fold(D_LRU)],
        out_shape=[fold_shape(D_FOURIER), fold_shape(D_FOURIER), fold_shape(D_FOURIER),
                   fold_shape(D_LRU), fold_shape(D_LRU), fold_shape(D_LRU)],
        scratch_shapes=[pltpu.VMEM((TILE + MC, D_MODEL), BF16),
                        pltpu.VMEM((TILE + 3 * MC, D_LRU), F32),
                        pltpu.VMEM((TILE, D_LRU), F32),
                        pltpu.VMEM((TILE, D_LRU), BF16),
                        pltpu.VMEM((TILE, 2 * D_LRU), F32),
                        pltpu.VMEM((TILE, D_LRU), F32),
                        pltpu.VMEM((TILE, D_LRU), F32),
                        pltpu.VMEM((8, D_LRU), F32),
                        pltpu.VMEM((8, D_LRU), F32)],
        compiler_params=pltpu.CompilerParams(dimension_semantics=("arbitrary", "arbitrary"),
                                             vmem_limit_bytes=VMEM_LIMIT_BYTES),
        name="fwd_pass",
    )(x3, x4, mod3, g, win, dftc, cw, cb, wgf, bgf, lam, fin)


def _dft1_kernel(p_ref, q_ref, f1_ref, yr_ref, yj_ref):
    rhs = jnp.concatenate([p_ref[0], q_ref[0]], axis=0)
    y = jnp.dot(f1_ref[...].astype(BF16), rhs, preferred_element_type=F32)
    yr_ref[0] = y[:DFT_N1].astype(BF16)
    yj_ref[0] = y[DFT_N1:].astype(BF16)


def _dft_stage1(p3, q3, f1):
    bsz, _, lanes = p3.shape
    blk = pl.BlockSpec((1, DFT_N1, F1_LANES), lambda b, i: (b, 0, i))
    shp = jax.ShapeDtypeStruct(p3.shape, BF16)
    return pl.pallas_call(
        _dft1_kernel,
        grid=(bsz, lanes // F1_LANES),
        in_specs=[blk, blk, pl.BlockSpec((2 * DFT_N1, 2 * DFT_N1), lambda b, i: (0, 0))],
        out_specs=[blk, blk],
        out_shape=[shp, shp],
        compiler_params=pltpu.CompilerParams(dimension_semantics=("arbitrary", "arbitrary")),
        name="dft_stage1",
    )(p3, q3, f1)


def _dft2_kernel(yr_ref, yj_ref, m2_ref, o_ref):
    for kk in range(F2_K1):
        rows = slice(kk * DFT_N2, (kk + 1) * DFT_N2)
        rhs = jnp.concatenate([yr_ref[0, rows, :], yj_ref[0, rows, :]], axis=0)
        o = jnp.dot(m2_ref[kk].astype(BF16), rhs, preferred_element_type=F32)
        o_ref[0, :, kk * D_FOURIER:(kk + 1) * D_FOURIER] = o.astype(BF16)


def _dft_stage2(yr, yj, m2):
    bsz = yr.shape[0]
    blk = pl.BlockSpec((1, F2_K1 * DFT_N2, D_FOURIER), lambda b, i: (b, i, 0))
    return pl.pallas_call(
        _dft2_kernel,
        grid=(bsz, DFT_N1 // F2_K1),
        in_specs=[blk, blk, pl.BlockSpec((F2_K1, DFT_N2, 2 * DFT_N2), lambda b, i: (i, 0, 0))],
        out_specs=pl.BlockSpec((1, DFT_N2, F2_K1 * D_FOURIER), lambda b, i: (b, 0, i)),
        out_shape=jax.ShapeDtypeStruct((bsz, DFT_N2, DFT_N1 * D_FOURIER), BF16),
        compiler_params=pltpu.CompilerParams(dimension_semantics=("arbitrary", "arbitrary")),
        name="dft_stage2",
    )(yr, yj, m2)


def _bwd_kernel(x_ref, yf_ref, gfs_ref, gls_ref, v_ref, hf_ref, mod_ref, fg_ref, wfour_ref, wout_ref,
                wg_ref, bg_ref, lam_ref, h0_ref, o_ref,
                vb_s, yl_s, g_s, a_s, h_s, lhs_s, hc_s):
    j = pl.program_id(1)

    @pl.when(j == 0)
    def _():
        hc_s[...] = jnp.broadcast_to(h0_ref[0, 1:2, :], hc_s.shape)

    for s in range(CH):
        vb_s[s * MC:(s + 1) * MC, :] = v_ref[0, 0, :, s * D_LRU:(s + 1) * D_LRU]
        yl_s[s * MC:(s + 1) * MC, :] = yf_ref[0, 0, :, s * D_FOURIER:(s + 1) * D_FOURIER]

    yfw = jnp.dot(yl_s[...], wfour_ref[...], preferred_element_type=F32)
    for s in range(CH):
        gate = gfs_ref[0, 0, :, s * D_FOURIER:(s + 1) * D_FOURIER].astype(F32)
        lhs_s[s * MC:(s + 1) * MC, 0:D_FOURIER] = (yfw[s * MC:(s + 1) * MC, :] * gate).astype(BF16)

    _gate_matmuls(vb_s, wg_ref, g_s, TILE)
    rate = _decay_rate(lam_ref[1:2, :])
    tot_a, tot_h = _local_scan(g_s, vb_s, bg_ref, rate, a_s, h_s, CH, reverse=True)
    hin, h_out = _chunk_carries(tot_a, tot_h, hc_s[0:1, :], reverse=True)
    hc_s[...] = jnp.broadcast_to(h_out, hc_s.shape)
    for s in range(CH):
        hb = h_s[s * MC:(s + 1) * MC, :] + a_s[s * MC:(s + 1) * MC, :] * hin
        yl = hb + hf_ref[0, 0, :, s * D_LRU:(s + 1) * D_LRU].astype(F32)
        gate = gls_ref[0, 0, :, s * D_LRU:(s + 1) * D_LRU].astype(F32)
        lhs_s[s * MC:(s + 1) * MC, D_FOURIER:] = (yl * gate).astype(BF16)

    proj = jnp.dot(lhs_s[...], wout_ref[...], preferred_element_type=F32)
    res_gate = mod_ref[0, :, 2 * D_MODEL:3 * D_MODEL]
    fg = fg_ref[...]
    for s in range(CH):
        r = x_ref[0, :, s * D_MODEL:(s + 1) * D_MODEL] + res_gate * proj[s * MC:(s + 1) * MC, :]
        ms = jnp.mean(r * r, axis=-1, keepdims=True)
        o_ref[0, :, s * D_MODEL:(s + 1) * D_MODEL] = (r * lax.rsqrt(ms + EPS)) * fg


def _bwd_pass(x3, yf, gfs, gls, v, hf, mod3, fg, wfour, wout, wgb, bgb, lam, fin):
    bsz = x3.shape[0]
    rev = lambda j: N_TILES - 1 - j
    full = lambda *shape: pl.BlockSpec(shape, lambda b, j: (0,) * len(shape))
    fold = lambda c: pl.BlockSpec((1, 1, MC, CH * c), lambda b, j: (b, rev(j), 0, 0))
    xblk = pl.BlockSpec((1, MC, CH * D_MODEL), lambda b, j: (b, rev(j), 0))
    return pl.pallas_call(
        _bwd_kernel,
        grid=(bsz, N_TILES),
        in_specs=[xblk, fold(D_FOURIER), fold(D_FOURIER), fold(D_LRU), fold(D_LRU), fold(D_LRU),
                  pl.BlockSpec((1, 1, 3 * D_MODEL), lambda b, j: (b, 0, 0)),
                  full(1, D_MODEL), full(D_FOURIER, D_FOURIER), full(D_MODEL, D_MODEL),
                  full(N_HEADS, HEAD_DIM, 2 * HEAD_DIM), full(1, 2 * D_LRU), full(2, D_LRU),
                  pl.BlockSpec((1, 2, D_LRU), lambda b, j: (b, 0, 0))],
        out_specs=xblk,
        out_shape=jax.ShapeDtypeStruct(x3.shape, F32),
        scratch_shapes=[pltpu.VMEM((TILE, D_LRU), BF16),
                        pltpu.VMEM((TILE, D_FOURIER), BF16),
                        pltpu.VMEM((TILE, 2 * D_LRU), F32),
                        pltpu.VMEM((TILE, D_LRU), F32),
                        pltpu.VMEM((TILE, D_LRU), F32),
                        pltpu.VMEM((TILE, D_MODEL), BF16),
                        pltpu.VMEM((8, D_LRU), F32)],
        compiler_params=pltpu.CompilerParams(dimension_semantics=("arbitrary", "arbitrary"),
                                             vmem_limit_bytes=VMEM_LIMIT_BYTES),
        name="bwd_pass",
    )(x3, yf, gfs, gls, v, hf, mod3, fg, wfour, wout, wgb, bgb, lam, fin)


def kernel(x, c, ctx, c_ctx, w_ada, b_ada, norm_gain, w_in, w_four, conv_w, conv_b, w_rg, b_rg,
           w_ig, b_ig, lam, w_out, final_gain):
    bsz = x.shape[0]
    assert x.shape == (bsz, SEQ, D_MODEL) and ctx.shape == (bsz, CTX_LEN, D_MODEL)
    assert w_ada.shape[0] == 1, "single-layer kernel"
    dftc, f1, m2 = _dft_constants()

    cc = jnp.concatenate([c, c_ctx[None, :], jnp.zeros((8 - bsz - 1, D_MODEL), F32)], axis=0)
    mod3 = _ada_mod(cc, w_ada[0], b_ada[0][None, :]).reshape(8, 1, 3 * D_MODEL)

    g = norm_gain[0][None, :]
    win = w_in[0].astype(BF16)
    wlx = win[:, 2 * D_FOURIER:2 * D_FOURIER + D_LRU]
    cw, cb = conv_w[0], conv_b[0][None, :]
    wg = jnp.concatenate([w_rg[0], w_ig[0]], axis=-1).astype(BF16)
    bg = jnp.concatenate([b_rg[0], b_ig[0]], axis=-1)[:, None, :]
    lam0 = lam[0]

    fin = _ctx_states(ctx.reshape(bsz, MC, CH_CTX * D_MODEL), mod3, g, wlx, cw, cb,
                      wg[0], bg[0], wg[1], bg[1], lam0)

    x3 = x.reshape(bsz, SEQ // CH, CH * D_MODEL)
    x4 = x.reshape(bsz, SEQ // 8, 8, D_MODEL)
    p, q, gfs, gls, v, hf = _fwd_pass(x3, x4, mod3, g, win, dftc, cw, cb, wg[0], bg[0], lam0, fin)

    lanes = DFT_N2 * D_FOURIER
    yr, yj = _dft_stage1(p.reshape(bsz, DFT_N1, lanes), q.reshape(bsz, DFT_N1, lanes), f1)
    yf = _dft_stage2(yr.reshape(bsz, SEQ, D_FOURIER), yj.reshape(bsz, SEQ, D_FOURIER), m2)
    yf = yf.reshape(bsz, N_TILES, MC, CH * D_FOURIER)

    out3 = _bwd_pass(x3, yf, gfs, gls, v, hf, mod3, final_gain[None, :], w_four[0].astype(BF16),
                     w_out[0].astype(BF16), wg[1], bg[1], lam0, fin)
    return out3.reshape(bsz, SEQ, D_MODEL)
```

```python
import numpy as np
import jax
import jax.numpy as jnp
from jax import lax
from jax.experimental import pallas as pl
from jax.experimental.pallas import tpu as pltpu

D_MODEL = 1024
SEQ = 8192
CTX_LEN = 256
D_FOURIER = 512
D_LRU = 512
N_GROUPS = 4
GROUP_DIM = 128
N_HEADS = 4
HEAD_DIM = 128
LRU_C = 8.0
EPS = 1e-6

LANES = 128
MC = 16
CH = 64
TILE = MC * CH
N_TILES = SEQ // TILE
CH_CTX = CTX_LEN // MC
N_QL = D_LRU // LANES

DFT_N1 = 128
DFT_N2 = 64
F1_LANES = 4096
F2_K1 = 16

VMEM_LIMIT_BYTES = 56 * 1024 * 1024

F32 = jnp.float32
BF16 = jnp.bfloat16


def _pitch(ch):
    return ch + 8


def _dft_constants():
    c = np.arange(GROUP_DIM)
    ang = 2.0 * np.pi * ((c[:, None] * c[None, :]) % GROUP_DIM) / GROUP_DIM
    dftc = np.concatenate([np.cos(ang), np.sin(ang)], axis=1) * 2.0 ** -3

    n = np.arange(DFT_N1)
    ang1 = 2.0 * np.pi * ((n[:, None] * n[None, :]) % DFT_N1) / DFT_N1
    c1, s1 = np.cos(ang1), np.sin(ang1)
    f1 = np.block([[c1, -s1], [s1, c1]]) * 2.0 ** -4

    k1 = np.arange(DFT_N1)[:, None, None]
    k2 = np.arange(DFT_N2)[None, :, None]
    n2 = np.arange(DFT_N2)[None, None, :]
    ang2 = 2.0 * np.pi * ((n2 * (k1 + DFT_N1 * k2)) % SEQ) / SEQ
    m2 = np.concatenate([np.cos(ang2), -np.sin(ang2)], axis=2) * 2.0 ** -3
    return jnp.asarray(dftc, dtype=F32), jnp.asarray(f1, dtype=F32), jnp.asarray(m2, dtype=F32)


def _sigmoid(x):
    return 0.5 * jnp.tanh(0.5 * x) + 0.5


def _silu(x):
    hx = 0.5 * x
    return hx * jnp.tanh(hx) + hx


def _decay_rate(lam_row):
    z = -lam_row
    return -LRU_C * (jnp.maximum(z, 0.0) + jnp.log1p(jnp.exp(-jnp.abs(z))))


def _modulate(xs, gmul, shift):
    ms = jnp.mean(xs * xs, axis=-1, keepdims=True)
    return (xs * lax.rsqrt(ms + EPS)) * gmul + shift


def _gate_ab(pre_r, pre_i, v, rate):
    r = _sigmoid(pre_r)
    i = _sigmoid(pre_i)
    a = jnp.exp(r * rate)
    om = 1.0 - a * a
    s = om * lax.rsqrt(jnp.maximum(om, 1e-30))
    return a, s * (i * v)


def _to_slab_order(u, up_s, uext_s, ch):
    pitch = _pitch(ch)
    for m in range(MC):
        for q in range(N_QL):
            up_s[q, m * pitch:m * pitch + ch, :] = u[m * ch:(m + 1) * ch, q * LANES:(q + 1) * LANES]
    for s in range(ch):
        for q in range(N_QL):
            uext_s[(s + 2) * MC:(s + 3) * MC, q * LANES:(q + 1) * LANES] = (
                up_s[q, pl.ds(s, MC, stride=pitch), :])


def _conv_halo(uext_s, lookahead, c_prev1, c_prev2, ch):
    last = uext_s[(ch + 1) * MC:(ch + 2) * MC, :]
    last2 = uext_s[ch * MC:(ch + 1) * MC, :]
    first = uext_s[2 * MC:3 * MC, :]
    uext_s[0:MC, :] = jnp.concatenate([c_prev2, last2[:MC - 1]], axis=0)
    uext_s[MC:2 * MC, :] = jnp.concatenate([c_prev1, last[:MC - 1]], axis=0)
    uext_s[(ch + 2) * MC:(ch + 3) * MC, :] = jnp.concatenate([first[1:], lookahead], axis=0)


def _conv_slab(uext_s, s, cw_ref, cb_ref):
    acc = cb_ref[...] + cw_ref[0:1, :] * uext_s[s * MC:(s + 1) * MC, :]
    acc = acc + cw_ref[1:2, :] * uext_s[(s + 1) * MC:(s + 2) * MC, :]
    acc = acc + cw_ref[2:3, :] * uext_s[(s + 2) * MC:(s + 3) * MC, :]
    acc = acc + cw_ref[3:4, :] * uext_s[(s + 3) * MC:(s + 4) * MC, :]
    return acc


def _gate_matmuls(vb, wg_ref, g_s):
    for h in range(N_HEADS):
        res = jnp.dot(vb[:, h * HEAD_DIM:(h + 1) * HEAD_DIM], wg_ref[h], preferred_element_type=F32)
        g_s[:, h * HEAD_DIM:(h + 1) * HEAD_DIM] = res[:, :HEAD_DIM]
        g_s[:, D_LRU + h * HEAD_DIM:D_LRU + (h + 1) * HEAD_DIM] = res[:, HEAD_DIM:]


def _local_scan(g_s, load_v, bg_ref, rate, a_s, h_s, ch, reverse):
    b_r = bg_ref[:, :D_LRU]
    b_i = bg_ref[:, D_LRU:]

    def body(it, carry):
        acc_a, acc_h = carry
        s = (ch - 1 - it) if reverse else it
        rows = pl.ds(pl.multiple_of(s * MC, MC), MC)
        pre = g_s[rows, :]
        a, bx = _gate_ab(pre[:, :D_LRU] + b_r, pre[:, D_LRU:] + b_i, load_v(rows), rate)
        acc_a = a * acc_a
        acc_h = a * acc_h + bx
        a_s[rows, :] = acc_a
        h_s[rows, :] = acc_h
        return acc_a, acc_h

    init = (jnp.ones((MC, D_LRU), F32), jnp.zeros((MC, D_LRU), F32))
    return lax.fori_loop(0, ch, body, init)


def _chunk_carries(tot_a, tot_h, carry_in, reverse):
    row_id = lax.broadcasted_iota(jnp.int32, (MC, D_LRU), 0)
    hin = jnp.zeros((MC, D_LRU), F32)
    h = carry_in
    order = range(MC - 1, -1, -1) if reverse else range(MC)
    for m in order:
        hin = jnp.where(row_id == m, h, hin)
        h = tot_a[m:m + 1] * h + tot_h[m:m + 1]
    return hin, h


def _ada_kernel(c_ref, w_ref, b_ref, o_ref):
    s = _silu(c_ref[...]).astype(BF16)
    o_ref[...] = jnp.dot(s, w_ref[...].astype(BF16), preferred_element_type=F32) + b_ref[...]


def _ada_mod(cc, w_ada, b_ada):
    nblk = 512
    return pl.pallas_call(
        _ada_kernel,
        grid=(3 * D_MODEL // nblk,),
        in_specs=[pl.BlockSpec((8, D_MODEL), lambda i: (0, 0)),
                  pl.BlockSpec((D_MODEL, nblk), lambda i: (0, i)),
                  pl.BlockSpec((1, nblk), lambda i: (0, i))],
        out_specs=pl.BlockSpec((8, nblk), lambda i: (0, i)),
        out_shape=jax.ShapeDtypeStruct((8, 3 * D_MODEL), F32),
        name="ada_mod",
    )(cc, w_ada, b_ada)


def _ctx_kernel(x_ref, mod_ref, g_ref, wlx_ref, cw_ref, cb_ref, wgf_ref, bgf_ref, wgb_ref, bgb_ref,
                lam_ref, fin_ref, up_s, uext_s, vf_s, g_s, a_s, h_s):
    ch = CH_CTX
    shift = mod_ref[0, :, 0:D_MODEL]
    scale = mod_ref[0, :, D_MODEL:2 * D_MODEL]
    gmul = g_ref[...] * (1.0 + scale)
    lhs = _modulate(x_ref[0], gmul, shift).astype(BF16)
    u = jnp.dot(lhs, wlx_ref[...], preferred_element_type=F32)
    _to_slab_order(u, up_s, uext_s, ch)
    zero_row = jnp.zeros((1, D_LRU), F32)
    _conv_halo(uext_s, zero_row, zero_row, zero_row, ch)
    for s in range(ch):
        vf_s[s * MC:(s + 1) * MC, :] = _conv_slab(uext_s, s, cw_ref, cb_ref)
    vb = vf_s[...].astype(BF16)
    for d, (wg_ref, bg_ref) in enumerate(((wgf_ref, bgf_ref), (wgb_ref, bgb_ref))):
        _gate_matmuls(vb, wg_ref, g_s)
        rate = _decay_rate(lam_ref[d:d + 1, :])
        tot_a, tot_h = _local_scan(g_s, lambda rows: vf_s[rows, :], bg_ref, rate, a_s, h_s, ch,
                                   reverse=(d == 1))
        _, fin = _chunk_carries(tot_a, tot_h, zero_row, reverse=(d == 1))
        fin_ref[0, d:d + 1, :] = fin


def _ctx_states(ctx, mod3, g, wlx, cw, cb, wgf, bgf, wgb, bgb, lam):
    bsz = ctx.shape[0]
    n = CTX_LEN
    full = lambda *shape: pl.BlockSpec(shape, lambda b: (0,) * len(shape))
    return pl.pallas_call(
        _ctx_kernel,
        grid=(bsz,),
        in_specs=[pl.BlockSpec((1, n, D_MODEL), lambda b: (b, 0, 0)),
                  pl.BlockSpec((1, 1, 3 * D_MODEL), lambda b: (bsz, 0, 0)),
                  full(1, D_MODEL), full(D_MODEL, D_LRU), full(4, D_LRU), full(1, D_LRU),
                  full(N_HEADS, HEAD_DIM, 2 * HEAD_DIM), full(1, 2 * D_LRU),
                  full(N_HEADS, HEAD_DIM, 2 * HEAD_DIM), full(1, 2 * D_LRU),
                  full(2, D_LRU)],
        out_specs=pl.BlockSpec((1, 2, D_LRU), lambda b: (b, 0, 0)),
        out_shape=jax.ShapeDtypeStruct((bsz, 2, D_LRU), F32),
        scratch_shapes=[pltpu.VMEM((N_QL, MC * _pitch(CH_CTX), LANES), F32),
                        pltpu.VMEM((n + 3 * MC, D_LRU), F32),
                        pltpu.VMEM((n, D_LRU), F32),
                        pltpu.VMEM((n, 2 * D_LRU), F32),
                        pltpu.VMEM((n, D_LRU), F32),
                        pltpu.VMEM((n, D_LRU), F32)],
        compiler_params=pltpu.CompilerParams(dimension_semantics=("arbitrary",)),
        name="ctx_states",
    )(ctx, mod3, g, wlx, cw, cb, wgf, bgf, wgb, bgb, lam)


def _fwd_kernel(x_ref, xh_ref, mod_ref, g_ref, win_ref, dftc_ref, cw_ref, cb_ref, wg_ref, bg_ref,
                lam_ref, h0_ref,
                p_ref, q_ref, gfs_ref, gls_ref, v_ref, hf_ref,
                lhs_s, up_s, uext_s, vf_s, g_s, a_s, h_s, cc_s, hc_s):
    j = pl.program_id(1)

    @pl.when(j == 0)
    def _():
        cc_s[...] = jnp.zeros_like(cc_s)
        hc_s[...] = jnp.broadcast_to(h0_ref[0, 0:1, :], hc_s.shape)

    shift = mod_ref[0, :, 0:D_MODEL]
    scale = mod_ref[0, :, D_MODEL:2 * D_MODEL]
    gmul = g_ref[...] * (1.0 + scale)
    for m in range(MC):
        lhs_s[m * CH:(m + 1) * CH, :] = _modulate(x_ref[0, m * CH:(m + 1) * CH, :], gmul,
                                                  shift).astype(BF16)
    hh = _modulate(xh_ref[0, 0], gmul, shift)
    lhs_s[TILE:TILE + MC, :] = jnp.concatenate([hh, jnp.zeros_like(hh)], axis=0).astype(BF16)

    zf = jnp.dot(lhs_s[0:TILE, :], win_ref[:, 0:D_FOURIER], preferred_element_type=F32).astype(BF16)
    dftc = dftc_ref[...].astype(BF16)
    for g in range(N_GROUPS):
        cols = slice(g * GROUP_DIM, (g + 1) * GROUP_DIM)
        pq = jnp.dot(zf[:, cols], dftc, preferred_element_type=F32).astype(BF16)
        p_ref[0, :, cols] = pq[:, :GROUP_DIM]
        q_ref[0, :, cols] = pq[:, GROUP_DIM:]

    for cols, o_ref in ((D_FOURIER, gfs_ref), (2 * D_FOURIER + D_LRU, gls_ref)):
        o_ref[0] = _silu(jnp.dot(lhs_s[0:TILE, :], win_ref[:, cols:cols + D_LRU],
                                 preferred_element_type=F32)).astype(BF16)

    u = jnp.dot(lhs_s[...], win_ref[:, 2 * D_FOURIER:2 * D_FOURIER + D_LRU],
                preferred_element_type=F32)
    _to_slab_order(u, up_s, uext_s, CH)
    look = jnp.where(j == N_TILES - 1, 0.0, u[TILE:TILE + 1])
    _conv_halo(uext_s, look, cc_s[0:1, :], cc_s[1:2, :], CH)
    cc_s[0:1, :] = u[TILE - 1:TILE]
    cc_s[1:2, :] = u[TILE - 2:TILE - 1]
    for s in range(CH):
        v = _conv_slab(uext_s, s, cw_ref, cb_ref)
        vf_s[s * MC:(s + 1) * MC, :] = v
        v_ref[0, s * MC:(s + 1) * MC, :] = v.astype(BF16)

    _gate_matmuls(v_ref[0], wg_ref, g_s)
    rate = _decay_rate(lam_ref[0:1, :])
    tot_a, tot_h = _local_scan(g_s, lambda rows: vf_s[rows, :], bg_ref, rate, a_s, h_s, CH,
                               reverse=False)
    hin, h_out = _chunk_carries(tot_a, tot_h, hc_s[0:1, :], reverse=False)
    hc_s[...] = jnp.broadcast_to(h_out, hc_s.shape)
    for s in range(CH):
        rows = slice(s * MC, (s + 1) * MC)
        hf_ref[0, rows, :] = (h_s[rows, :] + a_s[rows, :] * hin).astype(BF16)


def _fwd_pass(x, x4, mod3, g, win, dftc, cw, cb, wgf, bgf, lam, fin):
    bsz = x.shape[0]
    full = lambda *shape: pl.BlockSpec(shape, lambda b, j: (0,) * len(shape))
    tile = lambda c: pl.BlockSpec((1, TILE, c), lambda b, j: (b, j, 0))
    shp = lambda c: jax.ShapeDtypeStruct((bsz, SEQ, c), BF16)
    rows8 = TILE // 8
    return pl.pallas_call(
        _fwd_kernel,
        grid=(bsz, N_TILES),
        in_specs=[tile(D_MODEL),
                  pl.BlockSpec((1, 1, 8, D_MODEL),
                               lambda b, j: (b, jnp.minimum((j + 1) * rows8, SEQ // 8 - 1), 0, 0)),
                  pl.BlockSpec((1, 1, 3 * D_MODEL), lambda b, j: (b, 0, 0)),
                  full(1, D_MODEL), full(D_MODEL, 2 * D_MODEL), full(GROUP_DIM, 2 * GROUP_DIM),
                  full(4, D_LRU), full(1, D_LRU),
                  full(N_HEADS, HEAD_DIM, 2 * HEAD_DIM), full(1, 2 * D_LRU),
                  full(2, D_LRU),
                  pl.BlockSpec((1, 2, D_LRU), lambda b, j: (b, 0, 0))],
        out_specs=[tile(D_FOURIER), tile(D_FOURIER), tile(D_FOURIER), tile(D_LRU), tile(D_LRU),
                   tile(D_LRU)],
        out_shape=[shp(D_FOURIER), shp(D_FOURIER), shp(D_FOURIER), shp(D_LRU), shp(D_LRU),
                   shp(D_LRU)],
        scratch_shapes=[pltpu.VMEM((TILE + MC, D_MODEL), BF16),
                        pltpu.VMEM((N_QL, MC * _pitch(CH), LANES), F32),
                        pltpu.VMEM((TILE + 3 * MC, D_LRU), F32),
                        pltpu.VMEM((TILE, D_LRU), F32),
                        pltpu.VMEM((TILE, 2 * D_LRU), F32),
                        pltpu.VMEM((TILE, D_LRU), F32),
                        pltpu.VMEM((TILE, D_LRU), F32),
                        pltpu.VMEM((8, D_LRU), F32),
                        pltpu.VMEM((8, D_LRU), F32)],
        compiler_params=pltpu.CompilerParams(dimension_semantics=("arbitrary", "arbitrary"),
                                             vmem_limit_bytes=VMEM_LIMIT_BYTES),
        name="fwd_pass",
    )(x, x4, mod3, g, win, dftc, cw, cb, wgf, bgf, lam, fin)


def _dft1_kernel(p_ref, q_ref, f1_ref, yr_ref, yj_ref):
    rhs = jnp.concatenate([p_ref[0], q_ref[0]], axis=0)
    y = jnp.dot(f1_ref[...].astype(BF16), rhs, preferred_element_type=F32)
    yr_ref[0] = y[:DFT_N1].astype(BF16)
    yj_ref[0] = y[DFT_N1:].astype(BF16)


def _dft_stage1(p3, q3, f1):
    bsz, _, lanes = p3.shape
    blk = pl.BlockSpec((1, DFT_N1, F1_LANES), lambda b, i: (b, 0, i))
    shp = jax.ShapeDtypeStruct(p3.shape, BF16)
    return pl.pallas_call(
        _dft1_kernel,
        grid=(bsz, lanes // F1_LANES),
        in_specs=[blk, blk, pl.BlockSpec((2 * DFT_N1, 2 * DFT_N1), lambda b, i: (0, 0))],
        out_specs=[blk, blk],
        out_shape=[shp, shp],
        compiler_params=pltpu.CompilerParams(dimension_semantics=("arbitrary", "arbitrary")),
        name="dft_stage1",
    )(p3, q3, f1)


def _dft2_kernel(yr_ref, yj_ref, m2_ref, o_ref):
    for kk in range(F2_K1):
        rows = slice(kk * DFT_N2, (kk + 1) * DFT_N2)
        rhs = jnp.concatenate([yr_ref[0, rows, :], yj_ref[0, rows, :]], axis=0)
        o = jnp.dot(m2_ref[kk].astype(BF16), rhs, preferred_element_type=F32)
        o_ref[0, :, kk * D_FOURIER:(kk + 1) * D_FOURIER] = o.astype(BF16)


def _dft_stage2(yr, yj, m2):
    bsz = yr.shape[0]
    blk = pl.BlockSpec((1, F2_K1 * DFT_N2, D_FOURIER), lambda b, i: (b, i, 0))
    return pl.pallas_call(
        _dft2_kernel,
        grid=(bsz, DFT_N1 // F2_K1),
        in_specs=[blk, blk, pl.BlockSpec((F2_K1, DFT_N2, 2 * DFT_N2), lambda b, i: (i, 0, 0))],
        out_specs=pl.BlockSpec((1, DFT_N2, F2_K1 * D_FOURIER), lambda b, i: (b, 0, i)),
        out_shape=jax.ShapeDtypeStruct((bsz, DFT_N2, DFT_N1 * D_FOURIER), BF16),
        compiler_params=pltpu.CompilerParams(dimension_semantics=("arbitrary", "arbitrary")),
        name="dft_stage2",
    )(yr, yj, m2)


def _bwd_kernel(x_ref, yf_ref, gfs_ref, gls_ref, v_ref, hf_ref, mod_ref, fg_ref, wfour_ref, wout_ref,
                wg_ref, bg_ref, lam_ref, h0_ref, o_ref,
                g_s, a_s, h_s, yp_s, lhs_s, hc_s):
    j = pl.program_id(1)

    @pl.when(j == 0)
    def _():
        hc_s[...] = jnp.broadcast_to(h0_ref[0, 1:2, :], hc_s.shape)

    yfw = jnp.dot(yf_ref[0], wfour_ref[...], preferred_element_type=F32)
    lhs_s[:, 0:D_FOURIER] = (yfw * gfs_ref[0].astype(F32)).astype(BF16)

    _gate_matmuls(v_ref[0], wg_ref, g_s)
    rate = _decay_rate(lam_ref[1:2, :])
    tot_a, tot_h = _local_scan(g_s, lambda rows: v_ref[0, rows, :].astype(F32), bg_ref, rate,
                               a_s, h_s, CH, reverse=True)
    hin, h_out = _chunk_carries(tot_a, tot_h, hc_s[0:1, :], reverse=True)
    hc_s[...] = jnp.broadcast_to(h_out, hc_s.shape)
    pitch = _pitch(CH)
    for s in range(CH):
        rows = slice(s * MC, (s + 1) * MC)
        yl = h_s[rows, :] + a_s[rows, :] * hin + hf_ref[0, rows, :].astype(F32)
        for q in range(N_QL):
            yp_s[q, pl.ds(s, MC, stride=pitch), :] = yl[:, q * LANES:(q + 1) * LANES]
    for m in range(MC):
        rows = slice(m * CH, (m + 1) * CH)
        for q in range(N_QL):
            cols = slice(q * LANES, (q + 1) * LANES)
            yl = yp_s[q, m * pitch:m * pitch + CH, :]
            lhs_s[rows, D_FOURIER + q * LANES:D_FOURIER + (q + 1) * LANES] = (
                yl * gls_ref[0, rows, cols].astype(F32)).astype(BF16)

    proj = jnp.dot(lhs_s[...], wout_ref[...], preferred_element_type=F32)
    res_gate = mod_ref[0, :, 2 * D_MODEL:3 * D_MODEL]
    fg = fg_ref[...]
    for m in range(MC):
        rows = slice(m * CH, (m + 1) * CH)
        r = x_ref[0, rows, :] + res_gate * proj[rows, :]
        ms = jnp.mean(r * r, axis=-1, keepdims=True)
        o_ref[0, rows, :] = (r * lax.rsqrt(ms + EPS)) * fg


def _bwd_pass(x, yf, gfs, gls, v, hf, mod3, fg, wfour, wout, wgb, bgb, lam, fin):
    bsz = x.shape[0]
    full = lambda *shape: pl.BlockSpec(shape, lambda b, j: (0,) * len(shape))
    tile = lambda c: pl.BlockSpec((1, TILE, c), lambda b, j: (b, N_TILES - 1 - j, 0))
    return pl.pallas_call(
        _bwd_kernel,
        grid=(bsz, N_TILES),
        in_specs=[tile(D_MODEL), tile(D_FOURIER), tile(D_FOURIER), tile(D_LRU), tile(D_LRU),
                  tile(D_LRU),
                  pl.BlockSpec((1, 1, 3 * D_MODEL), lambda b, j: (b, 0, 0)),
                  full(1, D_MODEL), full(D_FOURIER, D_FOURIER), full(D_MODEL, D_MODEL),
                  full(N_HEADS, HEAD_DIM, 2 * HEAD_DIM), full(1, 2 * D_LRU), full(2, D_LRU),
                  pl.BlockSpec((1, 2, D_LRU), lambda b, j: (b, 0, 0))],
        out_specs=tile(D_MODEL),
        out_shape=jax.ShapeDtypeStruct(x.shape, F32),
        scratch_shapes=[pltpu.VMEM((TILE, 2 * D_LRU), F32),
                        pltpu.VMEM((TILE, D_LRU), F32),
                        pltpu.VMEM((TILE, D_LRU), F32),
                        pltpu.VMEM((N_QL, MC * _pitch(CH), LANES), F32),
                        pltpu.VMEM((TILE, D_MODEL), BF16),
                        pltpu.VMEM((8, D_LRU), F32)],
        compiler_params=pltpu.CompilerParams(dimension_semantics=("arbitrary", "arbitrary"),
                                             vmem_limit_bytes=VMEM_LIMIT_BYTES),
        name="bwd_pass",
    )(x, yf, gfs, gls, v, hf, mod3, fg, wfour, wout, wgb, bgb, lam, fin)


def kernel(x, c, ctx, c_ctx, w_ada, b_ada, norm_gain, w_in, w_four, conv_w, conv_b, w_rg, b_rg,
           w_ig, b_ig, lam, w_out, final_gain):
    bsz = x.shape[0]
    assert x.shape == (bsz, SEQ, D_MODEL) and ctx.shape == (bsz, CTX_LEN, D_MODEL)
    assert w_ada.shape[0] == 1, "single-layer kernel"
    dftc, f1, m2 = _dft_constants()

    cc = jnp.concatenate([c, c_ctx[None, :], jnp.zeros((8 - bsz - 1, D_MODEL), F32)], axis=0)
    mod3 = _ada_mod(cc, w_ada[0], b_ada[0][None, :]).reshape(8, 1, 3 * D_MODEL)

    g = norm_gain[0][None, :]
    win = w_in[0].astype(BF16)
    wlx = win[:, 2 * D_FOURIER:2 * D_FOURIER + D_LRU]
    cw, cb = conv_w[0], conv_b[0][None, :]
    wg = jnp.concatenate([w_rg[0], w_ig[0]], axis=-1).astype(BF16)
    bg = jnp.concatenate([b_rg[0], b_ig[0]], axis=-1)[:, None, :]
    lam0 = lam[0]

    fin = _ctx_states(ctx, mod3, g, wlx, cw, cb, wg[0], bg[0], wg[1], bg[1], lam0)

    x4 = x.reshape(bsz, SEQ // 8, 8, D_MODEL)
    p, q, gfs, gls, v, hf = _fwd_pass(x, x4, mod3, g, win, dftc, cw, cb, wg[0], bg[0], lam0, fin)

    lanes = DFT_N2 * D_FOURIER
    yr, yj = _dft_stage1(p.reshape(bsz, DFT_N1, lanes), q.reshape(bsz, DFT_N1, lanes), f1)
    yf = _dft_stage2(yr.reshape(bsz, SEQ, D_FOURIER), yj.reshape(bsz, SEQ, D_FOURIER), m2)
    yf = yf.reshape(bsz, SEQ, D_FOURIER)

    return _bwd_pass(x, yf, gfs, gls, v, hf, mod3, final_gain[None, :], w_four[0].astype(BF16),
                     w_out[0].astype(BF16), wg[1], bg[1], lam0, fin)
```

```python
import numpy as np
import jax
import jax.numpy as jnp
from jax import lax
from jax.experimental import pallas as pl
from jax.experimental.pallas import tpu as pltpu

D_MODEL = 1024
SEQ = 8192
CTX_LEN = 256
D_FOURIER = 512
D_LRU = 512
N_GROUPS = 4
GROUP_DIM = 128
N_HEADS = 4
HEAD_DIM = 128
LRU_C = 8.0
EPS = 1e-6

LANES = 128
MC = 16
CH = 64
TILE = MC * CH
N_TILES = SEQ // TILE
CH_CTX = CTX_LEN // MC
N_QL = D_LRU // LANES

DFT_N1 = 128
DFT_N2 = 64

VMEM_LIMIT_BYTES = 56 * 1024 * 1024

F32 = jnp.float32
BF16 = jnp.bfloat16


def _pitch(ch):
    return ch + 8


def _dft_constants():
    c = np.arange(GROUP_DIM)
    ang = 2.0 * np.pi * ((c[:, None] * c[None, :]) % GROUP_DIM) / GROUP_DIM
    dftc = np.concatenate([np.cos(ang), np.sin(ang)], axis=1) * 2.0 ** -3

    n = np.arange(DFT_N1)
    ang1 = 2.0 * np.pi * ((n[:, None] * n[None, :]) % DFT_N1) / DFT_N1
    c1, s1 = np.cos(ang1), np.sin(ang1)
    f1 = np.block([[c1, -s1], [s1, c1]]) * 2.0 ** -4

    k1 = np.arange(DFT_N1)[:, None, None]
    k2 = np.arange(DFT_N2)[None, :, None]
    n2 = np.arange(DFT_N2)[None, None, :]
    ang2 = 2.0 * np.pi * ((n2 * (k1 + DFT_N1 * k2)) % SEQ) / SEQ
    m2 = np.concatenate([np.cos(ang2), -np.sin(ang2)], axis=2) * 2.0 ** -3
    return jnp.asarray(dftc, dtype=F32), jnp.asarray(f1, dtype=F32), jnp.asarray(m2, dtype=F32)


def _sigmoid(x):
    return 0.5 * jnp.tanh(0.5 * x) + 0.5


def _silu(x):
    hx = 0.5 * x
    return hx * jnp.tanh(hx) + hx


def _decay_rate(lam_row):
    z = -lam_row
    return -LRU_C * (jnp.maximum(z, 0.0) + jnp.log1p(jnp.exp(-jnp.abs(z))))


def _modulate(xs, gmul, shift):
    ms = jnp.mean(xs * xs, axis=-1, keepdims=True)
    return (xs * lax.rsqrt(ms + EPS)) * gmul + shift


def _gate_ab(pre_r, pre_i, v, rate):
    r = _sigmoid(pre_r)
    i = _sigmoid(pre_i)
    a = jnp.exp(r * rate)
    om = 1.0 - a * a
    s = om * lax.rsqrt(jnp.maximum(om, 1e-30))
    return a, s * (i * v)


def _to_slab_order(u, up_s, uext_s, ch):
    pitch = _pitch(ch)
    for m in range(MC):
        for q in range(N_QL):
            up_s[q, m * pitch:m * pitch + ch, :] = u[m * ch:(m + 1) * ch, q * LANES:(q + 1) * LANES]
    for s in range(ch):
        for q in range(N_QL):
            uext_s[(s + 2) * MC:(s + 3) * MC, q * LANES:(q + 1) * LANES] = (
                up_s[q, pl.ds(s, MC, stride=pitch), :])


def _conv_halo(uext_s, lookahead, c_prev1, c_prev2, ch):
    last = uext_s[(ch + 1) * MC:(ch + 2) * MC, :]
    last2 = uext_s[ch * MC:(ch + 1) * MC, :]
    first = uext_s[2 * MC:3 * MC, :]
    uext_s[0:MC, :] = jnp.concatenate([c_prev2, last2[:MC - 1]], axis=0)
    uext_s[MC:2 * MC, :] = jnp.concatenate([c_prev1, last[:MC - 1]], axis=0)
    uext_s[(ch + 2) * MC:(ch + 3) * MC, :] = jnp.concatenate([first[1:], lookahead], axis=0)


def _conv_slab(uext_s, s, cw_ref, cb_ref):
    acc = cb_ref[...] + cw_ref[0:1, :] * uext_s[s * MC:(s + 1) * MC, :]
    acc = acc + cw_ref[1:2, :] * uext_s[(s + 1) * MC:(s + 2) * MC, :]
    acc = acc + cw_ref[2:3, :] * uext_s[(s + 2) * MC:(s + 3) * MC, :]
    acc = acc + cw_ref[3:4, :] * uext_s[(s + 3) * MC:(s + 4) * MC, :]
    return acc


def _gate_matmuls(vb, wg_ref, g_s):
    for h in range(N_HEADS):
        res = jnp.dot(vb[:, h * HEAD_DIM:(h + 1) * HEAD_DIM], wg_ref[h], preferred_element_type=F32)
        g_s[:, h * HEAD_DIM:(h + 1) * HEAD_DIM] = res[:, :HEAD_DIM]
        g_s[:, D_LRU + h * HEAD_DIM:D_LRU + (h + 1) * HEAD_DIM] = res[:, HEAD_DIM:]


def _local_scan(g_s, load_v, bg_ref, rate, a_s, h_s, ch, reverse):
    b_r = bg_ref[:, :D_LRU]
    b_i = bg_ref[:, D_LRU:]

    def body(it, carry):
        acc_a, acc_h = carry
        s = (ch - 1 - it) if reverse else it
        rows = pl.ds(pl.multiple_of(s * MC, MC), MC)
        pre = g_s[rows, :]
        a, bx = _gate_ab(pre[:, :D_LRU] + b_r, pre[:, D_LRU:] + b_i, load_v(rows), rate)
        acc_a = a * acc_a
        acc_h = a * acc_h + bx
        a_s[rows, :] = acc_a
        h_s[rows, :] = acc_h
        return acc_a, acc_h

    init = (jnp.ones((MC, D_LRU), F32), jnp.zeros((MC, D_LRU), F32))
    return lax.fori_loop(0, ch, body, init)


def _chunk_carries(tot_a, tot_h, carry_in, reverse):
    row_id = lax.broadcasted_iota(jnp.int32, (MC, D_LRU), 0)
    hin = jnp.zeros((MC, D_LRU), F32)
    h = carry_in
    order = range(MC - 1, -1, -1) if reverse else range(MC)
    for m in order:
        hin = jnp.where(row_id == m, h, hin)
        h = tot_a[m:m + 1] * h + tot_h[m:m + 1]
    return hin, h


def _ada_kernel(c_ref, w_ref, b_ref, o_ref):
    s = _silu(c_ref[...]).astype(BF16)
    o_ref[...] = jnp.dot(s, w_ref[...].astype(BF16), preferred_element_type=F32) + b_ref[...]


def _ada_mod(cc, w_ada, b_ada):
    nblk = 512
    return pl.pallas_call(
        _ada_kernel,
        grid=(3 * D_MODEL // nblk,),
        in_specs=[pl.BlockSpec((8, D_MODEL), lambda i: (0, 0)),
                  pl.BlockSpec((D_MODEL, nblk), lambda i: (0, i)),
                  pl.BlockSpec((1, nblk), lambda i: (0, i))],
        out_specs=pl.BlockSpec((8, nblk), lambda i: (0, i)),
        out_shape=jax.ShapeDtypeStruct((8, 3 * D_MODEL), F32),
        name="ada_mod",
    )(cc, w_ada, b_ada)


def _ctx_kernel(x_ref, mod_ref, g_ref, wlx_ref, cw_ref, cb_ref, wgf_ref, bgf_ref, wgb_ref, bgb_ref,
                lam_ref, fin_ref, up_s, uext_s, vf_s, g_s, a_s, h_s):
    ch = CH_CTX
    shift = mod_ref[0, :, 0:D_MODEL]
    scale = mod_ref[0, :, D_MODEL:2 * D_MODEL]
    gmul = g_ref[...] * (1.0 + scale)
    lhs = _modulate(x_ref[0], gmul, shift).astype(BF16)
    u = jnp.dot(lhs, wlx_ref[...], preferred_element_type=F32)
    _to_slab_order(u, up_s, uext_s, ch)
    zero_row = jnp.zeros((1, D_LRU), F32)
    _conv_halo(uext_s, zero_row, zero_row, zero_row, ch)
    for s in range(ch):
        vf_s[s * MC:(s + 1) * MC, :] = _conv_slab(uext_s, s, cw_ref, cb_ref)
    vb = vf_s[...].astype(BF16)
    for d, (wg_ref, bg_ref) in enumerate(((wgf_ref, bgf_ref), (wgb_ref, bgb_ref))):
        _gate_matmuls(vb, wg_ref, g_s)
        rate = _decay_rate(lam_ref[d:d + 1, :])
        tot_a, tot_h = _local_scan(g_s, lambda rows: vf_s[rows, :], bg_ref, rate, a_s, h_s, ch,
                                   reverse=(d == 1))
        _, fin = _chunk_carries(tot_a, tot_h, zero_row, reverse=(d == 1))
        fin_ref[0, d:d + 1, :] = fin


def _ctx_states(ctx, mod3, g, wlx, cw, cb, wgf, bgf, wgb, bgb, lam):
    bsz = ctx.shape[0]
    n = CTX_LEN
    full = lambda *shape: pl.BlockSpec(shape, lambda b: (0,) * len(shape))
    return pl.pallas_call(
        _ctx_kernel,
        grid=(bsz,),
        in_specs=[pl.BlockSpec((1, n, D_MODEL), lambda b: (b, 0, 0)),
                  pl.BlockSpec((1, 1, 3 * D_MODEL), lambda b: (bsz, 0, 0)),
                  full(1, D_MODEL), full(D_MODEL, D_LRU), full(4, D_LRU), full(1, D_LRU),
                  full(N_HEADS, HEAD_DIM, 2 * HEAD_DIM), full(1, 2 * D_LRU),
                  full(N_HEADS, HEAD_DIM, 2 * HEAD_DIM), full(1, 2 * D_LRU),
                  full(2, D_LRU)],
        out_specs=pl.BlockSpec((1, 2, D_LRU), lambda b: (b, 0, 0)),
        out_shape=jax.ShapeDtypeStruct((bsz, 2, D_LRU), F32),
        scratch_shapes=[pltpu.VMEM((N_QL, MC * _pitch(CH_CTX), LANES), F32),
                        pltpu.VMEM((n + 3 * MC, D_LRU), F32),
                        pltpu.VMEM((n, D_LRU), F32),
                        pltpu.VMEM((n, 2 * D_LRU), F32),
                        pltpu.VMEM((n, D_LRU), F32),
                        pltpu.VMEM((n, D_LRU), F32)],
        compiler_params=pltpu.CompilerParams(dimension_semantics=("arbitrary",)),
        name="ctx_states",
    )(ctx, mod3, g, wlx, cw, cb, wgf, bgf, wgb, bgb, lam)


def _fwd_kernel(x_ref, xh_ref, mod_ref, g_ref, win_ref, dftc_ref, cw_ref, cb_ref, wg_ref, bg_ref,
                lam_ref, h0_ref,
                p_ref, q_ref, gfs_ref, gls_ref, v_ref, hf_ref,
                lhs_s, up_s, uext_s, vf_s, g_s, a_s, h_s, cc_s, hc_s):
    j = pl.program_id(1)

    @pl.when(j == 0)
    def _():
        cc_s[...] = jnp.zeros_like(cc_s)
        hc_s[...] = jnp.broadcast_to(h0_ref[0, 0:1, :], hc_s.shape)

    shift = mod_ref[0, :, 0:D_MODEL]
    scale = mod_ref[0, :, D_MODEL:2 * D_MODEL]
    gmul = g_ref[...] * (1.0 + scale)
    for m in range(MC):
        lhs_s[m * CH:(m + 1) * CH, :] = _modulate(x_ref[0, m * CH:(m + 1) * CH, :], gmul,
                                                  shift).astype(BF16)
    hh = _modulate(xh_ref[0, 0], gmul, shift)
    lhs_s[TILE:TILE + MC, :] = jnp.concatenate([hh, jnp.zeros_like(hh)], axis=0).astype(BF16)

    zf = jnp.dot(lhs_s[0:TILE, :], win_ref[:, 0:D_FOURIER], preferred_element_type=F32).astype(BF16)
    dftc = dftc_ref[...].astype(BF16)
    for g in range(N_GROUPS):
        cols = slice(g * GROUP_DIM, (g + 1) * GROUP_DIM)
        pq = jnp.dot(zf[:, cols], dftc, preferred_element_type=F32).astype(BF16)
        p_ref[0, :, cols] = pq[:, :GROUP_DIM]
        q_ref[0, :, cols] = pq[:, GROUP_DIM:]

    for cols, o_ref in ((D_FOURIER, gfs_ref), (2 * D_FOURIER + D_LRU, gls_ref)):
        o_ref[0] = _silu(jnp.dot(lhs_s[0:TILE, :], win_ref[:, cols:cols + D_LRU],
                                 preferred_element_type=F32)).astype(BF16)

    u = jnp.dot(lhs_s[...], win_ref[:, 2 * D_FOURIER:2 * D_FOURIER + D_LRU],
                preferred_element_type=F32)
    _to_slab_order(u, up_s, uext_s, CH)
    look = jnp.where(j == N_TILES - 1, 0.0, u[TILE:TILE + 1])
    _conv_halo(uext_s, look, cc_s[0:1, :], cc_s[1:2, :], CH)
    cc_s[0:1, :] = u[TILE - 1:TILE]
    cc_s[1:2, :] = u[TILE - 2:TILE - 1]
    for s in range(CH):
        v = _conv_slab(uext_s, s, cw_ref, cb_ref)
        vf_s[s * MC:(s + 1) * MC, :] = v
        v_ref[0, s * MC:(s + 1) * MC, :] = v.astype(BF16)

    _gate_matmuls(v_ref[0], wg_ref, g_s)
    rate = _decay_rate(lam_ref[0:1, :])
    tot_a, tot_h = _local_scan(g_s, lambda rows: vf_s[rows, :], bg_ref, rate, a_s, h_s, CH,
                               reverse=False)
    hin, h_out = _chunk_carries(tot_a, tot_h, hc_s[0:1, :], reverse=False)
    hc_s[...] = jnp.broadcast_to(h_out, hc_s.shape)
    for s in range(CH):
        rows = slice(s * MC, (s + 1) * MC)
        hf_ref[0, rows, :] = (h_s[rows, :] + a_s[rows, :] * hin).astype(BF16)


def _fwd_pass(x, x4, mod3, g, win, dftc, cw, cb, wgf, bgf, lam, fin):
    bsz = x.shape[0]
    full = lambda *shape: pl.BlockSpec(shape, lambda b, j: (0,) * len(shape))
    tile = lambda c: pl.BlockSpec((1, TILE, c), lambda b, j: (b, j, 0))
    shp = lambda c: jax.ShapeDtypeStruct((bsz, SEQ, c), BF16)
    rows8 = TILE // 8
    return pl.pallas_call(
        _fwd_kernel,
        grid=(bsz, N_TILES),
        in_specs=[tile(D_MODEL),
                  pl.BlockSpec((1, 1, 8, D_MODEL),
                               lambda b, j: (b, jnp.minimum((j + 1) * rows8, SEQ // 8 - 1), 0, 0)),
                  pl.BlockSpec((1, 1, 3 * D_MODEL), lambda b, j: (b, 0, 0)),
                  full(1, D_MODEL), full(D_MODEL, 2 * D_MODEL), full(GROUP_DIM, 2 * GROUP_DIM),
                  full(4, D_LRU), full(1, D_LRU),
                  full(N_HEADS, HEAD_DIM, 2 * HEAD_DIM), full(1, 2 * D_LRU),
                  full(2, D_LRU),
                  pl.BlockSpec((1, 2, D_LRU), lambda b, j: (b, 0, 0))],
        out_specs=[tile(D_FOURIER), tile(D_FOURIER), tile(D_FOURIER), tile(D_LRU), tile(D_LRU),
                   tile(D_LRU)],
        out_shape=[shp(D_FOURIER), shp(D_FOURIER), shp(D_FOURIER), shp(D_LRU), shp(D_LRU),
                   shp(D_LRU)],
        scratch_shapes=[pltpu.VMEM((TILE + MC, D_MODEL), BF16),
                        pltpu.VMEM((N_QL, MC * _pitch(CH), LANES), F32),
                        pltpu.VMEM((TILE + 3 * MC, D_LRU), F32),
                        pltpu.VMEM((TILE, D_LRU), F32),
                        pltpu.VMEM((TILE, 2 * D_LRU), F32),
                        pltpu.VMEM((TILE, D_LRU), F32),
                        pltpu.VMEM((TILE, D_LRU), F32),
                        pltpu.VMEM((8, D_LRU), F32),
                        pltpu.VMEM((8, D_LRU), F32)],
        compiler_params=pltpu.CompilerParams(dimension_semantics=("arbitrary", "arbitrary"),
                                             vmem_limit_bytes=VMEM_LIMIT_BYTES),
        name="fwd_pass",
    )(x, x4, mod3, g, win, dftc, cw, cb, wgf, bgf, lam, fin)


def _dft_kernel(p_ref, q_ref, f1_ref, m2_ref, o_ref, sp_s, sq_s, yr_s, yj_s, so_s):
    p1 = _pitch(DFT_N2)
    p2 = _pitch(DFT_N1)
    for n1 in range(DFT_N1):
        rows = slice(n1 * DFT_N2, (n1 + 1) * DFT_N2)
        sp_s[n1 * p1:n1 * p1 + DFT_N2, :] = p_ref[0, rows, :].astype(F32)
        sq_s[n1 * p1:n1 * p1 + DFT_N2, :] = q_ref[0, rows, :].astype(F32)
    f1 = f1_ref[...].astype(BF16)
    for n2 in range(DFT_N2):
        rhs = jnp.concatenate([sp_s[pl.ds(n2, DFT_N1, stride=p1), :],
                               sq_s[pl.ds(n2, DFT_N1, stride=p1), :]], axis=0).astype(BF16)
        y = jnp.dot(f1, rhs, preferred_element_type=F32)
        yr_s[n2 * p2:n2 * p2 + DFT_N1, :] = y[:DFT_N1]
        yj_s[n2 * p2:n2 * p2 + DFT_N1, :] = y[DFT_N1:]
    for k1 in range(DFT_N1):
        rhs = jnp.concatenate([yr_s[pl.ds(k1, DFT_N2, stride=p2), :],
                               yj_s[pl.ds(k1, DFT_N2, stride=p2), :]], axis=0).astype(BF16)
        so_s[pl.ds(k1, DFT_N2, stride=p2), :] = jnp.dot(m2_ref[k1].astype(BF16), rhs,
                                                         preferred_element_type=F32)
    for k2 in range(DFT_N2):
        o_ref[0, k2 * DFT_N1:(k2 + 1) * DFT_N1, :] = so_s[k2 * p2:k2 * p2 + DFT_N1, :].astype(BF16)


def _position_dft(p, q, f1, m2):
    bsz = p.shape[0]
    blk = pl.BlockSpec((1, SEQ, LANES), lambda b, i: (b, 0, i))
    pad1 = DFT_N1 * _pitch(DFT_N2)
    pad2 = DFT_N2 * _pitch(DFT_N1)
    return pl.pallas_call(
        _dft_kernel,
        grid=(bsz, D_FOURIER // LANES),
        in_specs=[blk, blk,
                  pl.BlockSpec((2 * DFT_N1, 2 * DFT_N1), lambda b, i: (0, 0)),
                  pl.BlockSpec((DFT_N1, DFT_N2, 2 * DFT_N2), lambda b, i: (0, 0, 0))],
        out_specs=blk,
        out_shape=jax.ShapeDtypeStruct(p.shape, BF16),
        scratch_shapes=[pltpu.VMEM((pad1, LANES), F32), pltpu.VMEM((pad1, LANES), F32),
                        pltpu.VMEM((pad2, LANES), F32), pltpu.VMEM((pad2, LANES), F32),
                        pltpu.VMEM((pad2, LANES), F32)],
        compiler_params=pltpu.CompilerParams(dimension_semantics=("arbitrary", "arbitrary"),
                                             vmem_limit_bytes=VMEM_LIMIT_BYTES),
        name="position_dft",
    )(p, q, f1, m2)


def _bwd_kernel(x_ref, yf_ref, gfs_ref, gls_ref, v_ref, hf_ref, mod_ref, fg_ref, wfour_ref, wout_ref,
                wg_ref, bg_ref, lam_ref, h0_ref, o_ref,
                g_s, a_s, h_s, yp_s, lhs_s, hc_s):
    j = pl.program_id(1)

    @pl.when(j == 0)
    def _():
        hc_s[...] = jnp.broadcast_to(h0_ref[0, 1:2, :], hc_s.shape)

    yfw = jnp.dot(yf_ref[0], wfour_ref[...], preferred_element_type=F32)
    lhs_s[:, 0:D_FOURIER] = (yfw * gfs_ref[0].astype(F32)).astype(BF16)

    _gate_matmuls(v_ref[0], wg_ref, g_s)
    rate = _decay_rate(lam_ref[1:2, :])
    tot_a, tot_h = _local_scan(g_s, lambda rows: v_ref[0, rows, :].astype(F32), bg_ref, rate,
                               a_s, h_s, CH, reverse=True)
    hin, h_out = _chunk_carries(tot_a, tot_h, hc_s[0:1, :], reverse=True)
    hc_s[...] = jnp.broadcast_to(h_out, hc_s.shape)
    pitch = _pitch(CH)
    for s in range(CH):
        rows = slice(s * MC, (s + 1) * MC)
        yl = h_s[rows, :] + a_s[rows, :] * hin + hf_ref[0, rows, :].astype(F32)
        for q in range(N_QL):
            yp_s[q, pl.ds(s, MC, stride=pitch), :] = yl[:, q * LANES:(q + 1) * LANES]
    for m in range(MC):
        rows = slice(m * CH, (m + 1) * CH)
        for q in range(N_QL):
            cols = slice(q * LANES, (q + 1) * LANES)
            yl = yp_s[q, m * pitch:m * pitch + CH, :]
            lhs_s[rows, D_FOURIER + q * LANES:D_FOURIER + (q + 1) * LANES] = (
                yl * gls_ref[0, rows, cols].astype(F32)).astype(BF16)

    proj = jnp.dot(lhs_s[...], wout_ref[...], preferred_element_type=F32)
    res_gate = mod_ref[0, :, 2 * D_MODEL:3 * D_MODEL]
    fg = fg_ref[...]
    for m in range(MC):
        rows = slice(m * CH, (m + 1) * CH)
        r = x_ref[0, rows, :] + res_gate * proj[rows, :]
        ms = jnp.mean(r * r, axis=-1, keepdims=True)
        o_ref[0, rows, :] = (r * lax.rsqrt(ms + EPS)) * fg


def _bwd_pass(x, yf, gfs, gls, v, hf, mod3, fg, wfour, wout, wgb, bgb, lam, fin):
    bsz = x.shape[0]
    full = lambda *shape: pl.BlockSpec(shape, lambda b, j: (0,) * len(shape))
    tile = lambda c: pl.BlockSpec((1, TILE, c), lambda b, j: (b, N_TILES - 1 - j, 0))
    return pl.pallas_call(
        _bwd_kernel,
        grid=(bsz, N_TILES),
        in_specs=[tile(D_MODEL), tile(D_FOURIER), tile(D_FOURIER), tile(D_LRU), tile(D_LRU),
                  tile(D_LRU),
                  pl.BlockSpec((1, 1, 3 * D_MODEL), lambda b, j: (b, 0, 0)),
                  full(1, D_MODEL), full(D_FOURIER, D_FOURIER), full(D_MODEL, D_MODEL),
                  full(N_HEADS, HEAD_DIM, 2 * HEAD_DIM), full(1, 2 * D_LRU), full(2, D_LRU),
                  pl.BlockSpec((1, 2, D_LRU), lambda b, j: (b, 0, 0))],
        out_specs=tile(D_MODEL),
        out_shape=jax.ShapeDtypeStruct(x.shape, F32),
        scratch_shapes=[pltpu.VMEM((TILE, 2 * D_LRU), F32),
                        pltpu.VMEM((TILE, D_LRU), F32),
                        pltpu.VMEM((TILE, D_LRU), F32),
                        pltpu.VMEM((N_QL, MC * _pitch(CH), LANES), F32),
                        pltpu.VMEM((TILE, D_MODEL), BF16),
                        pltpu.VMEM((8, D_LRU), F32)],
        compiler_params=pltpu.CompilerParams(dimension_semantics=("arbitrary", "arbitrary"),
                                             vmem_limit_bytes=VMEM_LIMIT_BYTES),
        name="bwd_pass",
    )(x, yf, gfs, gls, v, hf, mod3, fg, wfour, wout, wgb, bgb, lam, fin)


def kernel(x, c, ctx, c_ctx, w_ada, b_ada, norm_gain, w_in, w_four, conv_w, conv_b, w_rg, b_rg,
           w_ig, b_ig, lam, w_out, final_gain):
    bsz = x.shape[0]
    assert x.shape == (bsz, SEQ, D_MODEL) and ctx.shape == (bsz, CTX_LEN, D_MODEL)
    assert w_ada.shape[0] == 1, "single-layer kernel"
    dftc, f1, m2 = _dft_constants()

    cc = jnp.concatenate([c, c_ctx[None, :], jnp.zeros((8 - bsz - 1, D_MODEL), F32)], axis=0)
    mod3 = _ada_mod(cc, w_ada[0], b_ada[0][None, :]).reshape(8, 1, 3 * D_MODEL)

    g = norm_gain[0][None, :]
    win = w_in[0].astype(BF16)
    wlx = win[:, 2 * D_FOURIER:2 * D_FOURIER + D_LRU]
    cw, cb = conv_w[0], conv_b[0][None, :]
    wg = jnp.concatenate([w_rg[0], w_ig[0]], axis=-1).astype(BF16)
    bg = jnp.concatenate([b_rg[0], b_ig[0]], axis=-1)[:, None, :]
    lam0 = lam[0]

    fin = _ctx_states(ctx, mod3, g, wlx, cw, cb, wg[0], bg[0], wg[1], bg[1], lam0)

    x4 = x.reshape(bsz, SEQ // 8, 8, D_MODEL)
    p, q, gfs, gls, v, hf = _fwd_pass(x, x4, mod3, g, win, dftc, cw, cb, wg[0], bg[0], lam0, fin)

    yf = _position_dft(p, q, f1, m2)

    return _bwd_pass(x, yf, gfs, gls, v, hf, mod3, final_gain[None, :], w_four[0].astype(BF16),
                     w_out[0].astype(BF16), wg[1], bg[1], lam0, fin)
```

```python
import numpy as np
import jax
import jax.numpy as jnp
from jax import lax
from jax.experimental import pallas as pl
from jax.experimental.pallas import tpu as pltpu

D_MODEL = 1024
SEQ = 8192
CTX_LEN = 256
D_FOURIER = 512
D_LRU = 512
N_GROUPS = 4
GROUP_DIM = 128
N_HEADS = 4
HEAD_DIM = 128
LRU_C = 8.0
EPS = 1e-6

LANES = 128
MC = 16
CH = 64
TILE = MC * CH
N_TILES = SEQ // TILE
CH_CTX = CTX_LEN // MC
N_QL = D_LRU // LANES
RB = TILE
N_RB = TILE // RB
SPB = RB // MC

DFT_N1 = 128
DFT_N2 = 64

VMEM_LIMIT_BYTES = 56 * 1024 * 1024
F32 = jnp.float32
BF16 = jnp.bfloat16


def _pitch(ch):
    return ch + 8


def _dft_constants():
    c = np.arange(GROUP_DIM)
    ang = 2.0 * np.pi * ((c[:, None] * c[None, :]) % GROUP_DIM) / GROUP_DIM
    dftc = np.concatenate([np.cos(ang), np.sin(ang)], axis=1) * 2.0 ** -3

    n = np.arange(DFT_N1)
    ang1 = 2.0 * np.pi * ((n[:, None] * n[None, :]) % DFT_N1) / DFT_N1
    c1, s1 = np.cos(ang1), np.sin(ang1)
    f1 = np.block([[c1, -s1], [s1, c1]]) * 2.0 ** -4

    k1 = np.arange(DFT_N1)[:, None, None]
    k2 = np.arange(DFT_N2)[None, :, None]
    n2 = np.arange(DFT_N2)[None, None, :]
    ang2 = 2.0 * np.pi * ((n2 * (k1 + DFT_N1 * k2)) % SEQ) / SEQ
    m2 = np.concatenate([np.cos(ang2), -np.sin(ang2)], axis=2) * 2.0 ** -3
    return jnp.asarray(dftc, dtype=F32), jnp.asarray(f1, dtype=F32), jnp.asarray(m2, dtype=F32)


def _sigmoid(x):
    return 0.5 * jnp.tanh(0.5 * x) + 0.5


def _silu(x):
    hx = 0.5 * x
    return hx * jnp.tanh(hx) + hx


def _decay_rate(lam_row):
    z = -lam_row
    return -LRU_C * (jnp.maximum(z, 0.0) + jnp.log1p(jnp.exp(-jnp.abs(z))))


def _modulate(xs, gmul, shift):
    ms = jnp.mean(xs * xs, axis=-1, keepdims=True)
    return (xs * lax.rsqrt(ms + EPS)) * gmul + shift


def _gate_ab(pre_r, pre_i, v, rate):
    r = _sigmoid(pre_r)
    i = _sigmoid(pre_i)
    a = jnp.exp(r * rate)
    om = 1.0 - a * a
    s = om * lax.rsqrt(jnp.maximum(om, 1e-30))
    return a, s * (i * v)


def _ds(start, size):
    if isinstance(start, int):
        return slice(start, start + size)
    return pl.ds(pl.multiple_of(start, MC), size)


def _to_slab_order(u, up_s, uext_s, ch):
    pitch = _pitch(ch)
    for m in range(MC):
        for q in range(N_QL):
            up_s[q, m * pitch:m * pitch + ch, :] = u[m * ch:(m + 1) * ch, q * LANES:(q + 1) * LANES]
    for s in range(ch):
        for q in range(N_QL):
            uext_s[(s + 2) * MC:(s + 3) * MC, q * LANES:(q + 1) * LANES] = (
                up_s[q, pl.ds(s, MC, stride=pitch), :])


def _conv_halo(uext_s, lookahead, c_prev1, c_prev2, ch):
    last = uext_s[(ch + 1) * MC:(ch + 2) * MC, :]
    last2 = uext_s[ch * MC:(ch + 1) * MC, :]
    first = uext_s[2 * MC:3 * MC, :]
    uext_s[0:MC, :] = jnp.concatenate([c_prev2, last2[:MC - 1]], axis=0)
    uext_s[MC:2 * MC, :] = jnp.concatenate([c_prev1, last[:MC - 1]], axis=0)
    uext_s[(ch + 2) * MC:(ch + 3) * MC, :] = jnp.concatenate([first[1:], lookahead], axis=0)


def _conv_slab(uext_s, s, cw_ref, cb_ref):
    acc = cb_ref[...] + cw_ref[0:1, :] * uext_s[_ds(s * MC, MC), :]
    acc = acc + cw_ref[1:2, :] * uext_s[_ds((s + 1) * MC, MC), :]
    acc = acc + cw_ref[2:3, :] * uext_s[_ds((s + 2) * MC, MC), :]
    acc = acc + cw_ref[3:4, :] * uext_s[_ds((s + 3) * MC, MC), :]
    return acc


def _gate_matmuls(vb, wg_ref, g_s, rows):
    for h in range(N_HEADS):
        res = jnp.dot(vb[:, h * HEAD_DIM:(h + 1) * HEAD_DIM], wg_ref[h], preferred_element_type=F32)
        g_s[rows, h * HEAD_DIM:(h + 1) * HEAD_DIM] = res[:, :HEAD_DIM]
        g_s[rows, D_LRU + h * HEAD_DIM:D_LRU + (h + 1) * HEAD_DIM] = res[:, HEAD_DIM:]


def _scan_init():
    return jnp.ones((MC, D_LRU), F32), jnp.zeros((MC, D_LRU), F32)


def _scan_steps(g_s, load_v, bg_ref, rate, a_s, h_s, steps, carry):
    b_r = bg_ref[:, :D_LRU]
    b_i = bg_ref[:, D_LRU:]
    acc_a, acc_h = carry
    for s in steps:
        rows = _ds(s * MC, MC)
        pre = g_s[rows, :]
        a, bx = _gate_ab(pre[:, :D_LRU] + b_r, pre[:, D_LRU:] + b_i, load_v(rows), rate)
        acc_a = a * acc_a
        acc_h = a * acc_h + bx
        a_s[rows, :] = acc_a
        h_s[rows, :] = acc_h
    return acc_a, acc_h


def _chunk_carries(tot_a, tot_h, carry_in, reverse):
    row_id = lax.broadcasted_iota(jnp.int32, (MC, D_LRU), 0)
    hin = jnp.zeros((MC, D_LRU), F32)
    h = carry_in
    order = range(MC - 1, -1, -1) if reverse else range(MC)
    for m in order:
        hin = jnp.where(row_id == m, h, hin)
        h = tot_a[m:m + 1] * h + tot_h[m:m + 1]
    return hin, h


def _ada_kernel(c_ref, w_ref, b_ref, o_ref):
    s = _silu(c_ref[...]).astype(BF16)
    o_ref[...] = jnp.dot(s, w_ref[...].astype(BF16), preferred_element_type=F32) + b_ref[...]


def _ada_mod(cc, w_ada, b_ada):
    nblk = 512
    return pl.pallas_call(
        _ada_kernel,
        grid=(3 * D_MODEL // nblk,),
        in_specs=[pl.BlockSpec((8, D_MODEL), lambda i: (0, 0)),
                  pl.BlockSpec((D_MODEL, nblk), lambda i: (0, i)),
                  pl.BlockSpec((1, nblk), lambda i: (0, i))],
        out_specs=pl.BlockSpec((8, nblk), lambda i: (0, i)),
        out_shape=jax.ShapeDtypeStruct((8, 3 * D_MODEL), F32),
        name="ada_mod",
    )(cc, w_ada, b_ada)


def _ctx_kernel(x_ref, mod_ref, g_ref, wlx_ref, cw_ref, cb_ref, wgf_ref, bgf_ref, wgb_ref, bgb_ref,
                lam_ref, fin_ref, up_s, uext_s, vf_s, g_s, a_s, h_s):
    ch = CH_CTX
    shift = mod_ref[0, :, 0:D_MODEL]
    scale = mod_ref[0, :, D_MODEL:2 * D_MODEL]
    gmul = g_ref[...] * (1.0 + scale)
    lhs = _modulate(x_ref[0], gmul, shift).astype(BF16)
    u = jnp.dot(lhs, wlx_ref[...], preferred_element_type=F32)
    _to_slab_order(u, up_s, uext_s, ch)
    zero_row = jnp.zeros((1, D_LRU), F32)
    _conv_halo(uext_s, zero_row, zero_row, zero_row, ch)
    for s in range(ch):
        vf_s[s * MC:(s + 1) * MC, :] = _conv_slab(uext_s, s, cw_ref, cb_ref)
    vb = vf_s[...].astype(BF16)
    for d, (wg_ref, bg_ref) in enumerate(((wgf_ref, bgf_ref), (wgb_ref, bgb_ref))):
        _gate_matmuls(vb, wg_ref, g_s, slice(0, ch * MC))
        rate = _decay_rate(lam_ref[d:d + 1, :])
        steps = range(ch - 1, -1, -1) if d == 1 else range(ch)
        tot_a, tot_h = _scan_steps(g_s, lambda rows: vf_s[rows, :], bg_ref, rate, a_s, h_s, steps,
                                   _scan_init())
        _, fin = _chunk_carries(tot_a, tot_h, zero_row, reverse=(d == 1))
        fin_ref[0, d:d + 1, :] = fin


def _ctx_states(ctx, mod3, g, wlx, cw, cb, wgf, bgf, wgb, bgb, lam):
    bsz = ctx.shape[0]
    n = CTX_LEN
    full = lambda *shape: pl.BlockSpec(shape, lambda b: (0,) * len(shape))
    return pl.pallas_call(
        _ctx_kernel,
        grid=(bsz,),
        in_specs=[pl.BlockSpec((1, n, D_MODEL), lambda b: (b, 0, 0)),
                  pl.BlockSpec((1, 1, 3 * D_MODEL), lambda b: (bsz, 0, 0)),
                  full(1, D_MODEL), full(D_MODEL, D_LRU), full(4, D_LRU), full(1, D_LRU),
                  full(N_HEADS, HEAD_DIM, 2 * HEAD_DIM), full(1, 2 * D_LRU),
                  full(N_HEADS, HEAD_DIM, 2 * HEAD_DIM), full(1, 2 * D_LRU),
                  full(2, D_LRU)],
        out_specs=pl.BlockSpec((1, 2, D_LRU), lambda b: (b, 0, 0)),
        out_shape=jax.ShapeDtypeStruct((bsz, 2, D_LRU), F32),
        scratch_shapes=[pltpu.VMEM((N_QL, MC * _pitch(CH_CTX), LANES), F32),
                        pltpu.VMEM((n + 3 * MC, D_LRU), F32),
                        pltpu.VMEM((n, D_LRU), F32),
                        pltpu.VMEM((n, 2 * D_LRU), F32),
                        pltpu.VMEM((n, D_LRU), F32),
                        pltpu.VMEM((n, D_LRU), F32)],
        compiler_params=pltpu.CompilerParams(dimension_semantics=("arbitrary",)),
        name="ctx_states",
    )(ctx, mod3, g, wlx, cw, cb, wgf, bgf, wgb, bgb, lam)


def _fwd_kernel(x_ref, xh_ref, mod_ref, g_ref, win_ref, dftc_ref, cw_ref, cb_ref, wg_ref, bg_ref,
                lam_ref, h0_ref,
                p_ref, q_ref, gfs_ref, gls_ref, v_ref, hf_ref,
                lhs_s, up_s, uext_s, vf_s, g_s, a_s, h_s, cc_s, hc_s):
    j = pl.program_id(1)

    @pl.when(j == 0)
    def _():
        cc_s[...] = jnp.zeros_like(cc_s)
        hc_s[...] = jnp.broadcast_to(h0_ref[0, 0:1, :], hc_s.shape)

    shift = mod_ref[0, :, 0:D_MODEL]
    scale = mod_ref[0, :, D_MODEL:2 * D_MODEL]
    gmul = g_ref[...] * (1.0 + scale)
    for m in range(MC):
        lhs_s[m * CH:(m + 1) * CH, :] = _modulate(x_ref[0, m * CH:(m + 1) * CH, :], gmul,
                                                  shift).astype(BF16)
    hh = _modulate(xh_ref[0, 0], gmul, shift)
    lhs_s[TILE:TILE + MC, :] = jnp.concatenate([hh, jnp.zeros_like(hh)], axis=0).astype(BF16)

    u = jnp.dot(lhs_s[...], win_ref[:, 2 * D_FOURIER:2 * D_FOURIER + D_LRU],
                preferred_element_type=F32)

    _to_slab_order(u, up_s, uext_s, CH)
    look = jnp.where(j == N_TILES - 1, 0.0, u[TILE:TILE + 1])
    _conv_halo(uext_s, look, cc_s[0:1, :], cc_s[1:2, :], CH)
    cc_s[0:1, :] = u[TILE - 1:TILE]
    cc_s[1:2, :] = u[TILE - 2:TILE - 1]

    dftc = dftc_ref[...].astype(BF16)
    rate = _decay_rate(lam_ref[0:1, :])
    def fourier_block(r):
        rows = _ds(r * RB, RB)
        zf = jnp.dot(lhs_s[rows, :], win_ref[:, 0:D_FOURIER], preferred_element_type=F32).astype(BF16)
        for g in range(N_GROUPS):
            cols = slice(g * GROUP_DIM, (g + 1) * GROUP_DIM)
            pq = jnp.dot(zf[:, cols], dftc, preferred_element_type=F32).astype(BF16)
            p_ref[0, rows, cols] = pq[:, :GROUP_DIM]
            q_ref[0, rows, cols] = pq[:, GROUP_DIM:]

    def silu_block(r, c0, o_ref):
        rows = _ds(r * RB, RB)
        o_ref[0, rows, :] = _silu(jnp.dot(lhs_s[rows, :], win_ref[:, c0:c0 + D_LRU],
                                          preferred_element_type=F32)).astype(BF16)

    def conv_block(r):
        for i in range(SPB):
            rows = _ds((r * SPB + i) * MC, MC)
            v = _conv_slab(uext_s, r * SPB + i, cw_ref, cb_ref)
            vf_s[rows, :] = v
            v_ref[0, rows, :] = v.astype(BF16)

    def lru_block(r):
        rows = _ds(r * RB, RB)
        _gate_matmuls(v_ref[0, rows, :], wg_ref, g_s, rows)
        silu_block(r, D_FOURIER, gfs_ref)
        silu_block(r, 2 * D_FOURIER + D_LRU, gls_ref)

    def scan_block(r, carry):
        return _scan_steps(g_s, lambda rws: vf_s[rws, :], bg_ref, rate, a_s, h_s,
                           [r * SPB + i for i in range(SPB)], carry)

    def stage(r, carry):
        lru_block(r)
        fourier_block(r + 1)
        conv_block(r + 1)
        return scan_block(r, carry)

    fourier_block(0)
    conv_block(0)
    carry = lax.fori_loop(0, N_RB - 1, stage, _scan_init())
    lru_block(N_RB - 1)
    carry = scan_block(N_RB - 1, carry)

    tot_a, tot_h = carry
    hin, h_out = _chunk_carries(tot_a, tot_h, hc_s[0:1, :], reverse=False)
    hc_s[...] = jnp.broadcast_to(h_out, hc_s.shape)
    for s in range(CH):
        rows = slice(s * MC, (s + 1) * MC)
        hf_ref[0, rows, :] = (h_s[rows, :] + a_s[rows, :] * hin).astype(BF16)


def _fwd_pass(x, x4, mod3, g, win, dftc, cw, cb, wgf, bgf, lam, fin):
    bsz = x.shape[0]
    full = lambda *shape: pl.BlockSpec(shape, lambda b, j: (0,) * len(shape))
    tile = lambda c: pl.BlockSpec((1, TILE, c), lambda b, j: (b, j, 0))
    shp = lambda c: jax.ShapeDtypeStruct((bsz, SEQ, c), BF16)
    rows8 = TILE // 8
    return pl.pallas_call(
        _fwd_kernel,
        grid=(bsz, N_TILES),
        in_specs=[tile(D_MODEL),
                  pl.BlockSpec((1, 1, 8, D_MODEL),
                               lambda b, j: (b, jnp.minimum((j + 1) * rows8, SEQ // 8 - 1), 0, 0)),
                  pl.BlockSpec((1, 1, 3 * D_MODEL), lambda b, j: (b, 0, 0)),
                  full(1, D_MODEL), full(D_MODEL, 2 * D_MODEL), full(GROUP_DIM, 2 * GROUP_DIM),
                  full(4, D_LRU), full(1, D_LRU),
                  full(N_HEADS, HEAD_DIM, 2 * HEAD_DIM), full(1, 2 * D_LRU),
                  full(2, D_LRU),
                  pl.BlockSpec((1, 2, D_LRU), lambda b, j: (b, 0, 0))],
        out_specs=[tile(D_FOURIER), tile(D_FOURIER), tile(D_FOURIER), tile(D_LRU), tile(D_LRU),
                   tile(D_LRU)],
        out_shape=[shp(D_FOURIER), shp(D_FOURIER), shp(D_FOURIER), shp(D_LRU), shp(D_LRU),
                   shp(D_LRU)],
        scratch_shapes=[pltpu.VMEM((TILE + MC, D_MODEL), BF16),
                        pltpu.VMEM((N_QL, MC * _pitch(CH), LANES), F32),
                        pltpu.VMEM((TILE + 3 * MC, D_LRU), F32),
                        pltpu.VMEM((TILE, D_LRU), F32),
                        pltpu.VMEM((TILE, 2 * D_LRU), F32),
                        pltpu.VMEM((TILE, D_LRU), F32),
                        pltpu.VMEM((TILE, D_LRU), F32),
                        pltpu.VMEM((8, D_LRU), F32),
                        pltpu.VMEM((8, D_LRU), F32)],
        compiler_params=pltpu.CompilerParams(dimension_semantics=("arbitrary", "arbitrary"),
                                             vmem_limit_bytes=VMEM_LIMIT_BYTES),
        name="fwd_pass",
    )(x, x4, mod3, g, win, dftc, cw, cb, wgf, bgf, lam, fin)


def _dft_kernel(p_ref, q_ref, f1_ref, m2_ref, o_ref, sp_s, sq_s, yr_s, yj_s, so_s):
    p1 = _pitch(DFT_N2)
    p2 = _pitch(DFT_N1)
    for n1 in range(DFT_N1):
        rows = slice(n1 * DFT_N2, (n1 + 1) * DFT_N2)
        sp_s[n1 * p1:n1 * p1 + DFT_N2, :] = p_ref[0, rows, :].astype(F32)
        sq_s[n1 * p1:n1 * p1 + DFT_N2, :] = q_ref[0, rows, :].astype(F32)
    f1 = f1_ref[...].astype(BF16)
    for n2 in range(DFT_N2):
        rhs = jnp.concatenate([sp_s[pl.ds(n2, DFT_N1, stride=p1), :],
                               sq_s[pl.ds(n2, DFT_N1, stride=p1), :]], axis=0).astype(BF16)
        y = jnp.dot(f1, rhs, preferred_element_type=F32)
        yr_s[n2 * p2:n2 * p2 + DFT_N1, :] = y[:DFT_N1]
        yj_s[n2 * p2:n2 * p2 + DFT_N1, :] = y[DFT_N1:]
    for k1 in range(DFT_N1):
        rhs = jnp.concatenate([yr_s[pl.ds(k1, DFT_N2, stride=p2), :],
                               yj_s[pl.ds(k1, DFT_N2, stride=p2), :]], axis=0).astype(BF16)
        so_s[pl.ds(k1, DFT_N2, stride=p2), :] = jnp.dot(m2_ref[k1].astype(BF16), rhs,
                                                         preferred_element_type=F32)
    for k2 in range(DFT_N2):
        o_ref[0, k2 * DFT_N1:(k2 + 1) * DFT_N1, :] = so_s[k2 * p2:k2 * p2 + DFT_N1, :].astype(BF16)


def _position_dft(p, q, f1, m2):
    bsz = p.shape[0]
    blk = pl.BlockSpec((1, SEQ, LANES), lambda b, i: (b, 0, i))
    pad1 = DFT_N1 * _pitch(DFT_N2)
    pad2 = DFT_N2 * _pitch(DFT_N1)
    return pl.pallas_call(
        _dft_kernel,
        grid=(bsz, D_FOURIER // LANES),
        in_specs=[blk, blk,
                  pl.BlockSpec((2 * DFT_N1, 2 * DFT_N1), lambda b, i: (0, 0)),
                  pl.BlockSpec((DFT_N1, DFT_N2, 2 * DFT_N2), lambda b, i: (0, 0, 0))],
        out_specs=blk,
        out_shape=jax.ShapeDtypeStruct(p.shape, BF16),
        scratch_shapes=[pltpu.VMEM((pad1, LANES), F32), pltpu.VMEM((pad1, LANES), F32),
                        pltpu.VMEM((pad2, LANES), F32), pltpu.VMEM((pad2, LANES), F32),
                        pltpu.VMEM((pad2, LANES), F32)],
        compiler_params=pltpu.CompilerParams(dimension_semantics=("arbitrary", "arbitrary"),
                                             vmem_limit_bytes=VMEM_LIMIT_BYTES),
        name="position_dft",
    )(p, q, f1, m2)


def _bwd_kernel(x_ref, yf_ref, gfs_ref, gls_ref, v_ref, hf_ref, mod_ref, fg_ref, wfour_ref, wout_ref,
                wg_ref, bg_ref, lam_ref, h0_ref, o_ref,
                g_s, a_s, h_s, yp_s, lhs_s, hc_s):
    j = pl.program_id(1)

    @pl.when(j == 0)
    def _():
        hc_s[...] = jnp.broadcast_to(h0_ref[0, 1:2, :], hc_s.shape)

    rate = _decay_rate(lam_ref[1:2, :])
    carry = _scan_init()
    for r in range(N_RB - 1, -1, -1):
        rows = slice(r * RB, (r + 1) * RB)
        _gate_matmuls(v_ref[0, rows, :], wg_ref, g_s, rows)
        yfw = jnp.dot(yf_ref[0, rows, :], wfour_ref[...], preferred_element_type=F32)
        lhs_s[rows, 0:D_FOURIER] = (yfw * gfs_ref[0, rows, :].astype(F32)).astype(BF16)
        carry = _scan_steps(g_s, lambda rws: v_ref[0, rws, :].astype(F32), bg_ref, rate, a_s, h_s,
                            range((r + 1) * SPB - 1, r * SPB - 1, -1), carry)
    tot_a, tot_h = carry
    hin, h_out = _chunk_carries(tot_a, tot_h, hc_s[0:1, :], reverse=True)
    hc_s[...] = jnp.broadcast_to(h_out, hc_s.shape)
    pitch = _pitch(CH)
    for s in range(CH):
        rows = slice(s * MC, (s + 1) * MC)
        yl = h_s[rows, :] + a_s[rows, :] * hin + hf_ref[0, rows, :].astype(F32)
        for q in range(N_QL):
            yp_s[q, pl.ds(s, MC, stride=pitch), :] = yl[:, q * LANES:(q + 1) * LANES]
    for m in range(MC):
        rows = slice(m * CH, (m + 1) * CH)
        for q in range(N_QL):
            cols = slice(q * LANES, (q + 1) * LANES)
            yl = yp_s[q, m * pitch:m * pitch + CH, :]
            lhs_s[rows, D_FOURIER + q * LANES:D_FOURIER + (q + 1) * LANES] = (
                yl * gls_ref[0, rows, cols].astype(F32)).astype(BF16)

    res_gate = mod_ref[0, :, 2 * D_MODEL:3 * D_MODEL]
    fg = fg_ref[...]
    for r in range(N_RB):
        proj = jnp.dot(lhs_s[r * RB:(r + 1) * RB, :], wout_ref[...], preferred_element_type=F32)
        for m in range(RB // CH):
            rows = slice(r * RB + m * CH, r * RB + (m + 1) * CH)
            res = x_ref[0, rows, :] + res_gate * proj[m * CH:(m + 1) * CH, :]
            ms = jnp.mean(res * res, axis=-1, keepdims=True)
            o_ref[0, rows, :] = (res * lax.rsqrt(ms + EPS)) * fg


def _bwd_pass(x, yf, gfs, gls, v, hf, mod3, fg, wfour, wout, wgb, bgb, lam, fin):
    bsz = x.shape[0]
    full = lambda *shape: pl.BlockSpec(shape, lambda b, j: (0,) * len(shape))
    tile = lambda c: pl.BlockSpec((1, TILE, c), lambda b, j: (b, N_TILES - 1 - j, 0))
    return pl.pallas_call(
        _bwd_kernel,
        grid=(bsz, N_TILES),
        in_specs=[tile(D_MODEL), tile(D_FOURIER), tile(D_FOURIER), tile(D_LRU), tile(D_LRU),
                  tile(D_LRU),
                  pl.BlockSpec((1, 1, 3 * D_MODEL), lambda b, j: (b, 0, 0)),
                  full(1, D_MODEL), full(D_FOURIER, D_FOURIER), full(D_MODEL, D_MODEL),
                  full(N_HEADS, HEAD_DIM, 2 * HEAD_DIM), full(1, 2 * D_LRU), full(2, D_LRU),
                  pl.BlockSpec((1, 2, D_LRU), lambda b, j: (b, 0, 0))],
        out_specs=tile(D_MODEL),
        out_shape=jax.ShapeDtypeStruct(x.shape, F32),
        scratch_shapes=[pltpu.VMEM((TILE, 2 * D_LRU), F32),
                        pltpu.VMEM((TILE, D_LRU), F32),
                        pltpu.VMEM((TILE, D_LRU), F32),
                        pltpu.VMEM((N_QL, MC * _pitch(CH), LANES), F32),
                        pltpu.VMEM((TILE, D_MODEL), BF16),
                        pltpu.VMEM((8, D_LRU), F32)],
        compiler_params=pltpu.CompilerParams(dimension_semantics=("arbitrary", "arbitrary"),
                                             vmem_limit_bytes=VMEM_LIMIT_BYTES),
        name="bwd_pass",
    )(x, yf, gfs, gls, v, hf, mod3, fg, wfour, wout, wgb, bgb, lam, fin)


def kernel(x, c, ctx, c_ctx, w_ada, b_ada, norm_gain, w_in, w_four, conv_w, conv_b, w_rg, b_rg,
           w_ig, b_ig, lam, w_out, final_gain):
    bsz = x.shape[0]
    assert x.shape == (bsz, SEQ, D_MODEL) and ctx.shape == (bsz, CTX_LEN, D_MODEL)
    assert w_ada.shape[0] == 1, "single-layer kernel"
    dftc, f1, m2 = _dft_constants()

    cc = jnp.concatenate([c, c_ctx[None, :], jnp.zeros((8 - bsz - 1, D_MODEL), F32)], axis=0)
    mod3 = _ada_mod(cc, w_ada[0], b_ada[0][None, :]).reshape(8, 1, 3 * D_MODEL)

    g = norm_gain[0][None, :]
    win = w_in[0].astype(BF16)
    wlx = win[:, 2 * D_FOURIER:2 * D_FOURIER + D_LRU]
    cw, cb = conv_w[0], conv_b[0][None, :]
    wg = jnp.concatenate([w_rg[0], w_ig[0]], axis=-1).astype(BF16)
    bg = jnp.concatenate([b_rg[0], b_ig[0]], axis=-1)[:, None, :]
    lam0 = lam[0]

    fin = _ctx_states(ctx, mod3, g, wlx, cw, cb, wg[0], bg[0], wg[1], bg[1], lam0)

    x4 = x.reshape(bsz, SEQ // 8, 8, D_MODEL)
    p, q, gfs, gls, v, hf = _fwd_pass(x, x4, mod3, g, win, dftc, cw, cb, wg[0], bg[0], lam0, fin)

    yf = _position_dft(p, q, f1, m2)

    return _bwd_pass(x, yf, gfs, gls, v, hf, mod3, final_gain[None, :], w_four[0].astype(BF16),
                     w_out[0].astype(BF16), wg[1], bg[1], lam0, fin)
```

```python
import numpy as np
import jax
import jax.numpy as jnp
from jax import lax
from jax.experimental import pallas as pl
from jax.experimental.pallas import tpu as pltpu

D_MODEL = 1024
SEQ = 8192
CTX_LEN = 256
D_FOURIER = 512
D_LRU = 512
N_GROUPS = 4
GROUP_DIM = 128
N_HEADS = 4
HEAD_DIM = 128
LRU_C = 8.0
EPS = 1e-6

LANES = 128
MC = 16
CH = 64
TILE = MC * CH
N_TILES = SEQ // TILE
CH_CTX = CTX_LEN // MC
N_QL = D_LRU // LANES
RB = TILE
N_RB = TILE // RB
SPB = RB // MC

DFT_N1 = 128
DFT_N2 = 64

VMEM_LIMIT_BYTES = 56 * 1024 * 1024
F32 = jnp.float32
BF16 = jnp.bfloat16


def _pitch(ch):
    return ch + 8


def _dft_constants():
    c = np.arange(GROUP_DIM)
    ang = 2.0 * np.pi * ((c[:, None] * c[None, :]) % GROUP_DIM) / GROUP_DIM
    dftc = np.concatenate([np.cos(ang), np.sin(ang)], axis=1) * 2.0 ** -3

    n = np.arange(DFT_N1)
    ang1 = 2.0 * np.pi * ((n[:, None] * n[None, :]) % DFT_N1) / DFT_N1
    c1, s1 = np.cos(ang1), np.sin(ang1)
    f1 = np.block([[c1, -s1], [s1, c1]]) * 2.0 ** -4

    k1 = np.arange(DFT_N1)[:, None, None]
    k2 = np.arange(DFT_N2)[None, :, None]
    n2 = np.arange(DFT_N2)[None, None, :]
    ang2 = 2.0 * np.pi * ((n2 * (k1 + DFT_N1 * k2)) % SEQ) / SEQ
    m2 = np.concatenate([np.cos(ang2), -np.sin(ang2)], axis=2) * 2.0 ** -3
    return jnp.asarray(dftc, dtype=F32), jnp.asarray(f1, dtype=F32), jnp.asarray(m2, dtype=F32)


def _silu(x):
    hx = 0.5 * x
    return hx * jnp.tanh(hx) + hx


def _silu_of_half(hx):
    return hx * jnp.tanh(hx) + hx


def _half_decay_rate(lam_row):
    z = -lam_row
    return (-0.5 * LRU_C) * (jnp.maximum(z, 0.0) + jnp.log1p(jnp.exp(-jnp.abs(z))))


def _modulate(xs, gmul, shift):
    ms = jnp.mean(xs * xs, axis=-1, keepdims=True)
    return (xs * lax.rsqrt(ms + EPS)) * gmul + shift


def _gate_ab(hpre_r, hpre_i, hv, hrate):
    a = jnp.exp(jnp.tanh(hpre_r) * hrate + hrate)
    om = 1.0 - a * a
    s = om * lax.rsqrt(jnp.maximum(om, 1e-30))
    return a, s * ((jnp.tanh(hpre_i) + 1.0) * hv)


def _ds(start, size):
    if isinstance(start, int):
        return slice(start, start + size)
    return pl.ds(pl.multiple_of(start, MC), size)


def _to_slab_order(u, up_s, uext_s, ch):
    pitch = _pitch(ch)
    for m in range(MC):
        for q in range(N_QL):
            up_s[q, m * pitch:m * pitch + ch, :] = u[m * ch:(m + 1) * ch, q * LANES:(q + 1) * LANES]
    for s in range(ch):
        for q in range(N_QL):
            uext_s[(s + 2) * MC:(s + 3) * MC, q * LANES:(q + 1) * LANES] = (
                up_s[q, pl.ds(s, MC, stride=pitch), :])


def _conv_halo(uext_s, lookahead, c_prev1, c_prev2, ch):
    last = uext_s[(ch + 1) * MC:(ch + 2) * MC, :]
    last2 = uext_s[ch * MC:(ch + 1) * MC, :]
    first = uext_s[2 * MC:3 * MC, :]
    uext_s[0:MC, :] = jnp.concatenate([c_prev2, last2[:MC - 1]], axis=0)
    uext_s[MC:2 * MC, :] = jnp.concatenate([c_prev1, last[:MC - 1]], axis=0)
    uext_s[(ch + 2) * MC:(ch + 3) * MC, :] = jnp.concatenate([first[1:], lookahead], axis=0)


def _conv_slab(uext_s, s, cw_ref, cb_ref):
    acc = cb_ref[...]
    for k in range(4):
        acc = acc + cw_ref[k] * uext_s[_ds((s + k) * MC, MC), :]
    return acc


def _gate_matmuls(vb, wg_ref, g_s, rows):
    for h in range(N_HEADS):
        res = jnp.dot(vb[:, h * HEAD_DIM:(h + 1) * HEAD_DIM], wg_ref[h], preferred_element_type=F32)
        g_s[rows, h * HEAD_DIM:(h + 1) * HEAD_DIM] = res[:, :HEAD_DIM]
        g_s[rows, D_LRU + h * HEAD_DIM:D_LRU + (h + 1) * HEAD_DIM] = res[:, HEAD_DIM:]


def _scan_init():
    return jnp.ones((MC, D_LRU), F32), jnp.zeros((MC, D_LRU), F32)


def _scan_steps(g_s, load_v, bg_ref, hrate, a_s, h_s, steps, carry):
    b_r = bg_ref[:, :D_LRU]
    b_i = bg_ref[:, D_LRU:]
    acc_a, acc_h = carry
    for s in steps:
        rows = _ds(s * MC, MC)
        pre = g_s[rows, :]
        a, bx = _gate_ab(pre[:, :D_LRU] + b_r, pre[:, D_LRU:] + b_i, load_v(rows), hrate)
        acc_a = a * acc_a
        acc_h = a * acc_h + bx
        a_s[rows, :] = acc_a
        h_s[rows, :] = acc_h
    return acc_a, acc_h


def _chunk_carries(tot_a, tot_h, carry_in, reverse):
    row_id = lax.broadcasted_iota(jnp.int32, (MC, D_LRU), 0)
    hin = jnp.zeros((MC, D_LRU), F32)
    h = carry_in
    order = range(MC - 1, -1, -1) if reverse else range(MC)
    for m in order:
        hin = jnp.where(row_id == m, h, hin)
        h = tot_a[m:m + 1] * h + tot_h[m:m + 1]
    return hin, h


def _ada_kernel(c_ref, w_ref, b_ref, o_ref):
    s = _silu(c_ref[...]).astype(BF16)
    o_ref[...] = jnp.dot(s, w_ref[...].astype(BF16), preferred_element_type=F32) + b_ref[...]


def _ada_mod(cc, w_ada, b_ada):
    nblk = 512
    return pl.pallas_call(
        _ada_kernel,
        grid=(3 * D_MODEL // nblk,),
        in_specs=[pl.BlockSpec((8, D_MODEL), lambda i: (0, 0)),
                  pl.BlockSpec((D_MODEL, nblk), lambda i: (0, i)),
                  pl.BlockSpec((1, nblk), lambda i: (0, i))],
        out_specs=pl.BlockSpec((8, nblk), lambda i: (0, i)),
        out_shape=jax.ShapeDtypeStruct((8, 3 * D_MODEL), F32),
        name="ada_mod",
    )(cc, w_ada, b_ada)


def _ctx_kernel(x_ref, mod_ref, g_ref, wlx_ref, cw_ref, cb_ref, wgf_ref, bgf_ref, wgb_ref, bgb_ref,
                lam_ref, fin_ref, up_s, uext_s, vf_s, g_s, a_s, h_s):
    ch = CH_CTX
    shift = mod_ref[0, :, 0:D_MODEL]
    scale = mod_ref[0, :, D_MODEL:2 * D_MODEL]
    gmul = g_ref[...] * (1.0 + scale)
    lhs = _modulate(x_ref[0], gmul, shift).astype(BF16)
    u = jnp.dot(lhs, wlx_ref[...], preferred_element_type=F32)
    _to_slab_order(u, up_s, uext_s, ch)
    zero_row = jnp.zeros((1, D_LRU), F32)
    _conv_halo(uext_s, zero_row, zero_row, zero_row, ch)
    for s in range(ch):
        vf_s[s * MC:(s + 1) * MC, :] = _conv_slab(uext_s, s, cw_ref, cb_ref)
    vb = vf_s[...].astype(BF16)
    for d, (wg_ref, bg_ref) in enumerate(((wgf_ref, bgf_ref), (wgb_ref, bgb_ref))):
        _gate_matmuls(vb, wg_ref, g_s, slice(0, ch * MC))
        rate = _half_decay_rate(lam_ref[d:d + 1, :])
        steps = range(ch - 1, -1, -1) if d == 1 else range(ch)
        tot_a, tot_h = _scan_steps(g_s, lambda rows: vf_s[rows, :], bg_ref, rate, a_s, h_s, steps,
                                   _scan_init())
        _, fin = _chunk_carries(tot_a, tot_h, zero_row, reverse=(d == 1))
        fin_ref[0, d:d + 1, :] = fin


def _ctx_states(ctx, mod3, g, wlx, cw, cb, wgf, bgf, wgb, bgb, lam):
    bsz = ctx.shape[0]
    n = CTX_LEN
    full = lambda *shape: pl.BlockSpec(shape, lambda b: (0,) * len(shape))
    return pl.pallas_call(
        _ctx_kernel,
        grid=(bsz,),
        in_specs=[pl.BlockSpec((1, n, D_MODEL), lambda b: (b, 0, 0)),
                  pl.BlockSpec((1, 1, 3 * D_MODEL), lambda b: (bsz, 0, 0)),
                  full(1, D_MODEL), full(D_MODEL, D_LRU), full(4, MC, D_LRU), full(MC, D_LRU),
                  full(N_HEADS, HEAD_DIM, 2 * HEAD_DIM), full(1, 2 * D_LRU),
                  full(N_HEADS, HEAD_DIM, 2 * HEAD_DIM), full(1, 2 * D_LRU),
                  full(2, D_LRU)],
        out_specs=pl.BlockSpec((1, 2, D_LRU), lambda b: (b, 0, 0)),
        out_shape=jax.ShapeDtypeStruct((bsz, 2, D_LRU), F32),
        scratch_shapes=[pltpu.VMEM((N_QL, MC * _pitch(CH_CTX), LANES), F32),
                        pltpu.VMEM((n + 3 * MC, D_LRU), F32),
                        pltpu.VMEM((n, D_LRU), F32),
                        pltpu.VMEM((n, 2 * D_LRU), F32),
                        pltpu.VMEM((n, D_LRU), F32),
                        pltpu.VMEM((n, D_LRU), F32)],
        compiler_params=pltpu.CompilerParams(dimension_semantics=("arbitrary",)),
        name="ctx_states",
    )(ctx, mod3, g, wlx, cw, cb, wgf, bgf, wgb, bgb, lam)


def _fwd_kernel(x_ref, xh_ref, mod_ref, g_ref, win_ref, dftc_ref, cw_ref, cb_ref, wg_ref, bg_ref,
                lam_ref, h0_ref,
                p_ref, q_ref, gfs_ref, gls_ref, v_ref, hf_ref,
                lhs_s, up_s, uext_s, vf_s, g_s, a_s, h_s, cc_s, hc_s):
    j = pl.program_id(1)

    @pl.when(j == 0)
    def _():
        cc_s[...] = jnp.zeros_like(cc_s)
        hc_s[...] = jnp.broadcast_to(h0_ref[0, 0:1, :], hc_s.shape)

    shift = mod_ref[0, :, 0:D_MODEL]
    scale = mod_ref[0, :, D_MODEL:2 * D_MODEL]
    gmul = g_ref[...] * (1.0 + scale)
    for m in range(MC):
        lhs_s[m * CH:(m + 1) * CH, :] = _modulate(x_ref[0, m * CH:(m + 1) * CH, :], gmul,
                                                  shift).astype(BF16)
    hh = _modulate(xh_ref[0, 0], gmul, shift)
    lhs_s[TILE:TILE + MC, :] = jnp.concatenate([hh, jnp.zeros_like(hh)], axis=0).astype(BF16)

    u = jnp.dot(lhs_s[...], win_ref[:, 2 * D_FOURIER:2 * D_FOURIER + D_LRU],
                preferred_element_type=F32)

    _to_slab_order(u, up_s, uext_s, CH)
    look = jnp.where(j == N_TILES - 1, 0.0, u[TILE:TILE + 1])
    _conv_halo(uext_s, look, cc_s[0:1, :], cc_s[1:2, :], CH)
    cc_s[0:1, :] = u[TILE - 1:TILE]
    cc_s[1:2, :] = u[TILE - 2:TILE - 1]

    dftc = dftc_ref[...].astype(BF16)
    rate = _half_decay_rate(lam_ref[0:1, :])
    def fourier_block(r):
        rows = _ds(r * RB, RB)
        zf = jnp.dot(lhs_s[rows, :], win_ref[:, 0:D_FOURIER], preferred_element_type=F32).astype(BF16)
        for g in range(N_GROUPS):
            cols = slice(g * GROUP_DIM, (g + 1) * GROUP_DIM)
            pq = jnp.dot(zf[:, cols], dftc, preferred_element_type=F32).astype(BF16)
            p_ref[0, rows, cols] = pq[:, :GROUP_DIM]
            q_ref[0, rows, cols] = pq[:, GROUP_DIM:]

    def silu_block(r, c0, o_ref):
        rows = _ds(r * RB, RB)
        o_ref[0, rows, :] = _silu_of_half(jnp.dot(lhs_s[rows, :], win_ref[:, c0:c0 + D_LRU],
                                                  preferred_element_type=F32)).astype(BF16)

    def conv_block(r):
        for i in range(SPB):
            rows = _ds((r * SPB + i) * MC, MC)
            v = _conv_slab(uext_s, r * SPB + i, cw_ref, cb_ref)
            vf_s[rows, :] = v
            v_ref[0, rows, :] = v.astype(BF16)

    def lru_block(r):
        rows = _ds(r * RB, RB)
        _gate_matmuls(v_ref[0, rows, :], wg_ref, g_s, rows)
        silu_block(r, D_FOURIER, gfs_ref)
        silu_block(r, 2 * D_FOURIER + D_LRU, gls_ref)

    def scan_block(r, carry):
        return _scan_steps(g_s, lambda rws: vf_s[rws, :], bg_ref, rate, a_s, h_s,
                           [r * SPB + i for i in range(SPB)], carry)

    def stage(r, carry):
        lru_block(r)
        fourier_block(r + 1)
        conv_block(r + 1)
        return scan_block(r, carry)

    fourier_block(0)
    conv_block(0)
    carry = lax.fori_loop(0, N_RB - 1, stage, _scan_init())
    lru_block(N_RB - 1)
    carry = scan_block(N_RB - 1, carry)

    tot_a, tot_h = carry
    hin, h_out = _chunk_carries(tot_a, tot_h, hc_s[0:1, :], reverse=False)
    hc_s[...] = jnp.broadcast_to(h_out, hc_s.shape)
    for s in range(CH):
        rows = slice(s * MC, (s + 1) * MC)
        hf_ref[0, rows, :] = (h_s[rows, :] + a_s[rows, :] * hin).astype(BF16)


def _fwd_pass(x, x4, mod3, g, win, dftc, cw, cb, wgf, bgf, lam, fin):
    bsz = x.shape[0]
    full = lambda *shape: pl.BlockSpec(shape, lambda b, j: (0,) * len(shape))
    tile = lambda c: pl.BlockSpec((1, TILE, c), lambda b, j: (b, j, 0))
    shp = lambda c: jax.ShapeDtypeStruct((bsz, SEQ, c), BF16)
    rows8 = TILE // 8
    return pl.pallas_call(
        _fwd_kernel,
        grid=(bsz, N_TILES),
        in_specs=[tile(D_MODEL),
                  pl.BlockSpec((1, 1, 8, D_MODEL),
                               lambda b, j: (b, jnp.minimum((j + 1) * rows8, SEQ // 8 - 1), 0, 0)),
                  pl.BlockSpec((1, 1, 3 * D_MODEL), lambda b, j: (b, 0, 0)),
                  full(1, D_MODEL), full(D_MODEL, 2 * D_MODEL), full(GROUP_DIM, 2 * GROUP_DIM),
                  full(4, MC, D_LRU), full(MC, D_LRU),
                  full(N_HEADS, HEAD_DIM, 2 * HEAD_DIM), full(1, 2 * D_LRU),
                  full(2, D_LRU),
                  pl.BlockSpec((1, 2, D_LRU), lambda b, j: (b, 0, 0))],
        out_specs=[tile(D_FOURIER), tile(D_FOURIER), tile(D_FOURIER), tile(D_LRU), tile(D_LRU),
                   tile(D_LRU)],
        out_shape=[shp(D_FOURIER), shp(D_FOURIER), shp(D_FOURIER), shp(D_LRU), shp(D_LRU),
                   shp(D_LRU)],
        scratch_shapes=[pltpu.VMEM((TILE + MC, D_MODEL), BF16),
                        pltpu.VMEM((N_QL, MC * _pitch(CH), LANES), F32),
                        pltpu.VMEM((TILE + 3 * MC, D_LRU), F32),
                        pltpu.VMEM((TILE, D_LRU), F32),
                        pltpu.VMEM((TILE, 2 * D_LRU), F32),
                        pltpu.VMEM((TILE, D_LRU), F32),
                        pltpu.VMEM((TILE, D_LRU), F32),
                        pltpu.VMEM((8, D_LRU), F32),
                        pltpu.VMEM((8, D_LRU), F32)],
        compiler_params=pltpu.CompilerParams(dimension_semantics=("arbitrary", "arbitrary"),
                                             vmem_limit_bytes=VMEM_LIMIT_BYTES),
        name="fwd_pass",
    )(x, x4, mod3, g, win, dftc, cw, cb, wgf, bgf, lam, fin)


def _dft_kernel(p_ref, q_ref, f1_ref, m2_ref, o_ref, sp_s, sq_s, yr_s, yj_s, so_s):
    p1 = _pitch(DFT_N2)
    p2 = _pitch(DFT_N1)
    for n1 in range(DFT_N1):
        rows = slice(n1 * DFT_N2, (n1 + 1) * DFT_N2)
        sp_s[n1 * p1:n1 * p1 + DFT_N2, :] = p_ref[0, rows, :].astype(F32)
        sq_s[n1 * p1:n1 * p1 + DFT_N2, :] = q_ref[0, rows, :].astype(F32)
    f1 = f1_ref[...].astype(BF16)
    for n2 in range(DFT_N2):
        rhs = jnp.concatenate([sp_s[pl.ds(n2, DFT_N1, stride=p1), :],
                               sq_s[pl.ds(n2, DFT_N1, stride=p1), :]], axis=0).astype(BF16)
        y = jnp.dot(f1, rhs, preferred_element_type=F32)
        yr_s[n2 * p2:n2 * p2 + DFT_N1, :] = y[:DFT_N1]
        yj_s[n2 * p2:n2 * p2 + DFT_N1, :] = y[DFT_N1:]
    for k1 in range(DFT_N1):
        rhs = jnp.concatenate([yr_s[pl.ds(k1, DFT_N2, stride=p2), :],
                               yj_s[pl.ds(k1, DFT_N2, stride=p2), :]], axis=0).astype(BF16)
        so_s[pl.ds(k1, DFT_N2, stride=p2), :] = jnp.dot(m2_ref[k1].astype(BF16), rhs,
                                                         preferred_element_type=F32)
    for k2 in range(DFT_N2):
        o_ref[0, k2 * DFT_N1:(k2 + 1) * DFT_N1, :] = so_s[k2 * p2:k2 * p2 + DFT_N1, :].astype(BF16)


def _position_dft(p, q, f1, m2):
    bsz = p.shape[0]
    blk = pl.BlockSpec((1, SEQ, LANES), lambda b, i: (b, 0, i))
    pad1 = DFT_N1 * _pitch(DFT_N2)
    pad2 = DFT_N2 * _pitch(DFT_N1)
    return pl.pallas_call(
        _dft_kernel,
        grid=(bsz, D_FOURIER // LANES),
        in_specs=[blk, blk,
                  pl.BlockSpec((2 * DFT_N1, 2 * DFT_N1), lambda b, i: (0, 0)),
                  pl.BlockSpec((DFT_N1, DFT_N2, 2 * DFT_N2), lambda b, i: (0, 0, 0))],
        out_specs=blk,
        out_shape=jax.ShapeDtypeStruct(p.shape, BF16),
        scratch_shapes=[pltpu.VMEM((pad1, LANES), F32), pltpu.VMEM((pad1, LANES), F32),
                        pltpu.VMEM((pad2, LANES), F32), pltpu.VMEM((pad2, LANES), F32),
                        pltpu.VMEM((pad2, LANES), F32)],
        compiler_params=pltpu.CompilerParams(dimension_semantics=("arbitrary", "arbitrary"),
                                             vmem_limit_bytes=VMEM_LIMIT_BYTES),
        name="position_dft",
    )(p, q, f1, m2)


def _bwd_kernel(x_ref, yf_ref, gfs_ref, gls_ref, v_ref, hf_ref, mod_ref, fg_ref, wfour_ref, wout_ref,
                wg_ref, bg_ref, lam_ref, h0_ref, o_ref,
                g_s, a_s, h_s, yp_s, lhs_s, hc_s):
    j = pl.program_id(1)

    @pl.when(j == 0)
    def _():
        hc_s[...] = jnp.broadcast_to(h0_ref[0, 1:2, :], hc_s.shape)

    rate = _half_decay_rate(lam_ref[1:2, :])
    carry = _scan_init()
    for r in range(N_RB - 1, -1, -1):
        rows = slice(r * RB, (r + 1) * RB)
        _gate_matmuls(v_ref[0, rows, :], wg_ref, g_s, rows)
        yfw = jnp.dot(yf_ref[0, rows, :], wfour_ref[...], preferred_element_type=F32)
        lhs_s[rows, 0:D_FOURIER] = (yfw * gfs_ref[0, rows, :].astype(F32)).astype(BF16)
        carry = _scan_steps(g_s, lambda rws: v_ref[0, rws, :].astype(F32), bg_ref, rate, a_s, h_s,
                            range((r + 1) * SPB - 1, r * SPB - 1, -1), carry)
    tot_a, tot_h = carry
    hin, h_out = _chunk_carries(tot_a, tot_h, hc_s[0:1, :], reverse=True)
    hc_s[...] = jnp.broadcast_to(h_out, hc_s.shape)
    pitch = _pitch(CH)
    for s in range(CH):
        rows = slice(s * MC, (s + 1) * MC)
        yl = h_s[rows, :] + a_s[rows, :] * hin + hf_ref[0, rows, :].astype(F32)
        for q in range(N_QL):
            yp_s[q, pl.ds(s, MC, stride=pitch), :] = yl[:, q * LANES:(q + 1) * LANES]
    for m in range(MC):
        rows = slice(m * CH, (m + 1) * CH)
        for q in range(N_QL):
            cols = slice(q * LANES, (q + 1) * LANES)
            yl = yp_s[q, m * pitch:m * pitch + CH, :]
            lhs_s[rows, D_FOURIER + q * LANES:D_FOURIER + (q + 1) * LANES] = (
                yl * gls_ref[0, rows, cols].astype(F32)).astype(BF16)

    res_gate = mod_ref[0, :, 2 * D_MODEL:3 * D_MODEL]
    fg = fg_ref[...]
    for r in range(N_RB):
        proj = jnp.dot(lhs_s[r * RB:(r + 1) * RB, :], wout_ref[...], preferred_element_type=F32)
        for m in range(RB // CH):
            rows = slice(r * RB + m * CH, r * RB + (m + 1) * CH)
            res = x_ref[0, rows, :] + res_gate * proj[m * CH:(m + 1) * CH, :]
            ms = jnp.mean(res * res, axis=-1, keepdims=True)
            o_ref[0, rows, :] = (res * lax.rsqrt(ms + EPS)) * fg


def _bwd_pass(x, yf, gfs, gls, v, hf, mod3, fg, wfour, wout, wgb, bgb, lam, fin):
    bsz = x.shape[0]
    full = lambda *shape: pl.BlockSpec(shape, lambda b, j: (0,) * len(shape))
    tile = lambda c: pl.BlockSpec((1, TILE, c), lambda b, j: (b, N_TILES - 1 - j, 0))
    return pl.pallas_call(
        _bwd_kernel,
        grid=(bsz, N_TILES),
        in_specs=[tile(D_MODEL), tile(D_FOURIER), tile(D_FOURIER), tile(D_LRU), tile(D_LRU),
                  tile(D_LRU),
                  pl.BlockSpec((1, 1, 3 * D_MODEL), lambda b, j: (b, 0, 0)),
                  full(1, D_MODEL), full(D_FOURIER, D_FOURIER), full(D_MODEL, D_MODEL),
                  full(N_HEADS, HEAD_DIM, 2 * HEAD_DIM), full(1, 2 * D_LRU), full(2, D_LRU),
                  pl.BlockSpec((1, 2, D_LRU), lambda b, j: (b, 0, 0))],
        out_specs=tile(D_MODEL),
        out_shape=jax.ShapeDtypeStruct(x.shape, F32),
        scratch_shapes=[pltpu.VMEM((TILE, 2 * D_LRU), F32),
                        pltpu.VMEM((TILE, D_LRU), F32),
                        pltpu.VMEM((TILE, D_LRU), F32),
                        pltpu.VMEM((N_QL, MC * _pitch(CH), LANES), F32),
                        pltpu.VMEM((TILE, D_MODEL), BF16),
                        pltpu.VMEM((8, D_LRU), F32)],
        compiler_params=pltpu.CompilerParams(dimension_semantics=("arbitrary", "arbitrary"),
                                             vmem_limit_bytes=VMEM_LIMIT_BYTES),
        name="bwd_pass",
    )(x, yf, gfs, gls, v, hf, mod3, fg, wfour, wout, wgb, bgb, lam, fin)


def kernel(x, c, ctx, c_ctx, w_ada, b_ada, norm_gain, w_in, w_four, conv_w, conv_b, w_rg, b_rg,
           w_ig, b_ig, lam, w_out, final_gain):
    bsz = x.shape[0]
    assert x.shape == (bsz, SEQ, D_MODEL) and ctx.shape == (bsz, CTX_LEN, D_MODEL)
    assert w_ada.shape[0] == 1, "single-layer kernel"
    dftc, f1, m2 = _dft_constants()

    cc = jnp.concatenate([c, c_ctx[None, :], jnp.zeros((8 - bsz - 1, D_MODEL), F32)], axis=0)
    mod3 = _ada_mod(cc, w_ada[0], b_ada[0][None, :]).reshape(8, 1, 3 * D_MODEL)

    g = norm_gain[0][None, :]
    half_cols = jnp.concatenate([jnp.ones((D_FOURIER,), F32), jnp.full((D_FOURIER,), 0.5, F32),
                                 jnp.ones((D_LRU,), F32), jnp.full((D_LRU,), 0.5, F32)])
    win = (w_in[0] * half_cols[None, :]).astype(BF16)
    wlx = win[:, 2 * D_FOURIER:2 * D_FOURIER + D_LRU]
    cw = jnp.broadcast_to(0.5 * conv_w[0][:, None, :], (4, MC, D_LRU))
    cb = jnp.broadcast_to(0.5 * conv_b[0][None, :], (MC, D_LRU))
    wg = jnp.concatenate([w_rg[0], w_ig[0]], axis=-1).astype(BF16)
    bg = 0.5 * jnp.concatenate([b_rg[0], b_ig[0]], axis=-1)[:, None, :]
    lam0 = lam[0]

    fin = _ctx_states(ctx, mod3, g, wlx, cw, cb, wg[0], bg[0], wg[1], bg[1], lam0)

    x4 = x.reshape(bsz, SEQ // 8, 8, D_MODEL)
    p, q, gfs, gls, v, hf = _fwd_pass(x, x4, mod3, g, win, dftc, cw, cb, wg[0], bg[0], lam0, fin)

    yf = _position_dft(p, q, f1, m2)

    return _bwd_pass(x, yf, gfs, gls, v, hf, mod3, final_gain[None, :], w_four[0].astype(BF16),
                     w_out[0].astype(BF16), wg[1], bg[1], lam0, fin)
```

```python
import numpy as np
import jax
import jax.numpy as jnp
from jax import lax
from jax.experimental import pallas as pl
from jax.experimental.pallas import tpu as pltpu

D_MODEL = 1024
SEQ = 8192
CTX_LEN = 256
D_FOURIER = 512
D_LRU = 512
N_GROUPS = 4
GROUP_DIM = 128
N_HEADS = 4
HEAD_DIM = 128
LRU_C = 8.0
EPS = 1e-6

LANES = 128
MC = 16
CH = 64
TILE = MC * CH
N_TILES = SEQ // TILE
CH_CTX = CTX_LEN // MC
N_QL = D_LRU // LANES
LHS_ROWS = TILE + MC

DFT_N1 = 128
DFT_N2 = 64
DFT_NB = 2

VMEM_LIMIT_BYTES = 56 * 1024 * 1024
F32 = jnp.float32
BF16 = jnp.bfloat16


def _pitch(ch):
    return ch + 8


def _dft_constants():
    c = np.arange(GROUP_DIM)
    ang = 2.0 * np.pi * ((c[:, None] * c[None, :]) % GROUP_DIM) / GROUP_DIM
    dftc = np.concatenate([np.cos(ang), np.sin(ang)], axis=1) * 2.0 ** -3

    n = np.arange(DFT_N1)
    ang1 = 2.0 * np.pi * ((n[:, None] * n[None, :]) % DFT_N1) / DFT_N1
    c1, s1 = np.cos(ang1), np.sin(ang1)
    f1 = np.block([[c1, -s1], [s1, c1]]) * 2.0 ** -4

    k1 = np.arange(DFT_N1)[:, None, None]
    k2 = np.arange(DFT_N2)[None, :, None]
    n2 = np.arange(DFT_N2)[None, None, :]
    ang2 = 2.0 * np.pi * ((n2 * (k1 + DFT_N1 * k2)) % SEQ) / SEQ
    m2 = np.concatenate([np.cos(ang2), -np.sin(ang2)], axis=2) * 2.0 ** -3
    return jnp.asarray(dftc, dtype=F32), jnp.asarray(f1, dtype=F32), jnp.asarray(m2, dtype=F32)


def _silu(x):
    hx = 0.5 * x
    return hx * jnp.tanh(hx) + hx


def _silu_of_half(hx):
    return hx * jnp.tanh(hx) + hx


def _half_decay_rate(lam_row):
    z = -lam_row
    return (-0.5 * LRU_C) * (jnp.maximum(z, 0.0) + jnp.log1p(jnp.exp(-jnp.abs(z))))


def _modulate(xs, gmul, shift):
    ms = jnp.mean(xs * xs, axis=-1, keepdims=True)
    return (xs * lax.rsqrt(ms + EPS)) * gmul + shift


def _gate_ab(hpre_r, hpre_i, hv, hrate):
    a = jnp.exp(jnp.tanh(hpre_r) * hrate + hrate)
    om = 1.0 - a * a
    s = om * lax.rsqrt(jnp.maximum(om, 1e-30))
    return a, s * ((jnp.tanh(hpre_i) + 1.0) * hv)


def _ds(start, size):
    if isinstance(start, int):
        return slice(start, start + size)
    return pl.ds(pl.multiple_of(start, MC), size)


def _to_slab_order(u, up_s, uext_s, ch):
    pitch = _pitch(ch)
    for m in range(MC):
        for q in range(N_QL):
            up_s[q, m * pitch:m * pitch + ch, :] = u[m * ch:(m + 1) * ch, q * LANES:(q + 1) * LANES]
    for s in range(ch):
        for q in range(N_QL):
            uext_s[(s + 2) * MC:(s + 3) * MC, q * LANES:(q + 1) * LANES] = (
                up_s[q, pl.ds(s, MC, stride=pitch), :])


def _conv_halo(uext_s, lookahead, c_prev1, c_prev2, ch):
    last = uext_s[(ch + 1) * MC:(ch + 2) * MC, :]
    last2 = uext_s[ch * MC:(ch + 1) * MC, :]
    first = uext_s[2 * MC:3 * MC, :]
    uext_s[0:MC, :] = jnp.concatenate([c_prev2, last2[:MC - 1]], axis=0)
    uext_s[MC:2 * MC, :] = jnp.concatenate([c_prev1, last[:MC - 1]], axis=0)
    uext_s[(ch + 2) * MC:(ch + 3) * MC, :] = jnp.concatenate([first[1:], lookahead], axis=0)


def _conv_slab(uext_s, s, cw_ref, cb_ref):
    acc = cb_ref[...]
    for k in range(4):
        acc = acc + cw_ref[k] * uext_s[_ds((s + k) * MC, MC), :]
    return acc


def _gate_matmuls(vb, wg_ref, g_s, rows):
    for h in range(N_HEADS):
        res = jnp.dot(vb[:, h * HEAD_DIM:(h + 1) * HEAD_DIM], wg_ref[h], preferred_element_type=F32)
        g_s[rows, h * HEAD_DIM:(h + 1) * HEAD_DIM] = res[:, :HEAD_DIM]
        g_s[rows, D_LRU + h * HEAD_DIM:D_LRU + (h + 1) * HEAD_DIM] = res[:, HEAD_DIM:]


def _scan_init():
    return jnp.ones((MC, D_LRU), F32), jnp.zeros((MC, D_LRU), F32)


def _scan_steps(g_s, load_v, bg_ref, hrate, a_s, h_s, steps, carry):
    b_r = bg_ref[:, :D_LRU]
    b_i = bg_ref[:, D_LRU:]
    acc_a, acc_h = carry
    for s in steps:
        rows = _ds(s * MC, MC)
        pre = g_s[rows, :]
        a, bx = _gate_ab(pre[:, :D_LRU] + b_r, pre[:, D_LRU:] + b_i, load_v(rows), hrate)
        acc_a = a * acc_a
        acc_h = a * acc_h + bx
        a_s[rows, :] = acc_a
        h_s[rows, :] = acc_h
    return acc_a, acc_h


def _chunk_carries(tot_a, tot_h, carry_in, reverse):
    row_id = lax.broadcasted_iota(jnp.int32, (MC, D_LRU), 0)
    hin = jnp.zeros((MC, D_LRU), F32)
    h = carry_in
    order = range(MC - 1, -1, -1) if reverse else range(MC)
    for m in order:
        hin = jnp.where(row_id == m, h, hin)
        h = tot_a[m:m + 1] * h + tot_h[m:m + 1]
    return hin, h


def _ada_kernel(c_ref, w_ref, b_ref, o_ref):
    s = _silu(c_ref[...]).astype(BF16)
    o_ref[...] = jnp.dot(s, w_ref[...].astype(BF16), preferred_element_type=F32) + b_ref[...]


def _ada_mod(cc, w_ada, b_ada):
    nblk = D_MODEL
    return pl.pallas_call(
        _ada_kernel,
        grid=(3 * D_MODEL // nblk,),
        in_specs=[pl.BlockSpec((8, D_MODEL), lambda i: (0, 0)),
                  pl.BlockSpec((D_MODEL, nblk), lambda i: (0, i)),
                  pl.BlockSpec((1, nblk), lambda i: (0, i))],
        out_specs=pl.BlockSpec((8, nblk), lambda i: (0, i)),
        out_shape=jax.ShapeDtypeStruct((8, 3 * D_MODEL), F32),
        name="ada_mod",
    )(cc, w_ada, b_ada)


def _ctx_kernel(x_ref, mod_ref, g_ref, wlx_ref, cw_ref, cb_ref, wgf_ref, bgf_ref, wgb_ref, bgb_ref,
                lam_ref, fin_ref, up_s, uext_s, vf_s, g_s, a_s, h_s):
    ch = CH_CTX
    shift = mod_ref[0, :, 0:D_MODEL]
    scale = mod_ref[0, :, D_MODEL:2 * D_MODEL]
    gmul = g_ref[...] * (1.0 + scale)
    lhs = _modulate(x_ref[0], gmul, shift).astype(BF16)
    u = jnp.dot(lhs, wlx_ref[...], preferred_element_type=F32)
    _to_slab_order(u, up_s, uext_s, ch)
    zero_row = jnp.zeros((1, D_LRU), F32)
    _conv_halo(uext_s, zero_row, zero_row, zero_row, ch)
    for s in range(ch):
        vf_s[s * MC:(s + 1) * MC, :] = _conv_slab(uext_s, s, cw_ref, cb_ref)
    vb = vf_s[...].astype(BF16)
    for d, (wg_ref, bg_ref) in enumerate(((wgf_ref, bgf_ref), (wgb_ref, bgb_ref))):
        _gate_matmuls(vb, wg_ref, g_s, slice(0, ch * MC))
        rate = _half_decay_rate(lam_ref[d:d + 1, :])
        steps = range(ch - 1, -1, -1) if d == 1 else range(ch)
        tot_a, tot_h = _scan_steps(g_s, lambda rows: vf_s[rows, :], bg_ref, rate, a_s, h_s, steps,
                                   _scan_init())
        _, fin = _chunk_carries(tot_a, tot_h, zero_row, reverse=(d == 1))
        fin_ref[0, d:d + 1, :] = fin


def _ctx_states(ctx, mod3, g, wlx, cw, cb, wgf, bgf, wgb, bgb, lam):
    bsz = ctx.shape[0]
    n = CTX_LEN
    full = lambda *shape: pl.BlockSpec(shape, lambda b: (0,) * len(shape))
    return pl.pallas_call(
        _ctx_kernel,
        grid=(bsz,),
        in_specs=[pl.BlockSpec((1, n, D_MODEL), lambda b: (b, 0, 0)),
                  pl.BlockSpec((1, 1, 3 * D_MODEL), lambda b: (bsz, 0, 0)),
                  full(1, D_MODEL),
                  pl.BlockSpec((D_MODEL, D_LRU), lambda b: (0, 2)),
                  full(4, MC, D_LRU), full(MC, D_LRU),
                  full(N_HEADS, HEAD_DIM, 2 * HEAD_DIM), full(1, 2 * D_LRU),
                  full(N_HEADS, HEAD_DIM, 2 * HEAD_DIM), full(1, 2 * D_LRU),
                  full(2, D_LRU)],
        out_specs=pl.BlockSpec((1, 2, D_LRU), lambda b: (b, 0, 0)),
        out_shape=jax.ShapeDtypeStruct((bsz, 2, D_LRU), F32),
        scratch_shapes=[pltpu.VMEM((N_QL, MC * _pitch(CH_CTX), LANES), F32),
                        pltpu.VMEM((n + 3 * MC, D_LRU), F32),
                        pltpu.VMEM((n, D_LRU), F32),
                        pltpu.VMEM((n, 2 * D_LRU), F32),
                        pltpu.VMEM((n, D_LRU), F32),
                        pltpu.VMEM((n, D_LRU), F32)],
        compiler_params=pltpu.CompilerParams(dimension_semantics=("arbitrary",)),
        name="ctx_states",
    )(ctx, mod3, g, wlx, cw, cb, wgf, bgf, wgb, bgb, lam)


def _fwd_kernel(x_ref, xh_ref, mod_ref, g_ref, win_ref, dftc_ref, cw_ref, cb_ref, wg_ref,
                bg_ref, lam_ref, h0_ref,
                p_ref, q_ref, gfs_ref, gls_ref, v_ref, hf_ref,
                lhs_s, up_s, uext_s, vf_s, g_s, a_s, h_s, cc_s, hc_s):
    j = pl.program_id(1)

    @pl.when(j == 0)
    def _():
        cc_s[...] = jnp.zeros_like(cc_s)
        hc_s[...] = jnp.broadcast_to(h0_ref[0, 0:1, :], hc_s.shape)

    shift = mod_ref[0, :, 0:D_MODEL]
    scale = mod_ref[0, :, D_MODEL:2 * D_MODEL]
    gmul = g_ref[...] * (1.0 + scale)
    for m in range(MC):
        rows = slice(m * CH, (m + 1) * CH)
        lhs_s[rows, :] = _modulate(x_ref[0, rows, :], gmul, shift).astype(BF16)
    hh = _modulate(xh_ref[0], gmul, shift)
    lhs_s[TILE:LHS_ROWS, :] = jnp.concatenate([hh, jnp.zeros_like(hh)], axis=0).astype(BF16)

    u = jnp.dot(lhs_s[...], win_ref[:, 2 * D_FOURIER:2 * D_FOURIER + D_LRU],
                preferred_element_type=F32)

    _to_slab_order(u, up_s, uext_s, CH)
    look = jnp.where(j == N_TILES - 1, 0.0, u[TILE:TILE + 1])
    _conv_halo(uext_s, look, cc_s[0:1, :], cc_s[1:2, :], CH)
    cc_s[0:1, :] = u[TILE - 1:TILE]
    cc_s[1:2, :] = u[TILE - 2:TILE - 1]

    zf = jnp.dot(lhs_s[0:TILE, :], win_ref[:, 0:D_FOURIER],
                 preferred_element_type=F32).astype(BF16)
    dftc = dftc_ref[...].astype(BF16)
    for g in range(N_GROUPS):
        cols = slice(g * GROUP_DIM, (g + 1) * GROUP_DIM)
        pq = jnp.dot(zf[:, cols], dftc, preferred_element_type=F32).astype(BF16)
        p_ref[0, :, cols] = pq[:, :GROUP_DIM]
        q_ref[0, :, cols] = pq[:, GROUP_DIM:]

    for s in range(CH):
        rows = slice(s * MC, (s + 1) * MC)
        v = _conv_slab(uext_s, s, cw_ref, cb_ref)
        vf_s[rows, :] = v
        v_ref[0, rows, :] = v.astype(BF16)
    _gate_matmuls(v_ref[0], wg_ref, g_s, slice(0, TILE))

    for c0, o_ref in ((D_FOURIER, gfs_ref), (2 * D_FOURIER + D_LRU, gls_ref)):
        o_ref[0] = _silu_of_half(jnp.dot(lhs_s[0:TILE, :], win_ref[:, c0:c0 + D_LRU],
                                         preferred_element_type=F32)).astype(BF16)

    hrate = _half_decay_rate(lam_ref[0:1, :])
    tot_a, tot_h = _scan_steps(g_s, lambda rws: vf_s[rws, :], bg_ref, hrate, a_s, h_s,
                               range(CH), _scan_init())
    hin, h_out = _chunk_carries(tot_a, tot_h, hc_s[0:1, :], reverse=False)
    hc_s[...] = jnp.broadcast_to(h_out, hc_s.shape)
    for s in range(CH):
        rows = slice(s * MC, (s + 1) * MC)
        hf_ref[0, rows, :] = (h_s[rows, :] + a_s[rows, :] * hin).astype(BF16)


def _fwd_pass(x, mod3, g, win, dftc, cw, cb, wgf, bgf, lam, fin):
    bsz = x.shape[0]
    full = lambda *shape: pl.BlockSpec(shape, lambda b, j: (0,) * len(shape))
    tile = lambda c: pl.BlockSpec((1, TILE, c), lambda b, j: (b, j, 0))
    shp = lambda c: jax.ShapeDtypeStruct((bsz, SEQ, c), BF16)
    rows8 = TILE // 8
    return pl.pallas_call(
        _fwd_kernel,
        grid=(bsz, N_TILES),
        in_specs=[tile(D_MODEL),
                  pl.BlockSpec((1, 8, D_MODEL),
                               lambda b, j: (b, jnp.minimum((j + 1) * rows8, SEQ // 8 - 1), 0)),
                  pl.BlockSpec((1, 1, 3 * D_MODEL), lambda b, j: (b, 0, 0)),
                  full(1, D_MODEL), full(D_MODEL, 2 * D_MODEL), full(GROUP_DIM, 2 * GROUP_DIM),
                  full(4, MC, D_LRU), full(MC, D_LRU),
                  full(N_HEADS, HEAD_DIM, 2 * HEAD_DIM), full(1, 2 * D_LRU),
                  full(2, D_LRU),
                  pl.BlockSpec((1, 2, D_LRU), lambda b, j: (b, 0, 0))],
        out_specs=[tile(D_FOURIER), tile(D_FOURIER), tile(D_FOURIER), tile(D_LRU), tile(D_LRU),
                   tile(D_LRU)],
        out_shape=[shp(D_FOURIER), shp(D_FOURIER), shp(D_FOURIER), shp(D_LRU), shp(D_LRU),
                   shp(D_LRU)],
        scratch_shapes=[pltpu.VMEM((LHS_ROWS, D_MODEL), BF16),
                        pltpu.VMEM((N_QL, MC * _pitch(CH), LANES), F32),
                        pltpu.VMEM((TILE + 3 * MC, D_LRU), F32),
                        pltpu.VMEM((TILE, D_LRU), F32),
                        pltpu.VMEM((TILE, 2 * D_LRU), F32),
                        pltpu.VMEM((TILE, D_LRU), F32),
                        pltpu.VMEM((TILE, D_LRU), F32),
                        pltpu.VMEM((8, D_LRU), F32),
                        pltpu.VMEM((8, D_LRU), F32)],
        compiler_params=pltpu.CompilerParams(dimension_semantics=("arbitrary", "arbitrary"),
                                             vmem_limit_bytes=VMEM_LIMIT_BYTES),
        name="fwd_pass",
    )(x, x, mod3, g, win, dftc, cw, cb, wgf, bgf, lam, fin)


def _dft_kernel(p_ref, q_ref, f1_ref, m2_ref, o_ref, sp_s, sq_s, yr_s, yj_s, so_s):
    p1 = _pitch(DFT_N2)
    p2 = _pitch(DFT_N1)
    for n1 in range(DFT_N1):
        rows = slice(n1 * DFT_N2, (n1 + 1) * DFT_N2)
        sp_s[n1 * p1:n1 * p1 + DFT_N2, :] = p_ref[0, rows, :].astype(F32)
        sq_s[n1 * p1:n1 * p1 + DFT_N2, :] = q_ref[0, rows, :].astype(F32)
    f1 = f1_ref[...].astype(BF16)
    for n2 in range(0, DFT_N2, DFT_NB):
        rhs = jnp.concatenate(
            [jnp.concatenate([sp_s[pl.ds(n2 + i, DFT_N1, stride=p1), :],
                              sq_s[pl.ds(n2 + i, DFT_N1, stride=p1), :]], axis=0)
             for i in range(DFT_NB)], axis=1).astype(BF16)
        y = jnp.dot(f1, rhs, preferred_element_type=F32)
        for i in range(DFT_NB):
            lanes = slice(i * LANES, (i + 1) * LANES)
            yr_s[(n2 + i) * p2:(n2 + i) * p2 + DFT_N1, :] = y[:DFT_N1, lanes]
            yj_s[(n2 + i) * p2:(n2 + i) * p2 + DFT_N1, :] = y[DFT_N1:, lanes]
    for k1 in range(DFT_N1):
        rhs = jnp.concatenate([yr_s[pl.ds(k1, DFT_N2, stride=p2), :],
                               yj_s[pl.ds(k1, DFT_N2, stride=p2), :]], axis=0).astype(BF16)
        so_s[pl.ds(k1, DFT_N2, stride=p2), :] = jnp.dot(m2_ref[k1].astype(BF16), rhs,
                                                         preferred_element_type=F32)
    for k2 in range(DFT_N2):
        o_ref[0, k2 * DFT_N1:(k2 + 1) * DFT_N1, :] = so_s[k2 * p2:k2 * p2 + DFT_N1, :].astype(BF16)


def _position_dft(p, q, f1, m2):
    bsz = p.shape[0]
    blk = pl.BlockSpec((1, SEQ, LANES), lambda b, i: (b, 0, i))
    pad1 = DFT_N1 * _pitch(DFT_N2)
    pad2 = DFT_N2 * _pitch(DFT_N1)
    return pl.pallas_call(
        _dft_kernel,
        grid=(bsz, D_FOURIER // LANES),
        in_specs=[blk, blk,
                  pl.BlockSpec((2 * DFT_N1, 2 * DFT_N1), lambda b, i: (0, 0)),
                  pl.BlockSpec((DFT_N1, DFT_N2, 2 * DFT_N2), lambda b, i: (0, 0, 0))],
        out_specs=blk,
        out_shape=jax.ShapeDtypeStruct(p.shape, BF16),
        scratch_shapes=[pltpu.VMEM((pad1, LANES), F32), pltpu.VMEM((pad1, LANES), F32),
                        pltpu.VMEM((pad2, LANES), F32), pltpu.VMEM((pad2, LANES), F32),
                        pltpu.VMEM((pad2, LANES), F32)],
        compiler_params=pltpu.CompilerParams(dimension_semantics=("arbitrary", "arbitrary"),
                                             vmem_limit_bytes=VMEM_LIMIT_BYTES),
        name="position_dft",
    )(p, q, f1, m2)


def _bwd_kernel(x_ref, yf_ref, gfs_ref, gls_ref, v_ref, hf_ref, mod_ref, fg_ref, wfour_ref, wout_ref,
                wg_ref, bg_ref, lam_ref, h0_ref, o_ref,
                g_s, a_s, h_s, yp_s, lhs_s, hc_s):
    j = pl.program_id(1)

    @pl.when(j == 0)
    def _():
        hc_s[...] = jnp.broadcast_to(h0_ref[0, 1:2, :], hc_s.shape)

    _gate_matmuls(v_ref[0], wg_ref, g_s, slice(0, TILE))
    yfw = jnp.dot(yf_ref[0], wfour_ref[...], preferred_element_type=F32)
    lhs_s[:, 0:D_FOURIER] = (yfw * gfs_ref[0].astype(F32)).astype(BF16)

    hrate = _half_decay_rate(lam_ref[1:2, :])
    carry = _scan_steps(g_s, lambda rws: v_ref[0, rws, :].astype(F32), bg_ref, hrate, a_s, h_s,
                        range(CH - 1, -1, -1), _scan_init())
    tot_a, tot_h = carry
    hin, h_out = _chunk_carries(tot_a, tot_h, hc_s[0:1, :], reverse=True)
    hc_s[...] = jnp.broadcast_to(h_out, hc_s.shape)
    pitch = _pitch(CH)
    for s in range(CH):
        rows = slice(s * MC, (s + 1) * MC)
        yl = h_s[rows, :] + a_s[rows, :] * hin + hf_ref[0, rows, :].astype(F32)
        for q in range(N_QL):
            yp_s[q, pl.ds(s, MC, stride=pitch), :] = yl[:, q * LANES:(q + 1) * LANES]
    for m in range(MC):
        rows = slice(m * CH, (m + 1) * CH)
        for q in range(N_QL):
            cols = slice(q * LANES, (q + 1) * LANES)
            yl = yp_s[q, m * pitch:m * pitch + CH, :]
            lhs_s[rows, D_FOURIER + q * LANES:D_FOURIER + (q + 1) * LANES] = (
                yl * gls_ref[0, rows, cols].astype(F32)).astype(BF16)

    res_gate = mod_ref[0, :, 2 * D_MODEL:3 * D_MODEL]
    fg = fg_ref[...]
    proj = jnp.dot(lhs_s[...], wout_ref[...], preferred_element_type=F32)
    for m in range(MC):
        rows = slice(m * CH, (m + 1) * CH)
        res = x_ref[0, rows, :] + res_gate * proj[rows, :]
        ms = jnp.mean(res * res, axis=-1, keepdims=True)
        o_ref[0, rows, :] = (res * lax.rsqrt(ms + EPS)) * fg


def _bwd_pass(x, yf, gfs, gls, v, hf, mod3, fg, wfour, wout, wgb, bgb, lam, fin):
    bsz = x.shape[0]
    full = lambda *shape: pl.BlockSpec(shape, lambda b, j: (0,) * len(shape))
    tile = lambda c: pl.BlockSpec((1, TILE, c), lambda b, j: (b, N_TILES - 1 - j, 0))
    return pl.pallas_call(
        _bwd_kernel,
        grid=(bsz, N_TILES),
        in_specs=[tile(D_MODEL), tile(D_FOURIER), tile(D_FOURIER), tile(D_LRU), tile(D_LRU),
                  tile(D_LRU),
                  pl.BlockSpec((1, 1, 3 * D_MODEL), lambda b, j: (b, 0, 0)),
                  full(1, D_MODEL), full(D_FOURIER, D_FOURIER), full(D_MODEL, D_MODEL),
                  full(N_HEADS, HEAD_DIM, 2 * HEAD_DIM), full(1, 2 * D_LRU), full(2, D_LRU),
                  pl.BlockSpec((1, 2, D_LRU), lambda b, j: (b, 0, 0))],
        out_specs=tile(D_MODEL),
        out_shape=jax.ShapeDtypeStruct(x.shape, F32),
        scratch_shapes=[pltpu.VMEM((TILE, 2 * D_LRU), F32),
                        pltpu.VMEM((TILE, D_LRU), F32),
                        pltpu.VMEM((TILE, D_LRU), F32),
                        pltpu.VMEM((N_QL, MC * _pitch(CH), LANES), F32),
                        pltpu.VMEM((TILE, D_MODEL), BF16),
                        pltpu.VMEM((8, D_LRU), F32)],
        compiler_params=pltpu.CompilerParams(dimension_semantics=("arbitrary", "arbitrary"),
                                             vmem_limit_bytes=VMEM_LIMIT_BYTES),
        name="bwd_pass",
    )(x, yf, gfs, gls, v, hf, mod3, fg, wfour, wout, wgb, bgb, lam, fin)


def kernel(x, c, ctx, c_ctx, w_ada, b_ada, norm_gain, w_in, w_four, conv_w, conv_b, w_rg, b_rg,
           w_ig, b_ig, lam, w_out, final_gain):
    bsz = x.shape[0]
    assert x.shape == (bsz, SEQ, D_MODEL) and ctx.shape == (bsz, CTX_LEN, D_MODEL)
    assert w_ada.shape[0] == 1, "single-layer kernel"
    dftc, f1, m2 = _dft_constants()

    cc = jnp.concatenate([c, c_ctx[None, :], jnp.zeros((8 - bsz - 1, D_MODEL), F32)], axis=0)
    mod3 = _ada_mod(cc, w_ada[0], b_ada[0][None, :]).reshape(8, 1, 3 * D_MODEL)

    g = norm_gain[0][None, :]
    half_cols = jnp.concatenate([jnp.ones((D_FOURIER,), F32), jnp.full((D_FOURIER,), 0.5, F32),
                                 jnp.ones((D_LRU,), F32), jnp.full((D_LRU,), 0.5, F32)])
    win = (w_in[0] * half_cols[None, :]).astype(BF16)
    cw =jnp.broadcast_to(0.5 * conv_w[0][:, None, :], (4, MC, D_LRU))
    cb = jnp.broadcast_to(0.5 * conv_b[0][None, :], (MC, D_LRU))
    wg = jnp.concatenate([w_rg[0], w_ig[0]], axis=-1).astype(BF16)
    bg = 0.5 * jnp.concatenate([b_rg[0], b_ig[0]], axis=-1)[:, None, :]
    lam0 = lam[0]

    fin = _ctx_states(ctx, mod3, g, win, cw, cb, wg[0], bg[0], wg[1], bg[1], lam0)
    p, q, gfs, gls, v, hf = _fwd_pass(x, mod3, g, win, dftc, cw, cb, wg[0], bg[0], lam0, fin)

    yf = _position_dft(p, q, f1, m2)

    return _bwd_pass(x, yf, gfs, gls, v, hf, mod3, final_gain[None, :], w_four[0].astype(BF16),
                     w_out[0].astype(BF16), wg[1], bg[1], lam0, fin)
```

```python
import numpy as np
import jax
import jax.numpy as jnp
from jax import lax
from jax.experimental import pallas as pl
from jax.experimental.pallas import tpu as pltpu

D_MODEL = 1024
SEQ = 8192
CTX_LEN = 256
D_FOURIER = 512
D_LRU = 512
N_GROUPS = 4
GROUP_DIM = 128
N_HEADS = 4
HEAD_DIM = 128
LRU_C = 8.0
EPS = 1e-6

LANES = 128
MC = 16
CH = 64
TILE = MC * CH
N_TILES = SEQ // TILE
CH_CTX = CTX_LEN // MC
N_QL = D_LRU // LANES
LHS_ROWS = TILE + MC

DFT_N1 = 128
DFT_N2 = 64
DFT_NB = 2

VMEM_LIMIT_BYTES = 56 * 1024 * 1024
F32 = jnp.float32
BF16 = jnp.bfloat16


def _pitch(n):
    return n + 4


DFT_P1 = _pitch(DFT_N2)
DFT_P2 = _pitch(DFT_N1)


def _dft_constants():
    c = np.arange(GROUP_DIM)
    ang = 2.0 * np.pi * ((c[:, None] * c[None, :]) % GROUP_DIM) / GROUP_DIM
    dftc = np.concatenate([np.cos(ang), np.sin(ang)], axis=1) * 2.0 ** -3

    n = np.arange(DFT_N1)
    ang1 = 2.0 * np.pi * ((n[:, None] * n[None, :]) % DFT_N1) / DFT_N1
    c1, s1 = np.cos(ang1), np.sin(ang1)
    f1 = np.block([[c1, -s1], [s1, c1]]) * 2.0 ** -4

    k1 = np.arange(DFT_N1)[:, None, None]
    k2 = np.arange(DFT_N2)[None, :, None]
    n2 = np.arange(DFT_N2)[None, None, :]
    ang2 = 2.0 * np.pi * ((n2 * (k1 + DFT_N1 * k2)) % SEQ) / SEQ
    m2 = np.concatenate([np.cos(ang2), -np.sin(ang2)], axis=2) * 2.0 ** -3
    return jnp.asarray(dftc, dtype=F32), jnp.asarray(f1, dtype=F32), jnp.asarray(m2, dtype=F32)


def _silu(x):
    hx = 0.5 * x
    return hx * jnp.tanh(hx) + hx


def _silu_of_half(hx):
    return hx * jnp.tanh(hx) + hx


def _half_decay_rate(lam_row):
    z = -lam_row
    return (-0.5 * LRU_C) * (jnp.maximum(z, 0.0) + jnp.log1p(jnp.exp(-jnp.abs(z))))


def _modulate(xs, gmul, shift):
    ms = jnp.mean(xs * xs, axis=-1, keepdims=True)
    return (xs * lax.rsqrt(ms + EPS)) * gmul + shift


def _gate_ab(hpre_r, hpre_i, hv, hrate):
    a = jnp.exp(jnp.tanh(hpre_r) * hrate + hrate)
    om = 1.0 - a * a
    s = om * lax.rsqrt(jnp.maximum(om, 1e-30))
    return a, s * ((jnp.tanh(hpre_i) + 1.0) * hv)


def _ds(start, size):
    if isinstance(start, int):
        return slice(start, start + size)
    return pl.ds(pl.multiple_of(start, MC), size)


def _to_slab_order(u, up_s, uext_s, ch):
    pitch = _pitch(ch)
    for m in range(MC):
        for q in range(N_QL):
            up_s[q, m * pitch:m * pitch + ch, :] = u[m * ch:(m + 1) * ch, q * LANES:(q + 1) * LANES]
    for s in range(ch):
        for q in range(N_QL):
            uext_s[(s + 2) * MC:(s + 3) * MC, q * LANES:(q + 1) * LANES] = (
                up_s[q, pl.ds(s, MC, stride=pitch), :])


def _conv_halo(uext_s, lookahead, c_prev1, c_prev2, ch):
    last = uext_s[(ch + 1) * MC:(ch + 2) * MC, :]
    last2 = uext_s[ch * MC:(ch + 1) * MC, :]
    first = uext_s[2 * MC:3 * MC, :]
    uext_s[0:MC, :] = jnp.concatenate([c_prev2, last2[:MC - 1]], axis=0)
    uext_s[MC:2 * MC, :] = jnp.concatenate([c_prev1, last[:MC - 1]], axis=0)
    uext_s[(ch + 2) * MC:(ch + 3) * MC, :] = jnp.concatenate([first[1:], lookahead], axis=0)


def _conv_slab(uext_s, s, cw_ref, cb_ref):
    acc = cb_ref[...]
    for k in range(4):
        acc = acc + cw_ref[k] * uext_s[_ds((s + k) * MC, MC), :]
    return acc


def _gate_matmuls(vb, wg_ref, g_s, rows):
    for h in range(N_HEADS):
        res = jnp.dot(vb[:, h * HEAD_DIM:(h + 1) * HEAD_DIM], wg_ref[h], preferred_element_type=F32)
        g_s[rows, h * HEAD_DIM:(h + 1) * HEAD_DIM] = res[:, :HEAD_DIM]
        g_s[rows, D_LRU + h * HEAD_DIM:D_LRU + (h + 1) * HEAD_DIM] = res[:, HEAD_DIM:]


def _scan_init():
    return jnp.ones((MC, D_LRU), F32), jnp.zeros((MC, D_LRU), F32)


def _scan_steps(g_s, load_v, bg_ref, hrate, a_s, h_s, steps, carry):
    b_r = bg_ref[:, :D_LRU]
    b_i = bg_ref[:, D_LRU:]
    acc_a, acc_h = carry
    for s in steps:
        rows = _ds(s * MC, MC)
        pre = g_s[rows, :]
        a, bx = _gate_ab(pre[:, :D_LRU] + b_r, pre[:, D_LRU:] + b_i, load_v(rows), hrate)
        acc_a = a * acc_a
        acc_h = a * acc_h + bx
        a_s[rows, :] = acc_a
        h_s[rows, :] = acc_h
    return acc_a, acc_h


def _chunk_carries(tot_a, tot_h, carry_in, reverse):
    row_id = lax.broadcasted_iota(jnp.int32, (MC, D_LRU), 0)
    hin = jnp.zeros((MC, D_LRU), F32)
    h = carry_in
    order = range(MC - 1, -1, -1) if reverse else range(MC)
    for m in order:
        hin = jnp.where(row_id == m, h, hin)
        h = tot_a[m:m + 1] * h + tot_h[m:m + 1]
    return hin, h


def _ada_kernel(c_ref, w_ref, b_ref, o_ref):
    s = _silu(c_ref[...]).astype(BF16)
    o_ref[...] = jnp.dot(s, w_ref[...].astype(BF16), preferred_element_type=F32) + b_ref[...]


def _ada_mod(cc, w_ada, b_ada):
    nblk = D_MODEL
    return pl.pallas_call(
        _ada_kernel,
        grid=(3 * D_MODEL // nblk,),
        in_specs=[pl.BlockSpec((8, D_MODEL), lambda i: (0, 0)),
                  pl.BlockSpec((D_MODEL, nblk), lambda i: (0, i)),
                  pl.BlockSpec((1, nblk), lambda i: (0, i))],
        out_specs=pl.BlockSpec((8, nblk), lambda i: (0, i)),
        out_shape=jax.ShapeDtypeStruct((8, 3 * D_MODEL), F32),
        name="ada_mod",
    )(cc, w_ada, b_ada)


def _ctx_kernel(x_ref, mod_ref, g_ref, wlx_ref, cw_ref, cb_ref, wgf_ref, bgf_ref, wgb_ref, bgb_ref,
                lam_ref, fin_ref, up_s, uext_s, vf_s, g_s, a_s, h_s):
    ch = CH_CTX
    shift = mod_ref[0, :, 0:D_MODEL]
    scale = mod_ref[0, :, D_MODEL:2 * D_MODEL]
    gmul = g_ref[...] * (1.0 + scale)
    lhs = _modulate(x_ref[0], gmul, shift).astype(BF16)
    u = jnp.dot(lhs, wlx_ref[...], preferred_element_type=F32)
    _to_slab_order(u, up_s, uext_s, ch)
    zero_row = jnp.zeros((1, D_LRU), F32)
    _conv_halo(uext_s, zero_row, zero_row, zero_row, ch)
    for s in range(ch):
        vf_s[s * MC:(s + 1) * MC, :] = _conv_slab(uext_s, s, cw_ref, cb_ref)
    vb = vf_s[...].astype(BF16)
    for d, (wg_ref, bg_ref) in enumerate(((wgf_ref, bgf_ref), (wgb_ref, bgb_ref))):
        _gate_matmuls(vb, wg_ref, g_s, slice(0, ch * MC))
        rate = _half_decay_rate(lam_ref[d:d + 1, :])
        steps = range(ch - 1, -1, -1) if d == 1 else range(ch)
        tot_a, tot_h = _scan_steps(g_s, lambda rows: vf_s[rows, :], bg_ref, rate, a_s, h_s, steps,
                                   _scan_init())
        _, fin = _chunk_carries(tot_a, tot_h, zero_row, reverse=(d == 1))
        fin_ref[0, d:d + 1, :] = fin


def _ctx_states(ctx, mod3, g, wlx, cw, cb, wgf, bgf, wgb, bgb, lam):
    bsz = ctx.shape[0]
    n = CTX_LEN
    full = lambda *shape: pl.BlockSpec(shape, lambda b: (0,) * len(shape))
    return pl.pallas_call(
        _ctx_kernel,
        grid=(bsz,),
        in_specs=[pl.BlockSpec((1, n, D_MODEL), lambda b: (b, 0, 0)),
                  pl.BlockSpec((1, 1, 3 * D_MODEL), lambda b: (bsz, 0, 0)),
                  full(1, D_MODEL),
                  pl.BlockSpec((D_MODEL, D_LRU), lambda b: (0, 2)),
                  full(4, MC, D_LRU), full(MC, D_LRU),
                  full(N_HEADS, HEAD_DIM, 2 * HEAD_DIM), full(1, 2 * D_LRU),
                  full(N_HEADS, HEAD_DIM, 2 * HEAD_DIM), full(1, 2 * D_LRU),
                  full(2, D_LRU)],
        out_specs=pl.BlockSpec((1, 2, D_LRU), lambda b: (b, 0, 0)),
        out_shape=jax.ShapeDtypeStruct((bsz, 2, D_LRU), F32),
        scratch_shapes=[pltpu.VMEM((N_QL, MC * _pitch(CH_CTX), LANES), F32),
                        pltpu.VMEM((n + 3 * MC, D_LRU), F32),
                        pltpu.VMEM((n, D_LRU), F32),
                        pltpu.VMEM((n, 2 * D_LRU), F32),
                        pltpu.VMEM((n, D_LRU), F32),
                        pltpu.VMEM((n, D_LRU), F32)],
        compiler_params=pltpu.CompilerParams(dimension_semantics=("arbitrary",)),
        name="ctx_states",
    )(ctx, mod3, g, wlx, cw, cb, wgf, bgf, wgb, bgb, lam)


def _fwd_kernel(x_ref, xh_ref, mod_ref, g_ref, win_ref, dftc_ref, cw_ref, cb_ref, wg_ref,
                bg_ref, lam_ref, h0_ref,
                p_ref, q_ref, gfs_ref, gls_ref, v_ref, hf_ref,
                lhs_s, up_s, uext_s, vf_s, g_s, a_s, h_s, cc_s, hc_s):
    j = pl.program_id(1)

    @pl.when(j == 0)
    def _():
        cc_s[...] = jnp.zeros_like(cc_s)
        hc_s[...] = jnp.broadcast_to(h0_ref[0, 0:1, :], hc_s.shape)

    shift = mod_ref[0, :, 0:D_MODEL]
    scale = mod_ref[0, :, D_MODEL:2 * D_MODEL]
    gmul = g_ref[...] * (1.0 + scale)
    for m in range(MC):
        rows = slice(m * CH, (m + 1) * CH)
        lhs_s[rows, :] = _modulate(x_ref[0, rows, :], gmul, shift).astype(BF16)
    hh = _modulate(xh_ref[0], gmul, shift)
    lhs_s[TILE:LHS_ROWS, :] = jnp.concatenate([hh, jnp.zeros_like(hh)], axis=0).astype(BF16)

    u = jnp.dot(lhs_s[...], win_ref[:, 2 * D_FOURIER:2 * D_FOURIER + D_LRU],
                preferred_element_type=F32)

    _to_slab_order(u, up_s, uext_s, CH)
    look = jnp.where(j == N_TILES - 1, 0.0, u[TILE:TILE + 1])
    _conv_halo(uext_s, look, cc_s[0:1, :], cc_s[1:2, :], CH)
    cc_s[0:1, :] = u[TILE - 1:TILE]
    cc_s[1:2, :] = u[TILE - 2:TILE - 1]

    zf = jnp.dot(lhs_s[0:TILE, :], win_ref[:, 0:D_FOURIER],
                 preferred_element_type=F32).astype(BF16)
    dftc = dftc_ref[...].astype(BF16)
    for g in range(N_GROUPS):
        cols = slice(g * GROUP_DIM, (g + 1) * GROUP_DIM)
        pq = jnp.dot(zf[:, cols], dftc, preferred_element_type=F32).astype(BF16)
        p_ref[0, :, cols] = pq[:, :GROUP_DIM]
        q_ref[0, :, cols] = pq[:, GROUP_DIM:]

    for s in range(CH):
        rows = slice(s * MC, (s + 1) * MC)
        v = _conv_slab(uext_s, s, cw_ref, cb_ref)
        vf_s[rows, :] = v
        v_ref[0, rows, :] = v.astype(BF16)
    _gate_matmuls(v_ref[0], wg_ref, g_s, slice(0, TILE))

    for c0, o_ref in ((D_FOURIER, gfs_ref), (2 * D_FOURIER + D_LRU, gls_ref)):
        o_ref[0] = _silu_of_half(jnp.dot(lhs_s[0:TILE, :], win_ref[:, c0:c0 + D_LRU],
                                         preferred_element_type=F32)).astype(BF16)

    hrate = _half_decay_rate(lam_ref[0:1, :])
    tot_a, tot_h = _scan_steps(g_s, lambda rws: vf_s[rws, :], bg_ref, hrate, a_s, h_s,
                               range(CH), _scan_init())
    hin, h_out = _chunk_carries(tot_a, tot_h, hc_s[0:1, :], reverse=False)
    hc_s[...] = jnp.broadcast_to(h_out, hc_s.shape)
    for s in range(CH):
        rows = slice(s * MC, (s + 1) * MC)
        hf_ref[0, rows, :] = (h_s[rows, :] + a_s[rows, :] * hin).astype(BF16)


def _fwd_pass(x, mod3, g, win, dftc, cw, cb, wgf, bgf, lam, fin):
    bsz = x.shape[0]
    full = lambda *shape: pl.BlockSpec(shape, lambda b, j: (0,) * len(shape))
    tile = lambda c: pl.BlockSpec((1, TILE, c), lambda b, j: (b, j, 0))
    shp = lambda c: jax.ShapeDtypeStruct((bsz, SEQ, c), BF16)
    rows8 = TILE // 8
    return pl.pallas_call(
        _fwd_kernel,
        grid=(bsz, N_TILES),
        in_specs=[tile(D_MODEL),
                  pl.BlockSpec((1, 8, D_MODEL),
                               lambda b, j: (b, jnp.minimum((j + 1) * rows8, SEQ // 8 - 1), 0)),
                  pl.BlockSpec((1, 1, 3 * D_MODEL), lambda b, j: (b, 0, 0)),
                  full(1, D_MODEL), full(D_MODEL, 2 * D_MODEL), full(GROUP_DIM, 2 * GROUP_DIM),
                  full(4, MC, D_LRU), full(MC, D_LRU),
                  full(N_HEADS, HEAD_DIM, 2 * HEAD_DIM), full(1, 2 * D_LRU),
                  full(2, D_LRU),
                  pl.BlockSpec((1, 2, D_LRU), lambda b, j: (b, 0, 0))],
        out_specs=[tile(D_FOURIER), tile(D_FOURIER), tile(D_FOURIER), tile(D_LRU), tile(D_LRU),
                   tile(D_LRU)],
        out_shape=[shp(D_FOURIER), shp(D_FOURIER), shp(D_FOURIER), shp(D_LRU), shp(D_LRU),
                   shp(D_LRU)],
        scratch_shapes=[pltpu.VMEM((LHS_ROWS, D_MODEL), BF16),
                        pltpu.VMEM((N_QL, MC * _pitch(CH), LANES), F32),
                        pltpu.VMEM((TILE + 3 * MC, D_LRU), F32),
                        pltpu.VMEM((TILE, D_LRU), F32),
                        pltpu.VMEM((TILE, 2 * D_LRU), F32),
                        pltpu.VMEM((TILE, D_LRU), F32),
                        pltpu.VMEM((TILE, D_LRU), F32),
                        pltpu.VMEM((8, D_LRU), F32),
                        pltpu.VMEM((8, D_LRU), F32)],
        compiler_params=pltpu.CompilerParams(dimension_semantics=("arbitrary", "arbitrary"),
                                             vmem_limit_bytes=VMEM_LIMIT_BYTES),
        name="fwd_pass",
    )(x, x, mod3, g, win, dftc, cw, cb, wgf, bgf, lam, fin)


def _dft_kernel(p_ref, q_ref, f1_ref, m2_ref, o_ref, sp_s, sq_s, yr_s, yj_s, so_s):
    p1 = DFT_P1
    p2 = DFT_P2
    for n1 in range(DFT_N1):
        rows = slice(n1 * DFT_N2, (n1 + 1) * DFT_N2)
        sp_s[n1 * p1:n1 * p1 + DFT_N2, :] = p_ref[0, rows, :].astype(F32)
        sq_s[n1 * p1:n1 * p1 + DFT_N2, :] = q_ref[0, rows, :].astype(F32)
    f1 = f1_ref[...].astype(BF16)
    for n2 in range(0, DFT_N2, DFT_NB):
        rhs = jnp.concatenate(
            [jnp.concatenate([sp_s[pl.ds(n2 + i, DFT_N1, stride=p1), :],
                              sq_s[pl.ds(n2 + i, DFT_N1, stride=p1), :]], axis=0)
             for i in range(DFT_NB)], axis=1).astype(BF16)
        y = jnp.dot(f1, rhs, preferred_element_type=F32)
        for i in range(DFT_NB):
            lanes = slice(i * LANES, (i + 1) * LANES)
            yr_s[(n2 + i) * p2:(n2 + i) * p2 + DFT_N1, :] = y[:DFT_N1, lanes]
            yj_s[(n2 + i) * p2:(n2 + i) * p2 + DFT_N1, :] = y[DFT_N1:, lanes]
    for k1 in range(DFT_N1):
        rhs = jnp.concatenate([yr_s[pl.ds(k1, DFT_N2, stride=p2), :],
                               yj_s[pl.ds(k1, DFT_N2, stride=p2), :]], axis=0).astype(BF16)
        so_s[pl.ds(k1, DFT_N2, stride=p2), :] = jnp.dot(m2_ref[k1].astype(BF16), rhs,
                                                         preferred_element_type=F32)
    for k2 in range(DFT_N2):
        o_ref[0, k2 * DFT_N1:(k2 + 1) * DFT_N1, :] = so_s[k2 * p2:k2 * p2 + DFT_N1, :].astype(BF16)


def _position_dft(p, q, f1, m2):
    bsz = p.shape[0]
    blk = pl.BlockSpec((1, SEQ, LANES), lambda b, i: (b, 0, i))
    pad1 = DFT_N1 * DFT_P1
    pad2 = DFT_N2 * DFT_P2
    return pl.pallas_call(
        _dft_kernel,
        grid=(bsz, D_FOURIER // LANES),
        in_specs=[blk, blk,
                  pl.BlockSpec((2 * DFT_N1, 2 * DFT_N1), lambda b, i: (0, 0)),
                  pl.BlockSpec((DFT_N1, DFT_N2, 2 * DFT_N2), lambda b, i: (0, 0, 0))],
        out_specs=blk,
        out_shape=jax.ShapeDtypeStruct(p.shape, BF16),
        scratch_shapes=[pltpu.VMEM((pad1, LANES), F32), pltpu.VMEM((pad1, LANES), F32),
                        pltpu.VMEM((pad2, LANES), F32), pltpu.VMEM((pad2, LANES), F32),
                        pltpu.VMEM((pad2, LANES), F32)],
        compiler_params=pltpu.CompilerParams(dimension_semantics=("arbitrary", "arbitrary"),
                                             vmem_limit_bytes=VMEM_LIMIT_BYTES),
        name="position_dft",
    )(p, q, f1, m2)


def _bwd_kernel(x_ref, yf_ref, gfs_ref, gls_ref, v_ref, hf_ref, mod_ref, fg_ref, wfour_ref, wout_ref,
                wg_ref, bg_ref, lam_ref, h0_ref, o_ref,
                g_s, a_s, h_s, yp_s, lhs_s, hc_s):
    j = pl.program_id(1)

    @pl.when(j == 0)
    def _():
        hc_s[...] = jnp.broadcast_to(h0_ref[0, 1:2, :], hc_s.shape)

    _gate_matmuls(v_ref[0], wg_ref, g_s, slice(0, TILE))
    yfw = jnp.dot(yf_ref[0], wfour_ref[...], preferred_element_type=F32)
    lhs_s[:, 0:D_FOURIER] = (yfw * gfs_ref[0].astype(F32)).astype(BF16)

    hrate = _half_decay_rate(lam_ref[1:2, :])
    carry = _scan_steps(g_s, lambda rws: v_ref[0, rws, :].astype(F32), bg_ref, hrate, a_s, h_s,
                        range(CH - 1, -1, -1), _scan_init())
    tot_a, tot_h = carry
    hin, h_out = _chunk_carries(tot_a, tot_h, hc_s[0:1, :], reverse=True)
    hc_s[...] = jnp.broadcast_to(h_out, hc_s.shape)
    pitch = _pitch(CH)
    for s in range(CH):
        rows = slice(s * MC, (s + 1) * MC)
        yl = h_s[rows, :] + a_s[rows, :] * hin + hf_ref[0, rows, :].astype(F32)
        for q in range(N_QL):
            yp_s[q, pl.ds(s, MC, stride=pitch), :] = yl[:, q * LANES:(q + 1) * LANES]
    for m in range(MC):
        rows = slice(m * CH, (m + 1) * CH)
        for q in range(N_QL):
            cols = slice(q * LANES, (q + 1) * LANES)
            yl = yp_s[q, m * pitch:m * pitch + CH, :]
            lhs_s[rows, D_FOURIER + q * LANES:D_FOURIER + (q + 1) * LANES] = (
                yl * gls_ref[0, rows, cols].astype(F32)).astype(BF16)

    res_gate = mod_ref[0, :, 2 * D_MODEL:3 * D_MODEL]
    fg = fg_ref[...]
    proj = jnp.dot(lhs_s[...], wout_ref[...], preferred_element_type=F32)
    for m in range(MC):
        rows = slice(m * CH, (m + 1) * CH)
        res = x_ref[0, rows, :] + res_gate * proj[rows, :]
        ms = jnp.mean(res * res, axis=-1, keepdims=True)
        o_ref[0, rows, :] = (res * lax.rsqrt(ms + EPS)) * fg


def _bwd_pass(x, yf, gfs, gls, v, hf, mod3, fg, wfour, wout, wgb, bgb, lam, fin):
    bsz = x.shape[0]
    full = lambda *shape: pl.BlockSpec(shape, lambda b, j: (0,) * len(shape))
    tile = lambda c: pl.BlockSpec((1, TILE, c), lambda b, j: (b, N_TILES - 1 - j, 0))
    return pl.pallas_call(
        _bwd_kernel,
        grid=(bsz, N_TILES),
        in_specs=[tile(D_MODEL), tile(D_FOURIER), tile(D_FOURIER), tile(D_LRU), tile(D_LRU),
                  tile(D_LRU),
                  pl.BlockSpec((1, 1, 3 * D_MODEL), lambda b, j: (b, 0, 0)),
                  full(1, D_MODEL), full(D_FOURIER, D_FOURIER), full(D_MODEL, D_MODEL),
                  full(N_HEADS, HEAD_DIM, 2 * HEAD_DIM), full(1, 2 * D_LRU), full(2, D_LRU),
                  pl.BlockSpec((1, 2, D_LRU), lambda b, j: (b, 0, 0))],
        out_specs=tile(D_MODEL),
        out_shape=jax.ShapeDtypeStruct(x.shape, F32),
        scratch_shapes=[pltpu.VMEM((TILE, 2 * D_LRU), F32),
                        pltpu.VMEM((TILE, D_LRU), F32),
                        pltpu.VMEM((TILE, D_LRU), F32),
                        pltpu.VMEM((N_QL, MC * _pitch(CH), LANES), F32),
                        pltpu.VMEM((TILE, D_MODEL), BF16),
                        pltpu.VMEM((8, D_LRU), F32)],
        compiler_params=pltpu.CompilerParams(dimension_semantics=("arbitrary", "arbitrary"),
                                             vmem_limit_bytes=VMEM_LIMIT_BYTES),
        name="bwd_pass",
    )(x, yf, gfs, gls, v, hf, mod3, fg, wfour, wout, wgb, bgb, lam, fin)


def kernel(x, c, ctx, c_ctx, w_ada, b_ada, norm_gain, w_in, w_four, conv_w, conv_b, w_rg, b_rg,
           w_ig, b_ig, lam, w_out, final_gain):
    bsz = x.shape[0]
    assert x.shape == (bsz, SEQ, D_MODEL) and ctx.shape == (bsz, CTX_LEN, D_MODEL)
    assert w_ada.shape[0] == 1, "single-layer kernel"
    dftc, f1, m2 = _dft_constants()

    cc = jnp.concatenate([c, c_ctx[None, :], jnp.zeros((8 - bsz - 1, D_MODEL), F32)], axis=0)
    mod3 = _ada_mod(cc, w_ada[0], b_ada[0][None, :]).reshape(8, 1, 3 * D_MODEL)

    g = norm_gain[0][None, :]
    half_cols = jnp.concatenate([jnp.ones((D_FOURIER,), F32), jnp.full((D_FOURIER,), 0.5, F32),
                                 jnp.ones((D_LRU,), F32), jnp.full((D_LRU,), 0.5, F32)])
    win = (w_in[0] * half_cols[None, :]).astype(BF16)
    cw =jnp.broadcast_to(0.5 * conv_w[0][:, None, :], (4, MC, D_LRU))
    cb = jnp.broadcast_to(0.5 * conv_b[0][None, :], (MC, D_LRU))
    wg = jnp.concatenate([w_rg[0], w_ig[0]], axis=-1).astype(BF16)
    bg = 0.5 * jnp.concatenate([b_rg[0], b_ig[0]], axis=-1)[:, None, :]
    lam0 = lam[0]

    fin = _ctx_states(ctx, mod3, g, win, cw, cb, wg[0], bg[0], wg[1], bg[1], lam0)
    p, q, gfs, gls, v, hf = _fwd_pass(x, mod3, g, win, dftc, cw, cb, wg[0], bg[0], lam0, fin)

    yf = _position_dft(p, q, f1, m2)

    return _bwd_pass(x, yf, gfs, gls, v, hf, mod3, final_gain[None, :], w_four[0].astype(BF16),
                     w_out[0].astype(BF16), wg[1], bg[1], lam0, fin)
```

```python
import functools

import numpy as np
import jax
import jax.numpy as jnp
from jax import lax
from jax.experimental import pallas as pl
from jax.experimental.pallas import tpu as pltpu

D_MODEL = 1024
SEQ = 8192
CTX_LEN = 256
D_FOURIER = 512
D_LRU = 512
N_GROUPS = 4
GROUP_DIM = 128
N_HEADS = 4
HEAD_DIM = 128
LRU_C = 8.0
EPS = 1e-6

LANES = 128
MC = 16
CH = 64
TILE = MC * CH
N_TILES = SEQ // TILE
CH_CTX = CTX_LEN // MC
N_QL = D_LRU // LANES
LHS_ROWS = TILE + MC
DFT_N1 = 128
DFT_N2 = 64
DFT_NB = 2

VMEM_LIMIT_BYTES = 56 * 1024 * 1024
F32 = jnp.float32
BF16 = jnp.bfloat16


def _pitch(n):
    return n + 4


DFT_P1 = _pitch(DFT_N2)
DFT_P2 = _pitch(DFT_N1)


def _dft_constants():
    c = np.arange(GROUP_DIM)
    ang = 2.0 * np.pi * ((c[:, None] * c[None, :]) % GROUP_DIM) / GROUP_DIM
    dftc = np.concatenate([np.cos(ang), np.sin(ang)], axis=1) * 2.0 ** -3

    n = np.arange(DFT_N1)
    ang1 = 2.0 * np.pi * ((n[:, None] * n[None, :]) % DFT_N1) / DFT_N1
    c1, s1 = np.cos(ang1), np.sin(ang1)
    f1 = np.block([[c1, -s1], [s1, c1]]) * 2.0 ** -4

    k1 = np.arange(DFT_N1)[:, None, None]
    k2 = np.arange(DFT_N2)[None, :, None]
    n2 = np.arange(DFT_N2)[None, None, :]
    ang2 = 2.0 * np.pi * ((n2 * (k1 + DFT_N1 * k2)) % SEQ) / SEQ
    m2 = np.concatenate([np.cos(ang2), -np.sin(ang2)], axis=2) * 2.0 ** -3
    return jnp.asarray(dftc, dtype=F32), jnp.asarray(f1, dtype=F32), jnp.asarray(m2, dtype=F32)


def _silu(x):
    hx = 0.5 * x
    return hx * jnp.tanh(hx) + hx


def _silu_of_half(hx):
    return hx * jnp.tanh(hx) + hx


def _half_decay_rate(lam_row):
    z = -lam_row
    return (-0.5 * LRU_C) * (jnp.maximum(z, 0.0) + jnp.log1p(jnp.exp(-jnp.abs(z))))


def _modulate(xs, gmul, shift):
    ms = jnp.mean(xs * xs, axis=-1, keepdims=True)
    return (xs * lax.rsqrt(ms + EPS)) * gmul + shift


def _gate_ab(hpre_r, hpre_i, hv, hrate):
    a = jnp.exp(jnp.tanh(hpre_r) * hrate + hrate)
    om = 1.0 - a * a
    s = om * lax.rsqrt(jnp.maximum(om, 1e-30))
    return a, s * ((jnp.tanh(hpre_i) + 1.0) * hv)


def _ds(start, size):
    if isinstance(start, int):
        return slice(start, start + size)
    return pl.ds(pl.multiple_of(start, MC), size)


def _to_slab_order(u, up_s, uext_s, ch):
    pitch = _pitch(ch)
    for m in range(MC):
        for q in range(N_QL):
            up_s[q, m * pitch:m * pitch + ch, :] = u[m * ch:(m + 1) * ch, q * LANES:(q + 1) * LANES]
    for s in range(ch):
        for q in range(N_QL):
            uext_s[(s + 2) * MC:(s + 3) * MC, q * LANES:(q + 1) * LANES] = (
                up_s[q, pl.ds(s, MC, stride=pitch), :])


def _conv_halo(uext_s, lookahead, c_prev1, c_prev2, ch):
    last = uext_s[(ch + 1) * MC:(ch + 2) * MC, :]
    last2 = uext_s[ch * MC:(ch + 1) * MC, :]
    first = uext_s[2 * MC:3 * MC, :]
    uext_s[0:MC, :] = jnp.concatenate([c_prev2, last2[:MC - 1]], axis=0)
    uext_s[MC:2 * MC, :] = jnp.concatenate([c_prev1, last[:MC - 1]], axis=0)
    uext_s[(ch + 2) * MC:(ch + 3) * MC, :] = jnp.concatenate([first[1:], lookahead], axis=0)


def _conv_slab(uext_s, s, cw_ref, cb_ref):
    acc = cb_ref[...]
    for k in range(4):
        acc = acc + cw_ref[k] * uext_s[_ds((s + k) * MC, MC), :]
    return acc


def _gate_matmuls(vb, wg_ref, g_s, rows):
    for h in range(N_HEADS):
        res = jnp.dot(vb[:, h * HEAD_DIM:(h + 1) * HEAD_DIM], wg_ref[h], preferred_element_type=F32)
        g_s[rows, h * HEAD_DIM:(h + 1) * HEAD_DIM] = res[:, :HEAD_DIM]
        g_s[rows, D_LRU + h * HEAD_DIM:D_LRU + (h + 1) * HEAD_DIM] = res[:, HEAD_DIM:]


def _scan_init():
    return jnp.ones((MC, D_LRU), F32), jnp.zeros((MC, D_LRU), F32)


def _scan_steps(g_s, load_v, bg_ref, hrate, a_s, h_s, steps, carry):
    b_r = bg_ref[:, :D_LRU]
    b_i = bg_ref[:, D_LRU:]
    acc_a, acc_h = carry
    for s in steps:
        rows = _ds(s * MC, MC)
        pre = g_s[rows, :]
        a, bx = _gate_ab(pre[:, :D_LRU] + b_r, pre[:, D_LRU:] + b_i, load_v(rows), hrate)
        acc_a = a * acc_a
        acc_h = a * acc_h + bx
        a_s[rows, :] = acc_a
        h_s[rows, :] = acc_h
    return acc_a, acc_h


def _chunk_carries(tot_a, tot_h, carry_in, reverse):
    row_id = lax.broadcasted_iota(jnp.int32, (MC, D_LRU), 0)
    hin = jnp.zeros((MC, D_LRU), F32)
    h = carry_in
    order = range(MC - 1, -1, -1) if reverse else range(MC)
    for m in order:
        hin = jnp.where(row_id == m, h, hin)
        h = tot_a[m:m + 1] * h + tot_h[m:m + 1]
    return hin, h


def _prep_kernel(c_ref, wada_ref, bada_ref, win_ref, wout_ref, wfour_ref, wrg_ref, wig_ref, brg_ref,
                 big_ref, cw_ref, cb_ref,
                 mod_ref, win_o, wout_o, wfour_o, wg_o, bg_o, cwr_o, cbr_o):
    i = pl.program_id(0)
    s = _silu(c_ref[...]).astype(BF16)
    mod_ref[...] = (jnp.dot(s, wada_ref[0].astype(BF16), preferred_element_type=F32)
                    + bada_ref[...])

    @pl.when(i < 2)
    def _():
        win_o[:, 0:D_FOURIER] = win_ref[0, :, 0:D_FOURIER].astype(BF16)
        win_o[:, D_FOURIER:] = (0.5 * win_ref[0, :, D_FOURIER:]).astype(BF16)

    @pl.when(i == 2)
    def _():
        wout_o[...] = wout_ref[0].astype(BF16)
        wfour_o[...] = wfour_ref[0].astype(BF16)
        for d in range(2):
            for h in range(N_HEADS):
                wg_o[d, h] = jnp.concatenate([wrg_ref[0, d, h], wig_ref[0, d, h]],
                                             axis=1).astype(BF16)
            bg_o[d] = 0.5 * jnp.concatenate([brg_ref[0, d:d + 1, :], big_ref[0, d:d + 1, :]], axis=1)
        for k in range(4):
            cwr_o[k] = jnp.broadcast_to(0.5 * cw_ref[0, k:k + 1, :], (MC, D_LRU))
        cbr_o[...] = jnp.broadcast_to(0.5 * cb_ref[...], (MC, D_LRU))


def _prep(cc, w_ada, b_ada, w_in, w_out, w_four, w_rg, w_ig, b_rg, b_ig, conv_w, conv_b):
    full = lambda *shape: pl.BlockSpec(shape, lambda i: (0,) * len(shape))
    half = lambda i: jnp.minimum(i, 1)
    gshape = (1, 2, N_HEADS, HEAD_DIM, HEAD_DIM)
    return pl.pallas_call(
        _prep_kernel,
        grid=(3,),
        in_specs=[full(8, D_MODEL),
                  pl.BlockSpec((1, D_MODEL, D_MODEL), lambda i: (0, 0, i)),
                  pl.BlockSpec((1, D_MODEL), lambda i: (0, i)),
                  pl.BlockSpec((1, D_MODEL, D_MODEL), lambda i: (0, 0, half(i))),
                  full(1, D_MODEL, D_MODEL), full(1, D_FOURIER, D_FOURIER),
                  full(*gshape), full(*gshape), full(1, 2, D_LRU), full(1, 2, D_LRU),
                  full(1, 4, D_LRU), full(1, D_LRU)],
        out_specs=[pl.BlockSpec((8, D_MODEL), lambda i: (0, i)),
                   pl.BlockSpec((D_MODEL, D_MODEL), lambda i: (0, half(i))),
                   full(D_MODEL, D_MODEL), full(D_FOURIER, D_FOURIER),
                   full(2, N_HEADS, HEAD_DIM, 2 * HEAD_DIM), full(2, 1, 2 * D_LRU),
                   full(4, MC, D_LRU), full(MC, D_LRU)],
        out_shape=[jax.ShapeDtypeStruct((8, 3 * D_MODEL), F32),
                   jax.ShapeDtypeStruct((D_MODEL, 2 * D_MODEL), BF16),
                   jax.ShapeDtypeStruct((D_MODEL, D_MODEL), BF16),
                   jax.ShapeDtypeStruct((D_FOURIER, D_FOURIER), BF16),
                   jax.ShapeDtypeStruct((2, N_HEADS, HEAD_DIM, 2 * HEAD_DIM), BF16),
                   jax.ShapeDtypeStruct((2, 1, 2 * D_LRU), F32),
                   jax.ShapeDtypeStruct((4, MC, D_LRU), F32),
                   jax.ShapeDtypeStruct((MC, D_LRU), F32)],
        compiler_params=pltpu.CompilerParams(dimension_semantics=("arbitrary",),
                                             vmem_limit_bytes=VMEM_LIMIT_BYTES),
        name="prep",
    )(cc, w_ada, b_ada, w_in, w_out, w_four, w_rg, w_ig, b_rg, b_ig, conv_w, conv_b)


def _ctx_kernel(x_ref, mod_ref, g_ref, wlx_ref, cw_ref, cb_ref, wgf_ref, bgf_ref, wgb_ref, bgb_ref,
                lam_ref, fin_ref, up_s, uext_s, vf_s, g_s, a_s, h_s, *, mod_row):
    ch = CH_CTX
    shift = mod_ref[mod_row:mod_row + 1, 0:D_MODEL]
    scale = mod_ref[mod_row:mod_row + 1, D_MODEL:2 * D_MODEL]
    gmul = g_ref[...] * (1.0 + scale)
    lhs = _modulate(x_ref[0], gmul, shift).astype(BF16)
    u = jnp.dot(lhs, wlx_ref[...], preferred_element_type=F32)
    _to_slab_order(u, up_s, uext_s, ch)
    zero_row = jnp.zeros((1, D_LRU), F32)
    _conv_halo(uext_s, zero_row, zero_row, zero_row, ch)
    for s in range(ch):
        vf_s[s * MC:(s + 1) * MC, :] = _conv_slab(uext_s, s, cw_ref, cb_ref)
    vb = vf_s[...].astype(BF16)
    for d, (wg_ref, bg_ref) in enumerate(((wgf_ref, bgf_ref), (wgb_ref, bgb_ref))):
        _gate_matmuls(vb, wg_ref, g_s, slice(0, ch * MC))
        rate = _half_decay_rate(lam_ref[d:d + 1, :])
        steps = range(ch - 1, -1, -1) if d == 1 else range(ch)
        tot_a, tot_h = _scan_steps(g_s, lambda rows: vf_s[rows, :], bg_ref, rate, a_s, h_s, steps,
                                   _scan_init())
        _, fin = _chunk_carries(tot_a, tot_h, zero_row, reverse=(d == 1))
        fin_ref[0, d:d + 1, :] = fin


def _gate_specs(direction):
    return (pl.BlockSpec((None, N_HEADS, HEAD_DIM, 2 * HEAD_DIM), lambda *_: (direction, 0, 0, 0)),
            pl.BlockSpec((None, 1, 2 * D_LRU), lambda *_: (direction, 0, 0)))


def _ctx_states(ctx, mod, g, win, cw, cb, wg, bg, lam):
    bsz = ctx.shape[0]
    n = CTX_LEN
    full = lambda *shape: pl.BlockSpec(shape, lambda b: (0,) * len(shape))
    return pl.pallas_call(
        functools.partial(_ctx_kernel, mod_row=bsz),
        grid=(bsz,),
        in_specs=[pl.BlockSpec((1, n, D_MODEL), lambda b: (b, 0, 0)),
                  full(8, 3 * D_MODEL),
                  full(1, D_MODEL),
                  pl.BlockSpec((D_MODEL, D_LRU), lambda b: (0, 2)),
                  full(4, MC, D_LRU), full(MC, D_LRU),
                  *_gate_specs(0), *_gate_specs(1),
                  full(2, D_LRU)],
        out_specs=pl.BlockSpec((1, 2, D_LRU), lambda b: (b, 0, 0)),
        out_shape=jax.ShapeDtypeStruct((bsz, 2, D_LRU), F32),
        scratch_shapes=[pltpu.VMEM((N_QL, MC * _pitch(CH_CTX), LANES), F32),
                        pltpu.VMEM((n + 3 * MC, D_LRU), F32),
                        pltpu.VMEM((n, D_LRU), F32),
                        pltpu.VMEM((n, 2 * D_LRU), F32),
                        pltpu.VMEM((n, D_LRU), F32),
                        pltpu.VMEM((n, D_LRU), F32)],
        compiler_params=pltpu.CompilerParams(dimension_semantics=("arbitrary",)),
        name="ctx_states",
    )(ctx, mod, g, win, cw, cb, wg, bg, wg, bg, lam)


def _fwd_kernel(x_ref, xh_ref, mod_ref, g_ref, win_ref, dftc_ref, cw_ref, cb_ref, wg_ref,
                bg_ref, lam_ref, h0_ref,
                p_ref, q_ref, gfs_ref, gls_ref, v_ref, hf_ref,
                lhs_s, up_s, uext_s, vf_s, g_s, a_s, h_s, cc_s, hc_s):
    j = pl.program_id(1)

    @pl.when(j == 0)
    def _():
        cc_s[...] = jnp.zeros_like(cc_s)
        hc_s[...] = jnp.broadcast_to(h0_ref[0, 0:1, :], hc_s.shape)

    brow = pl.ds(pl.program_id(0), 1)
    shift = mod_ref[brow, 0:D_MODEL]
    scale = mod_ref[brow, D_MODEL:2 * D_MODEL]
    gmul = g_ref[...] * (1.0 + scale)
    for m in range(MC):
        rows = slice(m * CH, (m + 1) * CH)
        lhs_s[rows, :] = _modulate(x_ref[0, rows, :], gmul, shift).astype(BF16)
    hh = _modulate(xh_ref[0], gmul, shift)
    lhs_s[TILE:LHS_ROWS, :] = jnp.concatenate([hh, jnp.zeros_like(hh)], axis=0).astype(BF16)

    u = jnp.dot(lhs_s[...], win_ref[:, 2 * D_FOURIER:2 * D_FOURIER + D_LRU],
                preferred_element_type=F32)

    _to_slab_order(u, up_s, uext_s, CH)
    look = jnp.where(j == N_TILES - 1, 0.0, u[TILE:TILE + 1])
    _conv_halo(uext_s, look, cc_s[0:1, :], cc_s[1:2, :], CH)
    cc_s[0:1, :] = u[TILE - 1:TILE]
    cc_s[1:2, :] = u[TILE - 2:TILE - 1]

    zf = jnp.dot(lhs_s[0:TILE, :], win_ref[:, 0:D_FOURIER],
                 preferred_element_type=F32).astype(BF16)
    dftc = dftc_ref[...].astype(BF16)
    for g in range(N_GROUPS):
        cols = slice(g * GROUP_DIM, (g + 1) * GROUP_DIM)
        pq = jnp.dot(zf[:, cols], dftc, preferred_element_type=F32).astype(BF16)
        p_ref[0, :, cols] = pq[:, :GROUP_DIM]
        q_ref[0, :, cols] = pq[:, GROUP_DIM:]

    for s in range(CH):
        rows = slice(s * MC, (s + 1) * MC)
        v = _conv_slab(uext_s, s, cw_ref, cb_ref)
        vf_s[rows, :] = v
        v_ref[0, rows, :] = v.astype(BF16)
    _gate_matmuls(v_ref[0], wg_ref, g_s, slice(0, TILE))

    for c0, o_ref in ((D_FOURIER, gfs_ref), (2 * D_FOURIER + D_LRU, gls_ref)):
        o_ref[0] = _silu_of_half(jnp.dot(lhs_s[0:TILE, :], win_ref[:, c0:c0 + D_LRU],
                                         preferred_element_type=F32)).astype(BF16)

    hrate = _half_decay_rate(lam_ref[0:1, :])
    tot_a, tot_h = _scan_steps(g_s, lambda rws: vf_s[rws, :], bg_ref, hrate, a_s, h_s,
                               range(CH), _scan_init())
    hin, h_out = _chunk_carries(tot_a, tot_h, hc_s[0:1, :], reverse=False)
    hc_s[...] = jnp.broadcast_to(h_out, hc_s.shape)
    for s in range(CH):
        rows = slice(s * MC, (s + 1) * MC)
        hf_ref[0, rows, :] = (h_s[rows, :] + a_s[rows, :] * hin).astype(BF16)


def _fwd_pass(x, mod, g, win, dftc, cw, cb, wg, bg, lam, fin):
    bsz = x.shape[0]
    full = lambda *shape: pl.BlockSpec(shape, lambda b, j: (0,) * len(shape))
    tile = lambda c: pl.BlockSpec((1, TILE, c), lambda b, j: (b, j, 0))
    shp = lambda c: jax.ShapeDtypeStruct((bsz, SEQ, c), BF16)
    rows8 = TILE // 8
    return pl.pallas_call(
        _fwd_kernel,
        grid=(bsz, N_TILES),
        in_specs=[tile(D_MODEL),
                  pl.BlockSpec((1, 8, D_MODEL),
                               lambda b, j: (b, jnp.minimum((j + 1) * rows8, SEQ // 8 - 1), 0)),
                  full(8, 3 * D_MODEL),
                  full(1, D_MODEL), full(D_MODEL, 2 * D_MODEL), full(GROUP_DIM, 2 * GROUP_DIM),
                  full(4, MC, D_LRU), full(MC, D_LRU),
                  *_gate_specs(0),
                  full(2, D_LRU),
                  pl.BlockSpec((1, 2, D_LRU), lambda b, j: (b, 0, 0))],
        out_specs=[tile(D_FOURIER), tile(D_FOURIER), tile(D_FOURIER), tile(D_LRU), tile(D_LRU),
                   tile(D_LRU)],
        out_shape=[shp(D_FOURIER), shp(D_FOURIER), shp(D_FOURIER), shp(D_LRU), shp(D_LRU),
                   shp(D_LRU)],
        scratch_shapes=[pltpu.VMEM((LHS_ROWS, D_MODEL), BF16),
                        pltpu.VMEM((N_QL, MC * _pitch(CH), LANES), F32),
                        pltpu.VMEM((TILE + 3 * MC, D_LRU), F32),
                        pltpu.VMEM((TILE, D_LRU), F32),
                        pltpu.VMEM((TILE, 2 * D_LRU), F32),
                        pltpu.VMEM((TILE, D_LRU), F32),
                        pltpu.VMEM((TILE, D_LRU), F32),
                        pltpu.VMEM((8, D_LRU), F32),
                        pltpu.VMEM((8, D_LRU), F32)],
        compiler_params=pltpu.CompilerParams(dimension_semantics=("arbitrary", "arbitrary"),
                                             vmem_limit_bytes=VMEM_LIMIT_BYTES),
        name="fwd_pass",
    )(x, x, mod, g, win, dftc, cw, cb, wg, bg, lam, fin)


def _dft_kernel(p_ref, q_ref, f1_ref, m2_ref, o_ref, sp_s, sq_s, yr_s, yj_s, so_s):
    p1 = DFT_P1
    p2 = DFT_P2
    for n1 in range(DFT_N1):
        rows = slice(n1 * DFT_N2, (n1 + 1) * DFT_N2)
        sp_s[n1 * p1:n1 * p1 + DFT_N2, :] = p_ref[0, rows, :].astype(F32)
        sq_s[n1 * p1:n1 * p1 + DFT_N2, :] = q_ref[0, rows, :].astype(F32)
    f1 = f1_ref[...].astype(BF16)
    for n2 in range(0, DFT_N2, DFT_NB):
        rhs = jnp.concatenate(
            [jnp.concatenate([sp_s[pl.ds(n2 + i, DFT_N1, stride=p1), :],
                              sq_s[pl.ds(n2 + i, DFT_N1, stride=p1), :]], axis=0)
             for i in range(DFT_NB)], axis=1).astype(BF16)
        y = jnp.dot(f1, rhs, preferred_element_type=F32)
        for i in range(DFT_NB):
            lanes = slice(i * LANES, (i + 1) * LANES)
            yr_s[(n2 + i) * p2:(n2 + i) * p2 + DFT_N1, :] = y[:DFT_N1, lanes]
            yj_s[(n2 + i) * p2:(n2 + i) * p2 + DFT_N1, :] = y[DFT_N1:, lanes]
    for k1 in range(DFT_N1):
        rhs = jnp.concatenate([yr_s[pl.ds(k1, DFT_N2, stride=p2), :],
                               yj_s[pl.ds(k1, DFT_N2, stride=p2), :]], axis=0).astype(BF16)
        so_s[pl.ds(k1, DFT_N2, stride=p2), :] = jnp.dot(m2_ref[k1].astype(BF16), rhs,
                                                         preferred_element_type=F32)
    for k2 in range(DFT_N2):
        o_ref[0, k2 * DFT_N1:(k2 + 1) * DFT_N1, :] = so_s[k2 * p2:k2 * p2 + DFT_N1, :].astype(BF16)


def _position_dft(p, q, f1, m2):
    bsz = p.shape[0]
    blk = pl.BlockSpec((1, SEQ, LANES), lambda b, i: (b, 0, i))
    pad1 = DFT_N1 * DFT_P1
    pad2 = DFT_N2 * DFT_P2
    return pl.pallas_call(
        _dft_kernel,
        grid=(bsz, D_FOURIER // LANES),
        in_specs=[blk, blk,
                  pl.BlockSpec((2 * DFT_N1, 2 * DFT_N1), lambda b, i: (0, 0)),
                  pl.BlockSpec((DFT_N1, DFT_N2, 2 * DFT_N2), lambda b, i: (0, 0, 0))],
        out_specs=blk,
        out_shape=jax.ShapeDtypeStruct(p.shape, BF16),
        scratch_shapes=[pltpu.VMEM((pad1, LANES), F32), pltpu.VMEM((pad1, LANES), F32),
                        pltpu.VMEM((pad2, LANES), F32), pltpu.VMEM((pad2, LANES), F32),
                        pltpu.VMEM((pad2, LANES), F32)],
        compiler_params=pltpu.CompilerParams(dimension_semantics=("arbitrary", "arbitrary"),
                                             vmem_limit_bytes=VMEM_LIMIT_BYTES),
        name="position_dft",
    )(p, q, f1, m2)


def _bwd_kernel(x_ref, yf_ref, gfs_ref, gls_ref, v_ref, hf_ref, mod_ref, fg_ref, wfour_ref, wout_ref,
                wg_ref, bg_ref, lam_ref, h0_ref, o_ref,
                g_s, a_s, h_s, yp_s, lhs_s, hc_s):
    j = pl.program_id(1)

    @pl.when(j == 0)
    def _():
        hc_s[...] = jnp.broadcast_to(h0_ref[0, 1:2, :], hc_s.shape)

    _gate_matmuls(v_ref[0], wg_ref, g_s, slice(0, TILE))
    yfw = jnp.dot(yf_ref[0], wfour_ref[...], preferred_element_type=F32)
    lhs_s[:, 0:D_FOURIER] = (yfw * gfs_ref[0].astype(F32)).astype(BF16)

    hrate = _half_decay_rate(lam_ref[1:2, :])
    carry = _scan_steps(g_s, lambda rws: v_ref[0, rws, :].astype(F32), bg_ref, hrate, a_s, h_s,
                        range(CH - 1, -1, -1), _scan_init())
    tot_a, tot_h = carry
    hin, h_out = _chunk_carries(tot_a, tot_h, hc_s[0:1, :], reverse=True)
    hc_s[...] = jnp.broadcast_to(h_out, hc_s.shape)
    pitch = _pitch(CH)
    for s in range(CH):
        rows = slice(s * MC, (s + 1) * MC)
        yl = h_s[rows, :] + a_s[rows, :] * hin + hf_ref[0, rows, :].astype(F32)
        for q in range(N_QL):
            yp_s[q, pl.ds(s, MC, stride=pitch), :] = yl[:, q * LANES:(q + 1) * LANES]
    for m in range(MC):
        rows = slice(m * CH, (m + 1) * CH)
        for q in range(N_QL):
            cols = slice(q * LANES, (q + 1) * LANES)
            yl = yp_s[q, m * pitch:m * pitch + CH, :]
            lhs_s[rows, D_FOURIER + q * LANES:D_FOURIER + (q + 1) * LANES] = (
                yl * gls_ref[0, rows, cols].astype(F32)).astype(BF16)

    res_gate = mod_ref[pl.ds(pl.program_id(0), 1), 2 * D_MODEL:3 * D_MODEL]
    fg = fg_ref[...]
    proj = jnp.dot(lhs_s[...], wout_ref[...], preferred_element_type=F32)
    for m in range(MC):
        rows = slice(m * CH, (m + 1) * CH)
        res = x_ref[0, rows, :] + res_gate * proj[rows, :]
        ms = jnp.mean(res * res, axis=-1, keepdims=True)
        o_ref[0, rows, :] = (res * lax.rsqrt(ms + EPS)) * fg


def _bwd_pass(x, yf, gfs, gls, v, hf, mod, fg, wfour, wout, wg, bg, lam, fin):
    bsz = x.shape[0]
    full = lambda *shape: pl.BlockSpec(shape, lambda b, j: (0,) * len(shape))
    tile = lambda c: pl.BlockSpec((1, TILE, c), lambda b, j: (b, N_TILES - 1 - j, 0))
    return pl.pallas_call(
        _bwd_kernel,
        grid=(bsz, N_TILES),
        in_specs=[tile(D_MODEL), tile(D_FOURIER), tile(D_FOURIER), tile(D_LRU), tile(D_LRU),
                  tile(D_LRU),
                  full(8, 3 * D_MODEL),
                  full(1, D_MODEL), full(D_FOURIER, D_FOURIER), full(D_MODEL, D_MODEL),
                  *_gate_specs(1), full(2, D_LRU),
                  pl.BlockSpec((1, 2, D_LRU), lambda b, j: (b, 0, 0))],
        out_specs=tile(D_MODEL),
        out_shape=jax.ShapeDtypeStruct(x.shape, F32),
        scratch_shapes=[pltpu.VMEM((TILE, 2 * D_LRU), F32),
                        pltpu.VMEM((TILE, D_LRU), F32),
                        pltpu.VMEM((TILE, D_LRU), F32),
                        pltpu.VMEM((N_QL, MC * _pitch(CH), LANES), F32),
                        pltpu.VMEM((TILE, D_MODEL), BF16),
                        pltpu.VMEM((8, D_LRU), F32)],
        compiler_params=pltpu.CompilerParams(dimension_semantics=("arbitrary", "arbitrary"),
                                             vmem_limit_bytes=VMEM_LIMIT_BYTES),
        name="bwd_pass",
    )(x, yf, gfs, gls, v, hf, mod, fg, wfour, wout, wg, bg, lam, fin)


def kernel(x, c, ctx, c_ctx, w_ada, b_ada, norm_gain, w_in, w_four, conv_w, conv_b, w_rg, b_rg,
           w_ig, b_ig, lam, w_out, final_gain):
    bsz = x.shape[0]
    assert x.shape == (bsz, SEQ, D_MODEL) and ctx.shape == (bsz, CTX_LEN, D_MODEL)
    assert w_ada.shape[0] == 1, "single-layer kernel"
    dftc, f1, m2 = _dft_constants()

    cc = jnp.concatenate([c, c_ctx[None, :], jnp.zeros((8 - bsz - 1, D_MODEL), F32)], axis=0)
    mod, win, wout, wfour, wg, bg, cw, cb = _prep(cc, w_ada, b_ada, w_in, w_out, w_four, w_rg, w_ig,
                                                  b_rg, b_ig, conv_w, conv_b)
    g = norm_gain[0][None, :]
    lam0 = lam[0]

    fin = _ctx_states(ctx, mod, g, win, cw, cb, wg, bg, lam0)
    p, q, gfs, gls, v, hf = _fwd_pass(x, mod, g, win, dftc, cw, cb, wg, bg, lam0, fin)

    yf = _position_dft(p, q, f1, m2)

    return _bwd_pass(x, yf, gfs, gls, v, hf, mod, final_gain[None, :], wfour, wout, wg, bg, lam0, fin)
```

```python
import functools

import numpy as np
import jax
import jax.numpy as jnp
from jax import lax
from jax.experimental import pallas as pl
from jax.experimental.pallas import tpu as pltpu

D_MODEL = 1024
SEQ = 8192
CTX_LEN = 256
D_FOURIER = 512
D_LRU = 512
N_GROUPS = 4
GROUP_DIM = 128
N_HEADS = 4
HEAD_DIM = 128
LRU_C = 8.0
EPS = 1e-6

LANES = 128
MC = 16
CH = 64
TILE = MC * CH
N_TILES = SEQ // TILE
CH_CTX = CTX_LEN // MC
N_QL = D_LRU // LANES
LHS_ROWS = TILE + MC
DFT_N1 = 128
DFT_N2 = 64
DFT_NB = 2

VMEM_LIMIT_BYTES = 56 * 1024 * 1024
F32 = jnp.float32
BF16 = jnp.bfloat16


def _pitch(n):
    return n + 4


DFT_P1 = _pitch(DFT_N2)
DFT_P2 = _pitch(DFT_N1)


def _dft_constants():
    c = np.arange(GROUP_DIM)
    ang = 2.0 * np.pi * ((c[:, None] * c[None, :]) % GROUP_DIM) / GROUP_DIM
    dftc = np.concatenate([np.cos(ang), np.sin(ang)], axis=1) * 2.0 ** -3

    n = np.arange(DFT_N1)
    ang1 = 2.0 * np.pi * ((n[:, None] * n[None, :]) % DFT_N1) / DFT_N1
    c1, s1 = np.cos(ang1), np.sin(ang1)
    f1 = np.block([[c1, -s1], [s1, c1]]) * 2.0 ** -4

    k1 = np.arange(DFT_N1)[:, None, None]
    k2 = np.arange(DFT_N2)[None, :, None]
    n2 = np.arange(DFT_N2)[None, None, :]
    ang2 = 2.0 * np.pi * ((n2 * (k1 + DFT_N1 * k2)) % SEQ) / SEQ
    m2 = np.concatenate([np.cos(ang2), -np.sin(ang2)], axis=2) * 2.0 ** -3
    return jnp.asarray(dftc, dtype=F32), jnp.asarray(f1, dtype=F32), jnp.asarray(m2, dtype=F32)


def _silu(x):
    hx = 0.5 * x
    return hx * jnp.tanh(hx) + hx


def _silu_of_half(hx):
    return hx * jnp.tanh(hx) + hx


def _half_decay_rate(lam_row):
    z = -lam_row
    return (-0.5 * LRU_C) * (jnp.maximum(z, 0.0) + jnp.log1p(jnp.exp(-jnp.abs(z))))


def _modulate(xs, gmul, shift):
    ms = jnp.mean(xs * xs, axis=-1, keepdims=True)
    return (xs * lax.rsqrt(ms + EPS)) * gmul + shift


def _gate_ab(hpre_r, hpre_i, hv, hrate):
    a = jnp.exp(jnp.tanh(hpre_r) * hrate + hrate)
    om = 1.0 - a * a
    s = om * lax.rsqrt(jnp.maximum(om, 1e-30))
    return a, s * ((jnp.tanh(hpre_i) + 1.0) * hv)


def _ds(start, size):
    if isinstance(start, int):
        return slice(start, start + size)
    return pl.ds(pl.multiple_of(start, MC), size)


def _to_slab_order(u, up_s, uext_s, ch):
    pitch = _pitch(ch)
    for m in range(MC):
        for q in range(N_QL):
            up_s[q, m * pitch:m * pitch + ch, :] = u[m * ch:(m + 1) * ch, q * LANES:(q + 1) * LANES]
    for s in range(ch):
        for q in range(N_QL):
            uext_s[(s + 2) * MC:(s + 3) * MC, q * LANES:(q + 1) * LANES] = (
                up_s[q, pl.ds(s, MC, stride=pitch), :])


def _conv_halo(uext_s, lookahead, c_prev1, c_prev2, ch):
    last = uext_s[(ch + 1) * MC:(ch + 2) * MC, :]
    last2 = uext_s[ch * MC:(ch + 1) * MC, :]
    first = uext_s[2 * MC:3 * MC, :]
    uext_s[0:MC, :] = jnp.concatenate([c_prev2, last2[:MC - 1]], axis=0)
    uext_s[MC:2 * MC, :] = jnp.concatenate([c_prev1, last[:MC - 1]], axis=0)
    uext_s[(ch + 2) * MC:(ch + 3) * MC, :] = jnp.concatenate([first[1:], lookahead], axis=0)


def _conv_slab(uext_s, s, cw_ref, cb_ref):
    acc = cb_ref[...]
    for k in range(4):
        acc = acc + cw_ref[k] * uext_s[_ds((s + k) * MC, MC), :]
    return acc


def _gate_matmuls(vb, wg_ref, g_s, rows):
    for h in range(N_HEADS):
        res = jnp.dot(vb[:, h * HEAD_DIM:(h + 1) * HEAD_DIM], wg_ref[h], preferred_element_type=F32)
        g_s[rows, h * HEAD_DIM:(h + 1) * HEAD_DIM] = res[:, :HEAD_DIM]
        g_s[rows, D_LRU + h * HEAD_DIM:D_LRU + (h + 1) * HEAD_DIM] = res[:, HEAD_DIM:]


def _scan_init():
    return jnp.ones((MC, D_LRU), F32), jnp.zeros((MC, D_LRU), F32)


def _scan_steps(g_s, load_v, bg_ref, hrate, a_s, h_s, steps, carry):
    b_r = bg_ref[:, :D_LRU]
    b_i = bg_ref[:, D_LRU:]
    acc_a, acc_h = carry
    for s in steps:
        rows = _ds(s * MC, MC)
        pre = g_s[rows, :]
        a, bx = _gate_ab(pre[:, :D_LRU] + b_r, pre[:, D_LRU:] + b_i, load_v(rows), hrate)
        acc_a = a * acc_a
        acc_h = a * acc_h + bx
        a_s[rows, :] = acc_a
        h_s[rows, :] = acc_h
    return acc_a, acc_h


def _chunk_carries(tot_a, tot_h, carry_in, reverse):
    row_id = lax.broadcasted_iota(jnp.int32, (MC, D_LRU), 0)
    hin = jnp.zeros((MC, D_LRU), F32)
    h = carry_in
    order = range(MC - 1, -1, -1) if reverse else range(MC)
    for m in order:
        hin = jnp.where(row_id == m, h, hin)
        h = tot_a[m:m + 1] * h + tot_h[m:m + 1]
    return hin, h


def _prep_kernel(c_ref, wada_ref, bada_ref, win_ref, wout_ref, wfour_ref, wrg_ref, wig_ref, brg_ref,
                 big_ref, cw_ref, cb_ref,
                 mod_ref, win_o, wout_o, wfour_o, wg_o, bg_o, cwr_o, cbr_o):
    i = pl.program_id(0)
    s = _silu(c_ref[...]).astype(BF16)
    mod_ref[...] = (jnp.dot(s, wada_ref[0].astype(BF16), preferred_element_type=F32)
                    + bada_ref[...])

    @pl.when(i < 2)
    def _():
        win_o[:, 0:D_FOURIER] = win_ref[0, :, 0:D_FOURIER].astype(BF16)
        win_o[:, D_FOURIER:] = (0.5 * win_ref[0, :, D_FOURIER:]).astype(BF16)

    @pl.when(i == 2)
    def _():
        wout_o[...] = wout_ref[0].astype(BF16)
        wfour_o[...] = wfour_ref[0].astype(BF16)
        for d in range(2):
            for h in range(N_HEADS):
                wg_o[d, h] = jnp.concatenate([wrg_ref[0, d, h], wig_ref[0, d, h]],
                                             axis=1).astype(BF16)
            bg_o[d] = 0.5 * jnp.concatenate([brg_ref[0, d:d + 1, :], big_ref[0, d:d + 1, :]], axis=1)
        for k in range(4):
            cwr_o[k] = jnp.broadcast_to(0.5 * cw_ref[0, k:k + 1, :], (MC, D_LRU))
        cbr_o[...] = jnp.broadcast_to(0.5 * cb_ref[...], (MC, D_LRU))


def _prep(cc, w_ada, b_ada, w_in, w_out, w_four, w_rg, w_ig, b_rg, b_ig, conv_w, conv_b):
    full = lambda *shape: pl.BlockSpec(shape, lambda i: (0,) * len(shape))
    half = lambda i: jnp.minimum(i, 1)
    gshape = (1, 2, N_HEADS, HEAD_DIM, HEAD_DIM)
    return pl.pallas_call(
        _prep_kernel,
        grid=(3,),
        in_specs=[full(8, D_MODEL),
                  pl.BlockSpec((1, D_MODEL, D_MODEL), lambda i: (0, 0, i)),
                  pl.BlockSpec((1, D_MODEL), lambda i: (0, i)),
                  pl.BlockSpec((1, D_MODEL, D_MODEL), lambda i: (0, 0, half(i))),
                  full(1, D_MODEL, D_MODEL), full(1, D_FOURIER, D_FOURIER),
                  full(*gshape), full(*gshape), full(1, 2, D_LRU), full(1, 2, D_LRU),
                  full(1, 4, D_LRU), full(1, D_LRU)],
        out_specs=[pl.BlockSpec((8, D_MODEL), lambda i: (0, i)),
                   pl.BlockSpec((D_MODEL, D_MODEL), lambda i: (0, half(i))),
                   full(D_MODEL, D_MODEL), full(D_FOURIER, D_FOURIER),
                   full(2, N_HEADS, HEAD_DIM, 2 * HEAD_DIM), full(2, 1, 2 * D_LRU),
                   full(4, MC, D_LRU), full(MC, D_LRU)],
        out_shape=[jax.ShapeDtypeStruct((8, 3 * D_MODEL), F32),
                   jax.ShapeDtypeStruct((D_MODEL, 2 * D_MODEL), BF16),
                   jax.ShapeDtypeStruct((D_MODEL, D_MODEL), BF16),
                   jax.ShapeDtypeStruct((D_FOURIER, D_FOURIER), BF16),
                   jax.ShapeDtypeStruct((2, N_HEADS, HEAD_DIM, 2 * HEAD_DIM), BF16),
                   jax.ShapeDtypeStruct((2, 1, 2 * D_LRU), F32),
                   jax.ShapeDtypeStruct((4, MC, D_LRU), F32),
                   jax.ShapeDtypeStruct((MC, D_LRU), F32)],
        compiler_params=pltpu.CompilerParams(dimension_semantics=("arbitrary",),
                                             vmem_limit_bytes=VMEM_LIMIT_BYTES),
        name="prep",
    )(cc, w_ada, b_ada, w_in, w_out, w_four, w_rg, w_ig, b_rg, b_ig, conv_w, conv_b)


def _ctx_kernel(x_ref, mod_ref, g_ref, wlx_ref, cw_ref, cb_ref, wgf_ref, bgf_ref, wgb_ref, bgb_ref,
                lam_ref, fin_ref, up_s, uext_s, vf_s, g_s, a_s, h_s, *, mod_row):
    ch = CH_CTX
    shift = mod_ref[mod_row:mod_row + 1, 0:D_MODEL]
    scale = mod_ref[mod_row:mod_row + 1, D_MODEL:2 * D_MODEL]
    gmul = g_ref[...] * (1.0 + scale)
    lhs = _modulate(x_ref[0], gmul, shift).astype(BF16)
    u = jnp.dot(lhs, wlx_ref[...], preferred_element_type=F32)
    _to_slab_order(u, up_s, uext_s, ch)
    zero_row = jnp.zeros((1, D_LRU), F32)
    _conv_halo(uext_s, zero_row, zero_row, zero_row, ch)
    for s in range(ch):
        vf_s[s * MC:(s + 1) * MC, :] = _conv_slab(uext_s, s, cw_ref, cb_ref)
    vb = vf_s[...].astype(BF16)
    for d, (wg_ref, bg_ref) in enumerate(((wgf_ref, bgf_ref), (wgb_ref, bgb_ref))):
        _gate_matmuls(vb, wg_ref, g_s, slice(0, ch * MC))
        rate = _half_decay_rate(lam_ref[d:d + 1, :])
        steps = range(ch - 1, -1, -1) if d == 1 else range(ch)
        tot_a, tot_h = _scan_steps(g_s, lambda rows: vf_s[rows, :], bg_ref, rate, a_s, h_s, steps,
                                   _scan_init())
        _, fin = _chunk_carries(tot_a, tot_h, zero_row, reverse=(d == 1))
        fin_ref[0, d:d + 1, :] = fin


def _gate_specs(direction):
    return (pl.BlockSpec((None, N_HEADS, HEAD_DIM, 2 * HEAD_DIM), lambda *_: (direction, 0, 0, 0)),
            pl.BlockSpec((None, 1, 2 * D_LRU), lambda *_: (direction, 0, 0)))


def _ctx_states(ctx, mod, g, win, cw, cb, wg, bg, lam):
    bsz = ctx.shape[0]
    n = CTX_LEN
    full = lambda *shape: pl.BlockSpec(shape, lambda b: (0,) * len(shape))
    return pl.pallas_call(
        functools.partial(_ctx_kernel, mod_row=bsz),
        grid=(bsz,),
        in_specs=[pl.BlockSpec((1, n, D_MODEL), lambda b: (b, 0, 0)),
                  full(8, 3 * D_MODEL),
                  full(1, D_MODEL),
                  pl.BlockSpec((D_MODEL, D_LRU), lambda b: (0, 2)),
                  full(4, MC, D_LRU), full(MC, D_LRU),
                  *_gate_specs(0), *_gate_specs(1),
                  full(2, D_LRU)],
        out_specs=pl.BlockSpec((1, 2, D_LRU), lambda b: (b, 0, 0)),
        out_shape=jax.ShapeDtypeStruct((bsz, 2, D_LRU), F32),
        scratch_shapes=[pltpu.VMEM((N_QL, MC * _pitch(CH_CTX), LANES), F32),
                        pltpu.VMEM((n + 3 * MC, D_LRU), F32),
                        pltpu.VMEM((n, D_LRU), F32),
                        pltpu.VMEM((n, 2 * D_LRU), F32),
                        pltpu.VMEM((n, D_LRU), F32),
                        pltpu.VMEM((n, D_LRU), F32)],
        compiler_params=pltpu.CompilerParams(dimension_semantics=("arbitrary",)),
        name="ctx_states",
    )(ctx, mod, g, win, cw, cb, wg, bg, wg, bg, lam)


def _fwd_kernel(x_ref, xh_ref, mod_ref, g_ref, win_ref, cw_ref, cb_ref, wg_ref,
                bg_ref, lam_ref, h0_ref,
                uf_ref, gfs_ref, gls_ref, v_ref, hf_ref,
                lhs_s, up_s, uext_s, vf_s, g_s, a_s, h_s, cc_s, hc_s):
    j = pl.program_id(1)

    @pl.when(j == 0)
    def _():
        cc_s[...] = jnp.zeros_like(cc_s)
        hc_s[...] = jnp.broadcast_to(h0_ref[0, 0:1, :], hc_s.shape)

    brow = pl.ds(pl.program_id(0), 1)
    shift = mod_ref[brow, 0:D_MODEL]
    scale = mod_ref[brow, D_MODEL:2 * D_MODEL]
    gmul = g_ref[...] * (1.0 + scale)
    for m in range(MC):
        rows = slice(m * CH, (m + 1) * CH)
        lhs_s[rows, :] = _modulate(x_ref[0, rows, :], gmul, shift).astype(BF16)
    hh = _modulate(xh_ref[0], gmul, shift)
    lhs_s[TILE:LHS_ROWS, :] = jnp.concatenate([hh, jnp.zeros_like(hh)], axis=0).astype(BF16)

    u = jnp.dot(lhs_s[...], win_ref[:, 2 * D_FOURIER:2 * D_FOURIER + D_LRU],
                preferred_element_type=F32)

    _to_slab_order(u, up_s, uext_s, CH)
    look = jnp.where(j == N_TILES - 1, 0.0, u[TILE:TILE + 1])
    _conv_halo(uext_s, look, cc_s[0:1, :], cc_s[1:2, :], CH)
    cc_s[0:1, :] = u[TILE - 1:TILE]
    cc_s[1:2, :] = u[TILE - 2:TILE - 1]

    uf_ref[0] = jnp.dot(lhs_s[0:TILE, :], win_ref[:, 0:D_FOURIER],
                        preferred_element_type=F32).astype(BF16)

    for s in range(CH):
        rows = slice(s * MC, (s + 1) * MC)
        v = _conv_slab(uext_s, s, cw_ref, cb_ref)
        vf_s[rows, :] = v
        v_ref[0, rows, :] = v.astype(BF16)
    _gate_matmuls(v_ref[0], wg_ref, g_s, slice(0, TILE))

    for c0, o_ref in ((D_FOURIER, gfs_ref), (2 * D_FOURIER + D_LRU, gls_ref)):
        o_ref[0] = _silu_of_half(jnp.dot(lhs_s[0:TILE, :], win_ref[:, c0:c0 + D_LRU],
                                         preferred_element_type=F32)).astype(BF16)

    hrate = _half_decay_rate(lam_ref[0:1, :])
    tot_a, tot_h = _scan_steps(g_s, lambda rws: vf_s[rws, :], bg_ref, hrate, a_s, h_s,
                               range(CH), _scan_init())
    hin, h_out = _chunk_carries(tot_a, tot_h, hc_s[0:1, :], reverse=False)
    hc_s[...] = jnp.broadcast_to(h_out, hc_s.shape)
    for s in range(CH):
        rows = slice(s * MC, (s + 1) * MC)
        hf_ref[0, rows, :] = (h_s[rows, :] + a_s[rows, :] * hin).astype(BF16)


def _fwd_pass(x, mod, g, win, cw, cb, wg, bg, lam, fin):
    bsz = x.shape[0]
    full = lambda *shape: pl.BlockSpec(shape, lambda b, j: (0,) * len(shape))
    tile = lambda c: pl.BlockSpec((1, TILE, c), lambda b, j: (b, j, 0))
    shp = lambda c: jax.ShapeDtypeStruct((bsz, SEQ, c), BF16)
    rows8 = TILE // 8
    return pl.pallas_call(
        _fwd_kernel,
        grid=(bsz, N_TILES),
        in_specs=[tile(D_MODEL),
                  pl.BlockSpec((1, 8, D_MODEL),
                               lambda b, j: (b, jnp.minimum((j + 1) * rows8, SEQ // 8 - 1), 0)),
                  full(8, 3 * D_MODEL),
                  full(1, D_MODEL), full(D_MODEL, 2 * D_MODEL),
                  full(4, MC, D_LRU), full(MC, D_LRU),
                  *_gate_specs(0),
                  full(2, D_LRU),
                  pl.BlockSpec((1, 2, D_LRU), lambda b, j: (b, 0, 0))],
        out_specs=[tile(D_FOURIER), tile(D_FOURIER), tile(D_LRU), tile(D_LRU), tile(D_LRU)],
        out_shape=[shp(D_FOURIER), shp(D_FOURIER), shp(D_LRU), shp(D_LRU), shp(D_LRU)],
        scratch_shapes=[pltpu.VMEM((LHS_ROWS, D_MODEL), BF16),
                        pltpu.VMEM((N_QL, MC * _pitch(CH), LANES), F32),
                        pltpu.VMEM((TILE + 3 * MC, D_LRU), F32),
                        pltpu.VMEM((TILE, D_LRU), F32),
                        pltpu.VMEM((TILE, 2 * D_LRU), F32),
                        pltpu.VMEM((TILE, D_LRU), F32),
                        pltpu.VMEM((TILE, D_LRU), F32),
                        pltpu.VMEM((8, D_LRU), F32),
                        pltpu.VMEM((8, D_LRU), F32)],
        compiler_params=pltpu.CompilerParams(dimension_semantics=("arbitrary", "arbitrary"),
                                             vmem_limit_bytes=VMEM_LIMIT_BYTES),
        name="fwd_pass",
    )(x, x, mod, g, win, cw, cb, wg, bg, lam, fin)


def _dft_kernel(u_ref, dftc_ref, f1_ref, m2_ref, o_ref, sp_s, sq_s, yr_s, yj_s, so_s):
    p1 = DFT_P1
    p2 = DFT_P2
    dftc = dftc_ref[...].astype(BF16)
    chunks = TILE // DFT_N2
    for blk in range(SEQ // TILE):
        pq = jnp.dot(u_ref[0, blk * TILE:(blk + 1) * TILE, :], dftc, preferred_element_type=F32)
        for m in range(chunks):
            n1 = blk * chunks + m
            rows = slice(m * DFT_N2, (m + 1) * DFT_N2)
            sp_s[n1 * p1:n1 * p1 + DFT_N2, :] = pq[rows, :GROUP_DIM]
            sq_s[n1 * p1:n1 * p1 + DFT_N2, :] = pq[rows, GROUP_DIM:]
    f1 = f1_ref[...].astype(BF16)
    for n2 in range(0, DFT_N2, DFT_NB):
        rhs = jnp.concatenate(
            [jnp.concatenate([sp_s[pl.ds(n2 + i, DFT_N1, stride=p1), :],
                              sq_s[pl.ds(n2 + i, DFT_N1, stride=p1), :]], axis=0)
             for i in range(DFT_NB)], axis=1).astype(BF16)
        y = jnp.dot(f1, rhs, preferred_element_type=F32)
        for i in range(DFT_NB):
            lanes = slice(i * LANES, (i + 1) * LANES)
            yr_s[(n2 + i) * p2:(n2 + i) * p2 + DFT_N1, :] = y[:DFT_N1, lanes]
            yj_s[(n2 + i) * p2:(n2 + i) * p2 + DFT_N1, :] = y[DFT_N1:, lanes]
    for k1 in range(DFT_N1):
        rhs = jnp.concatenate([yr_s[pl.ds(k1, DFT_N2, stride=p2), :],
                               yj_s[pl.ds(k1, DFT_N2, stride=p2), :]], axis=0).astype(BF16)
        so_s[pl.ds(k1, DFT_N2, stride=p2), :] = jnp.dot(m2_ref[k1].astype(BF16), rhs,
                                                         preferred_element_type=F32)
    for k2 in range(DFT_N2):
        o_ref[0, k2 * DFT_N1:(k2 + 1) * DFT_N1, :] = so_s[k2 * p2:k2 * p2 + DFT_N1, :].astype(BF16)


def _fourier_dft(uf, dftc, f1, m2):
    bsz = uf.shape[0]
    blk = pl.BlockSpec((1, SEQ, GROUP_DIM), lambda b, i: (b, 0, i))
    pad1 = DFT_N1 * DFT_P1
    pad2 = DFT_N2 * DFT_P2
    return pl.pallas_call(
        _dft_kernel,
        grid=(bsz, N_GROUPS),
        in_specs=[blk,
                  pl.BlockSpec((GROUP_DIM, 2 * GROUP_DIM), lambda b, i: (0, 0)),
                  pl.BlockSpec((2 * DFT_N1, 2 * DFT_N1), lambda b, i: (0, 0)),
                  pl.BlockSpec((DFT_N1, DFT_N2, 2 * DFT_N2), lambda b, i: (0, 0, 0))],
        out_specs=blk,
        out_shape=jax.ShapeDtypeStruct(uf.shape, BF16),
        scratch_shapes=[pltpu.VMEM((pad1, LANES), F32), pltpu.VMEM((pad1, LANES), F32),
                        pltpu.VMEM((pad2, LANES), F32), pltpu.VMEM((pad2, LANES), F32),
                        pltpu.VMEM((pad2, LANES), F32)],
        compiler_params=pltpu.CompilerParams(dimension_semantics=("arbitrary", "arbitrary"),
                                             vmem_limit_bytes=VMEM_LIMIT_BYTES),
        name="fourier_dft",
    )(uf, dftc, f1, m2)


def _bwd_kernel(x_ref, yf_ref, gfs_ref, gls_ref, v_ref, hf_ref, mod_ref, fg_ref, wfour_ref, wout_ref,
                wg_ref, bg_ref, lam_ref, h0_ref, o_ref,
                g_s, a_s, h_s, yp_s, lhs_s, hc_s):
    j = pl.program_id(1)

    @pl.when(j == 0)
    def _():
        hc_s[...] = jnp.broadcast_to(h0_ref[0, 1:2, :], hc_s.shape)

    _gate_matmuls(v_ref[0], wg_ref, g_s, slice(0, TILE))
    yfw = jnp.dot(yf_ref[0], wfour_ref[...], preferred_element_type=F32)
    lhs_s[:, 0:D_FOURIER] = (yfw * gfs_ref[0].astype(F32)).astype(BF16)

    hrate = _half_decay_rate(lam_ref[1:2, :])
    carry = _scan_steps(g_s, lambda rws: v_ref[0, rws, :].astype(F32), bg_ref, hrate, a_s, h_s,
                        range(CH - 1, -1, -1), _scan_init())
    tot_a, tot_h = carry
    hin, h_out = _chunk_carries(tot_a, tot_h, hc_s[0:1, :], reverse=True)
    hc_s[...] = jnp.broadcast_to(h_out, hc_s.shape)
    pitch = _pitch(CH)
    for s in range(CH):
        rows = slice(s * MC, (s + 1) * MC)
        yl = h_s[rows, :] + a_s[rows, :] * hin + hf_ref[0, rows, :].astype(F32)
        for q in range(N_QL):
            yp_s[q, pl.ds(s, MC, stride=pitch), :] = yl[:, q * LANES:(q + 1) * LANES]
    for m in range(MC):
        rows = slice(m * CH, (m + 1) * CH)
        for q in range(N_QL):
            cols = slice(q * LANES, (q + 1) * LANES)
            yl = yp_s[q, m * pitch:m * pitch + CH, :]
            lhs_s[rows, D_FOURIER + q * LANES:D_FOURIER + (q + 1) * LANES] = (
                yl * gls_ref[0, rows, cols].astype(F32)).astype(BF16)

    res_gate = mod_ref[pl.ds(pl.program_id(0), 1), 2 * D_MODEL:3 * D_MODEL]
    fg = fg_ref[...]
    proj = jnp.dot(lhs_s[...], wout_ref[...], preferred_element_type=F32)
    for m in range(MC):
        rows = slice(m * CH, (m + 1) * CH)
        res = x_ref[0, rows, :] + res_gate * proj[rows, :]
        ms = jnp.mean(res * res, axis=-1, keepdims=True)
        o_ref[0, rows, :] = (res * lax.rsqrt(ms + EPS)) * fg


def _bwd_pass(x, yf, gfs, gls, v, hf, mod, fg, wfour, wout, wg, bg, lam, fin):
    bsz = x.shape[0]
    full = lambda *shape: pl.BlockSpec(shape, lambda b, j: (0,) * len(shape))
    tile = lambda c: pl.BlockSpec((1, TILE, c), lambda b, j: (b, N_TILES - 1 - j, 0))
    return pl.pallas_call(
        _bwd_kernel,
        grid=(bsz, N_TILES),
        in_specs=[tile(D_MODEL), tile(D_FOURIER), tile(D_FOURIER), tile(D_LRU), tile(D_LRU),
                  tile(D_LRU),
                  full(8, 3 * D_MODEL),
                  full(1, D_MODEL), full(D_FOURIER, D_FOURIER), full(D_MODEL, D_MODEL),
                  *_gate_specs(1), full(2, D_LRU),
                  pl.BlockSpec((1, 2, D_LRU), lambda b, j: (b, 0, 0))],
        out_specs=tile(D_MODEL),
        out_shape=jax.ShapeDtypeStruct(x.shape, F32),
        scratch_shapes=[pltpu.VMEM((TILE, 2 * D_LRU), F32),
                        pltpu.VMEM((TILE, D_LRU), F32),
                        pltpu.VMEM((TILE, D_LRU), F32),
                        pltpu.VMEM((N_QL, MC * _pitch(CH), LANES), F32),
                        pltpu.VMEM((TILE, D_MODEL), BF16),
                        pltpu.VMEM((8, D_LRU), F32)],
        compiler_params=pltpu.CompilerParams(dimension_semantics=("arbitrary", "arbitrary"),
                                             vmem_limit_bytes=VMEM_LIMIT_BYTES),
        name="bwd_pass",
    )(x, yf, gfs, gls, v, hf, mod, fg, wfour, wout, wg, bg, lam, fin)


def kernel(x, c, ctx, c_ctx, w_ada, b_ada, norm_gain, w_in, w_four, conv_w, conv_b, w_rg, b_rg,
           w_ig, b_ig, lam, w_out, final_gain):
    bsz = x.shape[0]
    assert x.shape == (bsz, SEQ, D_MODEL) and ctx.shape == (bsz, CTX_LEN, D_MODEL)
    assert w_ada.shape[0] == 1, "single-layer kernel"
    dftc, f1, m2 = _dft_constants()

    cc = jnp.concatenate([c, c_ctx[None, :], jnp.zeros((8 - bsz - 1, D_MODEL), F32)], axis=0)
    mod, win, wout, wfour, wg, bg, cw, cb = _prep(cc, w_ada, b_ada, w_in, w_out, w_four, w_rg, w_ig,
                                                  b_rg, b_ig, conv_w, conv_b)
    g = norm_gain[0][None, :]
    lam0 = lam[0]

    fin = _ctx_states(ctx, mod, g, win, cw, cb, wg, bg, lam0)
    uf, gfs, gls, v, hf = _fwd_pass(x, mod, g, win, cw, cb, wg, bg, lam0, fin)

    yf = _fourier_dft(uf, dftc, f1, m2)

    return _bwd_pass(x, yf, gfs, gls, v, hf, mod, final_gain[None, :], wfour, wout, wg, bg, lam0, fin)
```

```python
import functools

import numpy as np
import jax
import jax.numpy as jnp
from jax import lax
from jax.experimental import pallas as pl
from jax.experimental.pallas import tpu as pltpu

D_MODEL = 1024
SEQ = 8192
CTX_LEN = 256
D_FOURIER = 512
D_LRU = 512
N_GROUPS = 4
GROUP_DIM = 128
N_HEADS = 4
HEAD_DIM = 128
LRU_C = 8.0
EPS = 1e-6

LANES = 128
MC = 16
CH = 64
TILE = MC * CH
N_TILES = SEQ // TILE
CH_CTX = CTX_LEN // MC
N_QL = D_LRU // LANES
LHS_ROWS = TILE + MC
PREP_COLS = 512
N_PREP_STEPS = 3 * D_MODEL // PREP_COLS
N_WIN_BLOCKS = 2 * D_MODEL // PREP_COLS
DFT_N1 = 128
DFT_N2 = 64
DFT_NB = 2

VMEM_LIMIT_BYTES = 56 * 1024 * 1024
F32 = jnp.float32
BF16 = jnp.bfloat16


def _pitch(n):
    return n + 4


DFT_P1 = _pitch(DFT_N2)
DFT_P2 = _pitch(DFT_N1)


def _dft_constants():
    c = np.arange(GROUP_DIM)
    ang = 2.0 * np.pi * ((c[:, None] * c[None, :]) % GROUP_DIM) / GROUP_DIM
    dftc = np.concatenate([np.cos(ang), np.sin(ang)], axis=1) * 2.0 ** -3

    n = np.arange(DFT_N1)
    ang1 = 2.0 * np.pi * ((n[:, None] * n[None, :]) % DFT_N1) / DFT_N1
    c1, s1 = np.cos(ang1), np.sin(ang1)
    f1 = np.block([[c1, -s1], [s1, c1]]) * 2.0 ** -4

    k1 = np.arange(DFT_N1)[:, None, None]
    k2 = np.arange(DFT_N2)[None, :, None]
    n2 = np.arange(DFT_N2)[None, None, :]
    ang2 = 2.0 * np.pi * ((n2 * (k1 + DFT_N1 * k2)) % SEQ) / SEQ
    m2 = np.concatenate([np.cos(ang2), -np.sin(ang2)], axis=2) * 2.0 ** -3
    return jnp.asarray(dftc, dtype=F32), jnp.asarray(f1, dtype=F32), jnp.asarray(m2, dtype=F32)


def _silu(x):
    hx = 0.5 * x
    return hx * jnp.tanh(hx) + hx


def _silu_of_half(hx):
    return hx * jnp.tanh(hx) + hx


def _half_decay_rate(lam_row):
    z = -lam_row
    return (-0.5 * LRU_C) * (jnp.maximum(z, 0.0) + jnp.log1p(jnp.exp(-jnp.abs(z))))


def _modulate(xs, gmul, shift):
    ms = jnp.mean(xs * xs, axis=-1, keepdims=True)
    return (xs * lax.rsqrt(ms + EPS)) * gmul + shift


def _gate_ab(hpre_r, hpre_i, hv, hrate):
    a = jnp.exp(jnp.tanh(hpre_r) * hrate + hrate)
    om = 1.0 - a * a
    s = om * lax.rsqrt(jnp.maximum(om, 1e-30))
    return a, s * ((jnp.tanh(hpre_i) + 1.0) * hv)


def _ds(start, size):
    if isinstance(start, int):
        return slice(start, start + size)
    return pl.ds(pl.multiple_of(start, MC), size)


def _to_slab_order(u, up_s, uext_s, ch):
    pitch = _pitch(ch)
    for m in range(MC):
        for q in range(N_QL):
            up_s[q, m * pitch:m * pitch + ch, :] = u[m * ch:(m + 1) * ch, q * LANES:(q + 1) * LANES]
    for s in range(ch):
        for q in range(N_QL):
            uext_s[(s + 2) * MC:(s + 3) * MC, q * LANES:(q + 1) * LANES] = (
                up_s[q, pl.ds(s, MC, stride=pitch), :])


def _conv_halo(uext_s, lookahead, c_prev1, c_prev2, ch):
    last = uext_s[(ch + 1) * MC:(ch + 2) * MC, :]
    last2 = uext_s[ch * MC:(ch + 1) * MC, :]
    first = uext_s[2 * MC:3 * MC, :]
    uext_s[0:MC, :] = jnp.concatenate([c_prev2, last2[:MC - 1]], axis=0)
    uext_s[MC:2 * MC, :] = jnp.concatenate([c_prev1, last[:MC - 1]], axis=0)
    uext_s[(ch + 2) * MC:(ch + 3) * MC, :] = jnp.concatenate([first[1:], lookahead], axis=0)


def _conv_slab(uext_s, s, cw_ref, cb_ref):
    acc = cb_ref[...]
    for k in range(4):
        acc = acc + cw_ref[k] * uext_s[_ds((s + k) * MC, MC), :]
    return acc


def _gate_matmuls(vb, wg_ref, g_s, rows):
    for h in range(N_HEADS):
        res = jnp.dot(vb[:, h * HEAD_DIM:(h + 1) * HEAD_DIM], wg_ref[h], preferred_element_type=F32)
        g_s[rows, h * HEAD_DIM:(h + 1) * HEAD_DIM] = res[:, :HEAD_DIM]
        g_s[rows, D_LRU + h * HEAD_DIM:D_LRU + (h + 1) * HEAD_DIM] = res[:, HEAD_DIM:]


def _scan_init():
    return jnp.ones((MC, D_LRU), F32), jnp.zeros((MC, D_LRU), F32)


def _scan_steps(g_s, load_v, bg_ref, hrate, a_s, h_s, steps, carry):
    b_r = bg_ref[:, :D_LRU]
    b_i = bg_ref[:, D_LRU:]
    acc_a, acc_h = carry
    for s in steps:
        rows = _ds(s * MC, MC)
        pre = g_s[rows, :]
        a, bx = _gate_ab(pre[:, :D_LRU] + b_r, pre[:, D_LRU:] + b_i, load_v(rows), hrate)
        acc_a = a * acc_a
        acc_h = a * acc_h + bx
        a_s[rows, :] = acc_a
        h_s[rows, :] = acc_h
    return acc_a, acc_h


def _chunk_carries(tot_a, tot_h, carry_in, reverse):
    row_id = lax.broadcasted_iota(jnp.int32, (MC, D_LRU), 0)
    hin = jnp.zeros((MC, D_LRU), F32)
    h = carry_in
    order = range(MC - 1, -1, -1) if reverse else range(MC)
    for m in order:
        hin = jnp.where(row_id == m, h, hin)
        h = tot_a[m:m + 1] * h + tot_h[m:m + 1]
    return hin, h


def _prep_kernel(c_ref, cctx_ref, wada_ref, bada_ref, win_ref, wout_ref, wfour_ref, wrg_ref, wig_ref,
                 brg_ref, big_ref, cw_ref, cb_ref,
                 mod_ref, win_o, wout_o, wfour_o, wg_o, bg_o, cwr_o, cbr_o):
    i = pl.program_id(0)
    bsz = c_ref.shape[0]
    cc = jnp.concatenate([c_ref[...], cctx_ref[...], jnp.zeros((8 - bsz - 1, D_MODEL), F32)], axis=0)
    s = _silu(cc).astype(BF16)
    mod_ref[...] = (jnp.dot(s, wada_ref[0].astype(BF16), preferred_element_type=F32)
                    + bada_ref[...])

    @pl.when(i < N_WIN_BLOCKS)
    def _():
        scale = jnp.where(lax.rem(i, 2) == 1, 0.5, 1.0)
        win_o[...] = (scale * win_ref[0]).astype(BF16)

    @pl.when(i >= N_WIN_BLOCKS)
    def _():
        wout_o[...] = wout_ref[0].astype(BF16)

    @pl.when(i == N_PREP_STEPS - 1)
    def _():
        wfour_o[...] = wfour_ref[0].astype(BF16)
        for d in range(2):
            for h in range(N_HEADS):
                wg_o[d, h] = jnp.concatenate([wrg_ref[0, d, h], wig_ref[0, d, h]],
                                             axis=1).astype(BF16)
            bg_o[d] = 0.5 * jnp.concatenate([brg_ref[0, d:d + 1, :], big_ref[0, d:d + 1, :]], axis=1)
        for k in range(4):
            cwr_o[k] = jnp.broadcast_to(0.5 * cw_ref[0, k:k + 1, :], (MC, D_LRU))
        cbr_o[...] = jnp.broadcast_to(0.5 * cb_ref[...], (MC, D_LRU))


def _prep(c, c_ctx, w_ada, b_ada, w_in, w_out, w_four, w_rg, w_ig, b_rg, b_ig, conv_w, conv_b):
    full = lambda *shape: pl.BlockSpec(shape, lambda i: (0,) * len(shape))
    win_blk = lambda i: jnp.minimum(i, N_WIN_BLOCKS - 1)
    wout_blk = lambda i: jnp.maximum(i - N_WIN_BLOCKS, 0)
    gshape = (1, 2, N_HEADS, HEAD_DIM, HEAD_DIM)
    return pl.pallas_call(
        _prep_kernel,
        grid=(N_PREP_STEPS,),
        in_specs=[full(*c.shape), full(1, D_MODEL),
                  pl.BlockSpec((1, D_MODEL, PREP_COLS), lambda i: (0, 0, i)),
                  pl.BlockSpec((1, PREP_COLS), lambda i: (0, i)),
                  pl.BlockSpec((1, D_MODEL, PREP_COLS), lambda i: (0, 0, win_blk(i))),
                  pl.BlockSpec((1, D_MODEL // 2, D_MODEL), lambda i: (0, wout_blk(i), 0)),
                  full(1, D_FOURIER, D_FOURIER),
                  full(*gshape), full(*gshape), full(1, 2, D_LRU), full(1, 2, D_LRU),
                  full(1, 4, D_LRU), full(1, D_LRU)],
        out_specs=[pl.BlockSpec((8, PREP_COLS), lambda i: (0, i)),
                   pl.BlockSpec((D_MODEL, PREP_COLS), lambda i: (0, win_blk(i))),
                   pl.BlockSpec((D_MODEL // 2, D_MODEL), lambda i: (wout_blk(i), 0)),
                   full(D_FOURIER, D_FOURIER),
                   full(2, N_HEADS, HEAD_DIM, 2 * HEAD_DIM), full(2, 1, 2 * D_LRU),
                   full(4, MC, D_LRU), full(MC, D_LRU)],
        out_shape=[jax.ShapeDtypeStruct((8, 3 * D_MODEL), F32),
                   jax.ShapeDtypeStruct((D_MODEL, 2 * D_MODEL), BF16),
                   jax.ShapeDtypeStruct((D_MODEL, D_MODEL), BF16),
                   jax.ShapeDtypeStruct((D_FOURIER, D_FOURIER), BF16),
                   jax.ShapeDtypeStruct((2, N_HEADS, HEAD_DIM, 2 * HEAD_DIM), BF16),
                   jax.ShapeDtypeStruct((2, 1, 2 * D_LRU), F32),
                   jax.ShapeDtypeStruct((4, MC, D_LRU), F32),
                   jax.ShapeDtypeStruct((MC, D_LRU), F32)],
        compiler_params=pltpu.CompilerParams(dimension_semantics=("arbitrary",),
                                             vmem_limit_bytes=VMEM_LIMIT_BYTES),
        name="prep",
    )(c, c_ctx, w_ada, b_ada, w_in, w_out, w_four, w_rg, w_ig, b_rg, b_ig, conv_w, conv_b)


def _ctx_kernel(x_ref, mod_ref, g_ref, wlx_ref, cw_ref, cb_ref, wgf_ref, bgf_ref, wgb_ref, bgb_ref,
                lam_ref, fin_ref, up_s, uext_s, vf_s, g_s, a_s, h_s, *, mod_row):
    ch = CH_CTX
    shift = mod_ref[mod_row:mod_row + 1, 0:D_MODEL]
    scale = mod_ref[mod_row:mod_row + 1, D_MODEL:2 * D_MODEL]
    gmul = g_ref[...] * (1.0 + scale)
    lhs = _modulate(x_ref[0], gmul, shift).astype(BF16)
    u = jnp.dot(lhs, wlx_ref[...], preferred_element_type=F32)
    _to_slab_order(u, up_s, uext_s, ch)
    zero_row = jnp.zeros((1, D_LRU), F32)
    _conv_halo(uext_s, zero_row, zero_row, zero_row, ch)
    for s in range(ch):
        vf_s[s * MC:(s + 1) * MC, :] = _conv_slab(uext_s, s, cw_ref, cb_ref)
    vb = vf_s[...].astype(BF16)
    for d, (wg_ref, bg_ref) in enumerate(((wgf_ref, bgf_ref), (wgb_ref, bgb_ref))):
        _gate_matmuls(vb, wg_ref, g_s, slice(0, ch * MC))
        rate = _half_decay_rate(lam_ref[d:d + 1, :])
        steps = range(ch - 1, -1, -1) if d == 1 else range(ch)
        tot_a, tot_h = _scan_steps(g_s, lambda rows: vf_s[rows, :], bg_ref, rate, a_s, h_s, steps,
                                   _scan_init())
        _, fin = _chunk_carries(tot_a, tot_h, zero_row, reverse=(d == 1))
        fin_ref[0, d:d + 1, :] = fin


def _gate_specs(direction):
    return (pl.BlockSpec((None, N_HEADS, HEAD_DIM, 2 * HEAD_DIM), lambda *_: (direction, 0, 0, 0)),
            pl.BlockSpec((None, 1, 2 * D_LRU), lambda *_: (direction, 0, 0)))


def _ctx_states(ctx, mod, g, win, cw, cb, wg, bg, lam):
    bsz = ctx.shape[0]
    n = CTX_LEN
    full = lambda *shape: pl.BlockSpec(shape, lambda b: (0,) * len(shape))
    return pl.pallas_call(
        functools.partial(_ctx_kernel, mod_row=bsz),
        grid=(bsz,),
        in_specs=[pl.BlockSpec((1, n, D_MODEL), lambda b: (b, 0, 0)),
                  full(8, 3 * D_MODEL),
                  full(1, D_MODEL),
                  pl.BlockSpec((D_MODEL, D_LRU), lambda b: (0, 2)),
                  full(4, MC, D_LRU), full(MC, D_LRU),
                  *_gate_specs(0), *_gate_specs(1),
                  full(2, D_LRU)],
        out_specs=pl.BlockSpec((1, 2, D_LRU), lambda b: (b, 0, 0)),
        out_shape=jax.ShapeDtypeStruct((bsz, 2, D_LRU), F32),
        scratch_shapes=[pltpu.VMEM((N_QL, MC * _pitch(CH_CTX), LANES), F32),
                        pltpu.VMEM((n + 3 * MC, D_LRU), F32),
                        pltpu.VMEM((n, D_LRU), F32),
                        pltpu.VMEM((n, 2 * D_LRU), F32),
                        pltpu.VMEM((n, D_LRU), F32),
                        pltpu.VMEM((n, D_LRU), F32)],
        compiler_params=pltpu.CompilerParams(dimension_semantics=("arbitrary",)),
        name="ctx_states",
    )(ctx, mod, g, win, cw, cb, wg, bg, wg, bg, lam)


def _fwd_kernel(x_ref, xh_ref, mod_ref, g_ref, win_ref, cw_ref, cb_ref, wg_ref,
                bg_ref, lam_ref, h0_ref,
                uf_ref, gfs_ref, gls_ref, v_ref, hf_ref,
                lhs_s, up_s, uext_s, vf_s, g_s, a_s, h_s, cc_s, hc_s):
    j = pl.program_id(1)

    @pl.when(j == 0)
    def _():
        cc_s[...] = jnp.zeros_like(cc_s)
        hc_s[...] = jnp.broadcast_to(h0_ref[0, 0:1, :], hc_s.shape)

    brow = pl.ds(pl.program_id(0), 1)
    shift = mod_ref[brow, 0:D_MODEL]
    scale = mod_ref[brow, D_MODEL:2 * D_MODEL]
    gmul = g_ref[...] * (1.0 + scale)
    for m in range(MC):
        rows = slice(m * CH, (m + 1) * CH)
        lhs_s[rows, :] = _modulate(x_ref[0, rows, :], gmul, shift).astype(BF16)
    hh = _modulate(xh_ref[0], gmul, shift)
    lhs_s[TILE:LHS_ROWS, :] = jnp.concatenate([hh, jnp.zeros_like(hh)], axis=0).astype(BF16)

    u = jnp.dot(lhs_s[...], win_ref[:, 2 * D_FOURIER:2 * D_FOURIER + D_LRU],
                preferred_element_type=F32)

    _to_slab_order(u, up_s, uext_s, CH)
    look = jnp.where(j == N_TILES - 1, 0.0, u[TILE:TILE + 1])
    _conv_halo(uext_s, look, cc_s[0:1, :], cc_s[1:2, :], CH)
    cc_s[0:1, :] = u[TILE - 1:TILE]
    cc_s[1:2, :] = u[TILE - 2:TILE - 1]

    uf_ref[0] = jnp.dot(lhs_s[0:TILE, :], win_ref[:, 0:D_FOURIER],
                        preferred_element_type=F32).astype(BF16)

    for s in range(CH):
        rows = slice(s * MC, (s + 1) * MC)
        v = _conv_slab(uext_s, s, cw_ref, cb_ref)
        vf_s[rows, :] = v
        v_ref[0, rows, :] = v.astype(BF16)
    _gate_matmuls(v_ref[0], wg_ref, g_s, slice(0, TILE))

    for c0, o_ref in ((D_FOURIER, gfs_ref), (2 * D_FOURIER + D_LRU, gls_ref)):
        o_ref[0] = _silu_of_half(jnp.dot(lhs_s[0:TILE, :], win_ref[:, c0:c0 + D_LRU],
                                         preferred_element_type=F32)).astype(BF16)

    hrate = _half_decay_rate(lam_ref[0:1, :])
    tot_a, tot_h = _scan_steps(g_s, lambda rws: vf_s[rws, :], bg_ref, hrate, a_s, h_s,
                               range(CH), _scan_init())
    hin, h_out = _chunk_carries(tot_a, tot_h, hc_s[0:1, :], reverse=False)
    hc_s[...] = jnp.broadcast_to(h_out, hc_s.shape)
    for s in range(CH):
        rows = slice(s * MC, (s + 1) * MC)
        hf_ref[0, rows, :] = (h_s[rows, :] + a_s[rows, :] * hin).astype(BF16)


def _fwd_pass(x, mod, g, win, cw, cb, wg, bg, lam, fin):
    bsz = x.shape[0]
    full = lambda *shape: pl.BlockSpec(shape, lambda b, j: (0,) * len(shape))
    tile = lambda c: pl.BlockSpec((1, TILE, c), lambda b, j: (b, j, 0))
    shp = lambda c: jax.ShapeDtypeStruct((bsz, SEQ, c), BF16)
    rows8 = TILE // 8
    return pl.pallas_call(
        _fwd_kernel,
        grid=(bsz, N_TILES),
        in_specs=[tile(D_MODEL),
                  pl.BlockSpec((1, 8, D_MODEL),
                               lambda b, j: (b, jnp.minimum((j + 1) * rows8, SEQ // 8 - 1), 0)),
                  full(8, 3 * D_MODEL),
                  full(1, D_MODEL), full(D_MODEL, 2 * D_MODEL),
                  full(4, MC, D_LRU), full(MC, D_LRU),
                  *_gate_specs(0),
                  full(2, D_LRU),
                  pl.BlockSpec((1, 2, D_LRU), lambda b, j: (b, 0, 0))],
        out_specs=[tile(D_FOURIER), tile(D_FOURIER), tile(D_LRU), tile(D_LRU), tile(D_LRU)],
        out_shape=[shp(D_FOURIER), shp(D_FOURIER), shp(D_LRU), shp(D_LRU), shp(D_LRU)],
        scratch_shapes=[pltpu.VMEM((LHS_ROWS, D_MODEL), BF16),
                        pltpu.VMEM((N_QL, MC * _pitch(CH), LANES), F32),
                        pltpu.VMEM((TILE + 3 * MC, D_LRU), F32),
                        pltpu.VMEM((TILE, D_LRU), F32),
                        pltpu.VMEM((TILE, 2 * D_LRU), F32),
                        pltpu.VMEM((TILE, D_LRU), F32),
                        pltpu.VMEM((TILE, D_LRU), F32),
                        pltpu.VMEM((8, D_LRU), F32),
                        pltpu.VMEM((8, D_LRU), F32)],
        compiler_params=pltpu.CompilerParams(dimension_semantics=("arbitrary", "arbitrary"),
                                             vmem_limit_bytes=VMEM_LIMIT_BYTES),
        name="fwd_pass",
    )(x, x, mod, g, win, cw, cb, wg, bg, lam, fin)


def _dft_kernel(u_ref, dftc_ref, f1_ref, m2_ref, o_ref, sp_s, sq_s, yr_s, yj_s, so_s):
    p1 = DFT_P1
    p2 = DFT_P2
    dftc = dftc_ref[...].astype(BF16)
    chunks = TILE // DFT_N2
    for blk in range(SEQ // TILE):
        pq = jnp.dot(u_ref[0, blk * TILE:(blk + 1) * TILE, :], dftc, preferred_element_type=F32)
        for m in range(chunks):
            n1 = blk * chunks + m
            rows = slice(m * DFT_N2, (m + 1) * DFT_N2)
            sp_s[n1 * p1:n1 * p1 + DFT_N2, :] = pq[rows, :GROUP_DIM]
            sq_s[n1 * p1:n1 * p1 + DFT_N2, :] = pq[rows, GROUP_DIM:]
    f1 = f1_ref[...].astype(BF16)
    for n2 in range(0, DFT_N2, DFT_NB):
        rhs = jnp.concatenate(
            [jnp.concatenate([sp_s[pl.ds(n2 + i, DFT_N1, stride=p1), :],
                              sq_s[pl.ds(n2 + i, DFT_N1, stride=p1), :]], axis=0)
             for i in range(DFT_NB)], axis=1).astype(BF16)
        y = jnp.dot(f1, rhs, preferred_element_type=F32)
        for i in range(DFT_NB):
            lanes = slice(i * LANES, (i + 1) * LANES)
            yr_s[(n2 + i) * p2:(n2 + i) * p2 + DFT_N1, :] = y[:DFT_N1, lanes]
            yj_s[(n2 + i) * p2:(n2 + i) * p2 + DFT_N1, :] = y[DFT_N1:, lanes]
    for k1 in range(DFT_N1):
        rhs = jnp.concatenate([yr_s[pl.ds(k1, DFT_N2, stride=p2), :],
                               yj_s[pl.ds(k1, DFT_N2, stride=p2), :]], axis=0).astype(BF16)
        so_s[pl.ds(k1, DFT_N2, stride=p2), :] = jnp.dot(m2_ref[k1].astype(BF16), rhs,
                                                         preferred_element_type=F32)
    for k2 in range(DFT_N2):
        o_ref[0, k2 * DFT_N1:(k2 + 1) * DFT_N1, :] = so_s[k2 * p2:k2 * p2 + DFT_N1, :].astype(BF16)


def _fourier_dft(uf, dftc, f1, m2):
    bsz = uf.shape[0]
    blk = pl.BlockSpec((1, SEQ, GROUP_DIM), lambda b, i: (b, 0, i))
    pad1 = DFT_N1 * DFT_P1
    pad2 = DFT_N2 * DFT_P2
    return pl.pallas_call(
        _dft_kernel,
        grid=(bsz, N_GROUPS),
        in_specs=[blk,
                  pl.BlockSpec((GROUP_DIM, 2 * GROUP_DIM), lambda b, i: (0, 0)),
                  pl.BlockSpec((2 * DFT_N1, 2 * DFT_N1), lambda b, i: (0, 0)),
                  pl.BlockSpec((DFT_N1, DFT_N2, 2 * DFT_N2), lambda b, i: (0, 0, 0))],
        out_specs=blk,
        out_shape=jax.ShapeDtypeStruct(uf.shape, BF16),
        scratch_shapes=[pltpu.VMEM((pad1, LANES), F32), pltpu.VMEM((pad1, LANES), F32),
                        pltpu.VMEM((pad2, LANES), F32), pltpu.VMEM((pad2, LANES), F32),
                        pltpu.VMEM((pad2, LANES), F32)],
        compiler_params=pltpu.CompilerParams(dimension_semantics=("arbitrary", "arbitrary"),
                                             vmem_limit_bytes=VMEM_LIMIT_BYTES),
        name="fourier_dft",
    )(uf, dftc, f1, m2)


def _bwd_kernel(x_ref, yf_ref, gfs_ref, gls_ref, v_ref, hf_ref, mod_ref, fg_ref, wfour_ref, wout_ref,
                wg_ref, bg_ref, lam_ref, h0_ref, o_ref,
                g_s, a_s, h_s, yp_s, lhs_s, hc_s):
    j = pl.program_id(1)

    @pl.when(j == 0)
    def _():
        hc_s[...] = jnp.broadcast_to(h0_ref[0, 1:2, :], hc_s.shape)

    _gate_matmuls(v_ref[0], wg_ref, g_s, slice(0, TILE))
    yfw = jnp.dot(yf_ref[0], wfour_ref[...], preferred_element_type=F32)
    lhs_s[:, 0:D_FOURIER] = (yfw * gfs_ref[0].astype(F32)).astype(BF16)

    hrate = _half_decay_rate(lam_ref[1:2, :])
    carry = _scan_steps(g_s, lambda rws: v_ref[0, rws, :].astype(F32), bg_ref, hrate, a_s, h_s,
                        range(CH - 1, -1, -1), _scan_init())
    tot_a, tot_h = carry
    hin, h_out = _chunk_carries(tot_a, tot_h, hc_s[0:1, :], reverse=True)
    hc_s[...] = jnp.broadcast_to(h_out, hc_s.shape)
    pitch = _pitch(CH)
    for s in range(CH):
        rows = slice(s * MC, (s + 1) * MC)
        yl = h_s[rows, :] + a_s[rows, :] * hin + hf_ref[0, rows, :].astype(F32)
        for q in range(N_QL):
            yp_s[q, pl.ds(s, MC, stride=pitch), :] = yl[:, q * LANES:(q + 1) * LANES]
    for m in range(MC):
        rows = slice(m * CH, (m + 1) * CH)
        for q in range(N_QL):
            cols = slice(q * LANES, (q + 1) * LANES)
            yl = yp_s[q, m * pitch:m * pitch + CH, :]
            lhs_s[rows, D_FOURIER + q * LANES:D_FOURIER + (q + 1) * LANES] = (
                yl * gls_ref[0, rows, cols].astype(F32)).astype(BF16)

    res_gate = mod_ref[pl.ds(pl.program_id(0), 1), 2 * D_MODEL:3 * D_MODEL]
    fg = fg_ref[...]
    proj = jnp.dot(lhs_s[...], wout_ref[...], preferred_element_type=F32)
    for m in range(MC):
        rows = slice(m * CH, (m + 1) * CH)
        res = x_ref[0, rows, :] + res_gate * proj[rows, :]
        ms = jnp.mean(res * res, axis=-1, keepdims=True)
        o_ref[0, rows, :] = (res * lax.rsqrt(ms + EPS)) * fg


def _bwd_pass(x, yf, gfs, gls, v, hf, mod, fg, wfour, wout, wg, bg, lam, fin):
    bsz = x.shape[0]
    full = lambda *shape: pl.BlockSpec(shape, lambda b, j: (0,) * len(shape))
    tile = lambda c: pl.BlockSpec((1, TILE, c), lambda b, j: (b, N_TILES - 1 - j, 0))
    return pl.pallas_call(
        _bwd_kernel,
        grid=(bsz, N_TILES),
        in_specs=[tile(D_MODEL), tile(D_FOURIER), tile(D_FOURIER), tile(D_LRU), tile(D_LRU),
                  tile(D_LRU),
                  full(8, 3 * D_MODEL),
                  full(1, D_MODEL), full(D_FOURIER, D_FOURIER), full(D_MODEL, D_MODEL),
                  *_gate_specs(1), full(2, D_LRU),
                  pl.BlockSpec((1, 2, D_LRU), lambda b, j: (b, 0, 0))],
        out_specs=tile(D_MODEL),
        out_shape=jax.ShapeDtypeStruct(x.shape, F32),
        scratch_shapes=[pltpu.VMEM((TILE, 2 * D_LRU), F32),
                        pltpu.VMEM((TILE, D_LRU), F32),
                        pltpu.VMEM((TILE, D_LRU), F32),
                        pltpu.VMEM((N_QL, MC * _pitch(CH), LANES), F32),
                        pltpu.VMEM((TILE, D_MODEL), BF16),
                        pltpu.VMEM((8, D_LRU), F32)],
        compiler_params=pltpu.CompilerParams(dimension_semantics=("arbitrary", "arbitrary"),
                                             vmem_limit_bytes=VMEM_LIMIT_BYTES),
        name="bwd_pass",
    )(x, yf, gfs, gls, v, hf, mod, fg, wfour, wout, wg, bg, lam, fin)


def kernel(x, c, ctx, c_ctx, w_ada, b_ada, norm_gain, w_in, w_four, conv_w, conv_b, w_rg, b_rg,
           w_ig, b_ig, lam, w_out, final_gain):
    bsz = x.shape[0]
    assert x.shape == (bsz, SEQ, D_MODEL) and ctx.shape == (bsz, CTX_LEN, D_MODEL)
    assert w_ada.shape[0] == 1, "single-layer kernel"
    dftc, f1, m2 = _dft_constants()

    mod, win, wout, wfour, wg, bg, cw, cb = _prep(c, c_ctx[None, :], w_ada, b_ada, w_in, w_out,
                                                  w_four, w_rg, w_ig, b_rg, b_ig, conv_w, conv_b)
    g = norm_gain[0][None, :]
    lam0 = lam[0]

    fin = _ctx_states(ctx, mod, g, win, cw, cb, wg, bg, lam0)
    uf, gfs, gls, v, hf = _fwd_pass(x, mod, g, win, cw, cb, wg, bg, lam0, fin)

    yf = _fourier_dft(uf, dftc, f1, m2)

    return _bwd_pass(x, yf, gfs, gls, v, hf, mod, final_gain[None, :], wfour, wout, wg, bg, lam0, fin)
```

```python
import functools

import numpy as np
import jax
import jax.numpy as jnp
from jax import lax
from jax.experimental import pallas as pl
from jax.experimental.pallas import tpu as pltpu

D_MODEL = 1024
SEQ = 8192
CTX_LEN = 256
D_FOURIER = 512
D_LRU = 512
N_GROUPS = 4
GROUP_DIM = 128
N_HEADS = 4
HEAD_DIM = 128
LRU_C = 8.0
EPS = 1e-6

LANES = 128
MC = 16
CH = 64
TILE = MC * CH
N_TILES = SEQ // TILE
CH_CTX = CTX_LEN // MC
N_QL = D_LRU // LANES
LHS_ROWS = TILE + MC
DFT_N1 = 128
DFT_N2 = 64
DFT_NB = 2

VMEM_LIMIT_BYTES = 56 * 1024 * 1024
F32 = jnp.float32
BF16 = jnp.bfloat16


def _pitch(n):
    return n + 4


DFT_P1 = _pitch(DFT_N2)
DFT_P2 = _pitch(DFT_N1)


def _dft_constants():
    c = np.arange(GROUP_DIM)
    ang = 2.0 * np.pi * ((c[:, None] * c[None, :]) % GROUP_DIM) / GROUP_DIM
    dftc = np.concatenate([np.cos(ang), np.sin(ang)], axis=1) * 2.0 ** -3

    n = np.arange(DFT_N1)
    ang1 = 2.0 * np.pi * ((n[:, None] * n[None, :]) % DFT_N1) / DFT_N1
    c1, s1 = np.cos(ang1), np.sin(ang1)
    f1 = np.block([[c1, -s1], [s1, c1]]) * 2.0 ** -4

    k1 = np.arange(DFT_N1)[:, None, None]
    k2 = np.arange(DFT_N2)[None, :, None]
    n2 = np.arange(DFT_N2)[None, None, :]
    ang2 = 2.0 * np.pi * ((n2 * (k1 + DFT_N1 * k2)) % SEQ) / SEQ
    m2 = np.concatenate([np.cos(ang2), -np.sin(ang2)], axis=2) * 2.0 ** -3
    return jnp.asarray(dftc, dtype=F32), jnp.asarray(f1, dtype=F32), jnp.asarray(m2, dtype=F32)


def _silu(x):
    hx = 0.5 * x
    return hx * jnp.tanh(hx) + hx


def _silu_of_half(hx):
    return hx * jnp.tanh(hx) + hx


def _half_decay_rate(lam_row):
    z = -lam_row
    return (-0.5 * LRU_C) * (jnp.maximum(z, 0.0) + jnp.log1p(jnp.exp(-jnp.abs(z))))


def _modulate(xs, gmul, shift):
    ms = jnp.mean(xs * xs, axis=-1, keepdims=True)
    return (xs * lax.rsqrt(ms + EPS)) * gmul + shift


def _gate_ab(hpre_r, hpre_i, hv, hrate):
    a = jnp.exp(jnp.tanh(hpre_r) * hrate + hrate)
    om = 1.0 - a * a
    s = om * lax.rsqrt(jnp.maximum(om, 1e-30))
    return a, s * ((jnp.tanh(hpre_i) + 1.0) * hv)


def _ds(start, size):
    if isinstance(start, int):
        return slice(start, start + size)
    return pl.ds(pl.multiple_of(start, MC), size)


def _to_slab_order(u, up_s, uext_s, ch):
    pitch = _pitch(ch)
    for m in range(MC):
        for q in range(N_QL):
            up_s[q, m * pitch:m * pitch + ch, :] = u[m * ch:(m + 1) * ch, q * LANES:(q + 1) * LANES]
    for s in range(ch):
        for q in range(N_QL):
            uext_s[(s + 2) * MC:(s + 3) * MC, q * LANES:(q + 1) * LANES] = (
                up_s[q, pl.ds(s, MC, stride=pitch), :])


def _conv_halo(uext_s, lookahead, c_prev1, c_prev2, ch):
    last = uext_s[(ch + 1) * MC:(ch + 2) * MC, :]
    last2 = uext_s[ch * MC:(ch + 1) * MC, :]
    first = uext_s[2 * MC:3 * MC, :]
    uext_s[0:MC, :] = jnp.concatenate([c_prev2, last2[:MC - 1]], axis=0)
    uext_s[MC:2 * MC, :] = jnp.concatenate([c_prev1, last[:MC - 1]], axis=0)
    uext_s[(ch + 2) * MC:(ch + 3) * MC, :] = jnp.concatenate([first[1:], lookahead], axis=0)


def _conv_slab(uext_s, s, cw_ref, cb_ref):
    acc = cb_ref[...]
    for k in range(4):
        acc = acc + cw_ref[k] * uext_s[_ds((s + k) * MC, MC), :]
    return acc


def _gate_matmuls(vb, wg_ref, g_s, rows):
    for h in range(N_HEADS):
        res = jnp.dot(vb[:, h * HEAD_DIM:(h + 1) * HEAD_DIM], wg_ref[h], preferred_element_type=F32)
        g_s[rows, h * HEAD_DIM:(h + 1) * HEAD_DIM] = res[:, :HEAD_DIM]
        g_s[rows, D_LRU + h * HEAD_DIM:D_LRU + (h + 1) * HEAD_DIM] = res[:, HEAD_DIM:]


def _scan_init():
    return jnp.ones((MC, D_LRU), F32), jnp.zeros((MC, D_LRU), F32)


def _scan_steps(g_s, load_v, bg_ref, hrate, a_s, h_s, steps, carry):
    b_r = bg_ref[:, :D_LRU]
    b_i = bg_ref[:, D_LRU:]
    acc_a, acc_h = carry
    for s in steps:
        rows = _ds(s * MC, MC)
        pre = g_s[rows, :]
        a, bx = _gate_ab(pre[:, :D_LRU] + b_r, pre[:, D_LRU:] + b_i, load_v(rows), hrate)
        acc_a = a * acc_a
        acc_h = a * acc_h + bx
        a_s[rows, :] = acc_a
        h_s[rows, :] = acc_h
    return acc_a, acc_h


def _chunk_carries(tot_a, tot_h, carry_in, reverse):
    row_id = lax.broadcasted_iota(jnp.int32, (MC, D_LRU), 0)
    hin = jnp.zeros((MC, D_LRU), F32)
    h = carry_in
    order = range(MC - 1, -1, -1) if reverse else range(MC)
    for m in order:
        hin = jnp.where(row_id == m, h, hin)
        h = tot_a[m:m + 1] * h + tot_h[m:m + 1]
    return hin, h


def _prep_kernel(c_ref, cctx_ref, wada_ref, bada_ref, win_ref, wout_ref, wfour_ref, wrg_ref, wig_ref,
                 brg_ref, big_ref, cw_ref, cb_ref,
                 mod_ref, win_o, wout_o, wfour_o, wg_o, bg_o, cwr_o, cbr_o):
    i = pl.program_id(0)
    bsz = c_ref.shape[0]
    cc = jnp.concatenate([c_ref[...], cctx_ref[...], jnp.zeros((8 - bsz - 1, D_MODEL), F32)], axis=0)
    s = _silu(cc).astype(BF16)
    mod_ref[...] = (jnp.dot(s, wada_ref[0].astype(BF16), preferred_element_type=F32)
                    + bada_ref[...])

    @pl.when(i < 2)
    def _():
        win_o[:, 0:D_FOURIER] = win_ref[0, :, 0:D_FOURIER].astype(BF16)
        win_o[:, D_FOURIER:] = (0.5 * win_ref[0, :, D_FOURIER:]).astype(BF16)

    @pl.when(i == 2)
    def _():
        wout_o[...] = wout_ref[0].astype(BF16)
        wfour_o[...] = wfour_ref[0].astype(BF16)
        for d in range(2):
            for h in range(N_HEADS):
                wg_o[d, h] = jnp.concatenate([wrg_ref[0, d, h], wig_ref[0, d, h]],
                                             axis=1).astype(BF16)
            bg_o[d] = 0.5 * jnp.concatenate([brg_ref[0, d:d + 1, :], big_ref[0, d:d + 1, :]], axis=1)
        for k in range(4):
            cwr_o[k] = jnp.broadcast_to(0.5 * cw_ref[0, k:k + 1, :], (MC, D_LRU))
        cbr_o[...] = jnp.broadcast_to(0.5 * cb_ref[...], (MC, D_LRU))


def _prep(c, c_ctx, w_ada, b_ada, w_in, w_out, w_four, w_rg, w_ig, b_rg, b_ig, conv_w, conv_b):
    full = lambda *shape: pl.BlockSpec(shape, lambda i: (0,) * len(shape))
    half = lambda i: jnp.minimum(i, 1)
    gshape = (1, 2, N_HEADS, HEAD_DIM, HEAD_DIM)
    return pl.pallas_call(
        _prep_kernel,
        grid=(3,),
        in_specs=[full(*c.shape), full(1, D_MODEL),
                  pl.BlockSpec((1, D_MODEL, D_MODEL), lambda i: (0, 0, i)),
                  pl.BlockSpec((1, D_MODEL), lambda i: (0, i)),
                  pl.BlockSpec((1, D_MODEL, D_MODEL), lambda i: (0, 0, half(i))),
                  full(1, D_MODEL, D_MODEL), full(1, D_FOURIER, D_FOURIER),
                  full(*gshape), full(*gshape), full(1, 2, D_LRU), full(1, 2, D_LRU),
                  full(1, 4, D_LRU), full(1, D_LRU)],
        out_specs=[pl.BlockSpec((8, D_MODEL), lambda i: (0, i)),
                   pl.BlockSpec((D_MODEL, D_MODEL), lambda i: (0, half(i))),
                   full(D_MODEL, D_MODEL), full(D_FOURIER, D_FOURIER),
                   full(2, N_HEADS, HEAD_DIM, 2 * HEAD_DIM), full(2, 1, 2 * D_LRU),
                   full(4, MC, D_LRU), full(MC, D_LRU)],
        out_shape=[jax.ShapeDtypeStruct((8, 3 * D_MODEL), F32),
                   jax.ShapeDtypeStruct((D_MODEL, 2 * D_MODEL), BF16),
                   jax.ShapeDtypeStruct((D_MODEL, D_MODEL), BF16),
                   jax.ShapeDtypeStruct((D_FOURIER, D_FOURIER), BF16),
                   jax.ShapeDtypeStruct((2, N_HEADS, HEAD_DIM, 2 * HEAD_DIM), BF16),
                   jax.ShapeDtypeStruct((2, 1, 2 * D_LRU), F32),
                   jax.ShapeDtypeStruct((4, MC, D_LRU), F32),
                   jax.ShapeDtypeStruct((MC, D_LRU), F32)],
        compiler_params=pltpu.CompilerParams(dimension_semantics=("arbitrary",),
                                             vmem_limit_bytes=VMEM_LIMIT_BYTES),
        name="prep",
    )(c, c_ctx, w_ada, b_ada, w_in, w_out, w_four, w_rg, w_ig, b_rg, b_ig, conv_w, conv_b)


def _ctx_kernel(x_ref, mod_ref, g_ref, wlx_ref, cw_ref, cb_ref, wgf_ref, bgf_ref, wgb_ref, bgb_ref,
                lam_ref, fin_ref, up_s, uext_s, vf_s, g_s, a_s, h_s, *, mod_row):
    ch = CH_CTX
    shift = mod_ref[mod_row:mod_row + 1, 0:D_MODEL]
    scale = mod_ref[mod_row:mod_row + 1, D_MODEL:2 * D_MODEL]
    gmul = g_ref[...] * (1.0 + scale)
    lhs = _modulate(x_ref[0], gmul, shift).astype(BF16)
    u = jnp.dot(lhs, wlx_ref[...], preferred_element_type=F32)
    _to_slab_order(u, up_s, uext_s, ch)
    zero_row = jnp.zeros((1, D_LRU), F32)
    _conv_halo(uext_s, zero_row, zero_row, zero_row, ch)
    for s in range(ch):
        vf_s[s * MC:(s + 1) * MC, :] = _conv_slab(uext_s, s, cw_ref, cb_ref)
    vb = vf_s[...].astype(BF16)
    for d, (wg_ref, bg_ref) in enumerate(((wgf_ref, bgf_ref), (wgb_ref, bgb_ref))):
        _gate_matmuls(vb, wg_ref, g_s, slice(0, ch * MC))
        rate = _half_decay_rate(lam_ref[d:d + 1, :])
        steps = range(ch - 1, -1, -1) if d == 1 else range(ch)
        tot_a, tot_h = _scan_steps(g_s, lambda rows: vf_s[rows, :], bg_ref, rate, a_s, h_s, steps,
                                   _scan_init())
        _, fin = _chunk_carries(tot_a, tot_h, zero_row, reverse=(d == 1))
        fin_ref[0, d:d + 1, :] = fin


def _gate_specs(direction):
    return (pl.BlockSpec((None, N_HEADS, HEAD_DIM, 2 * HEAD_DIM), lambda *_: (direction, 0, 0, 0)),
            pl.BlockSpec((None, 1, 2 * D_LRU), lambda *_: (direction, 0, 0)))


def _ctx_states(ctx, mod, g, win, cw, cb, wg, bg, lam):
    bsz = ctx.shape[0]
    n = CTX_LEN
    full = lambda *shape: pl.BlockSpec(shape, lambda b: (0,) * len(shape))
    return pl.pallas_call(
        functools.partial(_ctx_kernel, mod_row=bsz),
        grid=(bsz,),
        in_specs=[pl.BlockSpec((1, n, D_MODEL), lambda b: (b, 0, 0)),
                  full(8, 3 * D_MODEL),
                  full(1, D_MODEL),
                  pl.BlockSpec((D_MODEL, D_LRU), lambda b: (0, 2)),
                  full(4, MC, D_LRU), full(MC, D_LRU),
                  *_gate_specs(0), *_gate_specs(1),
                  full(2, D_LRU)],
        out_specs=pl.BlockSpec((1, 2, D_LRU), lambda b: (b, 0, 0)),
        out_shape=jax.ShapeDtypeStruct((bsz, 2, D_LRU), F32),
        scratch_shapes=[pltpu.VMEM((N_QL, MC * _pitch(CH_CTX), LANES), F32),
                        pltpu.VMEM((n + 3 * MC, D_LRU), F32),
                        pltpu.VMEM((n, D_LRU), F32),
                        pltpu.VMEM((n, 2 * D_LRU), F32),
                        pltpu.VMEM((n, D_LRU), F32),
                        pltpu.VMEM((n, D_LRU), F32)],
        compiler_params=pltpu.CompilerParams(dimension_semantics=("arbitrary",)),
        name="ctx_states",
    )(ctx, mod, g, win, cw, cb, wg, bg, wg, bg, lam)


def _fwd_kernel(x_ref, xh_ref, mod_ref, g_ref, win_ref, cw_ref, cb_ref, wg_ref,
                bg_ref, lam_ref, h0_ref,
                uf_ref, gfs_ref, gls_ref, v_ref, hf_ref,
                lhs_s, up_s, uext_s, vf_s, g_s, a_s, h_s, cc_s, hc_s):
    j = pl.program_id(1)

    @pl.when(j == 0)
    def _():
        cc_s[...] = jnp.zeros_like(cc_s)
        hc_s[...] = jnp.broadcast_to(h0_ref[0, 0:1, :], hc_s.shape)

    brow = pl.ds(pl.program_id(0), 1)
    shift = mod_ref[brow, 0:D_MODEL]
    scale = mod_ref[brow, D_MODEL:2 * D_MODEL]
    gmul = g_ref[...] * (1.0 + scale)
    for m in range(MC):
        rows = slice(m * CH, (m + 1) * CH)
        lhs_s[rows, :] = _modulate(x_ref[0, rows, :], gmul, shift).astype(BF16)
    hh = _modulate(xh_ref[0], gmul, shift)
    lhs_s[TILE:LHS_ROWS, :] = jnp.concatenate([hh, jnp.zeros_like(hh)], axis=0).astype(BF16)

    u = jnp.dot(lhs_s[...], win_ref[:, 2 * D_FOURIER:2 * D_FOURIER + D_LRU],
                preferred_element_type=F32)

    _to_slab_order(u, up_s, uext_s, CH)
    look = jnp.where(j == N_TILES - 1, 0.0, u[TILE:TILE + 1])
    _conv_halo(uext_s, look, cc_s[0:1, :], cc_s[1:2, :], CH)
    cc_s[0:1, :] = u[TILE - 1:TILE]
    cc_s[1:2, :] = u[TILE - 2:TILE - 1]

    uf_ref[0] = jnp.dot(lhs_s[0:TILE, :], win_ref[:, 0:D_FOURIER],
                        preferred_element_type=F32).astype(BF16)

    for s in range(CH):
        rows = slice(s * MC, (s + 1) * MC)
        v = _conv_slab(uext_s, s, cw_ref, cb_ref)
        vf_s[rows, :] = v
        v_ref[0, rows, :] = v.astype(BF16)
    _gate_matmuls(v_ref[0], wg_ref, g_s, slice(0, TILE))

    for c0, o_ref in ((D_FOURIER, gfs_ref), (2 * D_FOURIER + D_LRU, gls_ref)):
        o_ref[0] = _silu_of_half(jnp.dot(lhs_s[0:TILE, :], win_ref[:, c0:c0 + D_LRU],
                                         preferred_element_type=F32)).astype(BF16)

    hrate = _half_decay_rate(lam_ref[0:1, :])
    tot_a, tot_h = _scan_steps(g_s, lambda rws: vf_s[rws, :], bg_ref, hrate, a_s, h_s,
                               range(CH), _scan_init())
    hin, h_out = _chunk_carries(tot_a, tot_h, hc_s[0:1, :], reverse=False)
    hc_s[...] = jnp.broadcast_to(h_out, hc_s.shape)
    for s in range(CH):
        rows = slice(s * MC, (s + 1) * MC)
        hf_ref[0, rows, :] = (h_s[rows, :] + a_s[rows, :] * hin).astype(BF16)


def _fwd_pass(x, mod, g, win, cw, cb, wg, bg, lam, fin):
    bsz = x.shape[0]
    full = lambda *shape: pl.BlockSpec(shape, lambda b, j: (0,) * len(shape))
    tile = lambda c: pl.BlockSpec((1, TILE, c), lambda b, j: (b, j, 0))
    shp = lambda c: jax.ShapeDtypeStruct((bsz, SEQ, c), BF16)
    rows8 = TILE // 8
    return pl.pallas_call(
        _fwd_kernel,
        grid=(bsz, N_TILES),
        in_specs=[tile(D_MODEL),
                  pl.BlockSpec((1, 8, D_MODEL),
                               lambda b, j: (b, jnp.minimum((j + 1) * rows8, SEQ // 8 - 1), 0)),
                  full(8, 3 * D_MODEL),
                  full(1, D_MODEL), full(D_MODEL, 2 * D_MODEL),
                  full(4, MC, D_LRU), full(MC, D_LRU),
                  *_gate_specs(0),
                  full(2, D_LRU),
                  pl.BlockSpec((1, 2, D_LRU), lambda b, j: (b, 0, 0))],
        out_specs=[tile(D_FOURIER), tile(D_FOURIER), tile(D_LRU), tile(D_LRU), tile(D_LRU)],
        out_shape=[shp(D_FOURIER), shp(D_FOURIER), shp(D_LRU), shp(D_LRU), shp(D_LRU)],
        scratch_shapes=[pltpu.VMEM((LHS_ROWS, D_MODEL), BF16),
                        pltpu.VMEM((N_QL, MC * _pitch(CH), LANES), F32),
                        pltpu.VMEM((TILE + 3 * MC, D_LRU), F32),
                        pltpu.VMEM((TILE, D_LRU), F32),
                        pltpu.VMEM((TILE, 2 * D_LRU), F32),
                        pltpu.VMEM((TILE, D_LRU), F32),
                        pltpu.VMEM((TILE, D_LRU), F32),
                        pltpu.VMEM((8, D_LRU), F32),
                        pltpu.VMEM((8, D_LRU), F32)],
        compiler_params=pltpu.CompilerParams(dimension_semantics=("arbitrary", "arbitrary"),
                                             vmem_limit_bytes=VMEM_LIMIT_BYTES),
        name="fwd_pass",
    )(x, x, mod, g, win, cw, cb, wg, bg, lam, fin)


def _dft_kernel(u_ref, dftc_ref, f1_ref, m2_ref, o_ref, sp_s, sq_s, yr_s, yj_s, so_s):
    p1 = DFT_P1
    p2 = DFT_P2
    dftc = dftc_ref[...].astype(BF16)
    chunks = TILE // DFT_N2
    for blk in range(SEQ // TILE):
        pq = jnp.dot(u_ref[0, blk * TILE:(blk + 1) * TILE, :], dftc, preferred_element_type=F32)
        for m in range(chunks):
            n1 = blk * chunks + m
            rows = slice(m * DFT_N2, (m + 1) * DFT_N2)
            sp_s[n1 * p1:n1 * p1 + DFT_N2, :] = pq[rows, :GROUP_DIM]
            sq_s[n1 * p1:n1 * p1 + DFT_N2, :] = pq[rows, GROUP_DIM:]
    f1 = f1_ref[...].astype(BF16)
    for n2 in range(0, DFT_N2, DFT_NB):
        rhs = jnp.concatenate(
            [jnp.concatenate([sp_s[pl.ds(n2 + i, DFT_N1, stride=p1), :],
                              sq_s[pl.ds(n2 + i, DFT_N1, stride=p1), :]], axis=0)
             for i in range(DFT_NB)], axis=1).astype(BF16)
        y = jnp.dot(f1, rhs, preferred_element_type=F32)
        for i in range(DFT_NB):
            lanes = slice(i * LANES, (i + 1) * LANES)
            yr_s[(n2 + i) * p2:(n2 + i) * p2 + DFT_N1, :] = y[:DFT_N1, lanes]
            yj_s[(n2 + i) * p2:(n2 + i) * p2 + DFT_N1, :] = y[DFT_N1:, lanes]
    for k1 in range(DFT_N1):
        rhs = jnp.concatenate([yr_s[pl.ds(k1, DFT_N2, stride=p2), :],
                               yj_s[pl.ds(k1, DFT_N2, stride=p2), :]], axis=0).astype(BF16)
        so_s[pl.ds(k1, DFT_N2, stride=p2), :] = jnp.dot(m2_ref[k1].astype(BF16), rhs,
                                                         preferred_element_type=F32)
    for k2 in range(DFT_N2):
        o_ref[0, k2 * DFT_N1:(k2 + 1) * DFT_N1, :] = so_s[k2 * p2:k2 * p2 + DFT_N1, :].astype(BF16)


def _fourier_dft(uf, dftc, f1, m2):
    bsz = uf.shape[0]
    blk = pl.BlockSpec((1, SEQ, GROUP_DIM), lambda b, i: (b, 0, i))
    pad1 = DFT_N1 * DFT_P1
    pad2 = DFT_N2 * DFT_P2
    return pl.pallas_call(
        _dft_kernel,
        grid=(bsz, N_GROUPS),
        in_specs=[blk,
                  pl.BlockSpec((GROUP_DIM, 2 * GROUP_DIM), lambda b, i: (0, 0)),
                  pl.BlockSpec((2 * DFT_N1, 2 * DFT_N1), lambda b, i: (0, 0)),
                  pl.BlockSpec((DFT_N1, DFT_N2, 2 * DFT_N2), lambda b, i: (0, 0, 0))],
        out_specs=blk,
        out_shape=jax.ShapeDtypeStruct(uf.shape, BF16),
        scratch_shapes=[pltpu.VMEM((pad1, LANES), F32), pltpu.VMEM((pad1, LANES), F32),
                        pltpu.VMEM((pad2, LANES), F32), pltpu.VMEM((pad2, LANES), F32),
                        pltpu.VMEM((pad2, LANES), F32)],
        compiler_params=pltpu.CompilerParams(dimension_semantics=("arbitrary", "arbitrary"),
                                             vmem_limit_bytes=VMEM_LIMIT_BYTES),
        name="fourier_dft",
    )(uf, dftc, f1, m2)


def _bwd_kernel(x_ref, yf_ref, gfs_ref, gls_ref, v_ref, hf_ref, mod_ref, fg_ref, wfour_ref, wout_ref,
                wg_ref, bg_ref, lam_ref, h0_ref, o_ref,
                g_s, a_s, h_s, yp_s, lhs_s, hc_s):
    j = pl.program_id(1)

    @pl.when(j == 0)
    def _():
        hc_s[...] = jnp.broadcast_to(h0_ref[0, 1:2, :], hc_s.shape)

    _gate_matmuls(v_ref[0], wg_ref, g_s, slice(0, TILE))
    yfw = jnp.dot(yf_ref[0], wfour_ref[...], preferred_element_type=F32)
    lhs_s[:, 0:D_FOURIER] = (yfw * gfs_ref[0].astype(F32)).astype(BF16)

    hrate = _half_decay_rate(lam_ref[1:2, :])
    carry = _scan_steps(g_s, lambda rws: v_ref[0, rws, :].astype(F32), bg_ref, hrate, a_s, h_s,
                        range(CH - 1, -1, -1), _scan_init())
    tot_a, tot_h = carry
    hin, h_out = _chunk_carries(tot_a, tot_h, hc_s[0:1, :], reverse=True)
    hc_s[...] = jnp.broadcast_to(h_out, hc_s.shape)
    pitch = _pitch(CH)
    for s in range(CH):
        rows = slice(s * MC, (s + 1) * MC)
        yl = h_s[rows, :] + a_s[rows, :] * hin + hf_ref[0, rows, :].astype(F32)
        for q in range(N_QL):
            yp_s[q, pl.ds(s, MC, stride=pitch), :] = yl[:, q * LANES:(q + 1) * LANES]
    for m in range(MC):
        rows = slice(m * CH, (m + 1) * CH)
        for q in range(N_QL):
            cols = slice(q * LANES, (q + 1) * LANES)
            yl = yp_s[q, m * pitch:m * pitch + CH, :]
            lhs_s[rows, D_FOURIER + q * LANES:D_FOURIER + (q + 1) * LANES] = (
                yl * gls_ref[0, rows, cols].astype(F32)).astype(BF16)

    res_gate = mod_ref[pl.ds(pl.program_id(0), 1), 2 * D_MODEL:3 * D_MODEL]
    fg = fg_ref[...]
    proj = jnp.dot(lhs_s[...], wout_ref[...], preferred_element_type=F32)
    for m in range(MC):
        rows = slice(m * CH, (m + 1) * CH)
        res = x_ref[0, rows, :] + res_gate * proj[rows, :]
        ms = jnp.mean(res * res, axis=-1, keepdims=True)
        o_ref[0, rows, :] = (res * lax.rsqrt(ms + EPS)) * fg


def _bwd_pass(x, yf, gfs, gls, v, hf, mod, fg, wfour, wout, wg, bg, lam, fin):
    bsz = x.shape[0]
    full = lambda *shape: pl.BlockSpec(shape, lambda b, j: (0,) * len(shape))
    tile = lambda c: pl.BlockSpec((1, TILE, c), lambda b, j: (b, N_TILES - 1 - j, 0))
    return pl.pallas_call(
        _bwd_kernel,
        grid=(bsz, N_TILES),
        in_specs=[tile(D_MODEL), tile(D_FOURIER), tile(D_FOURIER), tile(D_LRU), tile(D_LRU),
                  tile(D_LRU),
                  full(8, 3 * D_MODEL),
                  full(1, D_MODEL), full(D_FOURIER, D_FOURIER), full(D_MODEL, D_MODEL),
                  *_gate_specs(1), full(2, D_LRU),
                  pl.BlockSpec((1, 2, D_LRU), lambda b, j: (b, 0, 0))],
        out_specs=tile(D_MODEL),
        out_shape=jax.ShapeDtypeStruct(x.shape, F32),
        scratch_shapes=[pltpu.VMEM((TILE, 2 * D_LRU), F32),
                        pltpu.VMEM((TILE, D_LRU), F32),
                        pltpu.VMEM((TILE, D_LRU), F32),
                        pltpu.VMEM((N_QL, MC * _pitch(CH), LANES), F32),
                        pltpu.VMEM((TILE, D_MODEL), BF16),
                        pltpu.VMEM((8, D_LRU), F32)],
        compiler_params=pltpu.CompilerParams(dimension_semantics=("arbitrary", "arbitrary"),
                                             vmem_limit_bytes=VMEM_LIMIT_BYTES),
        name="bwd_pass",
    )(x, yf, gfs, gls, v, hf, mod, fg, wfour, wout, wg, bg, lam, fin)


def kernel(x, c, ctx, c_ctx, w_ada, b_ada, norm_gain, w_in, w_four, conv_w, conv_b, w_rg, b_rg,
           w_ig, b_ig, lam, w_out, final_gain):
    bsz = x.shape[0]
    assert x.shape == (bsz, SEQ, D_MODEL) and ctx.shape == (bsz, CTX_LEN, D_MODEL)
    assert w_ada.shape[0] == 1, "single-layer kernel"
    dftc, f1, m2 = _dft_constants()

    mod, win, wout, wfour, wg, bg, cw, cb = _prep(c, c_ctx[None, :], w_ada, b_ada, w_in, w_out,
                                                  w_four, w_rg, w_ig, b_rg, b_ig, conv_w, conv_b)
    g = norm_gain[0][None, :]
    lam0 = lam[0]

    fin = _ctx_states(ctx, mod, g, win, cw, cb, wg, bg, lam0)
    uf, gfs, gls, v, hf = _fwd_pass(x, mod, g, win, cw, cb, wg, bg, lam0, fin)

    yf = _fourier_dft(uf, dftc, f1, m2)

    return _bwd_pass(x, yf, gfs, gls, v, hf, mod, final_gain[None, :], wfour, wout, wg, bg, lam0, fin)
```

```python
import functools

import numpy as np
import jax
import jax.numpy as jnp
from jax import lax
from jax.experimental import pallas as pl
from jax.experimental.pallas import tpu as pltpu

D_MODEL = 1024
SEQ = 8192
CTX_LEN = 256
D_FOURIER = 512
D_LRU = 512
N_GROUPS = 4
GROUP_DIM = 128
N_HEADS = 4
HEAD_DIM = 128
LRU_C = 8.0
EPS = 1e-6
LOG2_E = float(np.log2(np.e))

LANES = 128
MC = 16
CH = 64
TILE = MC * CH
N_TILES = SEQ // TILE
CH_CTX = CTX_LEN // MC
N_QL = D_LRU // LANES
assert HEAD_DIM == LANES
LHS_ROWS = TILE + MC
DFT_N1 = 128
DFT_N2 = 64
DFT_NB = 2

VMEM_LIMIT_BYTES = 56 * 1024 * 1024
F32 = jnp.float32
BF16 = jnp.bfloat16


def _pitch(n):
    return n + 4


DFT_P1 = _pitch(DFT_N2)
DFT_P2 = _pitch(DFT_N1)


def _dft_constants():
    c = np.arange(GROUP_DIM)
    ang = 2.0 * np.pi * ((c[:, None] * c[None, :]) % GROUP_DIM) / GROUP_DIM
    dftc = np.concatenate([np.cos(ang), np.sin(ang)], axis=1) * 2.0 ** -3

    n = np.arange(DFT_N1)
    ang1 = 2.0 * np.pi * ((n[:, None] * n[None, :]) % DFT_N1) / DFT_N1
    c1, s1 = np.cos(ang1), np.sin(ang1)
    f1 = np.block([[c1, -s1], [s1, c1]]) * 2.0 ** -4

    k1 = np.arange(DFT_N1)[:, None, None]
    k2 = np.arange(DFT_N2)[None, :, None]
    n2 = np.arange(DFT_N2)[None, None, :]
    ang2 = 2.0 * np.pi * ((n2 * (k1 + DFT_N1 * k2)) % SEQ) / SEQ
    m2 = np.concatenate([np.cos(ang2), -np.sin(ang2)], axis=2) * 2.0 ** -3
    return jnp.asarray(dftc, dtype=F32), jnp.asarray(f1, dtype=F32), jnp.asarray(m2, dtype=F32)


def _silu(x):
    hx = 0.5 * x
    return hx * jnp.tanh(hx) + hx


def _silu_of_half(hx):
    return hx * jnp.tanh(hx) + hx


def _half_decay_rate(lam_row):
    z = -lam_row
    return (-0.5 * LRU_C * LOG2_E) * (jnp.maximum(z, 0.0) + jnp.log1p(jnp.exp(-jnp.abs(z))))


def _modulate(xs, gmul, shift):
    ms = jnp.mean(xs * xs, axis=-1, keepdims=True)
    return (xs * lax.rsqrt(ms + EPS)) * gmul + shift


def _gate_ab(hpre_r, hpre_i, hv, hrate):
    a = jnp.exp2(jnp.tanh(hpre_r) * hrate + hrate)
    om = 1.0 - a * a
    s = om * lax.rsqrt(jnp.maximum(om, 1e-30))
    return a, s * ((jnp.tanh(hpre_i) + 1.0) * hv)


def _ds(start, size):
    if isinstance(start, int):
        return slice(start, start + size)
    return pl.ds(pl.multiple_of(start, MC), size)


def _to_slab_order(u, up_s, uext_s, ch):
    pitch = _pitch(ch)
    for m in range(MC):
        for q in range(N_QL):
            up_s[q, m * pitch:m * pitch + ch, :] = u[m * ch:(m + 1) * ch, q * LANES:(q + 1) * LANES]
    for s in range(ch):
        for q in range(N_QL):
            uext_s[(s + 2) * MC:(s + 3) * MC, q * LANES:(q + 1) * LANES] = (
                up_s[q, pl.ds(s, MC, stride=pitch), :])


def _conv_halo(uext_s, lookahead, c_prev1, c_prev2, ch):
    last = uext_s[(ch + 1) * MC:(ch + 2) * MC, :]
    last2 = uext_s[ch * MC:(ch + 1) * MC, :]
    first = uext_s[2 * MC:3 * MC, :]
    uext_s[0:MC, :] = jnp.concatenate([c_prev2, last2[:MC - 1]], axis=0)
    uext_s[MC:2 * MC, :] = jnp.concatenate([c_prev1, last[:MC - 1]], axis=0)
    uext_s[(ch + 2) * MC:(ch + 3) * MC, :] = jnp.concatenate([first[1:], lookahead], axis=0)


def _conv_slab(uext_s, s, cw_ref, cb_ref):
    acc = cb_ref[...]
    for k in range(4):
        acc = acc + cw_ref[k] * uext_s[_ds((s + k) * MC, MC), :]
    return acc


def _head_cols(h, base=0):
    return slice(base + h * HEAD_DIM, base + (h + 1) * HEAD_DIM)


def _gate_terms(h, vb, hv, wg_ref, bg_ref, hrate, g_s, rows):
    cols, icols = _head_cols(h), _head_cols(h, D_LRU)
    res = jnp.dot(vb, wg_ref[h], preferred_element_type=F32)
    a, bx = _gate_ab(res[:, :HEAD_DIM] + bg_ref[:, cols], res[:, HEAD_DIM:] + bg_ref[:, icols],
                     hv, hrate[:, cols])
    g_s[rows, cols] = a
    g_s[rows, icols] = bx


def _scan_head(h, g_s, a_s, h_s, steps):
    cols, icols = _head_cols(h), _head_cols(h, D_LRU)
    acc_a = jnp.ones((MC, HEAD_DIM), F32)
    acc_h = jnp.zeros((MC, HEAD_DIM), F32)
    for s in steps:
        rows = slice(s * MC, (s + 1) * MC)
        a = g_s[rows, cols]
        acc_a = a * acc_a
        acc_h = a * acc_h + g_s[rows, icols]
        a_s[rows, cols] = acc_a
        h_s[rows, cols] = acc_h
    return acc_a, acc_h


def _chunk_carries(tot_a, tot_h, carry_in, reverse):
    row_id = lax.broadcasted_iota(jnp.int32, tot_a.shape, 0)
    hin = jnp.zeros(tot_a.shape, F32)
    h = carry_in
    order = range(MC - 1, -1, -1) if reverse else range(MC)
    for m in order:
        hin = jnp.where(row_id == m, h, hin)
        h = tot_a[m:m + 1] * h + tot_h[m:m + 1]
    return hin, h


def _prep_kernel(c_ref, cctx_ref, wada_ref, bada_ref, win_ref, wout_ref, wfour_ref, wrg_ref, wig_ref,
                 brg_ref, big_ref, cw_ref, cb_ref,
                 mod_ref, win_o, wout_o, wfour_o, wg_o, bg_o, cwr_o, cbr_o):
    i = pl.program_id(0)
    bsz = c_ref.shape[0]
    cc = jnp.concatenate([c_ref[...], cctx_ref[...], jnp.zeros((8 - bsz - 1, D_MODEL), F32)], axis=0)
    s = _silu(cc).astype(BF16)
    mod_ref[...] = (jnp.dot(s, wada_ref[0].astype(BF16), preferred_element_type=F32)
                    + bada_ref[...])

    @pl.when(i < 2)
    def _():
        win_o[:, 0:D_FOURIER] = win_ref[0, :, 0:D_FOURIER].astype(BF16)
        win_o[:, D_FOURIER:] = (0.5 * win_ref[0, :, D_FOURIER:]).astype(BF16)

    @pl.when(i == 2)
    def _():
        wout_o[...] = wout_ref[0].astype(BF16)
        wfour_o[...] = wfour_ref[0].astype(BF16)
        for d in range(2):
            for h in range(N_HEADS):
                wg_o[d, h] = jnp.concatenate([wrg_ref[0, d, h], wig_ref[0, d, h]],
                                             axis=1).astype(BF16)
            bg_o[d] = 0.5 * jnp.concatenate([brg_ref[0, d:d + 1, :], big_ref[0, d:d + 1, :]], axis=1)
        for k in range(4):
            cwr_o[k] = jnp.broadcast_to(0.5 * cw_ref[0, k:k + 1, :], (MC, D_LRU))
        cbr_o[...] = jnp.broadcast_to(0.5 * cb_ref[...], (MC, D_LRU))


def _prep(c, c_ctx, w_ada, b_ada, w_in, w_out, w_four, w_rg, w_ig, b_rg, b_ig, conv_w, conv_b):
    full = lambda *shape: pl.BlockSpec(shape, lambda i: (0,) * len(shape))
    half = lambda i: jnp.minimum(i, 1)
    gshape = (1, 2, N_HEADS, HEAD_DIM, HEAD_DIM)
    return pl.pallas_call(
        _prep_kernel,
        grid=(3,),
        in_specs=[full(*c.shape), full(1, D_MODEL),
                  pl.BlockSpec((1, D_MODEL, D_MODEL), lambda i: (0, 0, i)),
                  pl.BlockSpec((1, D_MODEL), lambda i: (0, i)),
                  pl.BlockSpec((1, D_MODEL, D_MODEL), lambda i: (0, 0, half(i))),
                  full(1, D_MODEL, D_MODEL), full(1, D_FOURIER, D_FOURIER),
                  full(*gshape), full(*gshape), full(1, 2, D_LRU), full(1, 2, D_LRU),
                  full(1, 4, D_LRU), full(1, D_LRU)],
        out_specs=[pl.BlockSpec((8, D_MODEL), lambda i: (0, i)),
                   pl.BlockSpec((D_MODEL, D_MODEL), lambda i: (0, half(i))),
                   full(D_MODEL, D_MODEL), full(D_FOURIER, D_FOURIER),
                   full(2, N_HEADS, HEAD_DIM, 2 * HEAD_DIM), full(2, 1, 2 * D_LRU),
                   full(4, MC, D_LRU), full(MC, D_LRU)],
        out_shape=[jax.ShapeDtypeStruct((8, 3 * D_MODEL), F32),
                   jax.ShapeDtypeStruct((D_MODEL, 2 * D_MODEL), BF16),
                   jax.ShapeDtypeStruct((D_MODEL, D_MODEL), BF16),
                   jax.ShapeDtypeStruct((D_FOURIER, D_FOURIER), BF16),
                   jax.ShapeDtypeStruct((2, N_HEADS, HEAD_DIM, 2 * HEAD_DIM), BF16),
                   jax.ShapeDtypeStruct((2, 1, 2 * D_LRU), F32),
                   jax.ShapeDtypeStruct((4, MC, D_LRU), F32),
                   jax.ShapeDtypeStruct((MC, D_LRU), F32)],
        compiler_params=pltpu.CompilerParams(dimension_semantics=("arbitrary",),
                                             vmem_limit_bytes=VMEM_LIMIT_BYTES),
        name="prep",
    )(c, c_ctx, w_ada, b_ada, w_in, w_out, w_four, w_rg, w_ig, b_rg, b_ig, conv_w, conv_b)


def _ctx_kernel(x_ref, mod_ref, g_ref, wlx_ref, cw_ref, cb_ref, wgf_ref, bgf_ref, wgb_ref, bgb_ref,
                lam_ref, fin_ref, up_s, uext_s, vf_s, g_s, a_s, h_s, *, mod_row):
    ch = CH_CTX
    shift = mod_ref[mod_row:mod_row + 1, 0:D_MODEL]
    scale = mod_ref[mod_row:mod_row + 1, D_MODEL:2 * D_MODEL]
    gmul = g_ref[...] * (1.0 + scale)
    lhs = _modulate(x_ref[0], gmul, shift).astype(BF16)
    u = jnp.dot(lhs, wlx_ref[...], preferred_element_type=F32)
    _to_slab_order(u, up_s, uext_s, ch)
    zero_row = jnp.zeros((1, D_LRU), F32)
    _conv_halo(uext_s, zero_row, zero_row, zero_row, ch)
    for s in range(ch):
        vf_s[s * MC:(s + 1) * MC, :] = _conv_slab(uext_s, s, cw_ref, cb_ref)
    vb = vf_s[...].astype(BF16)
    for d, (wg_ref, bg_ref) in enumerate(((wgf_ref, bgf_ref), (wgb_ref, bgb_ref))):
        rate = _half_decay_rate(lam_ref[d:d + 1, :])
        steps = range(ch - 1, -1, -1) if d == 1 else range(ch)
        for h in range(N_HEADS):
            cols = _head_cols(h)
            _gate_terms(h, vb[:, cols], vf_s[:, cols], wg_ref, bg_ref, rate, g_s,
                        slice(0, ch * MC))
            tot_a, tot_h = _scan_head(h, g_s, a_s, h_s, steps)
            _, fin = _chunk_carries(tot_a, tot_h, zero_row[:, cols], reverse=(d == 1))
            fin_ref[0, d:d + 1, cols] = fin


def _gate_specs(direction):
    return (pl.BlockSpec((None, N_HEADS, HEAD_DIM, 2 * HEAD_DIM), lambda *_: (direction, 0, 0, 0)),
            pl.BlockSpec((None, 1, 2 * D_LRU), lambda *_: (direction, 0, 0)))


def _ctx_states(ctx, mod, g, win, cw, cb, wg, bg, lam):
    bsz = ctx.shape[0]
    n = CTX_LEN
    full = lambda *shape: pl.BlockSpec(shape, lambda b: (0,) * len(shape))
    return pl.pallas_call(
        functools.partial(_ctx_kernel, mod_row=bsz),
        grid=(bsz,),
        in_specs=[pl.BlockSpec((1, n, D_MODEL), lambda b: (b, 0, 0)),
                  full(8, 3 * D_MODEL),
                  full(1, D_MODEL),
                  pl.BlockSpec((D_MODEL, D_LRU), lambda b: (0, 2)),
                  full(4, MC, D_LRU), full(MC, D_LRU),
                  *_gate_specs(0), *_gate_specs(1),
                  full(2, D_LRU)],
        out_specs=pl.BlockSpec((1, 2, D_LRU), lambda b: (b, 0, 0)),
        out_shape=jax.ShapeDtypeStruct((bsz, 2, D_LRU), F32),
        scratch_shapes=[pltpu.VMEM((N_QL, MC * _pitch(CH_CTX), LANES), F32),
                        pltpu.VMEM((n + 3 * MC, D_LRU), F32),
                        pltpu.VMEM((n, D_LRU), F32),
                        pltpu.VMEM((n, 2 * D_LRU), F32),
                        pltpu.VMEM((n, D_LRU), F32),
                        pltpu.VMEM((n, D_LRU), F32)],
        compiler_params=pltpu.CompilerParams(dimension_semantics=("arbitrary",)),
        name="ctx_states",
    )(ctx, mod, g, win, cw, cb, wg, bg, wg, bg, lam)


def _fwd_kernel(x_ref, xh_ref, mod_ref, g_ref, win_ref, cw_ref, cb_ref, wg_ref,
                bg_ref, lam_ref, h0_ref,
                uf_ref, gfs_ref, gls_ref, v_ref, hf_ref,
                lhs_s, up_s, uext_s, vf_s, g_s, a_s, h_s, cc_s, hc_s):
    j = pl.program_id(1)

    @pl.when(j == 0)
    def _():
        cc_s[...] = jnp.zeros_like(cc_s)
        hc_s[...] = jnp.broadcast_to(h0_ref[0, 0:1, :], hc_s.shape)

    brow = pl.ds(pl.program_id(0), 1)
    shift = mod_ref[brow, 0:D_MODEL]
    scale = mod_ref[brow, D_MODEL:2 * D_MODEL]
    gmul = g_ref[...] * (1.0 + scale)
    for m in range(MC):
        rows = slice(m * CH, (m + 1) * CH)
        lhs_s[rows, :] = _modulate(x_ref[0, rows, :], gmul, shift).astype(BF16)
    hh = _modulate(xh_ref[0], gmul, shift)
    lhs_s[TILE:LHS_ROWS, :] = jnp.concatenate([hh, jnp.zeros_like(hh)], axis=0).astype(BF16)

    u = jnp.dot(lhs_s[...], win_ref[:, 2 * D_FOURIER:2 * D_FOURIER + D_LRU],
                preferred_element_type=F32)

    _to_slab_order(u, up_s, uext_s, CH)
    look = jnp.where(j == N_TILES - 1, 0.0, u[TILE:TILE + 1])
    _conv_halo(uext_s, look, cc_s[0:1, :], cc_s[1:2, :], CH)
    cc_s[0:1, :] = u[TILE - 1:TILE]
    cc_s[1:2, :] = u[TILE - 2:TILE - 1]

    uf_ref[0] = jnp.dot(lhs_s[0:TILE, :], win_ref[:, 0:D_FOURIER],
                        preferred_element_type=F32).astype(BF16)

    for s in range(CH):
        rows = slice(s * MC, (s + 1) * MC)
        v = _conv_slab(uext_s, s, cw_ref, cb_ref)
        vf_s[rows, :] = v
        v_ref[0, rows, :] = v.astype(BF16)

    halves = (slice(0, TILE // 2), slice(TILE // 2, TILE))
    hrate = _half_decay_rate(lam_ref[0:1, :])
    for h in range(N_HEADS):
        cols = _head_cols(h)
        _gate_terms(h, v_ref[0, :, cols], vf_s[:, cols], wg_ref, bg_ref, hrate, g_s, slice(0, TILE))
        o_ref, c0 = ((gfs_ref, D_FOURIER), (gls_ref, 2 * D_FOURIER + D_LRU))[h // 2]
        rq = halves[h % 2]
        o_ref[0, rq, :] = _silu_of_half(jnp.dot(lhs_s[rq, :], win_ref[:, c0:c0 + D_LRU],
                                                preferred_element_type=F32)).astype(BF16)
        tot_a, tot_h = _scan_head(h, g_s, a_s, h_s, range(CH))
        hin, h_out = _chunk_carries(tot_a, tot_h, hc_s[0:1, cols], reverse=False)
        hc_s[:, cols] = jnp.broadcast_to(h_out, (hc_s.shape[0], HEAD_DIM))
        for s in range(CH):
            rows = slice(s * MC, (s + 1) * MC)
            hf_ref[0, rows, cols] = (h_s[rows, cols] + a_s[rows, cols] * hin).astype(BF16)


def _fwd_pass(x, mod, g, win, cw, cb, wg, bg, lam, fin):
    bsz = x.shape[0]
    full = lambda *shape: pl.BlockSpec(shape, lambda b, j: (0,) * len(shape))
    tile = lambda c: pl.BlockSpec((1, TILE, c), lambda b, j: (b, j, 0))
    shp = lambda c: jax.ShapeDtypeStruct((bsz, SEQ, c), BF16)
    rows8 = TILE // 8
    return pl.pallas_call(
        _fwd_kernel,
        grid=(bsz, N_TILES),
        in_specs=[tile(D_MODEL),
                  pl.BlockSpec((1, 8, D_MODEL),
                               lambda b, j: (b, jnp.minimum((j + 1) * rows8, SEQ // 8 - 1), 0)),
                  full(8, 3 * D_MODEL),
                  full(1, D_MODEL), full(D_MODEL, 2 * D_MODEL),
                  full(4, MC, D_LRU), full(MC, D_LRU),
                  *_gate_specs(0),
                  full(2, D_LRU),
                  pl.BlockSpec((1, 2, D_LRU), lambda b, j: (b, 0, 0))],
        out_specs=[tile(D_FOURIER), tile(D_FOURIER), tile(D_LRU), tile(D_LRU), tile(D_LRU)],
        out_shape=[shp(D_FOURIER), shp(D_FOURIER), shp(D_LRU), shp(D_LRU), shp(D_LRU)],
        scratch_shapes=[pltpu.VMEM((LHS_ROWS, D_MODEL), BF16),
                        pltpu.VMEM((N_QL, MC * _pitch(CH), LANES), F32),
                        pltpu.VMEM((TILE + 3 * MC, D_LRU), F32),
                        pltpu.VMEM((TILE, D_LRU), F32),
                        pltpu.VMEM((TILE, 2 * D_LRU), F32),
                        pltpu.VMEM((TILE, D_LRU), F32),
                        pltpu.VMEM((TILE, D_LRU), F32),
                        pltpu.VMEM((8, D_LRU), F32),
                        pltpu.VMEM((8, D_LRU), F32)],
        compiler_params=pltpu.CompilerParams(dimension_semantics=("arbitrary", "arbitrary"),
                                             vmem_limit_bytes=VMEM_LIMIT_BYTES),
        name="fwd_pass",
    )(x, x, mod, g, win, cw, cb, wg, bg, lam, fin)


def _dft_kernel(u_ref, dftc_ref, f1_ref, m2_ref, o_ref, sp_s, sq_s, yr_s, yj_s, so_s):
    p1 = DFT_P1
    p2 = DFT_P2
    dftc = dftc_ref[...].astype(BF16)
    chunks = TILE // DFT_N2
    for blk in range(SEQ // TILE):
        pq = jnp.dot(u_ref[0, blk * TILE:(blk + 1) * TILE, :], dftc, preferred_element_type=F32)
        for m in range(chunks):
            n1 = blk * chunks + m
            rows = slice(m * DFT_N2, (m + 1) * DFT_N2)
            sp_s[n1 * p1:n1 * p1 + DFT_N2, :] = pq[rows, :GROUP_DIM]
            sq_s[n1 * p1:n1 * p1 + DFT_N2, :] = pq[rows, GROUP_DIM:]
    f1 = f1_ref[...].astype(BF16)
    for n2 in range(0, DFT_N2, DFT_NB):
        rhs = jnp.concatenate(
            [jnp.concatenate([sp_s[pl.ds(n2 + i, DFT_N1, stride=p1), :],
                              sq_s[pl.ds(n2 + i, DFT_N1, stride=p1), :]], axis=0)
             for i in range(DFT_NB)], axis=1).astype(BF16)
        y = jnp.dot(f1, rhs, preferred_element_type=F32)
        for i in range(DFT_NB):
            lanes = slice(i * LANES, (i + 1) * LANES)
            yr_s[(n2 + i) * p2:(n2 + i) * p2 + DFT_N1, :] = y[:DFT_N1, lanes]
            yj_s[(n2 + i) * p2:(n2 + i) * p2 + DFT_N1, :] = y[DFT_N1:, lanes]
    for k1 in range(DFT_N1):
        rhs = jnp.concatenate([yr_s[pl.ds(k1, DFT_N2, stride=p2), :],
                               yj_s[pl.ds(k1, DFT_N2, stride=p2), :]], axis=0).astype(BF16)
        so_s[pl.ds(k1, DFT_N2, stride=p2), :] = jnp.dot(m2_ref[k1].astype(BF16), rhs,
                                                         preferred_element_type=F32)
    for k2 in range(DFT_N2):
        o_ref[0, k2 * DFT_N1:(k2 + 1) * DFT_N1, :] = so_s[k2 * p2:k2 * p2 + DFT_N1, :].astype(BF16)


def _fourier_dft(uf, dftc, f1, m2):
    bsz = uf.shape[0]
    blk = pl.BlockSpec((1, SEQ, GROUP_DIM), lambda b, i: (b, 0, i))
    pad1 = DFT_N1 * DFT_P1
    pad2 = DFT_N2 * DFT_P2
    return pl.pallas_call(
        _dft_kernel,
        grid=(bsz, N_GROUPS),
        in_specs=[blk,
                  pl.BlockSpec((GROUP_DIM, 2 * GROUP_DIM), lambda b, i: (0, 0)),
                  pl.BlockSpec((2 * DFT_N1, 2 * DFT_N1), lambda b, i: (0, 0)),
                  pl.BlockSpec((DFT_N1, DFT_N2, 2 * DFT_N2), lambda b, i: (0, 0, 0))],
        out_specs=blk,
        out_shape=jax.ShapeDtypeStruct(uf.shape, BF16),
        scratch_shapes=[pltpu.VMEM((pad1, LANES), F32), pltpu.VMEM((pad1, LANES), F32),
                        pltpu.VMEM((pad2, LANES), F32), pltpu.VMEM((pad2, LANES), F32),
                        pltpu.VMEM((pad2, LANES), F32)],
        compiler_params=pltpu.CompilerParams(dimension_semantics=("arbitrary", "arbitrary"),
                                             vmem_limit_bytes=VMEM_LIMIT_BYTES),
        name="fourier_dft",
    )(uf, dftc, f1, m2)


def _bwd_kernel(x_ref, yf_ref, gfs_ref, gls_ref, v_ref, hf_ref, mod_ref, fg_ref, wfour_ref, wout_ref,
                wg_ref, bg_ref, lam_ref, h0_ref, o_ref,
                g_s, a_s, h_s, yp_s, lhs_s, hc_s):
    j = pl.program_id(1)

    @pl.when(j == 0)
    def _():
        hc_s[...] = jnp.broadcast_to(h0_ref[0, 1:2, :], hc_s.shape)

    halves = (slice(0, TILE // 2), slice(TILE // 2, TILE))
    hrate = _half_decay_rate(lam_ref[1:2, :])
    pitch = _pitch(CH)
    for h in range(N_HEADS):
        cols = _head_cols(h)
        vb = v_ref[0, :, cols]
        _gate_terms(h, vb, vb.astype(F32), wg_ref, bg_ref, hrate, g_s, slice(0, TILE))
        if h < len(halves):
            rq = halves[h]
            yfw = jnp.dot(yf_ref[0, rq, :], wfour_ref[...], preferred_element_type=F32)
            lhs_s[rq, 0:D_FOURIER] = (yfw * gfs_ref[0, rq, :].astype(F32)).astype(BF16)
        tot_a, tot_h = _scan_head(h, g_s, a_s, h_s, range(CH - 1, -1, -1))
        hin, h_out = _chunk_carries(tot_a, tot_h, hc_s[0:1, cols], reverse=True)
        hc_s[:, cols] = jnp.broadcast_to(h_out, (hc_s.shape[0], HEAD_DIM))
        for s in range(CH):
            rows = slice(s * MC, (s + 1) * MC)
            yp_s[h, pl.ds(s, MC, stride=pitch), :] = (
                h_s[rows, cols] + a_s[rows, cols] * hin + hf_ref[0, rows, cols].astype(F32))
        for m in range(MC):
            rows = slice(m * CH, (m + 1) * CH)
            yl = yp_s[h, m * pitch:m * pitch + CH, :]
            lhs_s[rows, _head_cols(h, D_FOURIER)] = (yl * gls_ref[0, rows, cols].astype(F32)).astype(BF16)

    res_gate = mod_ref[pl.ds(pl.program_id(0), 1), 2 * D_MODEL:3 * D_MODEL]
    fg = fg_ref[...]
    proj = jnp.dot(lhs_s[...], wout_ref[...], preferred_element_type=F32)
    for m in range(MC):
        rows = slice(m * CH, (m + 1) * CH)
        res = x_ref[0, rows, :] + res_gate * proj[rows, :]
        ms = jnp.mean(res * res, axis=-1, keepdims=True)
        o_ref[0, rows, :] = (res * lax.rsqrt(ms + EPS)) * fg


def _bwd_pass(x, yf, gfs, gls, v, hf, mod, fg, wfour, wout, wg, bg, lam, fin):
    bsz = x.shape[0]
    full = lambda *shape: pl.BlockSpec(shape, lambda b, j: (0,) * len(shape))
    tile = lambda c: pl.BlockSpec((1, TILE, c), lambda b, j: (b, N_TILES - 1 - j, 0))
    return pl.pallas_call(
        _bwd_kernel,
        grid=(bsz, N_TILES),
        in_specs=[tile(D_MODEL), tile(D_FOURIER), tile(D_FOURIER), tile(D_LRU), tile(D_LRU),
                  tile(D_LRU),
                  full(8, 3 * D_MODEL),
                  full(1, D_MODEL), full(D_FOURIER, D_FOURIER), full(D_MODEL, D_MODEL),
                  *_gate_specs(1), full(2, D_LRU),
                  pl.BlockSpec((1, 2, D_LRU), lambda b, j: (b, 0, 0))],
        out_specs=tile(D_MODEL),
        out_shape=jax.ShapeDtypeStruct(x.shape, F32),
        scratch_shapes=[pltpu.VMEM((TILE, 2 * D_LRU), F32),
                        pltpu.VMEM((TILE, D_LRU), F32),
                        pltpu.VMEM((TILE, D_LRU), F32),
                        pltpu.VMEM((N_QL, MC * _pitch(CH), LANES), F32),
                        pltpu.VMEM((TILE, D_MODEL), BF16),
                        pltpu.VMEM((8, D_LRU), F32)],
        compiler_params=pltpu.CompilerParams(dimension_semantics=("arbitrary", "arbitrary"),
                                             vmem_limit_bytes=VMEM_LIMIT_BYTES),
        name="bwd_pass",
    )(x, yf, gfs, gls, v, hf, mod, fg, wfour, wout, wg, bg, lam, fin)


def kernel(x, c, ctx, c_ctx, w_ada, b_ada, norm_gain, w_in, w_four, conv_w, conv_b, w_rg, b_rg,
           w_ig, b_ig, lam, w_out, final_gain):
    bsz = x.shape[0]
    assert x.shape == (bsz, SEQ, D_MODEL) and ctx.shape == (bsz, CTX_LEN, D_MODEL)
    assert w_ada.shape[0] == 1, "single-layer kernel"
    dftc, f1, m2 = _dft_constants()

    mod, win, wout, wfour, wg, bg, cw, cb = _prep(c, c_ctx[None, :], w_ada, b_ada, w_in, w_out,
                                                  w_four, w_rg, w_ig, b_rg, b_ig, conv_w, conv_b)
    g = norm_gain[0][None, :]
    lam0 = lam[0]

    fin = _ctx_states(ctx, mod, g, win, cw, cb, wg, bg, lam0)
    uf, gfs, gls, v, hf = _fwd_pass(x, mod, g, win, cw, cb, wg, bg, lam0, fin)

    yf = _fourier_dft(uf, dftc, f1, m2)

    return _bwd_pass(x, yf, gfs, gls, v, hf, mod, final_gain[None, :], wfour, wout, wg, bg, lam0, fin)
```

```python
import functools

import numpy as np
import jax
import jax.numpy as jnp
from jax import lax
from jax.experimental import pallas as pl
from jax.experimental.pallas import tpu as pltpu

D_MODEL = 1024
SEQ = 8192
CTX_LEN = 256
D_FOURIER = 512
D_LRU = 512
N_GROUPS = 4
GROUP_DIM = 128
N_HEADS = 4
HEAD_DIM = 128
LRU_C = 8.0
EPS = 1e-6
LOG2_E = float(np.log2(np.e))

LANES = 128
MC = 16
CH = 64
TILE = MC * CH
N_TILES = SEQ // TILE
CH_CTX = CTX_LEN // MC
N_QL = D_LRU // LANES
LHS_ROWS = TILE + MC
DFT_N1 = 128
DFT_N2 = 64
DFT_NB = 2

VMEM_LIMIT_BYTES = 56 * 1024 * 1024
F32 = jnp.float32
BF16 = jnp.bfloat16


def _pitch(n):
    return n + 4


DFT_P1 = _pitch(DFT_N2)
DFT_P2 = _pitch(DFT_N1)


def _dft_constants():
    c = np.arange(GROUP_DIM)
    ang = 2.0 * np.pi * ((c[:, None] * c[None, :]) % GROUP_DIM) / GROUP_DIM
    dftc = np.concatenate([np.cos(ang), np.sin(ang)], axis=1) * 2.0 ** -3

    n = np.arange(DFT_N1)
    ang1 = 2.0 * np.pi * ((n[:, None] * n[None, :]) % DFT_N1) / DFT_N1
    c1, s1 = np.cos(ang1), np.sin(ang1)
    f1 = np.block([[c1, -s1], [s1, c1]]) * 2.0 ** -4

    k1 = np.arange(DFT_N1)[:, None, None]
    k2 = np.arange(DFT_N2)[None, :, None]
    n2 = np.arange(DFT_N2)[None, None, :]
    ang2 = 2.0 * np.pi * ((n2 * (k1 + DFT_N1 * k2)) % SEQ) / SEQ
    m2 = np.concatenate([np.cos(ang2), -np.sin(ang2)], axis=2) * 2.0 ** -3
    return jnp.asarray(dftc, dtype=F32), jnp.asarray(f1, dtype=F32), jnp.asarray(m2, dtype=F32)


def _silu(x):
    hx = 0.5 * x
    return hx * jnp.tanh(hx) + hx


def _silu_of_half(hx):
    return hx * jnp.tanh(hx) + hx


def _half_decay_rate(lam_row):
    z = -lam_row
    return (-0.5 * LRU_C * LOG2_E) * (jnp.maximum(z, 0.0) + jnp.log1p(jnp.exp(-jnp.abs(z))))


def _modulate(xs, gmul, shift):
    ms = jnp.mean(xs * xs, axis=-1, keepdims=True)
    return (xs * lax.rsqrt(ms + EPS)) * gmul + shift


def _gate_ab(hpre_r, hpre_i, hv, hrate):
    a = jnp.exp2(jnp.tanh(hpre_r) * hrate + hrate)
    om = 1.0 - a * a
    s = om * lax.rsqrt(jnp.maximum(om, 1e-30))
    return a, s * ((jnp.tanh(hpre_i) + 1.0) * hv)


def _ds(start, size):
    if isinstance(start, int):
        return slice(start, start + size)
    return pl.ds(pl.multiple_of(start, MC), size)


def _to_slab_order(u, up_s, uext_s, ch):
    pitch = _pitch(ch)
    for m in range(MC):
        for q in range(N_QL):
            up_s[q, m * pitch:m * pitch + ch, :] = u[m * ch:(m + 1) * ch, q * LANES:(q + 1) * LANES]
    for s in range(ch):
        for q in range(N_QL):
            uext_s[(s + 2) * MC:(s + 3) * MC, q * LANES:(q + 1) * LANES] = (
                up_s[q, pl.ds(s, MC, stride=pitch), :])


def _conv_halo(uext_s, lookahead, c_prev1, c_prev2, ch):
    last = uext_s[(ch + 1) * MC:(ch + 2) * MC, :]
    last2 = uext_s[ch * MC:(ch + 1) * MC, :]
    first = uext_s[2 * MC:3 * MC, :]
    uext_s[0:MC, :] = jnp.concatenate([c_prev2, last2[:MC - 1]], axis=0)
    uext_s[MC:2 * MC, :] = jnp.concatenate([c_prev1, last[:MC - 1]], axis=0)
    uext_s[(ch + 2) * MC:(ch + 3) * MC, :] = jnp.concatenate([first[1:], lookahead], axis=0)


def _conv_slab(uext_s, s, cw_ref, cb_ref):
    acc = cb_ref[...]
    for k in range(4):
        acc = acc + cw_ref[k] * uext_s[_ds((s + k) * MC, MC), :]
    return acc


def _head_cols(h, base=0):
    return slice(base + h * HEAD_DIM, base + (h + 1) * HEAD_DIM)


def _gate_terms(h, vb, hv, wg_ref, bg_ref, hrate, g_s, rows):
    cols, icols = _head_cols(h), _head_cols(h, D_LRU)
    res = jnp.dot(vb, wg_ref[h], preferred_element_type=F32)
    a, bx = _gate_ab(res[:, :HEAD_DIM] + bg_ref[:, cols], res[:, HEAD_DIM:] + bg_ref[:, icols],
                     hv, hrate[:, cols])
    g_s[rows, cols] = a
    g_s[rows, icols] = bx


def _stored_terms(h, g_s):
    return lambda rows: (g_s[rows, _head_cols(h)], g_s[rows, _head_cols(h, D_LRU)])


def _scan(cols, terms, a_s, h_s, steps):
    width = cols.stop - cols.start
    acc_a = jnp.ones((MC, width), F32)
    acc_h = jnp.zeros((MC, width), F32)
    for s in steps:
        rows = slice(s * MC, (s + 1) * MC)
        a, bx = terms(rows)
        acc_a = a * acc_a
        acc_h = a * acc_h + bx
        a_s[rows, cols] = acc_a
        h_s[rows, cols] = acc_h
    return acc_a, acc_h


def _chunk_carries(tot_a, tot_h, carry_in, reverse):
    row_id = lax.broadcasted_iota(jnp.int32, tot_a.shape, 0)
    hin = jnp.zeros(tot_a.shape, F32)
    h = carry_in
    order = range(MC - 1, -1, -1) if reverse else range(MC)
    for m in order:
        hin = jnp.where(row_id == m, h, hin)
        h = tot_a[m:m + 1] * h + tot_h[m:m + 1]
    return hin, h


def _prep_kernel(c_ref, cctx_ref, wada_ref, bada_ref, win_ref, wout_ref, wfour_ref, wrg_ref, wig_ref,
                 brg_ref, big_ref, cw_ref, cb_ref,
                 mod_ref, win_o, wout_o, wfour_o, wg_o, bg_o, cwr_o, cbr_o):
    i = pl.program_id(0)
    bsz = c_ref.shape[0]
    cc = jnp.concatenate([c_ref[...], cctx_ref[...], jnp.zeros((8 - bsz - 1, D_MODEL), F32)], axis=0)
    s = _silu(cc).astype(BF16)
    mod_ref[...] = (jnp.dot(s, wada_ref[0].astype(BF16), preferred_element_type=F32)
                    + bada_ref[...])

    @pl.when(i < 2)
    def _():
        win_o[:, 0:D_FOURIER] = win_ref[0, :, 0:D_FOURIER].astype(BF16)
        win_o[:, D_FOURIER:] = (0.5 * win_ref[0, :, D_FOURIER:]).astype(BF16)

    @pl.when(i == 2)
    def _():
        wout_o[...] = wout_ref[0].astype(BF16)
        wfour_o[...] = wfour_ref[0].astype(BF16)
        for d in range(2):
            for h in range(N_HEADS):
                wg_o[d, h] = jnp.concatenate([wrg_ref[0, d, h], wig_ref[0, d, h]],
                                             axis=1).astype(BF16)
            bg_o[d] = 0.5 * jnp.concatenate([brg_ref[0, d:d + 1, :], big_ref[0, d:d + 1, :]], axis=1)
        for k in range(4):
            cwr_o[k] = jnp.broadcast_to(0.5 * cw_ref[0, k:k + 1, :], (MC, D_LRU))
        cbr_o[...] = jnp.broadcast_to(0.5 * cb_ref[...], (MC, D_LRU))


def _prep(c, c_ctx, w_ada, b_ada, w_in, w_out, w_four, w_rg, w_ig, b_rg, b_ig, conv_w, conv_b):
    full = lambda *shape: pl.BlockSpec(shape, lambda i: (0,) * len(shape))
    half = lambda i: jnp.minimum(i, 1)
    gshape = (1, 2, N_HEADS, HEAD_DIM, HEAD_DIM)
    return pl.pallas_call(
        _prep_kernel,
        grid=(3,),
        in_specs=[full(*c.shape), full(1, D_MODEL),
                  pl.BlockSpec((1, D_MODEL, D_MODEL), lambda i: (0, 0, i)),
                  pl.BlockSpec((1, D_MODEL), lambda i: (0, i)),
                  pl.BlockSpec((1, D_MODEL, D_MODEL), lambda i: (0, 0, half(i))),
                  full(1, D_MODEL, D_MODEL), full(1, D_FOURIER, D_FOURIER),
                  full(*gshape), full(*gshape), full(1, 2, D_LRU), full(1, 2, D_LRU),
                  full(1, 4, D_LRU), full(1, D_LRU)],
        out_specs=[pl.BlockSpec((8, D_MODEL), lambda i: (0, i)),
                   pl.BlockSpec((D_MODEL, D_MODEL), lambda i: (0, half(i))),
                   full(D_MODEL, D_MODEL), full(D_FOURIER, D_FOURIER),
                   full(2, N_HEADS, HEAD_DIM, 2 * HEAD_DIM), full(2, 1, 2 * D_LRU),
                   full(4, MC, D_LRU), full(MC, D_LRU)],
        out_shape=[jax.ShapeDtypeStruct((8, 3 * D_MODEL), F32),
                   jax.ShapeDtypeStruct((D_MODEL, 2 * D_MODEL), BF16),
                   jax.ShapeDtypeStruct((D_MODEL, D_MODEL), BF16),
                   jax.ShapeDtypeStruct((D_FOURIER, D_FOURIER), BF16),
                   jax.ShapeDtypeStruct((2, N_HEADS, HEAD_DIM, 2 * HEAD_DIM), BF16),
                   jax.ShapeDtypeStruct((2, 1, 2 * D_LRU), F32),
                   jax.ShapeDtypeStruct((4, MC, D_LRU), F32),
                   jax.ShapeDtypeStruct((MC, D_LRU), F32)],
        compiler_params=pltpu.CompilerParams(dimension_semantics=("arbitrary",),
                                             vmem_limit_bytes=VMEM_LIMIT_BYTES),
        name="prep",
    )(c, c_ctx, w_ada, b_ada, w_in, w_out, w_four, w_rg, w_ig, b_rg, b_ig, conv_w, conv_b)


def _ctx_kernel(x_ref, mod_ref, g_ref, wlx_ref, cw_ref, cb_ref, wgf_ref, bgf_ref, wgb_ref, bgb_ref,
                lam_ref, fin_ref, up_s, uext_s, vf_s, g_s, a_s, h_s, *, mod_row):
    ch = CH_CTX
    shift = mod_ref[mod_row:mod_row + 1, 0:D_MODEL]
    scale = mod_ref[mod_row:mod_row + 1, D_MODEL:2 * D_MODEL]
    gmul = g_ref[...] * (1.0 + scale)
    lhs = _modulate(x_ref[0], gmul, shift).astype(BF16)
    u = jnp.dot(lhs, wlx_ref[...], preferred_element_type=F32)
    _to_slab_order(u, up_s, uext_s, ch)
    zero_row = jnp.zeros((1, D_LRU), F32)
    _conv_halo(uext_s, zero_row, zero_row, zero_row, ch)
    for s in range(ch):
        vf_s[s * MC:(s + 1) * MC, :] = _conv_slab(uext_s, s, cw_ref, cb_ref)
    vb = vf_s[...].astype(BF16)
    for d, (wg_ref, bg_ref) in enumerate(((wgf_ref, bgf_ref), (wgb_ref, bgb_ref))):
        rate = _half_decay_rate(lam_ref[d:d + 1, :])
        steps = range(ch - 1, -1, -1) if d == 1 else range(ch)
        for h in range(N_HEADS):
            cols = _head_cols(h)
            _gate_terms(h, vb[:, cols], vf_s[:, cols], wg_ref, bg_ref, rate, g_s,
                        slice(0, ch * MC))
            tot_a, tot_h = _scan(cols, _stored_terms(h, g_s), a_s, h_s, steps)
            _, fin = _chunk_carries(tot_a, tot_h, zero_row[:, cols], reverse=(d == 1))
            fin_ref[0, d:d + 1, cols] = fin


def _gate_specs(direction):
    return (pl.BlockSpec((None, N_HEADS, HEAD_DIM, 2 * HEAD_DIM), lambda *_: (direction, 0, 0, 0)),
            pl.BlockSpec((None, 1, 2 * D_LRU), lambda *_: (direction, 0, 0)))


def _ctx_states(ctx, mod, g, win, cw, cb, wg, bg, lam):
    bsz = ctx.shape[0]
    n = CTX_LEN
    full = lambda *shape: pl.BlockSpec(shape, lambda b: (0,) * len(shape))
    return pl.pallas_call(
        functools.partial(_ctx_kernel, mod_row=bsz),
        grid=(bsz,),
        in_specs=[pl.BlockSpec((1, n, D_MODEL), lambda b: (b, 0, 0)),
                  full(8, 3 * D_MODEL),
                  full(1, D_MODEL),
                  pl.BlockSpec((D_MODEL, D_LRU), lambda b: (0, 2)),
                  full(4, MC, D_LRU), full(MC, D_LRU),
                  *_gate_specs(0), *_gate_specs(1),
                  full(2, D_LRU)],
        out_specs=pl.BlockSpec((1, 2, D_LRU), lambda b: (b, 0, 0)),
        out_shape=jax.ShapeDtypeStruct((bsz, 2, D_LRU), F32),
        scratch_shapes=[pltpu.VMEM((N_QL, MC * _pitch(CH_CTX), LANES), F32),
                        pltpu.VMEM((n + 3 * MC, D_LRU), F32),
                        pltpu.VMEM((n, D_LRU), F32),
                        pltpu.VMEM((n, 2 * D_LRU), F32),
                        pltpu.VMEM((n, D_LRU), F32),
                        pltpu.VMEM((n, D_LRU), F32)],
        compiler_params=pltpu.CompilerParams(dimension_semantics=("arbitrary",)),
        name="ctx_states",
    )(ctx, mod, g, win, cw, cb, wg, bg, wg, bg, lam)


def _fwd_kernel(x_ref, xh_ref, mod_ref, g_ref, win_ref, cw_ref, cb_ref, wg_ref,
                bg_ref, lam_ref, h0_ref,
                uf_ref, gfs_ref, gls_ref, v_ref, hf_ref,
                lhs_s, up_s, uext_s, vf_s, g_s, a_s, h_s, cc_s, hc_s):
    j = pl.program_id(1)

    @pl.when(j == 0)
    def _():
        cc_s[...] = jnp.zeros_like(cc_s)
        hc_s[...] = jnp.broadcast_to(h0_ref[0, 0:1, :], hc_s.shape)

    brow = pl.ds(pl.program_id(0), 1)
    shift = mod_ref[brow, 0:D_MODEL]
    scale = mod_ref[brow, D_MODEL:2 * D_MODEL]
    gmul = g_ref[...] * (1.0 + scale)
    for m in range(MC):
        rows = slice(m * CH, (m + 1) * CH)
        lhs_s[rows, :] = _modulate(x_ref[0, rows, :], gmul, shift).astype(BF16)
    hh = _modulate(xh_ref[0], gmul, shift)
    lhs_s[TILE:LHS_ROWS, :] = jnp.concatenate([hh, jnp.zeros_like(hh)], axis=0).astype(BF16)

    u = jnp.dot(lhs_s[...], win_ref[:, 2 * D_FOURIER:2 * D_FOURIER + D_LRU],
                preferred_element_type=F32)

    _to_slab_order(u, up_s, uext_s, CH)
    look = jnp.where(j == N_TILES - 1, 0.0, u[TILE:TILE + 1])
    _conv_halo(uext_s, look, cc_s[0:1, :], cc_s[1:2, :], CH)
    cc_s[0:1, :] = u[TILE - 1:TILE]
    cc_s[1:2, :] = u[TILE - 2:TILE - 1]

    uf_ref[0] = jnp.dot(lhs_s[0:TILE, :], win_ref[:, 0:D_FOURIER],
                        preferred_element_type=F32).astype(BF16)

    for s in range(CH):
        rows = slice(s * MC, (s + 1) * MC)
        v = _conv_slab(uext_s, s, cw_ref, cb_ref)
        vf_s[rows, :] = v
        v_ref[0, rows, :] = v.astype(BF16)

    halves = (slice(0, TILE // 2), slice(TILE // 2, TILE))
    hrate = _half_decay_rate(lam_ref[0:1, :])
    for h in range(N_HEADS):
        cols = _head_cols(h)
        _gate_terms(h, v_ref[0, :, cols], vf_s[:, cols], wg_ref, bg_ref, hrate, g_s, slice(0, TILE))
        o_ref, c0 = ((gfs_ref, D_FOURIER), (gls_ref, 2 * D_FOURIER + D_LRU))[h // 2]
        rq = halves[h % 2]
        o_ref[0, rq, :] = _silu_of_half(jnp.dot(lhs_s[rq, :], win_ref[:, c0:c0 + D_LRU],
                                                preferred_element_type=F32)).astype(BF16)
        tot_a, tot_h = _scan(cols, _stored_terms(h, g_s), a_s, h_s, range(CH))
        hin, h_out = _chunk_carries(tot_a, tot_h, hc_s[0:1, cols], reverse=False)
        hc_s[:, cols] = jnp.broadcast_to(h_out, (hc_s.shape[0], HEAD_DIM))
        for s in range(CH):
            rows = slice(s * MC, (s + 1) * MC)
            hf_ref[0, rows, cols] = (h_s[rows, cols] + a_s[rows, cols] * hin).astype(BF16)


def _fwd_pass(x, mod, g, win, cw, cb, wg, bg, lam, fin):
    bsz = x.shape[0]
    full = lambda *shape: pl.BlockSpec(shape, lambda b, j: (0,) * len(shape))
    tile = lambda c: pl.BlockSpec((1, TILE, c), lambda b, j: (b, j, 0))
    shp = lambda c: jax.ShapeDtypeStruct((bsz, SEQ, c), BF16)
    rows8 = TILE // 8
    return pl.pallas_call(
        _fwd_kernel,
        grid=(bsz, N_TILES),
        in_specs=[tile(D_MODEL),
                  pl.BlockSpec((1, 8, D_MODEL),
                               lambda b, j: (b, jnp.minimum((j + 1) * rows8, SEQ // 8 - 1), 0)),
                  full(8, 3 * D_MODEL),
                  full(1, D_MODEL), full(D_MODEL, 2 * D_MODEL),
                  full(4, MC, D_LRU), full(MC, D_LRU),
                  *_gate_specs(0),
                  full(2, D_LRU),
                  pl.BlockSpec((1, 2, D_LRU), lambda b, j: (b, 0, 0))],
        out_specs=[tile(D_FOURIER), tile(D_FOURIER), tile(D_LRU), tile(D_LRU), tile(D_LRU)],
        out_shape=[shp(D_FOURIER), shp(D_FOURIER), shp(D_LRU), shp(D_LRU), shp(D_LRU)],
        scratch_shapes=[pltpu.VMEM((LHS_ROWS, D_MODEL), BF16),
                        pltpu.VMEM((N_QL, MC * _pitch(CH), LANES), F32),
                        pltpu.VMEM((TILE + 3 * MC, D_LRU), F32),
                        pltpu.VMEM((TILE, D_LRU), F32),
                        pltpu.VMEM((TILE, 2 * D_LRU), F32),
                        pltpu.VMEM((TILE, D_LRU), F32),
                        pltpu.VMEM((TILE, D_LRU), F32),
                        pltpu.VMEM((8, D_LRU), F32),
                        pltpu.VMEM((8, D_LRU), F32)],
        compiler_params=pltpu.CompilerParams(dimension_semantics=("arbitrary", "arbitrary"),
                                             vmem_limit_bytes=VMEM_LIMIT_BYTES),
        name="fwd_pass",
    )(x, x, mod, g, win, cw, cb, wg, bg, lam, fin)


def _dft_kernel(u_ref, dftc_ref, f1_ref, m2_ref, o_ref, sp_s, sq_s, yr_s, yj_s, so_s):
    p1 = DFT_P1
    p2 = DFT_P2
    dftc = dftc_ref[...].astype(BF16)
    chunks = TILE // DFT_N2
    for blk in range(SEQ // TILE):
        pq = jnp.dot(u_ref[0, blk * TILE:(blk + 1) * TILE, :], dftc, preferred_element_type=F32)
        for m in range(chunks):
            n1 = blk * chunks + m
            rows = slice(m * DFT_N2, (m + 1) * DFT_N2)
            sp_s[n1 * p1:n1 * p1 + DFT_N2, :] = pq[rows, :GROUP_DIM]
            sq_s[n1 * p1:n1 * p1 + DFT_N2, :] = pq[rows, GROUP_DIM:]
    f1 = f1_ref[...].astype(BF16)
    for n2 in range(0, DFT_N2, DFT_NB):
        rhs = jnp.concatenate(
            [jnp.concatenate([sp_s[pl.ds(n2 + i, DFT_N1, stride=p1), :],
                              sq_s[pl.ds(n2 + i, DFT_N1, stride=p1), :]], axis=0)
             for i in range(DFT_NB)], axis=1).astype(BF16)
        y = jnp.dot(f1, rhs, preferred_element_type=F32)
        for i in range(DFT_NB):
            lanes = slice(i * LANES, (i + 1) * LANES)
            yr_s[(n2 + i) * p2:(n2 + i) * p2 + DFT_N1, :] = y[:DFT_N1, lanes]
            yj_s[(n2 + i) * p2:(n2 + i) * p2 + DFT_N1, :] = y[DFT_N1:, lanes]
    for k1 in range(DFT_N1):
        rhs = jnp.concatenate([yr_s[pl.ds(k1, DFT_N2, stride=p2), :],
                               yj_s[pl.ds(k1, DFT_N2, stride=p2), :]], axis=0).astype(BF16)
        so_s[pl.ds(k1, DFT_N2, stride=p2), :] = jnp.dot(m2_ref[k1].astype(BF16), rhs,
                                                         preferred_element_type=F32)
    for k2 in range(DFT_N2):
        o_ref[0, k2 * DFT_N1:(k2 + 1) * DFT_N1, :] = so_s[k2 * p2:k2 * p2 + DFT_N1, :].astype(BF16)


def _fourier_dft(uf, dftc, f1, m2):
    bsz = uf.shape[0]
    blk = pl.BlockSpec((1, SEQ, GROUP_DIM), lambda b, i: (b, 0, i))
    pad1 = DFT_N1 * DFT_P1
    pad2 = DFT_N2 * DFT_P2
    return pl.pallas_call(
        _dft_kernel,
        grid=(bsz, N_GROUPS),
        in_specs=[blk,
                  pl.BlockSpec((GROUP_DIM, 2 * GROUP_DIM), lambda b, i: (0, 0)),
                  pl.BlockSpec((2 * DFT_N1, 2 * DFT_N1), lambda b, i: (0, 0)),
                  pl.BlockSpec((DFT_N1, DFT_N2, 2 * DFT_N2), lambda b, i: (0, 0, 0))],
        out_specs=blk,
        out_shape=jax.ShapeDtypeStruct(uf.shape, BF16),
        scratch_shapes=[pltpu.VMEM((pad1, LANES), F32), pltpu.VMEM((pad1, LANES), F32),
                        pltpu.VMEM((pad2, LANES), F32), pltpu.VMEM((pad2, LANES), F32),
                        pltpu.VMEM((pad2, LANES), F32)],
        compiler_params=pltpu.CompilerParams(dimension_semantics=("arbitrary", "arbitrary"),
                                             vmem_limit_bytes=VMEM_LIMIT_BYTES),
        name="fourier_dft",
    )(uf, dftc, f1, m2)


def _bwd_kernel(x_ref, yf_ref, gfs_ref, gls_ref, v_ref, hf_ref, mod_ref, fg_ref, wfour_ref, wout_ref,
                wg_ref, bg_ref, lam_ref, h0_ref, o_ref,
                g_s, a_s, h_s, yp_s, lhs_s, hc_s):
    j = pl.program_id(1)

    @pl.when(j == 0)
    def _():
        hc_s[...] = jnp.broadcast_to(h0_ref[0, 1:2, :], hc_s.shape)

    for h in range(N_HEADS):
        res = jnp.dot(v_ref[0, :, _head_cols(h)], wg_ref[h], preferred_element_type=F32)
        g_s[:, _head_cols(h)] = res[:, :HEAD_DIM]
        g_s[:, _head_cols(h, D_LRU)] = res[:, HEAD_DIM:]
    yfw = jnp.dot(yf_ref[0], wfour_ref[...], preferred_element_type=F32)
    lhs_s[:, 0:D_FOURIER] = (yfw * gfs_ref[0].astype(F32)).astype(BF16)

    hrate = _half_decay_rate(lam_ref[1:2, :])
    b_r = bg_ref[:, :D_LRU]
    b_i = bg_ref[:, D_LRU:]
    terms = lambda rows: _gate_ab(g_s[rows, :D_LRU] + b_r, g_s[rows, D_LRU:] + b_i,
                                  v_ref[0, rows, :].astype(F32), hrate)
    tot_a, tot_h = _scan(slice(0, D_LRU), terms, a_s, h_s, range(CH - 1, -1, -1))
    hin, h_out = _chunk_carries(tot_a, tot_h, hc_s[0:1, :], reverse=True)
    hc_s[...] = jnp.broadcast_to(h_out, hc_s.shape)
    pitch = _pitch(CH)
    for s in range(CH):
        rows = slice(s * MC, (s + 1) * MC)
        yl = h_s[rows, :] + a_s[rows, :] * hin + hf_ref[0, rows, :].astype(F32)
        for q in range(N_QL):
            yp_s[q, pl.ds(s, MC, stride=pitch), :] = yl[:, q * LANES:(q + 1) * LANES]
    for m in range(MC):
        rows = slice(m * CH, (m + 1) * CH)
        for q in range(N_QL):
            cols = slice(q * LANES, (q + 1) * LANES)
            yl = yp_s[q, m * pitch:m * pitch + CH, :]
            lhs_s[rows, D_FOURIER + q * LANES:D_FOURIER + (q + 1) * LANES] = (
                yl * gls_ref[0, rows, cols].astype(F32)).astype(BF16)

    res_gate = mod_ref[pl.ds(pl.program_id(0), 1), 2 * D_MODEL:3 * D_MODEL]
    fg = fg_ref[...]
    proj = jnp.dot(lhs_s[...], wout_ref[...], preferred_element_type=F32)
    for m in range(MC):
        rows = slice(m * CH, (m + 1) * CH)
        res = x_ref[0, rows, :] + res_gate * proj[rows, :]
        ms = jnp.mean(res * res, axis=-1, keepdims=True)
        o_ref[0, rows, :] = (res * lax.rsqrt(ms + EPS)) * fg


def _bwd_pass(x, yf, gfs, gls, v, hf, mod, fg, wfour, wout, wg, bg, lam, fin):
    bsz = x.shape[0]
    full = lambda *shape: pl.BlockSpec(shape, lambda b, j: (0,) * len(shape))
    tile = lambda c: pl.BlockSpec((1, TILE, c), lambda b, j: (b, N_TILES - 1 - j, 0))
    return pl.pallas_call(
        _bwd_kernel,
        grid=(bsz, N_TILES),
        in_specs=[tile(D_MODEL), tile(D_FOURIER), tile(D_FOURIER), tile(D_LRU), tile(D_LRU),
                  tile(D_LRU),
                  full(8, 3 * D_MODEL),
                  full(1, D_MODEL), full(D_FOURIER, D_FOURIER), full(D_MODEL, D_MODEL),
                  *_gate_specs(1), full(2, D_LRU),
                  pl.BlockSpec((1, 2, D_LRU), lambda b, j: (b, 0, 0))],
        out_specs=tile(D_MODEL),
        out_shape=jax.ShapeDtypeStruct(x.shape, F32),
        scratch_shapes=[pltpu.VMEM((TILE, 2 * D_LRU), F32),
                        pltpu.VMEM((TILE, D_LRU), F32),
                        pltpu.VMEM((TILE, D_LRU), F32),
                        pltpu.VMEM((N_QL, MC * _pitch(CH), LANES), F32),
                        pltpu.VMEM((TILE, D_MODEL), BF16),
                        pltpu.VMEM((8, D_LRU), F32)],
        compiler_params=pltpu.CompilerParams(dimension_semantics=("arbitrary", "arbitrary"),
                                             vmem_limit_bytes=VMEM_LIMIT_BYTES),
        name="bwd_pass",
    )(x, yf, gfs, gls, v, hf, mod, fg, wfour, wout, wg, bg, lam, fin)


def kernel(x, c, ctx, c_ctx, w_ada, b_ada, norm_gain, w_in, w_four, conv_w, conv_b, w_rg, b_rg,
           w_ig, b_ig, lam, w_out, final_gain):
    bsz = x.shape[0]
    assert x.shape == (bsz, SEQ, D_MODEL) and ctx.shape == (bsz, CTX_LEN, D_MODEL)
    assert w_ada.shape[0] == 1, "single-layer kernel"
    dftc, f1, m2 = _dft_constants()

    mod, win, wout, wfour, wg, bg, cw, cb = _prep(c, c_ctx[None, :], w_ada, b_ada, w_in, w_out,
                                                  w_four, w_rg, w_ig, b_rg, b_ig, conv_w, conv_b)
    g = norm_gain[0][None, :]
    lam0 = lam[0]

    fin = _ctx_states(ctx, mod, g, win, cw, cb, wg, bg, lam0)
    uf, gfs, gls, v, hf = _fwd_pass(x, mod, g, win, cw, cb, wg, bg, lam0, fin)

    yf = _fourier_dft(uf, dftc, f1, m2)

    return _bwd_pass(x, yf, gfs, gls, v, hf, mod, final_gain[None, :], wfour, wout, wg, bg, lam0, fin)
```

```python
import functools

import numpy as np
import jax
import jax.numpy as jnp
from jax import lax
from jax.experimental import pallas as pl
from jax.experimental.pallas import tpu as pltpu

D_MODEL = 1024
SEQ = 8192
CTX_LEN = 256
D_FOURIER = 512
D_LRU = 512
N_GROUPS = 4
GROUP_DIM = 128
N_HEADS = 4
HEAD_DIM = 128
LRU_C = 8.0
EPS = 1e-6
LOG2_E = float(np.log2(np.e))

LANES = 128
MC = 16
CH = 64
TILE = MC * CH
N_TILES = SEQ // TILE
CH_CTX = CTX_LEN // MC
N_QL = D_LRU // LANES
LHS_ROWS = TILE + MC
DFT_N1 = 128
DFT_N2 = 64
DFT_NB = 2

VMEM_LIMIT_BYTES = 56 * 1024 * 1024
F32 = jnp.float32
BF16 = jnp.bfloat16


def _pitch(n):
    return n + 4


DFT_P1 = _pitch(DFT_N2)
DFT_P2 = _pitch(DFT_N1)


def _dft_constants():
    c = np.arange(GROUP_DIM)
    ang = 2.0 * np.pi * ((c[:, None] * c[None, :]) % GROUP_DIM) / GROUP_DIM
    dftc = np.concatenate([np.cos(ang), np.sin(ang)], axis=1) * 2.0 ** -3

    n = np.arange(DFT_N1)
    ang1 = 2.0 * np.pi * ((n[:, None] * n[None, :]) % DFT_N1) / DFT_N1
    c1, s1 = np.cos(ang1), np.sin(ang1)
    f1 = np.block([[c1, -s1], [s1, c1]]) * 2.0 ** -4

    k1 = np.arange(DFT_N1)[:, None, None]
    k2 = np.arange(DFT_N2)[None, :, None]
    n2 = np.arange(DFT_N2)[None, None, :]
    ang2 = 2.0 * np.pi * ((n2 * (k1 + DFT_N1 * k2)) % SEQ) / SEQ
    m2 = np.concatenate([np.cos(ang2), -np.sin(ang2)], axis=2) * 2.0 ** -3
    return jnp.asarray(dftc, dtype=F32), jnp.asarray(f1, dtype=F32), jnp.asarray(m2, dtype=F32)


def _silu(x):
    hx = 0.5 * x
    return hx * jnp.tanh(hx) + hx


def _silu_of_half(hx):
    return hx * jnp.tanh(hx) + hx


def _half_decay_rate(lam_row):
    z = -lam_row
    return (-0.5 * LRU_C * LOG2_E) * (jnp.maximum(z, 0.0) + jnp.log1p(jnp.exp(-jnp.abs(z))))


def _modulate(xs, gmul, shift):
    ms = jnp.mean(xs * xs, axis=-1, keepdims=True)
    return (xs * lax.rsqrt(ms + EPS)) * gmul + shift


def _gate_ab(hpre_r, hpre_i, hv, hrate):
    a = jnp.exp2(jnp.tanh(hpre_r) * hrate + hrate)
    om = 1.0 - a * a
    s = om * lax.rsqrt(jnp.maximum(om, 1e-30))
    return a, s * ((jnp.tanh(hpre_i) + 1.0) * hv)


def _ds(start, size):
    if isinstance(start, int):
        return slice(start, start + size)
    return pl.ds(pl.multiple_of(start, MC), size)


def _to_slab_order(u, up_s, uext_s, ch):
    pitch = _pitch(ch)
    for m in range(MC):
        for q in range(N_QL):
            up_s[q, m * pitch:m * pitch + ch, :] = u[m * ch:(m + 1) * ch, q * LANES:(q + 1) * LANES]
    for s in range(ch):
        for q in range(N_QL):
            uext_s[(s + 2) * MC:(s + 3) * MC, q * LANES:(q + 1) * LANES] = (
                up_s[q, pl.ds(s, MC, stride=pitch), :])


def _conv_halo(uext_s, lookahead, c_prev1, c_prev2, ch):
    last = uext_s[(ch + 1) * MC:(ch + 2) * MC, :]
    last2 = uext_s[ch * MC:(ch + 1) * MC, :]
    first = uext_s[2 * MC:3 * MC, :]
    uext_s[0:MC, :] = jnp.concatenate([c_prev2, last2[:MC - 1]], axis=0)
    uext_s[MC:2 * MC, :] = jnp.concatenate([c_prev1, last[:MC - 1]], axis=0)
    uext_s[(ch + 2) * MC:(ch + 3) * MC, :] = jnp.concatenate([first[1:], lookahead], axis=0)


def _conv_slab(uext_s, s, cw_ref, cb_ref):
    acc = cb_ref[...]
    for k in range(4):
        acc = acc + cw_ref[k] * uext_s[_ds((s + k) * MC, MC), :]
    return acc


def _head_cols(h, base=0):
    return slice(base + h * HEAD_DIM, base + (h + 1) * HEAD_DIM)


def _gate_terms(h, vb, hv, wg_ref, bg_ref, hrate, g_s, rows):
    cols, icols = _head_cols(h), _head_cols(h, D_LRU)
    res = jnp.dot(vb, wg_ref[h], preferred_element_type=F32)
    a, bx = _gate_ab(res[:, :HEAD_DIM] + bg_ref[:, cols], res[:, HEAD_DIM:] + bg_ref[:, icols],
                     hv, hrate[:, cols])
    g_s[rows, cols] = a
    g_s[rows, icols] = bx


def _stored_terms(h, g_s):
    return lambda rows: (g_s[rows, _head_cols(h)], g_s[rows, _head_cols(h, D_LRU)])


def _scan(cols, terms, a_s, h_s, steps):
    width = cols.stop - cols.start
    acc_a = jnp.ones((MC, width), F32)
    acc_h = jnp.zeros((MC, width), F32)
    for s in steps:
        rows = slice(s * MC, (s + 1) * MC)
        a, bx = terms(rows)
        acc_a = a * acc_a
        acc_h = a * acc_h + bx
        a_s[rows, cols] = acc_a
        h_s[rows, cols] = acc_h
    return acc_a, acc_h


def _chunk_carries(tot_a, tot_h, carry_in, reverse):
    row_id = lax.broadcasted_iota(jnp.int32, tot_a.shape, 0)
    hin = jnp.zeros(tot_a.shape, F32)
    h = carry_in
    order = range(MC - 1, -1, -1) if reverse else range(MC)
    for m in order:
        hin = jnp.where(row_id == m, h, hin)
        h = tot_a[m:m + 1] * h + tot_h[m:m + 1]
    return hin, h


def _prep_kernel(c_ref, cctx_ref, wada_ref, bada_ref, win_ref, wout_ref, wfour_ref, wrg_ref, wig_ref,
                 brg_ref, big_ref, cw_ref, cb_ref,
                 mod_ref, win_o, wout_o, wfour_o, wg_o, bg_o, cwr_o, cbr_o):
    i = pl.program_id(0)
    bsz = c_ref.shape[0]
    cc = jnp.concatenate([c_ref[...], cctx_ref[...], jnp.zeros((8 - bsz - 1, D_MODEL), F32)], axis=0)
    s = _silu(cc).astype(BF16)
    mod_ref[...] = (jnp.dot(s, wada_ref[0].astype(BF16), preferred_element_type=F32)
                    + bada_ref[...])

    @pl.when(i < 2)
    def _():
        win_o[:, 0:D_FOURIER] = win_ref[0, :, 0:D_FOURIER].astype(BF16)
        win_o[:, D_FOURIER:] = (0.5 * win_ref[0, :, D_FOURIER:]).astype(BF16)

    @pl.when(i == 2)
    def _():
        wout_o[...] = wout_ref[0].astype(BF16)
        wfour_o[...] = wfour_ref[0].astype(BF16)
        for d in range(2):
            for h in range(N_HEADS):
                wg_o[d, h] = jnp.concatenate([wrg_ref[0, d, h], wig_ref[0, d, h]],
                                             axis=1).astype(BF16)
            bg_o[d] = 0.5 * jnp.concatenate([brg_ref[0, d:d + 1, :], big_ref[0, d:d + 1, :]], axis=1)
        for k in range(4):
            cwr_o[k] = jnp.broadcast_to(0.5 * cw_ref[0, k:k + 1, :], (MC, D_LRU))
        cbr_o[...] = jnp.broadcast_to(0.5 * cb_ref[...], (MC, D_LRU))


def _prep(c, c_ctx, w_ada, b_ada, w_in, w_out, w_four, w_rg, w_ig, b_rg, b_ig, conv_w, conv_b):
    full = lambda *shape: pl.BlockSpec(shape, lambda i: (0,) * len(shape))
    half = lambda i: jnp.minimum(i, 1)
    gshape = (1, 2, N_HEADS, HEAD_DIM, HEAD_DIM)
    return pl.pallas_call(
        _prep_kernel,
        grid=(3,),
        in_specs=[full(*c.shape), full(1, D_MODEL),
                  pl.BlockSpec((1, D_MODEL, D_MODEL), lambda i: (0, 0, i)),
                  pl.BlockSpec((1, D_MODEL), lambda i: (0, i)),
                  pl.BlockSpec((1, D_MODEL, D_MODEL), lambda i: (0, 0, half(i))),
                  full(1, D_MODEL, D_MODEL), full(1, D_FOURIER, D_FOURIER),
                  full(*gshape), full(*gshape), full(1, 2, D_LRU), full(1, 2, D_LRU),
                  full(1, 4, D_LRU), full(1, D_LRU)],
        out_specs=[pl.BlockSpec((8, D_MODEL), lambda i: (0, i)),
                   pl.BlockSpec((D_MODEL, D_MODEL), lambda i: (0, half(i))),
                   full(D_MODEL, D_MODEL), full(D_FOURIER, D_FOURIER),
                   full(2, N_HEADS, HEAD_DIM, 2 * HEAD_DIM), full(2, 1, 2 * D_LRU),
                   full(4, MC, D_LRU), full(MC, D_LRU)],
        out_shape=[jax.ShapeDtypeStruct((8, 3 * D_MODEL), F32),
                   jax.ShapeDtypeStruct((D_MODEL, 2 * D_MODEL), BF16),
                   jax.ShapeDtypeStruct((D_MODEL, D_MODEL), BF16),
                   jax.ShapeDtypeStruct((D_FOURIER, D_FOURIER), BF16),
                   jax.ShapeDtypeStruct((2, N_HEADS, HEAD_DIM, 2 * HEAD_DIM), BF16),
                   jax.ShapeDtypeStruct((2, 1, 2 * D_LRU), F32),
                   jax.ShapeDtypeStruct((4, MC, D_LRU), F32),
                   jax.ShapeDtypeStruct((MC, D_LRU), F32)],
        compiler_params=pltpu.CompilerParams(dimension_semantics=("arbitrary",),
                                             vmem_limit_bytes=VMEM_LIMIT_BYTES),
        name="prep",
    )(c, c_ctx, w_ada, b_ada, w_in, w_out, w_four, w_rg, w_ig, b_rg, b_ig, conv_w, conv_b)


def _ctx_kernel(x_ref, mod_ref, g_ref, wlx_ref, cw_ref, cb_ref, wgf_ref, bgf_ref, wgb_ref, bgb_ref,
                lam_ref, fin_ref, up_s, uext_s, vf_s, g_s, a_s, h_s, *, mod_row):
    ch = CH_CTX
    shift = mod_ref[mod_row:mod_row + 1, 0:D_MODEL]
    scale = mod_ref[mod_row:mod_row + 1, D_MODEL:2 * D_MODEL]
    gmul = g_ref[...] * (1.0 + scale)
    lhs = _modulate(x_ref[0], gmul, shift).astype(BF16)
    u = jnp.dot(lhs, wlx_ref[...], preferred_element_type=F32)
    _to_slab_order(u, up_s, uext_s, ch)
    zero_row = jnp.zeros((1, D_LRU), F32)
    _conv_halo(uext_s, zero_row, zero_row, zero_row, ch)
    for s in range(ch):
        vf_s[s * MC:(s + 1) * MC, :] = _conv_slab(uext_s, s, cw_ref, cb_ref)
    vb = vf_s[...].astype(BF16)
    for d, (wg_ref, bg_ref) in enumerate(((wgf_ref, bgf_ref), (wgb_ref, bgb_ref))):
        rate = _half_decay_rate(lam_ref[d:d + 1, :])
        steps = range(ch - 1, -1, -1) if d == 1 else range(ch)
        for h in range(N_HEADS):
            cols = _head_cols(h)
            _gate_terms(h, vb[:, cols], vf_s[:, cols], wg_ref, bg_ref, rate, g_s,
                        slice(0, ch * MC))
            tot_a, tot_h = _scan(cols, _stored_terms(h, g_s), a_s, h_s, steps)
            _, fin = _chunk_carries(tot_a, tot_h, zero_row[:, cols], reverse=(d == 1))
            fin_ref[0, d:d + 1, cols] = fin


def _gate_specs(direction):
    return (pl.BlockSpec((None, N_HEADS, HEAD_DIM, 2 * HEAD_DIM), lambda *_: (direction, 0, 0, 0)),
            pl.BlockSpec((None, 1, 2 * D_LRU), lambda *_: (direction, 0, 0)))


def _ctx_states(ctx, mod, g, win, cw, cb, wg, bg, lam):
    bsz = ctx.shape[0]
    n = CTX_LEN
    full = lambda *shape: pl.BlockSpec(shape, lambda b: (0,) * len(shape))
    return pl.pallas_call(
        functools.partial(_ctx_kernel, mod_row=bsz),
        grid=(bsz,),
        in_specs=[pl.BlockSpec((1, n, D_MODEL), lambda b: (b, 0, 0)),
                  full(8, 3 * D_MODEL),
                  full(1, D_MODEL),
                  pl.BlockSpec((D_MODEL, D_LRU), lambda b: (0, 2)),
                  full(4, MC, D_LRU), full(MC, D_LRU),
                  *_gate_specs(0), *_gate_specs(1),
                  full(2, D_LRU)],
        out_specs=pl.BlockSpec((1, 2, D_LRU), lambda b: (b, 0, 0)),
        out_shape=jax.ShapeDtypeStruct((bsz, 2, D_LRU), F32),
        scratch_shapes=[pltpu.VMEM((N_QL, MC * _pitch(CH_CTX), LANES), F32),
                        pltpu.VMEM((n + 3 * MC, D_LRU), F32),
                        pltpu.VMEM((n, D_LRU), F32),
                        pltpu.VMEM((n, 2 * D_LRU), F32),
                        pltpu.VMEM((n, D_LRU), F32),
                        pltpu.VMEM((n, D_LRU), F32)],
        compiler_params=pltpu.CompilerParams(dimension_semantics=("arbitrary",)),
        name="ctx_states",
    )(ctx, mod, g, win, cw, cb, wg, bg, wg, bg, lam)


def _fwd_kernel(x_ref, xh_ref, mod_ref, g_ref, win_ref, cw_ref, cb_ref, wg_ref,
                bg_ref, lam_ref, h0_ref,
                uf_ref, gfs_ref, gls_ref, v_ref, hf_ref,
                lhs_s, up_s, uext_s, vf_s, g_s, a_s, h_s, cc_s, hc_s):
    j = pl.program_id(1)

    @pl.when(j == 0)
    def _():
        cc_s[...] = jnp.zeros_like(cc_s)
        hc_s[...] = jnp.broadcast_to(h0_ref[0, 0:1, :], hc_s.shape)

    brow = pl.ds(pl.program_id(0), 1)
    shift = mod_ref[brow, 0:D_MODEL]
    scale = mod_ref[brow, D_MODEL:2 * D_MODEL]
    gmul = g_ref[...] * (1.0 + scale)
    for m in range(MC):
        rows = slice(m * CH, (m + 1) * CH)
        lhs_s[rows, :] = _modulate(x_ref[0, rows, :], gmul, shift).astype(BF16)
    hh = _modulate(xh_ref[0], gmul, shift)
    lhs_s[TILE:LHS_ROWS, :] = jnp.concatenate([hh, jnp.zeros_like(hh)], axis=0).astype(BF16)

    u = jnp.dot(lhs_s[...], win_ref[:, 2 * D_FOURIER:2 * D_FOURIER + D_LRU],
                preferred_element_type=F32)

    _to_slab_order(u, up_s, uext_s, CH)
    look = jnp.where(j == N_TILES - 1, 0.0, u[TILE:TILE + 1])
    _conv_halo(uext_s, look, cc_s[0:1, :], cc_s[1:2, :], CH)
    cc_s[0:1, :] = u[TILE - 1:TILE]
    cc_s[1:2, :] = u[TILE - 2:TILE - 1]

    uf_ref[0] = jnp.dot(lhs_s[0:TILE, :], win_ref[:, 0:D_FOURIER],
                        preferred_element_type=F32).astype(BF16)

    for s in range(CH):
        rows = slice(s * MC, (s + 1) * MC)
        v = _conv_slab(uext_s, s, cw_ref, cb_ref)
        vf_s[rows, :] = v
        v_ref[0, rows, :] = v.astype(BF16)

    halves = (slice(0, TILE // 2), slice(TILE // 2, TILE))
    hrate = _half_decay_rate(lam_ref[0:1, :])
    for h in range(N_HEADS):
        cols = _head_cols(h)
        _gate_terms(h, v_ref[0, :, cols], vf_s[:, cols], wg_ref, bg_ref, hrate, g_s, slice(0, TILE))
        o_ref, c0 = ((gfs_ref, D_FOURIER), (gls_ref, 2 * D_FOURIER + D_LRU))[h // 2]
        rq = halves[h % 2]
        o_ref[0, rq, :] = _silu_of_half(jnp.dot(lhs_s[rq, :], win_ref[:, c0:c0 + D_LRU],
                                                preferred_element_type=F32)).astype(BF16)
        tot_a, tot_h = _scan(cols, _stored_terms(h, g_s), a_s, h_s, range(CH))
        hin, h_out = _chunk_carries(tot_a, tot_h, hc_s[0:1, cols], reverse=False)
        hc_s[:, cols] = jnp.broadcast_to(h_out, (hc_s.shape[0], HEAD_DIM))
        for s in range(CH):
            rows = slice(s * MC, (s + 1) * MC)
            hf_ref[0, rows, cols] = (h_s[rows, cols] + a_s[rows, cols] * hin).astype(BF16)


def _fwd_pass(x, mod, g, win, cw, cb, wg, bg, lam, fin):
    bsz = x.shape[0]
    full = lambda *shape: pl.BlockSpec(shape, lambda b, j: (0,) * len(shape))
    tile = lambda c: pl.BlockSpec((1, TILE, c), lambda b, j: (b, j, 0))
    shp = lambda c: jax.ShapeDtypeStruct((bsz, SEQ, c), BF16)
    rows8 = TILE // 8
    return pl.pallas_call(
        _fwd_kernel,
        grid=(bsz, N_TILES),
        in_specs=[tile(D_MODEL),
                  pl.BlockSpec((1, 8, D_MODEL),
                               lambda b, j: (b, jnp.minimum((j + 1) * rows8, SEQ // 8 - 1), 0)),
                  full(8, 3 * D_MODEL),
                  full(1, D_MODEL), full(D_MODEL, 2 * D_MODEL),
                  full(4, MC, D_LRU), full(MC, D_LRU),
                  *_gate_specs(0),
                  full(2, D_LRU),
                  pl.BlockSpec((1, 2, D_LRU), lambda b, j: (b, 0, 0))],
        out_specs=[tile(D_FOURIER), tile(D_FOURIER), tile(D_LRU), tile(D_LRU), tile(D_LRU)],
        out_shape=[shp(D_FOURIER), shp(D_FOURIER), shp(D_LRU), shp(D_LRU), shp(D_LRU)],
        scratch_shapes=[pltpu.VMEM((LHS_ROWS, D_MODEL), BF16),
                        pltpu.VMEM((N_QL, MC * _pitch(CH), LANES), F32),
                        pltpu.VMEM((TILE + 3 * MC, D_LRU), F32),
                        pltpu.VMEM((TILE, D_LRU), F32),
                        pltpu.VMEM((TILE, 2 * D_LRU), F32),
                        pltpu.VMEM((TILE, D_LRU), F32),
                        pltpu.VMEM((TILE, D_LRU), F32),
                        pltpu.VMEM((8, D_LRU), F32),
                        pltpu.VMEM((8, D_LRU), F32)],
        compiler_params=pltpu.CompilerParams(dimension_semantics=("arbitrary", "arbitrary"),
                                             vmem_limit_bytes=VMEM_LIMIT_BYTES),
        name="fwd_pass",
    )(x, x, mod, g, win, cw, cb, wg, bg, lam, fin)


def _dft_kernel(u_ref, dftc_ref, f1_ref, m2_ref, o_ref, sp_s, sq_s, yr_s, yj_s, so_s):
    p1 = DFT_P1
    p2 = DFT_P2
    dftc = dftc_ref[...].astype(BF16)
    chunks = TILE // DFT_N2
    for blk in range(SEQ // TILE):
        pq = jnp.dot(u_ref[0, blk * TILE:(blk + 1) * TILE, :], dftc, preferred_element_type=F32)
        for m in range(chunks):
            n1 = blk * chunks + m
            rows = slice(m * DFT_N2, (m + 1) * DFT_N2)
            sp_s[n1 * p1:n1 * p1 + DFT_N2, :] = pq[rows, :GROUP_DIM]
            sq_s[n1 * p1:n1 * p1 + DFT_N2, :] = pq[rows, GROUP_DIM:]
    f1 = f1_ref[...].astype(BF16)
    for n2 in range(0, DFT_N2, DFT_NB):
        rhs = jnp.concatenate(
            [jnp.concatenate([sp_s[pl.ds(n2 + i, DFT_N1, stride=p1), :],
                              sq_s[pl.ds(n2 + i, DFT_N1, stride=p1), :]], axis=0)
             for i in range(DFT_NB)], axis=1).astype(BF16)
        y = jnp.dot(f1, rhs, preferred_element_type=F32)
        for i in range(DFT_NB):
            lanes = slice(i * LANES, (i + 1) * LANES)
            yr_s[(n2 + i) * p2:(n2 + i) * p2 + DFT_N1, :] = y[:DFT_N1, lanes]
            yj_s[(n2 + i) * p2:(n2 + i) * p2 + DFT_N1, :] = y[DFT_N1:, lanes]
    for k1 in range(DFT_N1):
        rhs = jnp.concatenate([yr_s[pl.ds(k1, DFT_N2, stride=p2), :],
                               yj_s[pl.ds(k1, DFT_N2, stride=p2), :]], axis=0).astype(BF16)
        so_s[pl.ds(k1, DFT_N2, stride=p2), :] = jnp.dot(m2_ref[k1].astype(BF16), rhs,
                                                         preferred_element_type=F32)
    for k2 in range(DFT_N2):
        o_ref[0, k2 * DFT_N1:(k2 + 1) * DFT_N1, :] = so_s[k2 * p2:k2 * p2 + DFT_N1, :].astype(BF16)


def _fourier_dft(uf, dftc, f1, m2):
    bsz = uf.shape[0]
    blk = pl.BlockSpec((1, SEQ, GROUP_DIM), lambda b, i: (b, 0, i))
    pad1 = DFT_N1 * DFT_P1
    pad2 = DFT_N2 * DFT_P2
    return pl.pallas_call(
        _dft_kernel,
        grid=(bsz, N_GROUPS),
        in_specs=[blk,
                  pl.BlockSpec((GROUP_DIM, 2 * GROUP_DIM), lambda b, i: (0, 0)),
                  pl.BlockSpec((2 * DFT_N1, 2 * DFT_N1), lambda b, i: (0, 0)),
                  pl.BlockSpec((DFT_N1, DFT_N2, 2 * DFT_N2), lambda b, i: (0, 0, 0))],
        out_specs=blk,
        out_shape=jax.ShapeDtypeStruct(uf.shape, BF16),
        scratch_shapes=[pltpu.VMEM((pad1, LANES), F32), pltpu.VMEM((pad1, LANES), F32),
                        pltpu.VMEM((pad2, LANES), F32), pltpu.VMEM((pad2, LANES), F32),
                        pltpu.VMEM((pad2, LANES), F32)],
        compiler_params=pltpu.CompilerParams(dimension_semantics=("arbitrary", "arbitrary"),
                                             vmem_limit_bytes=VMEM_LIMIT_BYTES),
        name="fourier_dft",
    )(uf, dftc, f1, m2)


def _bwd_kernel(x_ref, yf_ref, gfs_ref, gls_ref, v_ref, hf_ref, mod_ref, fg_ref, wfour_ref, wout_ref,
                wg_ref, bg_ref, lam_ref, h0_ref, o_ref,
                g_s, a_s, h_s, yp_s, lhs_s, hc_s):
    j = pl.program_id(1)

    @pl.when(j == 0)
    def _():
        hc_s[...] = jnp.broadcast_to(h0_ref[0, 1:2, :], hc_s.shape)

    hrate = _half_decay_rate(lam_ref[1:2, :])
    halves = (slice(0, TILE // 2), slice(TILE // 2, TILE))
    for h in range(N_HEADS):
        vb = v_ref[0, :, _head_cols(h)]
        _gate_terms(h, vb, vb.astype(F32), wg_ref, bg_ref, hrate, g_s, slice(0, TILE))
        if h < len(halves):
            rq = halves[h]
            yfw = jnp.dot(yf_ref[0, rq, :], wfour_ref[...], preferred_element_type=F32)
            lhs_s[rq, 0:D_FOURIER] = (yfw * gfs_ref[0, rq, :].astype(F32)).astype(BF16)

    terms = lambda rows: (g_s[rows, :D_LRU], g_s[rows, D_LRU:])
    tot_a, tot_h = _scan(slice(0, D_LRU), terms, a_s, h_s, range(CH - 1, -1, -1))
    hin, h_out = _chunk_carries(tot_a, tot_h, hc_s[0:1, :], reverse=True)
    hc_s[...] = jnp.broadcast_to(h_out, hc_s.shape)
    pitch = _pitch(CH)
    for s in range(CH):
        rows = slice(s * MC, (s + 1) * MC)
        yl = h_s[rows, :] + a_s[rows, :] * hin + hf_ref[0, rows, :].astype(F32)
        for q in range(N_QL):
            yp_s[q, pl.ds(s, MC, stride=pitch), :] = yl[:, q * LANES:(q + 1) * LANES]
    for m in range(MC):
        rows = slice(m * CH, (m + 1) * CH)
        for q in range(N_QL):
            cols = slice(q * LANES, (q + 1) * LANES)
            yl = yp_s[q, m * pitch:m * pitch + CH, :]
            lhs_s[rows, D_FOURIER + q * LANES:D_FOURIER + (q + 1) * LANES] = (
                yl * gls_ref[0, rows, cols].astype(F32)).astype(BF16)

    res_gate = mod_ref[pl.ds(pl.program_id(0), 1), 2 * D_MODEL:3 * D_MODEL]
    fg = fg_ref[...]
    proj = jnp.dot(lhs_s[...], wout_ref[...], preferred_element_type=F32)
    for m in range(MC):
        rows = slice(m * CH, (m + 1) * CH)
        res = x_ref[0, rows, :] + res_gate * proj[rows, :]
        ms = jnp.mean(res * res, axis=-1, keepdims=True)
        o_ref[0, rows, :] = (res * lax.rsqrt(ms + EPS)) * fg


def _bwd_pass(x, yf, gfs, gls, v, hf, mod, fg, wfour, wout, wg, bg, lam, fin):
    bsz = x.shape[0]
    full = lambda *shape: pl.BlockSpec(shape, lambda b, j: (0,) * len(shape))
    tile = lambda c: pl.BlockSpec((1, TILE, c), lambda b, j: (b, N_TILES - 1 - j, 0))
    return pl.pallas_call(
        _bwd_kernel,
        grid=(bsz, N_TILES),
        in_specs=[tile(D_MODEL), tile(D_FOURIER), tile(D_FOURIER), tile(D_LRU), tile(D_LRU),
                  tile(D_LRU),
                  full(8, 3 * D_MODEL),
                  full(1, D_MODEL), full(D_FOURIER, D_FOURIER), full(D_MODEL, D_MODEL),
                  *_gate_specs(1), full(2, D_LRU),
                  pl.BlockSpec((1, 2, D_LRU), lambda b, j: (b, 0, 0))],
        out_specs=tile(D_MODEL),
        out_shape=jax.ShapeDtypeStruct(x.shape, F32),
        scratch_shapes=[pltpu.VMEM((TILE, 2 * D_LRU), F32),
                        pltpu.VMEM((TILE, D_LRU), F32),
                        pltpu.VMEM((TILE, D_LRU), F32),
                        pltpu.VMEM((N_QL, MC * _pitch(CH), LANES), F32),
                        pltpu.VMEM((TILE, D_MODEL), BF16),
                        pltpu.VMEM((8, D_LRU), F32)],
        compiler_params=pltpu.CompilerParams(dimension_semantics=("arbitrary", "arbitrary"),
                                             vmem_limit_bytes=VMEM_LIMIT_BYTES),
        name="bwd_pass",
    )(x, yf, gfs, gls, v, hf, mod, fg, wfour, wout, wg, bg, lam, fin)


def kernel(x, c, ctx, c_ctx, w_ada, b_ada, norm_gain, w_in, w_four, conv_w, conv_b, w_rg, b_rg,
           w_ig, b_ig, lam, w_out, final_gain):
    bsz = x.shape[0]
    assert x.shape == (bsz, SEQ, D_MODEL) and ctx.shape == (bsz, CTX_LEN, D_MODEL)
    assert w_ada.shape[0] == 1, "single-layer kernel"
    dftc, f1, m2 = _dft_constants()

    mod, win, wout, wfour, wg, bg, cw, cb = _prep(c, c_ctx[None, :], w_ada, b_ada, w_in, w_out,
                                                  w_four, w_rg, w_ig, b_rg, b_ig, conv_w, conv_b)
    g = norm_gain[0][None, :]
    lam0 = lam[0]

    fin = _ctx_states(ctx, mod, g, win, cw, cb, wg, bg, lam0)
    uf, gfs, gls, v, hf = _fwd_pass(x, mod, g, win, cw, cb, wg, bg, lam0, fin)

    yf = _fourier_dft(uf, dftc, f1, m2)

    return _bwd_pass(x, yf, gfs, gls, v, hf, mod, final_gain[None, :], wfour, wout, wg, bg, lam0, fin)
```

```python
import functools

import numpy as np
import jax
import jax.numpy as jnp
from jax import lax
from jax.experimental import pallas as pl
from jax.experimental.pallas import tpu as pltpu

D_MODEL = 1024
SEQ = 8192
CTX_LEN = 256
D_FOURIER = 512
D_LRU = 512
N_GROUPS = 4
GROUP_DIM = 128
N_HEADS = 4
HEAD_DIM = 128
LRU_C = 8.0
EPS = 1e-6
LOG2_E = float(np.log2(np.e))

LANES = 128
MC = 16
CH = 64
TILE = MC * CH
N_TILES = SEQ // TILE
CH_CTX = CTX_LEN // MC
N_QL = D_LRU // LANES
LHS_ROWS = TILE + MC
DFT_N1 = 128
DFT_N2 = 64
DFT_NB = 2

VMEM_LIMIT_BYTES = 56 * 1024 * 1024
F32 = jnp.float32
BF16 = jnp.bfloat16


def _pitch(n):
    return n + 4


DFT_P1 = _pitch(DFT_N2)
DFT_P2 = _pitch(DFT_N1)


def _dft_constants():
    c = np.arange(GROUP_DIM)
    ang = 2.0 * np.pi * ((c[:, None] * c[None, :]) % GROUP_DIM) / GROUP_DIM
    dftc = np.concatenate([np.cos(ang), np.sin(ang)], axis=1) * 2.0 ** -3

    n = np.arange(DFT_N1)
    ang1 = 2.0 * np.pi * ((n[:, None] * n[None, :]) % DFT_N1) / DFT_N1
    c1, s1 = np.cos(ang1), np.sin(ang1)
    f1 = np.block([[c1, -s1], [s1, c1]]) * 2.0 ** -4

    k1 = np.arange(DFT_N1)[:, None, None]
    k2 = np.arange(DFT_N2)[None, :, None]
    n2 = np.arange(DFT_N2)[None, None, :]
    ang2 = 2.0 * np.pi * ((n2 * (k1 + DFT_N1 * k2)) % SEQ) / SEQ
    m2 = np.concatenate([np.cos(ang2), -np.sin(ang2)], axis=2) * 2.0 ** -3
    return jnp.asarray(dftc, dtype=F32), jnp.asarray(f1, dtype=F32), jnp.asarray(m2, dtype=F32)


def _silu(x):
    hx = 0.5 * x
    return hx * jnp.tanh(hx) + hx


def _silu_of_half(hx):
    return hx * jnp.tanh(hx) + hx


def _half_decay_rate(lam_row):
    z = -lam_row
    return (-0.5 * LRU_C * LOG2_E) * (jnp.maximum(z, 0.0) + jnp.log1p(jnp.exp(-jnp.abs(z))))


def _modulate(xs, gmul, shift):
    ms = jnp.mean(xs * xs, axis=-1, keepdims=True)
    return (xs * lax.rsqrt(ms + EPS)) * gmul + shift


def _gate_ab(hpre_r, hpre_i, hv, hrate):
    a = jnp.exp2(jnp.tanh(hpre_r) * hrate + hrate)
    om = 1.0 - a * a
    s = om * lax.rsqrt(jnp.maximum(om, 1e-30))
    return a, s * ((jnp.tanh(hpre_i) + 1.0) * hv)


def _ds(start, size):
    if isinstance(start, int):
        return slice(start, start + size)
    return pl.ds(pl.multiple_of(start, MC), size)


def _to_slab_order(u, up_s, uext_s, ch):
    pitch = _pitch(ch)
    for m in range(MC):
        for q in range(N_QL):
            up_s[q, m * pitch:m * pitch + ch, :] = u[m * ch:(m + 1) * ch, q * LANES:(q + 1) * LANES]
    for s in range(ch):
        for q in range(N_QL):
            uext_s[(s + 2) * MC:(s + 3) * MC, q * LANES:(q + 1) * LANES] = (
                up_s[q, pl.ds(s, MC, stride=pitch), :])


def _conv_halo(uext_s, lookahead, c_prev1, c_prev2, ch):
    last = uext_s[(ch + 1) * MC:(ch + 2) * MC, :]
    last2 = uext_s[ch * MC:(ch + 1) * MC, :]
    first = uext_s[2 * MC:3 * MC, :]
    uext_s[0:MC, :] = jnp.concatenate([c_prev2, last2[:MC - 1]], axis=0)
    uext_s[MC:2 * MC, :] = jnp.concatenate([c_prev1, last[:MC - 1]], axis=0)
    uext_s[(ch + 2) * MC:(ch + 3) * MC, :] = jnp.concatenate([first[1:], lookahead], axis=0)


def _conv_slab(uext_s, s, cw_ref, cb_ref):
    acc = cb_ref[...]
    for k in range(4):
        acc = acc + cw_ref[k] * uext_s[_ds((s + k) * MC, MC), :]
    return acc


def _head_cols(h, base=0):
    return slice(base + h * HEAD_DIM, base + (h + 1) * HEAD_DIM)


def _gate_terms(h, vb, hv, wg_ref, bg_ref, hrate, g_s, rows):
    cols, icols = _head_cols(h), _head_cols(h, D_LRU)
    res = jnp.dot(vb, wg_ref[h], preferred_element_type=F32)
    a, bx = _gate_ab(res[:, :HEAD_DIM] + bg_ref[:, cols], res[:, HEAD_DIM:] + bg_ref[:, icols],
                     hv, hrate[:, cols])
    g_s[rows, cols] = a
    g_s[rows, icols] = bx


def _stored_terms(h, g_s):
    return lambda rows: (g_s[rows, _head_cols(h)], g_s[rows, _head_cols(h, D_LRU)])


def _scan(cols, terms, a_s, h_s, steps):
    width = cols.stop - cols.start
    acc_a = jnp.ones((MC, width), F32)
    acc_h = jnp.zeros((MC, width), F32)
    for s in steps:
        rows = slice(s * MC, (s + 1) * MC)
        a, bx = terms(rows)
        acc_a = a * acc_a
        acc_h = a * acc_h + bx
        a_s[rows, cols] = acc_a
        h_s[rows, cols] = acc_h
    return acc_a, acc_h


def _chunk_carries(tot_a, tot_h, carry_in, reverse):
    row_id = lax.broadcasted_iota(jnp.int32, tot_a.shape, 0)
    hin = jnp.zeros(tot_a.shape, F32)
    h = carry_in
    order = range(MC - 1, -1, -1) if reverse else range(MC)
    for m in order:
        hin = jnp.where(row_id == m, h, hin)
        h = tot_a[m:m + 1] * h + tot_h[m:m + 1]
    return hin, h


def _prep_kernel(c_ref, cctx_ref, wada_ref, bada_ref, win_ref, wout_ref, wfour_ref, wrg_ref, wig_ref,
                 brg_ref, big_ref, cw_ref, cb_ref,
                 mod_ref, win_o, wout_o, wfour_o, wg_o, bg_o, cwr_o, cbr_o):
    i = pl.program_id(0)
    bsz = c_ref.shape[0]
    cc = jnp.concatenate([c_ref[...], cctx_ref[...], jnp.zeros((8 - bsz - 1, D_MODEL), F32)], axis=0)
    s = _silu(cc).astype(BF16)
    mod_ref[...] = (jnp.dot(s, wada_ref[0].astype(BF16), preferred_element_type=F32)
                    + bada_ref[...])

    @pl.when(i < 2)
    def _():
        win_o[:, 0:D_FOURIER] = win_ref[0, :, 0:D_FOURIER].astype(BF16)
        win_o[:, D_FOURIER:] = (0.5 * win_ref[0, :, D_FOURIER:]).astype(BF16)

    @pl.when(i == 2)
    def _():
        wout_o[...] = wout_ref[0].astype(BF16)
        wfour_o[...] = wfour_ref[0].astype(BF16)
        for d in range(2):
            for h in range(N_HEADS):
                wg_o[d, h] = jnp.concatenate([wrg_ref[0, d, h], wig_ref[0, d, h]],
                                             axis=1).astype(BF16)
            bg_o[d] = 0.5 * jnp.concatenate([brg_ref[0, d:d + 1, :], big_ref[0, d:d + 1, :]], axis=1)
        for k in range(4):
            cwr_o[k] = jnp.broadcast_to(0.5 * cw_ref[0, k:k + 1, :], (MC, D_LRU))
        cbr_o[...] = jnp.broadcast_to(0.5 * cb_ref[...], (MC, D_LRU))


def _prep(c, c_ctx, w_ada, b_ada, w_in, w_out, w_four, w_rg, w_ig, b_rg, b_ig, conv_w, conv_b):
    full = lambda *shape: pl.BlockSpec(shape, lambda i: (0,) * len(shape))
    half = lambda i: jnp.minimum(i, 1)
    gshape = (1, 2, N_HEADS, HEAD_DIM, HEAD_DIM)
    return pl.pallas_call(
        _prep_kernel,
        grid=(3,),
        in_specs=[full(*c.shape), full(1, D_MODEL),
                  pl.BlockSpec((1, D_MODEL, D_MODEL), lambda i: (0, 0, i)),
                  pl.BlockSpec((1, D_MODEL), lambda i: (0, i)),
                  pl.BlockSpec((1, D_MODEL, D_MODEL), lambda i: (0, 0, half(i))),
                  full(1, D_MODEL, D_MODEL), full(1, D_FOURIER, D_FOURIER),
                  full(*gshape), full(*gshape), full(1, 2, D_LRU), full(1, 2, D_LRU),
                  full(1, 4, D_LRU), full(1, D_LRU)],
        out_specs=[pl.BlockSpec((8, D_MODEL), lambda i: (0, i)),
                   pl.BlockSpec((D_MODEL, D_MODEL), lambda i: (0, half(i))),
                   full(D_MODEL, D_MODEL), full(D_FOURIER, D_FOURIER),
                   full(2, N_HEADS, HEAD_DIM, 2 * HEAD_DIM), full(2, 1, 2 * D_LRU),
                   full(4, MC, D_LRU), full(MC, D_LRU)],
        out_shape=[jax.ShapeDtypeStruct((8, 3 * D_MODEL), F32),
                   jax.ShapeDtypeStruct((D_MODEL, 2 * D_MODEL), BF16),
                   jax.ShapeDtypeStruct((D_MODEL, D_MODEL), BF16),
                   jax.ShapeDtypeStruct((D_FOURIER, D_FOURIER), BF16),
                   jax.ShapeDtypeStruct((2, N_HEADS, HEAD_DIM, 2 * HEAD_DIM), BF16),
                   jax.ShapeDtypeStruct((2, 1, 2 * D_LRU), F32),
                   jax.ShapeDtypeStruct((4, MC, D_LRU), F32),
                   jax.ShapeDtypeStruct((MC, D_LRU), F32)],
        compiler_params=pltpu.CompilerParams(dimension_semantics=("arbitrary",),
                                             vmem_limit_bytes=VMEM_LIMIT_BYTES),
        name="prep",
    )(c, c_ctx, w_ada, b_ada, w_in, w_out, w_four, w_rg, w_ig, b_rg, b_ig, conv_w, conv_b)


def _ctx_kernel(x_ref, mod_ref, g_ref, wlx_ref, cw_ref, cb_ref, wgf_ref, bgf_ref, wgb_ref, bgb_ref,
                lam_ref, fin_ref, up_s, uext_s, vf_s, g_s, a_s, h_s, *, mod_row):
    ch = CH_CTX
    shift = mod_ref[mod_row:mod_row + 1, 0:D_MODEL]
    scale = mod_ref[mod_row:mod_row + 1, D_MODEL:2 * D_MODEL]
    gmul = g_ref[...] * (1.0 + scale)
    lhs = _modulate(x_ref[0], gmul, shift).astype(BF16)
    u = jnp.dot(lhs, wlx_ref[...], preferred_element_type=F32)
    _to_slab_order(u, up_s, uext_s, ch)
    zero_row = jnp.zeros((1, D_LRU), F32)
    _conv_halo(uext_s, zero_row, zero_row, zero_row, ch)
    for s in range(ch):
        vf_s[s * MC:(s + 1) * MC, :] = _conv_slab(uext_s, s, cw_ref, cb_ref)
    vb = vf_s[...].astype(BF16)
    for d, (wg_ref, bg_ref) in enumerate(((wgf_ref, bgf_ref), (wgb_ref, bgb_ref))):
        rate = _half_decay_rate(lam_ref[d:d + 1, :])
        steps = range(ch - 1, -1, -1) if d == 1 else range(ch)
        for h in range(N_HEADS):
            cols = _head_cols(h)
            _gate_terms(h, vb[:, cols], vf_s[:, cols], wg_ref, bg_ref, rate, g_s,
                        slice(0, ch * MC))
            tot_a, tot_h = _scan(cols, _stored_terms(h, g_s), a_s, h_s, steps)
            _, fin = _chunk_carries(tot_a, tot_h, zero_row[:, cols], reverse=(d == 1))
            fin_ref[0, d:d + 1, cols] = fin


def _gate_specs(direction):
    return (pl.BlockSpec((None, N_HEADS, HEAD_DIM, 2 * HEAD_DIM), lambda *_: (direction, 0, 0, 0)),
            pl.BlockSpec((None, 1, 2 * D_LRU), lambda *_: (direction, 0, 0)))


def _ctx_states(ctx, mod, g, win, cw, cb, wg, bg, lam):
    bsz = ctx.shape[0]
    n = CTX_LEN
    full = lambda *shape: pl.BlockSpec(shape, lambda b: (0,) * len(shape))
    return pl.pallas_call(
        functools.partial(_ctx_kernel, mod_row=bsz),
        grid=(bsz,),
        in_specs=[pl.BlockSpec((1, n, D_MODEL), lambda b: (b, 0, 0)),
                  full(8, 3 * D_MODEL),
                  full(1, D_MODEL),
                  pl.BlockSpec((D_MODEL, D_LRU), lambda b: (0, 2)),
                  full(4, MC, D_LRU), full(MC, D_LRU),
                  *_gate_specs(0), *_gate_specs(1),
                  full(2, D_LRU)],
        out_specs=pl.BlockSpec((1, 2, D_LRU), lambda b: (b, 0, 0)),
        out_shape=jax.ShapeDtypeStruct((bsz, 2, D_LRU), F32),
        scratch_shapes=[pltpu.VMEM((N_QL, MC * _pitch(CH_CTX), LANES), F32),
                        pltpu.VMEM((n + 3 * MC, D_LRU), F32),
                        pltpu.VMEM((n, D_LRU), F32),
                        pltpu.VMEM((n, 2 * D_LRU), F32),
                        pltpu.VMEM((n, D_LRU), F32),
                        pltpu.VMEM((n, D_LRU), F32)],
        compiler_params=pltpu.CompilerParams(dimension_semantics=("arbitrary",)),
        name="ctx_states",
    )(ctx, mod, g, win, cw, cb, wg, bg, wg, bg, lam)


def _fwd_kernel(x_ref, xh_ref, mod_ref, g_ref, win_ref, cw_ref, cb_ref, wg_ref,
                bg_ref, lam_ref, h0_ref,
                uf_ref, gfs_ref, gls_ref, v_ref, hf_ref,
                lhs_s, up_s, uext_s, vf_s, g_s, a_s, h_s, cc_s, hc_s):
    j = pl.program_id(1)

    @pl.when(j == 0)
    def _():
        cc_s[...] = jnp.zeros_like(cc_s)
        hc_s[...] = jnp.broadcast_to(h0_ref[0, 0:1, :], hc_s.shape)

    brow = pl.ds(pl.program_id(0), 1)
    shift = mod_ref[brow, 0:D_MODEL]
    scale = mod_ref[brow, D_MODEL:2 * D_MODEL]
    gmul = g_ref[...] * (1.0 + scale)
    for m in range(MC):
        rows = slice(m * CH, (m + 1) * CH)
        lhs_s[rows, :] = _modulate(x_ref[0, rows, :], gmul, shift).astype(BF16)
    hh = _modulate(xh_ref[0], gmul, shift)
    lhs_s[TILE:LHS_ROWS, :] = jnp.concatenate([hh, jnp.zeros_like(hh)], axis=0).astype(BF16)

    u = jnp.dot(lhs_s[...], win_ref[:, 2 * D_FOURIER:2 * D_FOURIER + D_LRU],
                preferred_element_type=F32)

    _to_slab_order(u, up_s, uext_s, CH)
    look = jnp.where(j == N_TILES - 1, 0.0, u[TILE:TILE + 1])
    _conv_halo(uext_s, look, cc_s[0:1, :], cc_s[1:2, :], CH)
    cc_s[0:1, :] = u[TILE - 1:TILE]
    cc_s[1:2, :] = u[TILE - 2:TILE - 1]

    uf_ref[0] = jnp.dot(lhs_s[0:TILE, :], win_ref[:, 0:D_FOURIER],
                        preferred_element_type=F32).astype(BF16)

    for s in range(CH):
        rows = slice(s * MC, (s + 1) * MC)
        v = _conv_slab(uext_s, s, cw_ref, cb_ref)
        vf_s[rows, :] = v
        v_ref[0, rows, :] = v.astype(BF16)

    halves = (slice(0, TILE // 2), slice(TILE // 2, TILE))
    hrate = _half_decay_rate(lam_ref[0:1, :])
    for h in range(N_HEADS):
        cols = _head_cols(h)
        _gate_terms(h, v_ref[0, :, cols], vf_s[:, cols], wg_ref, bg_ref, hrate, g_s, slice(0, TILE))
        o_ref, c0 = ((gfs_ref, D_FOURIER), (gls_ref, 2 * D_FOURIER + D_LRU))[h // 2]
        rq = halves[h % 2]
        o_ref[0, rq, :] = _silu_of_half(jnp.dot(lhs_s[rq, :], win_ref[:, c0:c0 + D_LRU],
                                                preferred_element_type=F32)).astype(BF16)

    terms = lambda rows: (g_s[rows, :D_LRU], g_s[rows, D_LRU:])
    tot_a, tot_h = _scan(slice(0, D_LRU), terms, a_s, h_s, range(CH))
    hin, h_out = _chunk_carries(tot_a, tot_h, hc_s[0:1, :], reverse=False)
    hc_s[...] = jnp.broadcast_to(h_out, hc_s.shape)
    for s in range(CH):
        rows = slice(s * MC, (s + 1) * MC)
        hf_ref[0, rows, :] = (h_s[rows, :] + a_s[rows, :] * hin).astype(BF16)


def _fwd_pass(x, mod, g, win, cw, cb, wg, bg, lam, fin):
    bsz = x.shape[0]
    full = lambda *shape: pl.BlockSpec(shape, lambda b, j: (0,) * len(shape))
    tile = lambda c: pl.BlockSpec((1, TILE, c), lambda b, j: (b, j, 0))
    shp = lambda c: jax.ShapeDtypeStruct((bsz, SEQ, c), BF16)
    rows8 = TILE // 8
    return pl.pallas_call(
        _fwd_kernel,
        grid=(bsz, N_TILES),
        in_specs=[tile(D_MODEL),
                  pl.BlockSpec((1, 8, D_MODEL),
                               lambda b, j: (b, jnp.minimum((j + 1) * rows8, SEQ // 8 - 1), 0)),
                  full(8, 3 * D_MODEL),
                  full(1, D_MODEL), full(D_MODEL, 2 * D_MODEL),
                  full(4, MC, D_LRU), full(MC, D_LRU),
                  *_gate_specs(0),
                  full(2, D_LRU),
                  pl.BlockSpec((1, 2, D_LRU), lambda b, j: (b, 0, 0))],
        out_specs=[tile(D_FOURIER), tile(D_FOURIER), tile(D_LRU), tile(D_LRU), tile(D_LRU)],
        out_shape=[shp(D_FOURIER), shp(D_FOURIER), shp(D_LRU), shp(D_LRU), shp(D_LRU)],
        scratch_shapes=[pltpu.VMEM((LHS_ROWS, D_MODEL), BF16),
                        pltpu.VMEM((N_QL, MC * _pitch(CH), LANES), F32),
                        pltpu.VMEM((TILE + 3 * MC, D_LRU), F32),
                        pltpu.VMEM((TILE, D_LRU), F32),
                        pltpu.VMEM((TILE, 2 * D_LRU), F32),
                        pltpu.VMEM((TILE, D_LRU), F32),
                        pltpu.VMEM((TILE, D_LRU), F32),
                        pltpu.VMEM((8, D_LRU), F32),
                        pltpu.VMEM((8, D_LRU), F32)],
        compiler_params=pltpu.CompilerParams(dimension_semantics=("arbitrary", "arbitrary"),
                                             vmem_limit_bytes=VMEM_LIMIT_BYTES),
        name="fwd_pass",
    )(x, x, mod, g, win, cw, cb, wg, bg, lam, fin)


def _dft_kernel(u_ref, dftc_ref, f1_ref, m2_ref, o_ref, sp_s, sq_s, yr_s, yj_s, so_s):
    p1 = DFT_P1
    p2 = DFT_P2
    dftc = dftc_ref[...].astype(BF16)
    chunks = TILE // DFT_N2
    for blk in range(SEQ // TILE):
        pq = jnp.dot(u_ref[0, blk * TILE:(blk + 1) * TILE, :], dftc, preferred_element_type=F32)
        for m in range(chunks):
            n1 = blk * chunks + m
            rows = slice(m * DFT_N2, (m + 1) * DFT_N2)
            sp_s[n1 * p1:n1 * p1 + DFT_N2, :] = pq[rows, :GROUP_DIM]
            sq_s[n1 * p1:n1 * p1 + DFT_N2, :] = pq[rows, GROUP_DIM:]
    f1 = f1_ref[...].astype(BF16)
    for n2 in range(0, DFT_N2, DFT_NB):
        rhs = jnp.concatenate(
            [jnp.concatenate([sp_s[pl.ds(n2 + i, DFT_N1, stride=p1), :],
                              sq_s[pl.ds(n2 + i, DFT_N1, stride=p1), :]], axis=0)
             for i in range(DFT_NB)], axis=1).astype(BF16)
        y = jnp.dot(f1, rhs, preferred_element_type=F32)
        for i in range(DFT_NB):
            lanes = slice(i * LANES, (i + 1) * LANES)
            yr_s[(n2 + i) * p2:(n2 + i) * p2 + DFT_N1, :] = y[:DFT_N1, lanes]
            yj_s[(n2 + i) * p2:(n2 + i) * p2 + DFT_N1, :] = y[DFT_N1:, lanes]
    for k1 in range(DFT_N1):
        rhs = jnp.concatenate([yr_s[pl.ds(k1, DFT_N2, stride=p2), :],
                               yj_s[pl.ds(k1, DFT_N2, stride=p2), :]], axis=0).astype(BF16)
        so_s[pl.ds(k1, DFT_N2, stride=p2), :] = jnp.dot(m2_ref[k1].astype(BF16), rhs,
                                                         preferred_element_type=F32)
    for k2 in range(DFT_N2):
        o_ref[0, k2 * DFT_N1:(k2 + 1) * DFT_N1, :] = so_s[k2 * p2:k2 * p2 + DFT_N1, :].astype(BF16)


def _fourier_dft(uf, dftc, f1, m2):
    bsz = uf.shape[0]
    blk = pl.BlockSpec((1, SEQ, GROUP_DIM), lambda b, i: (b, 0, i))
    pad1 = DFT_N1 * DFT_P1
    pad2 = DFT_N2 * DFT_P2
    return pl.pallas_call(
        _dft_kernel,
        grid=(bsz, N_GROUPS),
        in_specs=[blk,
                  pl.BlockSpec((GROUP_DIM, 2 * GROUP_DIM), lambda b, i: (0, 0)),
                  pl.BlockSpec((2 * DFT_N1, 2 * DFT_N1), lambda b, i: (0, 0)),
                  pl.BlockSpec((DFT_N1, DFT_N2, 2 * DFT_N2), lambda b, i: (0, 0, 0))],
        out_specs=blk,
        out_shape=jax.ShapeDtypeStruct(uf.shape, BF16),
        scratch_shapes=[pltpu.VMEM((pad1, LANES), F32), pltpu.VMEM((pad1, LANES), F32),
                        pltpu.VMEM((pad2, LANES), F32), pltpu.VMEM((pad2, LANES), F32),
                        pltpu.VMEM((pad2, LANES), F32)],
        compiler_params=pltpu.CompilerParams(dimension_semantics=("arbitrary", "arbitrary"),
                                             vmem_limit_bytes=VMEM_LIMIT_BYTES),
        name="fourier_dft",
    )(uf, dftc, f1, m2)


def _bwd_kernel(x_ref, yf_ref, gfs_ref, gls_ref, v_ref, hf_ref, mod_ref, fg_ref, wfour_ref, wout_ref,
                wg_ref, bg_ref, lam_ref, h0_ref, o_ref,
                g_s, a_s, h_s, yp_s, lhs_s, hc_s):
    j = pl.program_id(1)

    @pl.when(j == 0)
    def _():
        hc_s[...] = jnp.broadcast_to(h0_ref[0, 1:2, :], hc_s.shape)

    for h in range(N_HEADS):
        res = jnp.dot(v_ref[0, :, _head_cols(h)], wg_ref[h], preferred_element_type=F32)
        g_s[:, _head_cols(h)] = res[:, :HEAD_DIM]
        g_s[:, _head_cols(h, D_LRU)] = res[:, HEAD_DIM:]
    yfw = jnp.dot(yf_ref[0], wfour_ref[...], preferred_element_type=F32)
    lhs_s[:, 0:D_FOURIER] = (yfw * gfs_ref[0].astype(F32)).astype(BF16)

    hrate = _half_decay_rate(lam_ref[1:2, :])
    b_r = bg_ref[:, :D_LRU]
    b_i = bg_ref[:, D_LRU:]
    terms = lambda rows: _gate_ab(g_s[rows, :D_LRU] + b_r, g_s[rows, D_LRU:] + b_i,
                                  v_ref[0, rows, :].astype(F32), hrate)
    tot_a, tot_h = _scan(slice(0, D_LRU), terms, a_s, h_s, range(CH - 1, -1, -1))
    hin, h_out = _chunk_carries(tot_a, tot_h, hc_s[0:1, :], reverse=True)
    hc_s[...] = jnp.broadcast_to(h_out, hc_s.shape)
    pitch = _pitch(CH)
    for s in range(CH):
        rows = slice(s * MC, (s + 1) * MC)
        yl = h_s[rows, :] + a_s[rows, :] * hin + hf_ref[0, rows, :].astype(F32)
        for q in range(N_QL):
            yp_s[q, pl.ds(s, MC, stride=pitch), :] = yl[:, q * LANES:(q + 1) * LANES]
    for m in range(MC):
        rows = slice(m * CH, (m + 1) * CH)
        for q in range(N_QL):
            cols = slice(q * LANES, (q + 1) * LANES)
            yl = yp_s[q, m * pitch:m * pitch + CH, :]
            lhs_s[rows, D_FOURIER + q * LANES:D_FOURIER + (q + 1) * LANES] = (
                yl * gls_ref[0, rows, cols].astype(F32)).astype(BF16)

    res_gate = mod_ref[pl.ds(pl.program_id(0), 1), 2 * D_MODEL:3 * D_MODEL]
    fg = fg_ref[...]
    proj = jnp.dot(lhs_s[...], wout_ref[...], preferred_element_type=F32)
    for m in range(MC):
        rows = slice(m * CH, (m + 1) * CH)
        res = x_ref[0, rows, :] + res_gate * proj[rows, :]
        ms = jnp.mean(res * res, axis=-1, keepdims=True)
        o_ref[0, rows, :] = (res * lax.rsqrt(ms + EPS)) * fg


def _bwd_pass(x, yf, gfs, gls, v, hf, mod, fg, wfour, wout, wg, bg, lam, fin):
    bsz = x.shape[0]
    full = lambda *shape: pl.BlockSpec(shape, lambda b, j: (0,) * len(shape))
    tile = lambda c: pl.BlockSpec((1, TILE, c), lambda b, j: (b, N_TILES - 1 - j, 0))
    return pl.pallas_call(
        _bwd_kernel,
        grid=(bsz, N_TILES),
        in_specs=[tile(D_MODEL), tile(D_FOURIER), tile(D_FOURIER), tile(D_LRU), tile(D_LRU),
                  tile(D_LRU),
                  full(8, 3 * D_MODEL),
                  full(1, D_MODEL), full(D_FOURIER, D_FOURIER), full(D_MODEL, D_MODEL),
                  *_gate_specs(1), full(2, D_LRU),
                  pl.BlockSpec((1, 2, D_LRU), lambda b, j: (b, 0, 0))],
        out_specs=tile(D_MODEL),
        out_shape=jax.ShapeDtypeStruct(x.shape, F32),
        scratch_shapes=[pltpu.VMEM((TILE, 2 * D_LRU), F32),
                        pltpu.VMEM((TILE, D_LRU), F32),
                        pltpu.VMEM((TILE, D_LRU), F32),
                        pltpu.VMEM((N_QL, MC * _pitch(CH), LANES), F32),
                        pltpu.VMEM((TILE, D_MODEL), BF16),
                        pltpu.VMEM((8, D_LRU), F32)],
        compiler_params=pltpu.CompilerParams(dimension_semantics=("arbitrary", "arbitrary"),
                                             vmem_limit_bytes=VMEM_LIMIT_BYTES),
        name="bwd_pass",
    )(x, yf, gfs, gls, v, hf, mod, fg, wfour, wout, wg, bg, lam, fin)


def kernel(x, c, ctx, c_ctx, w_ada, b_ada, norm_gain, w_in, w_four, conv_w, conv_b, w_rg, b_rg,
           w_ig, b_ig, lam, w_out, final_gain):
    bsz = x.shape[0]
    assert x.shape == (bsz, SEQ, D_MODEL) and ctx.shape == (bsz, CTX_LEN, D_MODEL)
    assert w_ada.shape[0] == 1, "single-layer kernel"
    dftc, f1, m2 = _dft_constants()

    mod, win, wout, wfour, wg, bg, cw, cb = _prep(c, c_ctx[None, :], w_ada, b_ada, w_in, w_out,
                                                  w_four, w_rg, w_ig, b_rg, b_ig, conv_w, conv_b)
    g = norm_gain[0][None, :]
    lam0 = lam[0]

    fin = _ctx_states(ctx, mod, g, win, cw, cb, wg, bg, lam0)
    uf, gfs, gls, v, hf = _fwd_pass(x, mod, g, win, cw, cb, wg, bg, lam0, fin)

    yf = _fourier_dft(uf, dftc, f1, m2)

    return _bwd_pass(x, yf, gfs, gls, v, hf, mod, final_gain[None, :], wfour, wout, wg, bg, lam0, fin)
```

```python
import functools

import numpy as np
import jax
import jax.numpy as jnp
from jax import lax
from jax.experimental import pallas as pl
from jax.experimental.pallas import tpu as pltpu

D_MODEL = 1024
SEQ = 8192
CTX_LEN = 256
D_FOURIER = 512
D_LRU = 512
N_GROUPS = 4
GROUP_DIM = 128
N_HEADS = 4
HEAD_DIM = 128
LRU_C = 8.0
EPS = 1e-6
LOG2_E = float(np.log2(np.e))

LANES = 128
MC = 16
CH = 64
TILE = MC * CH
N_TILES = SEQ // TILE
CH_CTX = CTX_LEN // MC
N_QL = D_LRU // LANES
LHS_ROWS = TILE + MC
DFT_N1 = 128
DFT_N2 = 64
DFT_NB = 2

VMEM_LIMIT_BYTES = 56 * 1024 * 1024
F32 = jnp.float32
BF16 = jnp.bfloat16


def _pitch(n):
    return n + 4


DFT_P1 = _pitch(DFT_N2)
DFT_P2 = _pitch(DFT_N1)


def _dft_constants():
    c = np.arange(GROUP_DIM)
    ang = 2.0 * np.pi * ((c[:, None] * c[None, :]) % GROUP_DIM) / GROUP_DIM
    dftc = np.concatenate([np.cos(ang), np.sin(ang)], axis=1) * 2.0 ** -3

    n = np.arange(DFT_N1)
    ang1 = 2.0 * np.pi * ((n[:, None] * n[None, :]) % DFT_N1) / DFT_N1
    c1, s1 = np.cos(ang1), np.sin(ang1)
    f1 = np.block([[c1, -s1], [s1, c1]]) * 2.0 ** -4

    k1 = np.arange(DFT_N1)[:, None, None]
    k2 = np.arange(DFT_N2)[None, :, None]
    n2 = np.arange(DFT_N2)[None, None, :]
    ang2 = 2.0 * np.pi * ((n2 * (k1 + DFT_N1 * k2)) % SEQ) / SEQ
    m2 = np.concatenate([np.cos(ang2), -np.sin(ang2)], axis=2) * 2.0 ** -3
    return jnp.asarray(dftc, dtype=F32), jnp.asarray(f1, dtype=F32), jnp.asarray(m2, dtype=F32)


def _silu(x):
    hx = 0.5 * x
    return hx * jnp.tanh(hx) + hx


def _silu_of_half(hx):
    return hx * jnp.tanh(hx) + hx


def _half_decay_rate(lam_row):
    z = -lam_row
    return (-0.5 * LRU_C * LOG2_E) * (jnp.maximum(z, 0.0) + jnp.log1p(jnp.exp(-jnp.abs(z))))


def _modulate(xs, gmul, shift):
    ms = jnp.mean(xs * xs, axis=-1, keepdims=True)
    return (xs * lax.rsqrt(ms + EPS)) * gmul + shift


def _gate_ab(hpre_r, hpre_i, hv, hrate):
    a = jnp.exp2(jnp.tanh(hpre_r) * hrate + hrate)
    om = 1.0 - a * a
    s = om * lax.rsqrt(jnp.maximum(om, 1e-30))
    return a, s * ((jnp.tanh(hpre_i) + 1.0) * hv)


def _ds(start, size):
    if isinstance(start, int):
        return slice(start, start + size)
    return pl.ds(pl.multiple_of(start, MC), size)


def _to_slab_order(u, up_s, uext_s, ch):
    pitch = _pitch(ch)
    for m in range(MC):
        for q in range(N_QL):
            up_s[q, m * pitch:m * pitch + ch, :] = u[m * ch:(m + 1) * ch, q * LANES:(q + 1) * LANES]
    for s in range(ch):
        for q in range(N_QL):
            uext_s[(s + 2) * MC:(s + 3) * MC, q * LANES:(q + 1) * LANES] = (
                up_s[q, pl.ds(s, MC, stride=pitch), :])


def _conv_halo(uext_s, lookahead, c_prev1, c_prev2, ch):
    last = uext_s[(ch + 1) * MC:(ch + 2) * MC, :]
    last2 = uext_s[ch * MC:(ch + 1) * MC, :]
    first = uext_s[2 * MC:3 * MC, :]
    uext_s[0:MC, :] = jnp.concatenate([c_prev2, last2[:MC - 1]], axis=0)
    uext_s[MC:2 * MC, :] = jnp.concatenate([c_prev1, last[:MC - 1]], axis=0)
    uext_s[(ch + 2) * MC:(ch + 3) * MC, :] = jnp.concatenate([first[1:], lookahead], axis=0)


def _conv_slab(uext_s, s, cw_ref, cb_ref):
    acc = cb_ref[...]
    for k in range(4):
        acc = acc + cw_ref[k] * uext_s[_ds((s + k) * MC, MC), :]
    return acc


def _head_cols(h, base=0):
    return slice(base + h * HEAD_DIM, base + (h + 1) * HEAD_DIM)


def _gate_terms(h, vb, hv, wg_ref, bg_ref, hrate, g_s, rows):
    cols, icols = _head_cols(h), _head_cols(h, D_LRU)
    res = jnp.dot(vb, wg_ref[h], preferred_element_type=F32)
    a, bx = _gate_ab(res[:, :HEAD_DIM] + bg_ref[:, cols], res[:, HEAD_DIM:] + bg_ref[:, icols],
                     hv, hrate[:, cols])
    g_s[rows, cols] = a
    g_s[rows, icols] = bx


def _stored_terms(g_s):
    return lambda rows: (g_s[rows, :D_LRU], g_s[rows, D_LRU:])


def _scan(terms, a_s, h_s, steps):
    acc_a = jnp.ones((MC, D_LRU), F32)
    acc_h = jnp.zeros((MC, D_LRU), F32)
    for s in steps:
        rows = slice(s * MC, (s + 1) * MC)
        a, bx = terms(rows)
        acc_a = a * acc_a
        acc_h = a * acc_h + bx
        a_s[rows, :] = acc_a
        h_s[rows, :] = acc_h
    return acc_a, acc_h


def _chunk_carries(tot_a, tot_h, carry_in, reverse):
    row_id = lax.broadcasted_iota(jnp.int32, tot_a.shape, 0)
    hin = jnp.zeros(tot_a.shape, F32)
    h = carry_in
    order = range(MC - 1, -1, -1) if reverse else range(MC)
    for m in order:
        hin = jnp.where(row_id == m, h, hin)
        h = tot_a[m:m + 1] * h + tot_h[m:m + 1]
    return hin, h


def _prep_kernel(c_ref, cctx_ref, wada_ref, bada_ref, win_ref, wout_ref, wfour_ref, wrg_ref, wig_ref,
                 brg_ref, big_ref, cw_ref, cb_ref,
                 mod_ref, win_o, wout_o, wfour_o, wg_o, bg_o, cwr_o, cbr_o):
    i = pl.program_id(0)
    bsz = c_ref.shape[0]
    cc = jnp.concatenate([c_ref[...], cctx_ref[...], jnp.zeros((8 - bsz - 1, D_MODEL), F32)], axis=0)
    s = _silu(cc).astype(BF16)
    mod_ref[...] = (jnp.dot(s, wada_ref[0].astype(BF16), preferred_element_type=F32)
                    + bada_ref[...])

    @pl.when(i < 2)
    def _():
        win_o[:, 0:D_FOURIER] = win_ref[0, :, 0:D_FOURIER].astype(BF16)
        win_o[:, D_FOURIER:] = (0.5 * win_ref[0, :, D_FOURIER:]).astype(BF16)

    @pl.when(i == 2)
    def _():
        wout_o[...] = wout_ref[0].astype(BF16)
        wfour_o[...] = wfour_ref[0].astype(BF16)
        for d in range(2):
            for h in range(N_HEADS):
                wg_o[d, h] = jnp.concatenate([wrg_ref[0, d, h], wig_ref[0, d, h]],
                                             axis=1).astype(BF16)
            bg_o[d] = 0.5 * jnp.concatenate([brg_ref[0, d:d + 1, :], big_ref[0, d:d + 1, :]], axis=1)
        for k in range(4):
            cwr_o[k] = jnp.broadcast_to(0.5 * cw_ref[0, k:k + 1, :], (MC, D_LRU))
        cbr_o[...] = jnp.broadcast_to(0.5 * cb_ref[...], (MC, D_LRU))


def _prep(c, c_ctx, w_ada, b_ada, w_in, w_out, w_four, w_rg, w_ig, b_rg, b_ig, conv_w, conv_b):
    full = lambda *shape: pl.BlockSpec(shape, lambda i: (0,) * len(shape))
    half = lambda i: jnp.minimum(i, 1)
    gshape = (1, 2, N_HEADS, HEAD_DIM, HEAD_DIM)
    return pl.pallas_call(
        _prep_kernel,
        grid=(3,),
        in_specs=[full(*c.shape), full(1, D_MODEL),
                  pl.BlockSpec((1, D_MODEL, D_MODEL), lambda i: (0, 0, i)),
                  pl.BlockSpec((1, D_MODEL), lambda i: (0, i)),
                  pl.BlockSpec((1, D_MODEL, D_MODEL), lambda i: (0, 0, half(i))),
                  full(1, D_MODEL, D_MODEL), full(1, D_FOURIER, D_FOURIER),
                  full(*gshape), full(*gshape), full(1, 2, D_LRU), full(1, 2, D_LRU),
                  full(1, 4, D_LRU), full(1, D_LRU)],
        out_specs=[pl.BlockSpec((8, D_MODEL), lambda i: (0, i)),
                   pl.BlockSpec((D_MODEL, D_MODEL), lambda i: (0, half(i))),
                   full(D_MODEL, D_MODEL), full(D_FOURIER, D_FOURIER),
                   full(2, N_HEADS, HEAD_DIM, 2 * HEAD_DIM), full(2, 1, 2 * D_LRU),
                   full(4, MC, D_LRU), full(MC, D_LRU)],
        out_shape=[jax.ShapeDtypeStruct((8, 3 * D_MODEL), F32),
                   jax.ShapeDtypeStruct((D_MODEL, 2 * D_MODEL), BF16),
                   jax.ShapeDtypeStruct((D_MODEL, D_MODEL), BF16),
                   jax.ShapeDtypeStruct((D_FOURIER, D_FOURIER), BF16),
                   jax.ShapeDtypeStruct((2, N_HEADS, HEAD_DIM, 2 * HEAD_DIM), BF16),
                   jax.ShapeDtypeStruct((2, 1, 2 * D_LRU), F32),
                   jax.ShapeDtypeStruct((4, MC, D_LRU), F32),
                   jax.ShapeDtypeStruct((MC, D_LRU), F32)],
        compiler_params=pltpu.CompilerParams(dimension_semantics=("arbitrary",),
                                             vmem_limit_bytes=VMEM_LIMIT_BYTES),
        name="prep",
    )(c, c_ctx, w_ada, b_ada, w_in, w_out, w_four, w_rg, w_ig, b_rg, b_ig, conv_w, conv_b)


def _ctx_kernel(x_ref, mod_ref, g_ref, wlx_ref, cw_ref, cb_ref, wgf_ref, bgf_ref, wgb_ref, bgb_ref,
                lam_ref, fin_ref, up_s, uext_s, vf_s, g_s, a_s, h_s, *, mod_row):
    ch = CH_CTX
    shift = mod_ref[mod_row:mod_row + 1, 0:D_MODEL]
    scale = mod_ref[mod_row:mod_row + 1, D_MODEL:2 * D_MODEL]
    gmul = g_ref[...] * (1.0 + scale)
    lhs = _modulate(x_ref[0], gmul, shift).astype(BF16)
    u = jnp.dot(lhs, wlx_ref[...], preferred_element_type=F32)
    _to_slab_order(u, up_s, uext_s, ch)
    zero_row = jnp.zeros((1, D_LRU), F32)
    _conv_halo(uext_s, zero_row, zero_row, zero_row, ch)
    for s in range(ch):
        vf_s[s * MC:(s + 1) * MC, :] = _conv_slab(uext_s, s, cw_ref, cb_ref)
    vb = vf_s[...].astype(BF16)
    for d, (wg_ref, bg_ref) in enumerate(((wgf_ref, bgf_ref), (wgb_ref, bgb_ref))):
        rate = _half_decay_rate(lam_ref[d:d + 1, :])
        steps = range(ch - 1, -1, -1) if d == 1 else range(ch)
        for h in range(N_HEADS):
            cols = _head_cols(h)
            _gate_terms(h, vb[:, cols], vf_s[:, cols], wg_ref, bg_ref, rate, g_s,
                        slice(0, ch * MC))
        tot_a, tot_h = _scan(_stored_terms(g_s), a_s, h_s, steps)
        _, fin = _chunk_carries(tot_a, tot_h, zero_row, reverse=(d == 1))
        fin_ref[0, d:d + 1, :] = fin


def _gate_specs(direction):
    return (pl.BlockSpec((None, N_HEADS, HEAD_DIM, 2 * HEAD_DIM), lambda *_: (direction, 0, 0, 0)),
            pl.BlockSpec((None, 1, 2 * D_LRU), lambda *_: (direction, 0, 0)))


def _ctx_states(ctx, mod, g, win, cw, cb, wg, bg, lam):
    bsz = ctx.shape[0]
    n = CTX_LEN
    full = lambda *shape: pl.BlockSpec(shape, lambda b: (0,) * len(shape))
    return pl.pallas_call(
        functools.partial(_ctx_kernel, mod_row=bsz),
        grid=(bsz,),
        in_specs=[pl.BlockSpec((1, n, D_MODEL), lambda b: (b, 0, 0)),
                  full(8, 3 * D_MODEL),
                  full(1, D_MODEL),
                  pl.BlockSpec((D_MODEL, D_LRU), lambda b: (0, 2)),
                  full(4, MC, D_LRU), full(MC, D_LRU),
                  *_gate_specs(0), *_gate_specs(1),
                  full(2, D_LRU)],
        out_specs=pl.BlockSpec((1, 2, D_LRU), lambda b: (b, 0, 0)),
        out_shape=jax.ShapeDtypeStruct((bsz, 2, D_LRU), F32),
        scratch_shapes=[pltpu.VMEM((N_QL, MC * _pitch(CH_CTX), LANES), F32),
                        pltpu.VMEM((n + 3 * MC, D_LRU), F32),
                        pltpu.VMEM((n, D_LRU), F32),
                        pltpu.VMEM((n, 2 * D_LRU), F32),
                        pltpu.VMEM((n, D_LRU), F32),
                        pltpu.VMEM((n, D_LRU), F32)],
        compiler_params=pltpu.CompilerParams(dimension_semantics=("arbitrary",)),
        name="ctx_states",
    )(ctx, mod, g, win, cw, cb, wg, bg, wg, bg, lam)


def _fwd_kernel(x_ref, xh_ref, mod_ref, g_ref, win_ref, cw_ref, cb_ref, wg_ref,
                bg_ref, lam_ref, h0_ref,
                uf_ref, gfs_ref, gls_ref, v_ref, hf_ref,
                lhs_s, up_s, uext_s, vf_s, g_s, a_s, h_s, cc_s, hc_s):
    j = pl.program_id(1)

    @pl.when(j == 0)
    def _():
        cc_s[...] = jnp.zeros_like(cc_s)
        hc_s[...] = jnp.broadcast_to(h0_ref[0, 0:1, :], hc_s.shape)

    brow = pl.ds(pl.program_id(0), 1)
    shift = mod_ref[brow, 0:D_MODEL]
    scale = mod_ref[brow, D_MODEL:2 * D_MODEL]
    gmul = g_ref[...] * (1.0 + scale)
    for m in range(MC):
        rows = slice(m * CH, (m + 1) * CH)
        lhs_s[rows, :] = _modulate(x_ref[0, rows, :], gmul, shift).astype(BF16)
    hh = _modulate(xh_ref[0], gmul, shift)
    lhs_s[TILE:LHS_ROWS, :] = jnp.concatenate([hh, jnp.zeros_like(hh)], axis=0).astype(BF16)

    u = jnp.dot(lhs_s[...], win_ref[:, 2 * D_FOURIER:2 * D_FOURIER + D_LRU],
                preferred_element_type=F32)

    _to_slab_order(u, up_s, uext_s, CH)
    look = jnp.where(j == N_TILES - 1, 0.0, u[TILE:TILE + 1])
    _conv_halo(uext_s, look, cc_s[0:1, :], cc_s[1:2, :], CH)
    cc_s[0:1, :] = u[TILE - 1:TILE]
    cc_s[1:2, :] = u[TILE - 2:TILE - 1]

    uf_ref[0] = jnp.dot(lhs_s[0:TILE, :], win_ref[:, 0:D_FOURIER],
                        preferred_element_type=F32).astype(BF16)

    for s in range(CH):
        rows = slice(s * MC, (s + 1) * MC)
        v = _conv_slab(uext_s, s, cw_ref, cb_ref)
        vf_s[rows, :] = v
        v_ref[0, rows, :] = v.astype(BF16)

    halves = (slice(0, TILE // 2), slice(TILE // 2, TILE))
    hrate = _half_decay_rate(lam_ref[0:1, :])
    for h in range(N_HEADS):
        cols = _head_cols(h)
        _gate_terms(h, v_ref[0, :, cols], vf_s[:, cols], wg_ref, bg_ref, hrate, g_s, slice(0, TILE))
        o_ref, c0 = ((gfs_ref, D_FOURIER), (gls_ref, 2 * D_FOURIER + D_LRU))[h // 2]
        rq = halves[h % 2]
        o_ref[0, rq, :] = _silu_of_half(jnp.dot(lhs_s[rq, :], win_ref[:, c0:c0 + D_LRU],
                                                preferred_element_type=F32)).astype(BF16)

    tot_a, tot_h = _scan(_stored_terms(g_s), a_s, h_s, range(CH))
    hin, h_out = _chunk_carries(tot_a, tot_h, hc_s[0:1, :], reverse=False)
    hc_s[...] = jnp.broadcast_to(h_out, hc_s.shape)
    for s in range(CH):
        rows = slice(s * MC, (s + 1) * MC)
        hf_ref[0, rows, :] = (h_s[rows, :] + a_s[rows, :] * hin).astype(BF16)


def _fwd_pass(x, mod, g, win, cw, cb, wg, bg, lam, fin):
    bsz = x.shape[0]
    full = lambda *shape: pl.BlockSpec(shape, lambda b, j: (0,) * len(shape))
    tile = lambda c: pl.BlockSpec((1, TILE, c), lambda b, j: (b, j, 0))
    shp = lambda c: jax.ShapeDtypeStruct((bsz, SEQ, c), BF16)
    rows8 = TILE // 8
    return pl.pallas_call(
        _fwd_kernel,
        grid=(bsz, N_TILES),
        in_specs=[tile(D_MODEL),
                  pl.BlockSpec((1, 8, D_MODEL),
                               lambda b, j: (b, jnp.minimum((j + 1) * rows8, SEQ // 8 - 1), 0)),
                  full(8, 3 * D_MODEL),
                  full(1, D_MODEL), full(D_MODEL, 2 * D_MODEL),
                  full(4, MC, D_LRU), full(MC, D_LRU),
                  *_gate_specs(0),
                  full(2, D_LRU),
                  pl.BlockSpec((1, 2, D_LRU), lambda b, j: (b, 0, 0))],
        out_specs=[tile(D_FOURIER), tile(D_FOURIER), tile(D_LRU), tile(D_LRU), tile(D_LRU)],
        out_shape=[shp(D_FOURIER), shp(D_FOURIER), shp(D_LRU), shp(D_LRU), shp(D_LRU)],
        scratch_shapes=[pltpu.VMEM((LHS_ROWS, D_MODEL), BF16),
                        pltpu.VMEM((N_QL, MC * _pitch(CH), LANES), F32),
                        pltpu.VMEM((TILE + 3 * MC, D_LRU), F32),
                        pltpu.VMEM((TILE, D_LRU), F32),
                        pltpu.VMEM((TILE, 2 * D_LRU), F32),
                        pltpu.VMEM((TILE, D_LRU), F32),
                        pltpu.VMEM((TILE, D_LRU), F32),
                        pltpu.VMEM((8, D_LRU), F32),
                        pltpu.VMEM((8, D_LRU), F32)],
        compiler_params=pltpu.CompilerParams(dimension_semantics=("arbitrary", "arbitrary"),
                                             vmem_limit_bytes=VMEM_LIMIT_BYTES),
        name="fwd_pass",
    )(x, x, mod, g, win, cw, cb, wg, bg, lam, fin)


def _dft_kernel(u_ref, dftc_ref, f1_ref, m2_ref, o_ref, sp_s, sq_s, yr_s, yj_s, so_s):
    p1 = DFT_P1
    p2 = DFT_P2
    dftc = dftc_ref[...].astype(BF16)
    chunks = TILE // DFT_N2
    for blk in range(SEQ // TILE):
        pq = jnp.dot(u_ref[0, blk * TILE:(blk + 1) * TILE, :], dftc, preferred_element_type=F32)
        for m in range(chunks):
            n1 = blk * chunks + m
            rows = slice(m * DFT_N2, (m + 1) * DFT_N2)
            sp_s[n1 * p1:n1 * p1 + DFT_N2, :] = pq[rows, :GROUP_DIM]
            sq_s[n1 * p1:n1 * p1 + DFT_N2, :] = pq[rows, GROUP_DIM:]
    f1 = f1_ref[...].astype(BF16)
    for n2 in range(0, DFT_N2, DFT_NB):
        rhs = jnp.concatenate(
            [jnp.concatenate([sp_s[pl.ds(n2 + i, DFT_N1, stride=p1), :],
                              sq_s[pl.ds(n2 + i, DFT_N1, stride=p1), :]], axis=0)
             for i in range(DFT_NB)], axis=1).astype(BF16)
        y = jnp.dot(f1, rhs, preferred_element_type=F32)
        for i in range(DFT_NB):
            lanes = slice(i * LANES, (i + 1) * LANES)
            yr_s[(n2 + i) * p2:(n2 + i) * p2 + DFT_N1, :] = y[:DFT_N1, lanes]
            yj_s[(n2 + i) * p2:(n2 + i) * p2 + DFT_N1, :] = y[DFT_N1:, lanes]
    for k1 in range(DFT_N1):
        rhs = jnp.concatenate([yr_s[pl.ds(k1, DFT_N2, stride=p2), :],
                               yj_s[pl.ds(k1, DFT_N2, stride=p2), :]], axis=0).astype(BF16)
        so_s[pl.ds(k1, DFT_N2, stride=p2), :] = jnp.dot(m2_ref[k1].astype(BF16), rhs,
                                                         preferred_element_type=F32)
    for k2 in range(DFT_N2):
        o_ref[0, k2 * DFT_N1:(k2 + 1) * DFT_N1, :] = so_s[k2 * p2:k2 * p2 + DFT_N1, :].astype(BF16)


def _fourier_dft(uf, dftc, f1, m2):
    bsz = uf.shape[0]
    blk = pl.BlockSpec((1, SEQ, GROUP_DIM), lambda b, i: (b, 0, i))
    pad1 = DFT_N1 * DFT_P1
    pad2 = DFT_N2 * DFT_P2
    return pl.pallas_call(
        _dft_kernel,
        grid=(bsz, N_GROUPS),
        in_specs=[blk,
                  pl.BlockSpec((GROUP_DIM, 2 * GROUP_DIM), lambda b, i: (0, 0)),
                  pl.BlockSpec((2 * DFT_N1, 2 * DFT_N1), lambda b, i: (0, 0)),
                  pl.BlockSpec((DFT_N1, DFT_N2, 2 * DFT_N2), lambda b, i: (0, 0, 0))],
        out_specs=blk,
        out_shape=jax.ShapeDtypeStruct(uf.shape, BF16),
        scratch_shapes=[pltpu.VMEM((pad1, LANES), F32), pltpu.VMEM((pad1, LANES), F32),
                        pltpu.VMEM((pad2, LANES), F32), pltpu.VMEM((pad2, LANES), F32),
                        pltpu.VMEM((pad2, LANES), F32)],
        compiler_params=pltpu.CompilerParams(dimension_semantics=("arbitrary", "arbitrary"),
                                             vmem_limit_bytes=VMEM_LIMIT_BYTES),
        name="fourier_dft",
    )(uf, dftc, f1, m2)


def _bwd_kernel(x_ref, yf_ref, gfs_ref, gls_ref, v_ref, hf_ref, mod_ref, fg_ref, wfour_ref, wout_ref,
                wg_ref, bg_ref, lam_ref, h0_ref, o_ref,
                g_s, a_s, h_s, yp_s, lhs_s, hc_s):
    j = pl.program_id(1)

    @pl.when(j == 0)
    def _():
        hc_s[...] = jnp.broadcast_to(h0_ref[0, 1:2, :], hc_s.shape)

    for h in range(N_HEADS):
        res = jnp.dot(v_ref[0, :, _head_cols(h)], wg_ref[h], preferred_element_type=F32)
        g_s[:, _head_cols(h)] = res[:, :HEAD_DIM]
        g_s[:, _head_cols(h, D_LRU)] = res[:, HEAD_DIM:]
    yfw = jnp.dot(yf_ref[0], wfour_ref[...], preferred_element_type=F32)
    lhs_s[:, 0:D_FOURIER] = (yfw * gfs_ref[0].astype(F32)).astype(BF16)

    hrate = _half_decay_rate(lam_ref[1:2, :])
    b_r = bg_ref[:, :D_LRU]
    b_i = bg_ref[:, D_LRU:]
    terms = lambda rows: _gate_ab(g_s[rows, :D_LRU] + b_r, g_s[rows, D_LRU:] + b_i,
                                  v_ref[0, rows, :].astype(F32), hrate)
    tot_a, tot_h = _scan(terms, a_s, h_s, range(CH - 1, -1, -1))
    hin, h_out = _chunk_carries(tot_a, tot_h, hc_s[0:1, :], reverse=True)
    hc_s[...] = jnp.broadcast_to(h_out, hc_s.shape)
    pitch = _pitch(CH)
    for s in range(CH):
        rows = slice(s * MC, (s + 1) * MC)
        yl = h_s[rows, :] + a_s[rows, :] * hin + hf_ref[0, rows, :].astype(F32)
        for q in range(N_QL):
            yp_s[q, pl.ds(s, MC, stride=pitch), :] = yl[:, q * LANES:(q + 1) * LANES]
    for m in range(MC):
        rows = slice(m * CH, (m + 1) * CH)
        for q in range(N_QL):
            cols = slice(q * LANES, (q + 1) * LANES)
            yl = yp_s[q, m * pitch:m * pitch + CH, :]
            lhs_s[rows, D_FOURIER + q * LANES:D_FOURIER + (q + 1) * LANES] = (
                yl * gls_ref[0, rows, cols].astype(F32)).astype(BF16)

    res_gate = mod_ref[pl.ds(pl.program_id(0), 1), 2 * D_MODEL:3 * D_MODEL]
    fg = fg_ref[...]
    proj = jnp.dot(lhs_s[...], wout_ref[...], preferred_element_type=F32)
    for m in range(MC):
        rows = slice(m * CH, (m + 1) * CH)
        res = x_ref[0, rows, :] + res_gate * proj[rows, :]
        ms = jnp.mean(res * res, axis=-1, keepdims=True)
        o_ref[0, rows, :] = (res * lax.rsqrt(ms + EPS)) * fg


def _bwd_pass(x, yf, gfs, gls, v, hf, mod, fg, wfour, wout, wg, bg, lam, fin):
    bsz = x.shape[0]
    full = lambda *shape: pl.BlockSpec(shape, lambda b, j: (0,) * len(shape))
    tile = lambda c: pl.BlockSpec((1, TILE, c), lambda b, j: (b, N_TILES - 1 - j, 0))
    return pl.pallas_call(
        _bwd_kernel,
        grid=(bsz, N_TILES),
        in_specs=[tile(D_MODEL), tile(D_FOURIER), tile(D_FOURIER), tile(D_LRU), tile(D_LRU),
                  tile(D_LRU),
                  full(8, 3 * D_MODEL),
                  full(1, D_MODEL), full(D_FOURIER, D_FOURIER), full(D_MODEL, D_MODEL),
                  *_gate_specs(1), full(2, D_LRU),
                  pl.BlockSpec((1, 2, D_LRU), lambda b, j: (b, 0, 0))],
        out_specs=tile(D_MODEL),
        out_shape=jax.ShapeDtypeStruct(x.shape, F32),
        scratch_shapes=[pltpu.VMEM((TILE, 2 * D_LRU), F32),
                        pltpu.VMEM((TILE, D_LRU), F32),
                        pltpu.VMEM((TILE, D_LRU), F32),
                        pltpu.VMEM((N_QL, MC * _pitch(CH), LANES), F32),
                        pltpu.VMEM((TILE, D_MODEL), BF16),
                        pltpu.VMEM((8, D_LRU), F32)],
        compiler_params=pltpu.CompilerParams(dimension_semantics=("arbitrary", "arbitrary"),
                                             vmem_limit_bytes=VMEM_LIMIT_BYTES),
        name="bwd_pass",
    )(x, yf, gfs, gls, v, hf, mod, fg, wfour, wout, wg, bg, lam, fin)


def kernel(x, c, ctx, c_ctx, w_ada, b_ada, norm_gain, w_in, w_four, conv_w, conv_b, w_rg, b_rg,
           w_ig, b_ig, lam, w_out, final_gain):
    bsz = x.shape[0]
    assert x.shape == (bsz, SEQ, D_MODEL) and ctx.shape == (bsz, CTX_LEN, D_MODEL)
    assert w_ada.shape[0] == 1, "single-layer kernel"
    dftc, f1, m2 = _dft_constants()

    mod, win, wout, wfour, wg, bg, cw, cb = _prep(c, c_ctx[None, :], w_ada, b_ada, w_in, w_out,
                                                  w_four, w_rg, w_ig, b_rg, b_ig, conv_w, conv_b)
    g = norm_gain[0][None, :]
    lam0 = lam[0]

    fin = _ctx_states(ctx, mod, g, win, cw, cb, wg, bg, lam0)
    uf, gfs, gls, v, hf = _fwd_pass(x, mod, g, win, cw, cb, wg, bg, lam0, fin)

    yf = _fourier_dft(uf, dftc, f1, m2)

    return _bwd_pass(x, yf, gfs, gls, v, hf, mod, final_gain[None, :], wfour, wout, wg, bg, lam0, fin)
```

```python
import functools

import numpy as np
import jax
import jax.numpy as jnp
from jax import lax
from jax.experimental import pallas as pl
from jax.experimental.pallas import tpu as pltpu

D_MODEL = 1024
SEQ = 8192
CTX_LEN = 256
D_FOURIER = 512
D_LRU = 512
N_GROUPS = 4
GROUP_DIM = 128
N_HEADS = 4
HEAD_DIM = 128
LRU_C = 8.0
EPS = 1e-6
LOG2_E = float(np.log2(np.e))

LANES = 128
MC = 16
CH = 64
TILE = MC * CH
N_TILES = SEQ // TILE
CH_CTX = CTX_LEN // MC
N_QL = D_LRU // LANES
LHS_ROWS = TILE + MC
DFT_N1 = 128
DFT_N2 = 64
DFT_NB = 2

VMEM_LIMIT_BYTES = 56 * 1024 * 1024
F32 = jnp.float32
BF16 = jnp.bfloat16


def _pitch(n):
    return n + 4


DFT_P1 = _pitch(DFT_N2)
DFT_P2 = _pitch(DFT_N1)


def _dft_constants():
    c = np.arange(GROUP_DIM)
    ang = 2.0 * np.pi * ((c[:, None] * c[None, :]) % GROUP_DIM) / GROUP_DIM
    dftc = np.concatenate([np.cos(ang), np.sin(ang)], axis=1) * 2.0 ** -3

    n = np.arange(DFT_N1)
    ang1 = 2.0 * np.pi * ((n[:, None] * n[None, :]) % DFT_N1) / DFT_N1
    c1, s1 = np.cos(ang1), np.sin(ang1)
    f1 = np.block([[c1, -s1], [s1, c1]]) * 2.0 ** -4

    k1 = np.arange(DFT_N1)[:, None, None]
    k2 = np.arange(DFT_N2)[None, :, None]
    n2 = np.arange(DFT_N2)[None, None, :]
    ang2 = 2.0 * np.pi * ((n2 * (k1 + DFT_N1 * k2)) % SEQ) / SEQ
    m2 = np.concatenate([np.cos(ang2), -np.sin(ang2)], axis=2) * 2.0 ** -3
    return jnp.asarray(dftc, dtype=F32), jnp.asarray(f1, dtype=F32), jnp.asarray(m2, dtype=F32)


def _silu(x):
    hx = 0.5 * x
    return hx * jnp.tanh(hx) + hx


def _silu_of_half(hx):
    return hx * jnp.tanh(hx) + hx


def _half_decay_rate(lam_row):
    z = -lam_row
    return (-0.5 * LRU_C * LOG2_E) * (jnp.maximum(z, 0.0) + jnp.log1p(jnp.exp(-jnp.abs(z))))


def _modulate(xs, gmul, shift):
    ms = jnp.mean(xs * xs, axis=-1, keepdims=True)
    return (xs * lax.rsqrt(ms + EPS)) * gmul + shift


def _gate_ab(hpre_r, hpre_i, hv, hrate):
    a = jnp.exp2(jnp.tanh(hpre_r) * hrate + hrate)
    om = 1.0 - a * a
    s = om * lax.rsqrt(jnp.maximum(om, 1e-30))
    return a, s * ((jnp.tanh(hpre_i) + 1.0) * hv)


def _ds(start, size):
    if isinstance(start, int):
        return slice(start, start + size)
    return pl.ds(pl.multiple_of(start, MC), size)


def _to_slab_order(u, up_s, uext_s, ch):
    pitch = _pitch(ch)
    for m in range(MC):
        for q in range(N_QL):
            up_s[q, m * pitch:m * pitch + ch, :] = u[m * ch:(m + 1) * ch, q * LANES:(q + 1) * LANES]
    for s in range(ch):
        for q in range(N_QL):
            uext_s[(s + 2) * MC:(s + 3) * MC, q * LANES:(q + 1) * LANES] = (
                up_s[q, pl.ds(s, MC, stride=pitch), :])


def _conv_halo(uext_s, lookahead, c_prev1, c_prev2, ch):
    last = uext_s[(ch + 1) * MC:(ch + 2) * MC, :]
    last2 = uext_s[ch * MC:(ch + 1) * MC, :]
    first = uext_s[2 * MC:3 * MC, :]
    uext_s[0:MC, :] = jnp.concatenate([c_prev2, last2[:MC - 1]], axis=0)
    uext_s[MC:2 * MC, :] = jnp.concatenate([c_prev1, last[:MC - 1]], axis=0)
    uext_s[(ch + 2) * MC:(ch + 3) * MC, :] = jnp.concatenate([first[1:], lookahead], axis=0)


def _conv_slab(uext_s, s, cw_ref, cb_ref):
    acc = cb_ref[...]
    for k in range(4):
        acc = acc + cw_ref[k] * uext_s[_ds((s + k) * MC, MC), :]
    return acc


def _head_cols(h, base=0):
    return slice(base + h * HEAD_DIM, base + (h + 1) * HEAD_DIM)


def _gate_terms(h, vb, hv, wg_ref, bg_ref, hrate, g_s, rows):
    cols, icols = _head_cols(h), _head_cols(h, D_LRU)
    res = jnp.dot(vb, wg_ref[h], preferred_element_type=F32)
    a, bx = _gate_ab(res[:, :HEAD_DIM] + bg_ref[:, cols], res[:, HEAD_DIM:] + bg_ref[:, icols],
                     hv, hrate[:, cols])
    g_s[rows, cols] = a
    g_s[rows, icols] = bx


def _stored_terms(g_s):
    return lambda rows: (g_s[rows, :D_LRU], g_s[rows, D_LRU:])


def _scan(terms, a_s, h_s, steps):
    acc_a = jnp.ones((MC, D_LRU), F32)
    acc_h = jnp.zeros((MC, D_LRU), F32)
    for s in steps:
        rows = slice(s * MC, (s + 1) * MC)
        a, bx = terms(rows)
        acc_a = a * acc_a
        acc_h = a * acc_h + bx
        a_s[rows, :] = acc_a
        h_s[rows, :] = acc_h
    return acc_a, acc_h


def _chunk_carries(tot_a, tot_h, carry_in, reverse):
    row_id = lax.broadcasted_iota(jnp.int32, tot_a.shape, 0)
    hin = jnp.zeros(tot_a.shape, F32)
    h = carry_in
    order = range(MC - 1, -1, -1) if reverse else range(MC)
    for m in order:
        hin = jnp.where(row_id == m, h, hin)
        h = tot_a[m:m + 1] * h + tot_h[m:m + 1]
    return hin, h


def _prep_kernel(c_ref, cctx_ref, wada_ref, bada_ref, win_ref, wout_ref, wfour_ref, wrg_ref, wig_ref,
                 brg_ref, big_ref, cw_ref, cb_ref,
                 mod_ref, win_o, wout_o, wfour_o, wg_o, bg_o, cwr_o, cbr_o):
    i = pl.program_id(0)
    bsz = c_ref.shape[0]
    cc = jnp.concatenate([c_ref[...], cctx_ref[...], jnp.zeros((8 - bsz - 1, D_MODEL), F32)], axis=0)
    s = _silu(cc).astype(BF16)
    mod_ref[...] = (jnp.dot(s, wada_ref[0].astype(BF16), preferred_element_type=F32)
                    + bada_ref[...])

    @pl.when(i < 2)
    def _():
        win_o[:, 0:D_FOURIER] = win_ref[0, :, 0:D_FOURIER].astype(BF16)
        win_o[:, D_FOURIER:] = (0.5 * win_ref[0, :, D_FOURIER:]).astype(BF16)

    @pl.when(i == 2)
    def _():
        wout_o[...] = wout_ref[0].astype(BF16)
        wfour_o[...] = wfour_ref[0].astype(BF16)
        for d in range(2):
            for h in range(N_HEADS):
                wg_o[d, h] = jnp.concatenate([wrg_ref[0, d, h], wig_ref[0, d, h]],
                                             axis=1).astype(BF16)
            bg_o[d] = 0.5 * jnp.concatenate([brg_ref[0, d:d + 1, :], big_ref[0, d:d + 1, :]], axis=1)
        for k in range(4):
            cwr_o[k] = jnp.broadcast_to(0.5 * cw_ref[0, k:k + 1, :], (MC, D_LRU))
        cbr_o[...] = jnp.broadcast_to(0.5 * cb_ref[...], (MC, D_LRU))


def _prep(c, c_ctx, w_ada, b_ada, w_in, w_out, w_four, w_rg, w_ig, b_rg, b_ig, conv_w, conv_b):
    full = lambda *shape: pl.BlockSpec(shape, lambda i: (0,) * len(shape))
    half = lambda i: jnp.minimum(i, 1)
    gshape = (1, 2, N_HEADS, HEAD_DIM, HEAD_DIM)
    return pl.pallas_call(
        _prep_kernel,
        grid=(3,),
        in_specs=[full(*c.shape), full(1, D_MODEL),
                  pl.BlockSpec((1, D_MODEL, D_MODEL), lambda i: (0, 0, i)),
                  pl.BlockSpec((1, D_MODEL), lambda i: (0, i)),
                  pl.BlockSpec((1, D_MODEL, D_MODEL), lambda i: (0, 0, half(i))),
                  full(1, D_MODEL, D_MODEL), full(1, D_FOURIER, D_FOURIER),
                  full(*gshape), full(*gshape), full(1, 2, D_LRU), full(1, 2, D_LRU),
                  full(1, 4, D_LRU), full(1, D_LRU)],
        out_specs=[pl.BlockSpec((8, D_MODEL), lambda i: (0, i)),
                   pl.BlockSpec((D_MODEL, D_MODEL), lambda i: (0, half(i))),
                   full(D_MODEL, D_MODEL), full(D_FOURIER, D_FOURIER),
                   full(2, N_HEADS, HEAD_DIM, 2 * HEAD_DIM), full(2, 1, 2 * D_LRU),
                   full(4, MC, D_LRU), full(MC, D_LRU)],
        out_shape=[jax.ShapeDtypeStruct((8, 3 * D_MODEL), F32),
                   jax.ShapeDtypeStruct((D_MODEL, 2 * D_MODEL), BF16),
                   jax.ShapeDtypeStruct((D_MODEL, D_MODEL), BF16),
                   jax.ShapeDtypeStruct((D_FOURIER, D_FOURIER), BF16),
                   jax.ShapeDtypeStruct((2, N_HEADS, HEAD_DIM, 2 * HEAD_DIM), BF16),
                   jax.ShapeDtypeStruct((2, 1, 2 * D_LRU), F32),
                   jax.ShapeDtypeStruct((4, MC, D_LRU), F32),
                   jax.ShapeDtypeStruct((MC, D_LRU), F32)],
        compiler_params=pltpu.CompilerParams(dimension_semantics=("arbitrary",),
                                             vmem_limit_bytes=VMEM_LIMIT_BYTES),
        name="prep",
    )(c, c_ctx, w_ada, b_ada, w_in, w_out, w_four, w_rg, w_ig, b_rg, b_ig, conv_w, conv_b)


def _ctx_kernel(x_ref, mod_ref, g_ref, wlx_ref, cw_ref, cb_ref, wgf_ref, bgf_ref, wgb_ref, bgb_ref,
                lam_ref, fin_ref, up_s, uext_s, vf_s, g_s, a_s, h_s, *, mod_row):
    ch = CH_CTX
    shift = mod_ref[mod_row:mod_row + 1, 0:D_MODEL]
    scale = mod_ref[mod_row:mod_row + 1, D_MODEL:2 * D_MODEL]
    gmul = g_ref[...] * (1.0 + scale)
    lhs = _modulate(x_ref[0], gmul, shift).astype(BF16)
    u = jnp.dot(lhs, wlx_ref[...], preferred_element_type=F32)
    _to_slab_order(u, up_s, uext_s, ch)
    zero_row = jnp.zeros((1, D_LRU), F32)
    _conv_halo(uext_s, zero_row, zero_row, zero_row, ch)
    for s in range(ch):
        vf_s[s * MC:(s + 1) * MC, :] = _conv_slab(uext_s, s, cw_ref, cb_ref)
    vb = vf_s[...].astype(BF16)
    for d, (wg_ref, bg_ref) in enumerate(((wgf_ref, bgf_ref), (wgb_ref, bgb_ref))):
        rate = _half_decay_rate(lam_ref[d:d + 1, :])
        steps = range(ch - 1, -1, -1) if d == 1 else range(ch)
        for h in range(N_HEADS):
            cols = _head_cols(h)
            _gate_terms(h, vb[:, cols], vf_s[:, cols], wg_ref, bg_ref, rate, g_s,
                        slice(0, ch * MC))
        tot_a, tot_h = _scan(_stored_terms(g_s), a_s, h_s, steps)
        _, fin = _chunk_carries(tot_a, tot_h, zero_row, reverse=(d == 1))
        fin_ref[0, d:d + 1, :] = fin


def _gate_specs(direction):
    return (pl.BlockSpec((None, N_HEADS, HEAD_DIM, 2 * HEAD_DIM), lambda *_: (direction, 0, 0, 0)),
            pl.BlockSpec((None, 1, 2 * D_LRU), lambda *_: (direction, 0, 0)))


def _ctx_states(ctx, mod, g, win, cw, cb, wg, bg, lam):
    bsz = ctx.shape[0]
    n = CTX_LEN
    full = lambda *shape: pl.BlockSpec(shape, lambda b: (0,) * len(shape))
    return pl.pallas_call(
        functools.partial(_ctx_kernel, mod_row=bsz),
        grid=(bsz,),
        in_specs=[pl.BlockSpec((1, n, D_MODEL), lambda b: (b, 0, 0)),
                  full(8, 3 * D_MODEL),
                  full(1, D_MODEL),
                  pl.BlockSpec((D_MODEL, D_LRU), lambda b: (0, 2)),
                  full(4, MC, D_LRU), full(MC, D_LRU),
                  *_gate_specs(0), *_gate_specs(1),
                  full(2, D_LRU)],
        out_specs=pl.BlockSpec((1, 2, D_LRU), lambda b: (b, 0, 0)),
        out_shape=jax.ShapeDtypeStruct((bsz, 2, D_LRU), F32),
        scratch_shapes=[pltpu.VMEM((N_QL, MC * _pitch(CH_CTX), LANES), F32),
                        pltpu.VMEM((n + 3 * MC, D_LRU), F32),
                        pltpu.VMEM((n, D_LRU), F32),
                        pltpu.VMEM((n, 2 * D_LRU), F32),
                        pltpu.VMEM((n, D_LRU), F32),
                        pltpu.VMEM((n, D_LRU), F32)],
        compiler_params=pltpu.CompilerParams(dimension_semantics=("arbitrary",)),
        name="ctx_states",
    )(ctx, mod, g, win, cw, cb, wg, bg, wg, bg, lam)


def _fwd_kernel(x_ref, xh_ref, mod_ref, g_ref, win_ref, cw_ref, cb_ref, wg_ref,
                bg_ref, lam_ref, h0_ref,
                uf_ref, gfs_ref, gls_ref, v_ref, hf_ref,
                lhs_s, up_s, uext_s, vf_s, g_s, a_s, h_s, cc_s, hc_s):
    j = pl.program_id(1)

    @pl.when(j == 0)
    def _():
        cc_s[...] = jnp.zeros_like(cc_s)
        hc_s[...] = jnp.broadcast_to(h0_ref[0, 0:1, :], hc_s.shape)

    brow = pl.ds(pl.program_id(0), 1)
    shift = mod_ref[brow, 0:D_MODEL]
    scale = mod_ref[brow, D_MODEL:2 * D_MODEL]
    gmul = g_ref[...] * (1.0 + scale)
    for m in range(MC):
        rows = slice(m * CH, (m + 1) * CH)
        lhs_s[rows, :] = _modulate(x_ref[0, rows, :], gmul, shift).astype(BF16)
    hh = _modulate(xh_ref[0], gmul, shift)
    lhs_s[TILE:LHS_ROWS, :] = jnp.concatenate([hh, jnp.zeros_like(hh)], axis=0).astype(BF16)

    u = jnp.dot(lhs_s[...], win_ref[:, 2 * D_FOURIER:2 * D_FOURIER + D_LRU],
                preferred_element_type=F32)

    _to_slab_order(u, up_s, uext_s, CH)
    look = jnp.where(j == N_TILES - 1, 0.0, u[TILE:TILE + 1])
    _conv_halo(uext_s, look, cc_s[0:1, :], cc_s[1:2, :], CH)
    cc_s[0:1, :] = u[TILE - 1:TILE]
    cc_s[1:2, :] = u[TILE - 2:TILE - 1]

    uf_ref[0] = jnp.dot(lhs_s[0:TILE, :], win_ref[:, 0:D_FOURIER],
                        preferred_element_type=F32).astype(BF16)

    for s in range(CH):
        rows = slice(s * MC, (s + 1) * MC)
        v = _conv_slab(uext_s, s, cw_ref, cb_ref)
        vf_s[rows, :] = v
        v_ref[0, rows, :] = v.astype(BF16)

    halves = (slice(0, TILE // 2), slice(TILE // 2, TILE))
    hrate = _half_decay_rate(lam_ref[0:1, :])
    for h in range(N_HEADS):
        cols = _head_cols(h)
        _gate_terms(h, v_ref[0, :, cols], vf_s[:, cols], wg_ref, bg_ref, hrate, g_s, slice(0, TILE))
        o_ref, c0 = ((gfs_ref, D_FOURIER), (gls_ref, 2 * D_FOURIER + D_LRU))[h // 2]
        rq = halves[h % 2]
        o_ref[0, rq, :] = _silu_of_half(jnp.dot(lhs_s[rq, :], win_ref[:, c0:c0 + D_LRU],
                                                preferred_element_type=F32)).astype(BF16)

    tot_a, tot_h = _scan(_stored_terms(g_s), a_s, h_s, range(CH))
    hin, h_out = _chunk_carries(tot_a, tot_h, hc_s[0:1, :], reverse=False)
    hc_s[...] = jnp.broadcast_to(h_out, hc_s.shape)
    for s in range(CH):
        rows = slice(s * MC, (s + 1) * MC)
        hf_ref[0, rows, :] = (h_s[rows, :] + a_s[rows, :] * hin).astype(BF16)


def _fwd_pass(x, mod, g, win, cw, cb, wg, bg, lam, fin):
    bsz = x.shape[0]
    full = lambda *shape: pl.BlockSpec(shape, lambda b, j: (0,) * len(shape))
    tile = lambda c: pl.BlockSpec((1, TILE, c), lambda b, j: (b, j, 0))
    shp = lambda c: jax.ShapeDtypeStruct((bsz, SEQ, c), BF16)
    rows8 = TILE // 8
    return pl.pallas_call(
        _fwd_kernel,
        grid=(bsz, N_TILES),
        in_specs=[tile(D_MODEL),
                  pl.BlockSpec((1, 8, D_MODEL),
                               lambda b, j: (b, jnp.minimum((j + 1) * rows8, SEQ // 8 - 1), 0)),
                  full(8, 3 * D_MODEL),
                  full(1, D_MODEL), full(D_MODEL, 2 * D_MODEL),
                  full(4, MC, D_LRU), full(MC, D_LRU),
                  *_gate_specs(0),
                  full(2, D_LRU),
                  pl.BlockSpec((1, 2, D_LRU), lambda b, j: (b, 0, 0))],
        out_specs=[tile(D_FOURIER), tile(D_FOURIER), tile(D_LRU), tile(D_LRU), tile(D_LRU)],
        out_shape=[shp(D_FOURIER), shp(D_FOURIER), shp(D_LRU), shp(D_LRU), shp(D_LRU)],
        scratch_shapes=[pltpu.VMEM((LHS_ROWS, D_MODEL), BF16),
                        pltpu.VMEM((N_QL, MC * _pitch(CH), LANES), F32),
                        pltpu.VMEM((TILE + 3 * MC, D_LRU), F32),
                        pltpu.VMEM((TILE, D_LRU), F32),
                        pltpu.VMEM((TILE, 2 * D_LRU), F32),
                        pltpu.VMEM((TILE, D_LRU), F32),
                        pltpu.VMEM((TILE, D_LRU), F32),
                        pltpu.VMEM((8, D_LRU), F32),
                        pltpu.VMEM((8, D_LRU), F32)],
        compiler_params=pltpu.CompilerParams(dimension_semantics=("arbitrary", "arbitrary"),
                                             vmem_limit_bytes=VMEM_LIMIT_BYTES),
        name="fwd_pass",
    )(x, x, mod, g, win, cw, cb, wg, bg, lam, fin)


def _dft_kernel(u_ref, dftc_ref, f1_ref, m2_ref, o_ref, sp_s, sq_s, yr_s, yj_s, so_s):
    p1 = DFT_P1
    p2 = DFT_P2
    dftc = dftc_ref[...].astype(BF16)
    chunks = TILE // DFT_N2
    for blk in range(SEQ // TILE):
        pq = jnp.dot(u_ref[0, blk * TILE:(blk + 1) * TILE, :], dftc, preferred_element_type=F32)
        for m in range(chunks):
            n1 = blk * chunks + m
            rows = slice(m * DFT_N2, (m + 1) * DFT_N2)
            sp_s[n1 * p1:n1 * p1 + DFT_N2, :] = pq[rows, :GROUP_DIM]
            sq_s[n1 * p1:n1 * p1 + DFT_N2, :] = pq[rows, GROUP_DIM:]
    f1 = f1_ref[...].astype(BF16)
    for n2 in range(0, DFT_N2, DFT_NB):
        rhs = jnp.concatenate(
            [jnp.concatenate([sp_s[pl.ds(n2 + i, DFT_N1, stride=p1), :],
                              sq_s[pl.ds(n2 + i, DFT_N1, stride=p1), :]], axis=0)
             for i in range(DFT_NB)], axis=1).astype(BF16)
        y = jnp.dot(f1, rhs, preferred_element_type=F32)
        for i in range(DFT_NB):
            lanes = slice(i * LANES, (i + 1) * LANES)
            yr_s[(n2 + i) * p2:(n2 + i) * p2 + DFT_N1, :] = y[:DFT_N1, lanes]
            yj_s[(n2 + i) * p2:(n2 + i) * p2 + DFT_N1, :] = y[DFT_N1:, lanes]
    for k1 in range(DFT_N1):
        rhs = jnp.concatenate([yr_s[pl.ds(k1, DFT_N2, stride=p2), :],
                               yj_s[pl.ds(k1, DFT_N2, stride=p2), :]], axis=0).astype(BF16)
        so_s[pl.ds(k1, DFT_N2, stride=p2), :] = jnp.dot(m2_ref[k1].astype(BF16), rhs,
                                                         preferred_element_type=F32)
    for k2 in range(DFT_N2):
        o_ref[0, k2 * DFT_N1:(k2 + 1) * DFT_N1, :] = so_s[k2 * p2:k2 * p2 + DFT_N1, :].astype(BF16)


def _fourier_dft(uf, dftc, f1, m2):
    bsz = uf.shape[0]
    blk = pl.BlockSpec((1, SEQ, GROUP_DIM), lambda b, i: (b, 0, i))
    pad1 = DFT_N1 * DFT_P1
    pad2 = DFT_N2 * DFT_P2
    return pl.pallas_call(
        _dft_kernel,
        grid=(bsz, N_GROUPS),
        in_specs=[blk,
                  pl.BlockSpec((GROUP_DIM, 2 * GROUP_DIM), lambda b, i: (0, 0)),
                  pl.BlockSpec((2 * DFT_N1, 2 * DFT_N1), lambda b, i: (0, 0)),
                  pl.BlockSpec((DFT_N1, DFT_N2, 2 * DFT_N2), lambda b, i: (0, 0, 0))],
        out_specs=blk,
        out_shape=jax.ShapeDtypeStruct(uf.shape, BF16),
        scratch_shapes=[pltpu.VMEM((pad1, LANES), F32), pltpu.VMEM((pad1, LANES), F32),
                        pltpu.VMEM((pad2, LANES), F32), pltpu.VMEM((pad2, LANES), F32),
                        pltpu.VMEM((pad2, LANES), F32)],
        compiler_params=pltpu.CompilerParams(dimension_semantics=("arbitrary", "arbitrary"),
                                             vmem_limit_bytes=VMEM_LIMIT_BYTES),
        name="fourier_dft",
    )(uf, dftc, f1, m2)


def _bwd_kernel(x_ref, yf_ref, gfs_ref, gls_ref, v_ref, hf_ref, mod_ref, fg_ref, wfour_ref, wout_ref,
                wg_ref, bg_ref, lam_ref, h0_ref, o_ref,
                g_s, a_s, h_s, yp_s, lhs_s, hc_s):
    j = pl.program_id(1)

    @pl.when(j == 0)
    def _():
        hc_s[...] = jnp.broadcast_to(h0_ref[0, 1:2, :], hc_s.shape)

    for h in range(N_HEADS):
        res = jnp.dot(v_ref[0, :, _head_cols(h)], wg_ref[h], preferred_element_type=F32)
        g_s[:, _head_cols(h)] = res[:, :HEAD_DIM]
        g_s[:, _head_cols(h, D_LRU)] = res[:, HEAD_DIM:]
        if h < 2:
            rq = slice(h * (TILE // 2), (h + 1) * (TILE // 2))
            yfw = jnp.dot(yf_ref[0, rq, :], wfour_ref[...], preferred_element_type=F32)
            lhs_s[rq, 0:D_FOURIER] = (yfw * gfs_ref[0, rq, :].astype(F32)).astype(BF16)

    hrate = _half_decay_rate(lam_ref[1:2, :])
    b_r = bg_ref[:, :D_LRU]
    b_i = bg_ref[:, D_LRU:]
    terms = lambda rows: _gate_ab(g_s[rows, :D_LRU] + b_r, g_s[rows, D_LRU:] + b_i,
                                  v_ref[0, rows, :].astype(F32), hrate)
    tot_a, tot_h = _scan(terms, a_s, h_s, range(CH - 1, -1, -1))
    hin, h_out = _chunk_carries(tot_a, tot_h, hc_s[0:1, :], reverse=True)
    hc_s[...] = jnp.broadcast_to(h_out, hc_s.shape)
    pitch = _pitch(CH)
    for s in range(CH):
        rows = slice(s * MC, (s + 1) * MC)
        yl = h_s[rows, :] + a_s[rows, :] * hin + hf_ref[0, rows, :].astype(F32)
        for q in range(N_QL):
            yp_s[q, pl.ds(s, MC, stride=pitch), :] = yl[:, q * LANES:(q + 1) * LANES]
    for m in range(MC):
        rows = slice(m * CH, (m + 1) * CH)
        for q in range(N_QL):
            cols = slice(q * LANES, (q + 1) * LANES)
            yl = yp_s[q, m * pitch:m * pitch + CH, :]
            lhs_s[rows, D_FOURIER + q * LANES:D_FOURIER + (q + 1) * LANES] = (
                yl * gls_ref[0, rows, cols].astype(F32)).astype(BF16)

    res_gate = mod_ref[pl.ds(pl.program_id(0), 1), 2 * D_MODEL:3 * D_MODEL]
    fg = fg_ref[...]
    proj = jnp.dot(lhs_s[...], wout_ref[...], preferred_element_type=F32)
    for m in range(MC):
        rows = slice(m * CH, (m + 1) * CH)
        res = x_ref[0, rows, :] + res_gate * proj[rows, :]
        ms = jnp.mean(res * res, axis=-1, keepdims=True)
        o_ref[0, rows, :] = (res * lax.rsqrt(ms + EPS)) * fg


def _bwd_pass(x, yf, gfs, gls, v, hf, mod, fg, wfour, wout, wg, bg, lam, fin):
    bsz = x.shape[0]
    full = lambda *shape: pl.BlockSpec(shape, lambda b, j: (0,) * len(shape))
    tile = lambda c: pl.BlockSpec((1, TILE, c), lambda b, j: (b, N_TILES - 1 - j, 0))
    return pl.pallas_call(
        _bwd_kernel,
        grid=(bsz, N_TILES),
        in_specs=[tile(D_MODEL), tile(D_FOURIER), tile(D_FOURIER), tile(D_LRU), tile(D_LRU),
                  tile(D_LRU),
                  full(8, 3 * D_MODEL),
                  full(1, D_MODEL), full(D_FOURIER, D_FOURIER), full(D_MODEL, D_MODEL),
                  *_gate_specs(1), full(2, D_LRU),
                  pl.BlockSpec((1, 2, D_LRU), lambda b, j: (b, 0, 0))],
        out_specs=tile(D_MODEL),
        out_shape=jax.ShapeDtypeStruct(x.shape, F32),
        scratch_shapes=[pltpu.VMEM((TILE, 2 * D_LRU), F32),
                        pltpu.VMEM((TILE, D_LRU), F32),
                        pltpu.VMEM((TILE, D_LRU), F32),
                        pltpu.VMEM((N_QL, MC * _pitch(CH), LANES), F32),
                        pltpu.VMEM((TILE, D_MODEL), BF16),
                        pltpu.VMEM((8, D_LRU), F32)],
        compiler_params=pltpu.CompilerParams(dimension_semantics=("arbitrary", "arbitrary"),
                                             vmem_limit_bytes=VMEM_LIMIT_BYTES),
        name="bwd_pass",
    )(x, yf, gfs, gls, v, hf, mod, fg, wfour, wout, wg, bg, lam, fin)


def kernel(x, c, ctx, c_ctx, w_ada, b_ada, norm_gain, w_in, w_four, conv_w, conv_b, w_rg, b_rg,
           w_ig, b_ig, lam, w_out, final_gain):
    bsz = x.shape[0]
    assert x.shape == (bsz, SEQ, D_MODEL) and ctx.shape == (bsz, CTX_LEN, D_MODEL)
    assert w_ada.shape[0] == 1, "single-layer kernel"
    dftc, f1, m2 = _dft_constants()

    mod, win, wout, wfour, wg, bg, cw, cb = _prep(c, c_ctx[None, :], w_ada, b_ada, w_in, w_out,
                                                  w_four, w_rg, w_ig, b_rg, b_ig, conv_w, conv_b)
    g = norm_gain[0][None, :]
    lam0 = lam[0]

    fin = _ctx_states(ctx, mod, g, win, cw, cb, wg, bg, lam0)
    uf, gfs, gls, v, hf = _fwd_pass(x, mod, g, win, cw, cb, wg, bg, lam0, fin)

    yf = _fourier_dft(uf, dftc, f1, m2)

    return _bwd_pass(x, yf, gfs, gls, v, hf, mod, final_gain[None, :], wfour, wout, wg, bg, lam0, fin)
```

```python
import functools

import numpy as np
import jax
import jax.numpy as jnp
from jax import lax
from jax.experimental import pallas as pl
from jax.experimental.pallas import tpu as pltpu

D_MODEL = 1024
SEQ = 8192
CTX_LEN = 256
D_FOURIER = 512
D_LRU = 512
N_GROUPS = 4
GROUP_DIM = 128
N_HEADS = 4
HEAD_DIM = 128
LRU_C = 8.0
EPS = 1e-6
LOG2_E = float(np.log2(np.e))

LANES = 128
MC = 16
CH = 64
TILE = MC * CH
N_TILES = SEQ // TILE
CH_CTX = CTX_LEN // MC
N_QL = D_LRU // LANES
LHS_ROWS = TILE + MC
DFT_N1 = 128
DFT_N2 = 64
DFT_NB = 2

VMEM_LIMIT_BYTES = 56 * 1024 * 1024
F32 = jnp.float32
BF16 = jnp.bfloat16


def _pitch(n):
    return n + 4


DFT_P1 = _pitch(DFT_N2)
DFT_P2 = _pitch(DFT_N1)


def _dft_constants():
    c = np.arange(GROUP_DIM)
    ang = 2.0 * np.pi * ((c[:, None] * c[None, :]) % GROUP_DIM) / GROUP_DIM
    dftc = np.concatenate([np.cos(ang), np.sin(ang)], axis=1) * 2.0 ** -3

    n = np.arange(DFT_N1)
    ang1 = 2.0 * np.pi * ((n[:, None] * n[None, :]) % DFT_N1) / DFT_N1
    c1, s1 = np.cos(ang1), np.sin(ang1)
    f1 = np.block([[c1, -s1], [s1, c1]]) * 2.0 ** -4

    k1 = np.arange(DFT_N1)[:, None, None]
    k2 = np.arange(DFT_N2)[None, :, None]
    n2 = np.arange(DFT_N2)[None, None, :]
    ang2 = 2.0 * np.pi * ((n2 * (k1 + DFT_N1 * k2)) % SEQ) / SEQ
    m2 = np.concatenate([np.cos(ang2), -np.sin(ang2)], axis=2) * 2.0 ** -3
    return jnp.asarray(dftc, dtype=F32), jnp.asarray(f1, dtype=F32), jnp.asarray(m2, dtype=F32)


def _silu(x):
    hx = 0.5 * x
    return hx * jnp.tanh(hx) + hx


def _silu_of_half(hx):
    return hx * jnp.tanh(hx) + hx


def _half_decay_rate(lam_row):
    z = -lam_row
    return (-0.5 * LRU_C * LOG2_E) * (jnp.maximum(z, 0.0) + jnp.log1p(jnp.exp(-jnp.abs(z))))


def _modulate(xs, gmul, shift):
    ms = jnp.mean(xs * xs, axis=-1, keepdims=True)
    return (xs * lax.rsqrt(ms + EPS)) * gmul + shift


def _gate_ab(hpre_r, hpre_i, hv, hrate):
    a = jnp.exp2(jnp.tanh(hpre_r) * hrate + hrate)
    om = 1.0 - a * a
    s = om * lax.rsqrt(jnp.maximum(om, 1e-30))
    return a, s * ((jnp.tanh(hpre_i) + 1.0) * hv)


def _ds(start, size):
    if isinstance(start, int):
        return slice(start, start + size)
    return pl.ds(pl.multiple_of(start, MC), size)


def _to_slab_order(u, up_s, uext_s, ch):
    pitch = _pitch(ch)
    for m in range(MC):
        for q in range(N_QL):
            up_s[q, m * pitch:m * pitch + ch, :] = u[m * ch:(m + 1) * ch, q * LANES:(q + 1) * LANES]
    for s in range(ch):
        for q in range(N_QL):
            uext_s[(s + 2) * MC:(s + 3) * MC, q * LANES:(q + 1) * LANES] = (
                up_s[q, pl.ds(s, MC, stride=pitch), :])


def _conv_halo(uext_s, lookahead, c_prev1, c_prev2, ch):
    last = uext_s[(ch + 1) * MC:(ch + 2) * MC, :]
    last2 = uext_s[ch * MC:(ch + 1) * MC, :]
    first = uext_s[2 * MC:3 * MC, :]
    uext_s[0:MC, :] = jnp.concatenate([c_prev2, last2[:MC - 1]], axis=0)
    uext_s[MC:2 * MC, :] = jnp.concatenate([c_prev1, last[:MC - 1]], axis=0)
    uext_s[(ch + 2) * MC:(ch + 3) * MC, :] = jnp.concatenate([first[1:], lookahead], axis=0)


def _conv_slab(uext_s, s, cw_ref, cb_ref):
    acc = cb_ref[...]
    for k in range(4):
        acc = acc + cw_ref[k] * uext_s[_ds((s + k) * MC, MC), :]
    return acc


def _head_cols(h, base=0):
    return slice(base + h * HEAD_DIM, base + (h + 1) * HEAD_DIM)


def _gate_terms(h, vb, hv, wg_ref, bg_ref, hrate, g_s, rows):
    cols, icols = _head_cols(h), _head_cols(h, D_LRU)
    res = jnp.dot(vb, wg_ref[h], preferred_element_type=F32)
    a, bx = _gate_ab(res[:, :HEAD_DIM] + bg_ref[:, cols], res[:, HEAD_DIM:] + bg_ref[:, icols],
                     hv, hrate[:, cols])
    g_s[rows, cols] = a
    g_s[rows, icols] = bx


def _stored_terms(g_s):
    return lambda rows: (g_s[rows, :D_LRU], g_s[rows, D_LRU:])


def _scan(terms, a_s, h_s, steps):
    acc_a = jnp.ones((MC, D_LRU), F32)
    acc_h = jnp.zeros((MC, D_LRU), F32)
    for s in steps:
        rows = slice(s * MC, (s + 1) * MC)
        a, bx = terms(rows)
        acc_a = a * acc_a
        acc_h = a * acc_h + bx
        a_s[rows, :] = acc_a
        h_s[rows, :] = acc_h
    return acc_a, acc_h


def _chunk_carries(tot_a, tot_h, carry_in, reverse):
    row_id = lax.broadcasted_iota(jnp.int32, tot_a.shape, 0)
    hin = jnp.zeros(tot_a.shape, F32)
    h = carry_in
    order = range(MC - 1, -1, -1) if reverse else range(MC)
    for m in order:
        hin = jnp.where(row_id == m, h, hin)
        h = tot_a[m:m + 1] * h + tot_h[m:m + 1]
    return hin, h


def _prep_kernel(c_ref, cctx_ref, wada_ref, bada_ref, win_ref, wout_ref, wfour_ref, wrg_ref, wig_ref,
                 brg_ref, big_ref, cw_ref, cb_ref,
                 mod_ref, win_o, wout_o, wfour_o, wg_o, bg_o, cwr_o, cbr_o):
    i = pl.program_id(0)
    bsz = c_ref.shape[0]
    cc = jnp.concatenate([c_ref[...], cctx_ref[...], jnp.zeros((8 - bsz - 1, D_MODEL), F32)], axis=0)
    s = _silu(cc).astype(BF16)
    mod_ref[...] = (jnp.dot(s, wada_ref[0].astype(BF16), preferred_element_type=F32)
                    + bada_ref[...])

    @pl.when(i < 2)
    def _():
        win_o[:, 0:D_FOURIER] = win_ref[0, :, 0:D_FOURIER].astype(BF16)
        win_o[:, D_FOURIER:] = (0.5 * win_ref[0, :, D_FOURIER:]).astype(BF16)

    @pl.when(i == 2)
    def _():
        wout_o[...] = wout_ref[0].astype(BF16)
        wfour_o[...] = wfour_ref[0].astype(BF16)
        for d in range(2):
            for h in range(N_HEADS):
                wg_o[d, h] = jnp.concatenate([wrg_ref[0, d, h], wig_ref[0, d, h]],
                                             axis=1).astype(BF16)
            bg_o[d] = 0.5 * jnp.concatenate([brg_ref[0, d:d + 1, :], big_ref[0, d:d + 1, :]], axis=1)
        for k in range(4):
            cwr_o[k] = jnp.broadcast_to(0.5 * cw_ref[0, k:k + 1, :], (MC, D_LRU))
        cbr_o[...] = jnp.broadcast_to(0.5 * cb_ref[...], (MC, D_LRU))


def _prep(c, c_ctx, w_ada, b_ada, w_in, w_out, w_four, w_rg, w_ig, b_rg, b_ig, conv_w, conv_b):
    full = lambda *shape: pl.BlockSpec(shape, lambda i: (0,) * len(shape))
    half = lambda i: jnp.minimum(i, 1)
    gshape = (1, 2, N_HEADS, HEAD_DIM, HEAD_DIM)
    return pl.pallas_call(
        _prep_kernel,
        grid=(3,),
        in_specs=[full(*c.shape), full(1, D_MODEL),
                  pl.BlockSpec((1, D_MODEL, D_MODEL), lambda i: (0, 0, i)),
                  pl.BlockSpec((1, D_MODEL), lambda i: (0, i)),
                  pl.BlockSpec((1, D_MODEL, D_MODEL), lambda i: (0, 0, half(i))),
                  full(1, D_MODEL, D_MODEL), full(1, D_FOURIER, D_FOURIER),
                  full(*gshape), full(*gshape), full(1, 2, D_LRU), full(1, 2, D_LRU),
                  full(1, 4, D_LRU), full(1, D_LRU)],
        out_specs=[pl.BlockSpec((8, D_MODEL), lambda i: (0, i)),
                   pl.BlockSpec((D_MODEL, D_MODEL), lambda i: (0, half(i))),
                   full(D_MODEL, D_MODEL), full(D_FOURIER, D_FOURIER),
                   full(2, N_HEADS, HEAD_DIM, 2 * HEAD_DIM), full(2, 1, 2 * D_LRU),
                   full(4, MC, D_LRU), full(MC, D_LRU)],
        out_shape=[jax.ShapeDtypeStruct((8, 3 * D_MODEL), F32),
                   jax.ShapeDtypeStruct((D_MODEL, 2 * D_MODEL), BF16),
                   jax.ShapeDtypeStruct((D_MODEL, D_MODEL), BF16),
                   jax.ShapeDtypeStruct((D_FOURIER, D_FOURIER), BF16),
                   jax.ShapeDtypeStruct((2, N_HEADS, HEAD_DIM, 2 * HEAD_DIM), BF16),
                   jax.ShapeDtypeStruct((2, 1, 2 * D_LRU), F32),
                   jax.ShapeDtypeStruct((4, MC, D_LRU), F32),
                   jax.ShapeDtypeStruct((MC, D_LRU), F32)],
        compiler_params=pltpu.CompilerParams(dimension_semantics=("arbitrary",),
                                             vmem_limit_bytes=VMEM_LIMIT_BYTES),
        name="prep",
    )(c, c_ctx, w_ada, b_ada, w_in, w_out, w_four, w_rg, w_ig, b_rg, b_ig, conv_w, conv_b)


def _ctx_kernel(x_ref, mod_ref, g_ref, wlx_ref, cw_ref, cb_ref, wgf_ref, bgf_ref, wgb_ref, bgb_ref,
                lam_ref, fin_ref, up_s, uext_s, vf_s, g_s, a_s, h_s, *, mod_row):
    ch = CH_CTX
    shift = mod_ref[mod_row:mod_row + 1, 0:D_MODEL]
    scale = mod_ref[mod_row:mod_row + 1, D_MODEL:2 * D_MODEL]
    gmul = g_ref[...] * (1.0 + scale)
    lhs = _modulate(x_ref[0], gmul, shift).astype(BF16)
    u = jnp.dot(lhs, wlx_ref[...], preferred_element_type=F32)
    _to_slab_order(u, up_s, uext_s, ch)
    zero_row = jnp.zeros((1, D_LRU), F32)
    _conv_halo(uext_s, zero_row, zero_row, zero_row, ch)
    for s in range(ch):
        vf_s[s * MC:(s + 1) * MC, :] = _conv_slab(uext_s, s, cw_ref, cb_ref)
    vb = vf_s[...].astype(BF16)
    for d, (wg_ref, bg_ref) in enumerate(((wgf_ref, bgf_ref), (wgb_ref, bgb_ref))):
        rate = _half_decay_rate(lam_ref[d:d + 1, :])
        steps = range(ch - 1, -1, -1) if d == 1 else range(ch)
        for h in range(N_HEADS):
            cols = _head_cols(h)
            _gate_terms(h, vb[:, cols], vf_s[:, cols], wg_ref, bg_ref, rate, g_s,
                        slice(0, ch * MC))
        tot_a, tot_h = _scan(_stored_terms(g_s), a_s, h_s, steps)
        _, fin = _chunk_carries(tot_a, tot_h, zero_row, reverse=(d == 1))
        fin_ref[0, d:d + 1, :] = fin


def _gate_specs(direction):
    return (pl.BlockSpec((None, N_HEADS, HEAD_DIM, 2 * HEAD_DIM), lambda *_: (direction, 0, 0, 0)),
            pl.BlockSpec((None, 1, 2 * D_LRU), lambda *_: (direction, 0, 0)))


def _ctx_states(ctx, mod, g, win, cw, cb, wg, bg, lam):
    bsz = ctx.shape[0]
    n = CTX_LEN
    full = lambda *shape: pl.BlockSpec(shape, lambda b: (0,) * len(shape))
    return pl.pallas_call(
        functools.partial(_ctx_kernel, mod_row=bsz),
        grid=(bsz,),
        in_specs=[pl.BlockSpec((1, n, D_MODEL), lambda b: (b, 0, 0)),
                  full(8, 3 * D_MODEL),
                  full(1, D_MODEL),
                  pl.BlockSpec((D_MODEL, D_LRU), lambda b: (0, 2)),
                  full(4, MC, D_LRU), full(MC, D_LRU),
                  *_gate_specs(0), *_gate_specs(1),
                  full(2, D_LRU)],
        out_specs=pl.BlockSpec((1, 2, D_LRU), lambda b: (b, 0, 0)),
        out_shape=jax.ShapeDtypeStruct((bsz, 2, D_LRU), F32),
        scratch_shapes=[pltpu.VMEM((N_QL, MC * _pitch(CH_CTX), LANES), F32),
                        pltpu.VMEM((n + 3 * MC, D_LRU), F32),
                        pltpu.VMEM((n, D_LRU), F32),
                        pltpu.VMEM((n, 2 * D_LRU), F32),
                        pltpu.VMEM((n, D_LRU), F32),
                        pltpu.VMEM((n, D_LRU), F32)],
        compiler_params=pltpu.CompilerParams(dimension_semantics=("arbitrary",)),
        name="ctx_states",
    )(ctx, mod, g, win, cw, cb, wg, bg, wg, bg, lam)


def _fwd_kernel(x_ref, xh_ref, mod_ref, g_ref, win_ref, cw_ref, cb_ref, wg_ref,
                bg_ref, lam_ref, h0_ref,
                uf_ref, gfs_ref, gls_ref, v_ref, hf_ref,
                lhs_s, up_s, uext_s, vf_s, g_s, a_s, h_s, cc_s, hc_s):
    j = pl.program_id(1)

    @pl.when(j == 0)
    def _():
        cc_s[...] = jnp.zeros_like(cc_s)
        hc_s[...] = jnp.broadcast_to(h0_ref[0, 0:1, :], hc_s.shape)

    brow = pl.ds(pl.program_id(0), 1)
    shift = mod_ref[brow, 0:D_MODEL]
    scale = mod_ref[brow, D_MODEL:2 * D_MODEL]
    gmul = g_ref[...] * (1.0 + scale)
    for m in range(MC):
        rows = slice(m * CH, (m + 1) * CH)
        lhs_s[rows, :] = _modulate(x_ref[0, rows, :], gmul, shift).astype(BF16)
    hh = _modulate(xh_ref[0], gmul, shift)
    lhs_s[TILE:LHS_ROWS, :] = jnp.concatenate([hh, jnp.zeros_like(hh)], axis=0).astype(BF16)

    u = jnp.dot(lhs_s[...], win_ref[:, 2 * D_FOURIER:2 * D_FOURIER + D_LRU],
                preferred_element_type=F32)

    _to_slab_order(u, up_s, uext_s, CH)
    look = jnp.where(j == N_TILES - 1, 0.0, u[TILE:TILE + 1])
    _conv_halo(uext_s, look, cc_s[0:1, :], cc_s[1:2, :], CH)
    cc_s[0:1, :] = u[TILE - 1:TILE]
    cc_s[1:2, :] = u[TILE - 2:TILE - 1]

    def silu_gate(o_ref, c0, rq):
        o_ref[0, rq, :] = _silu_of_half(jnp.dot(lhs_s[rq, :], win_ref[:, c0:c0 + D_LRU],
                                                preferred_element_type=F32)).astype(BF16)

    silu_gate(gls_ref, 2 * D_FOURIER + D_LRU, slice(0, TILE))

    for s in range(CH):
        rows = slice(s * MC, (s + 1) * MC)
        v = _conv_slab(uext_s, s, cw_ref, cb_ref)
        vf_s[rows, :] = v
        v_ref[0, rows, :] = v.astype(BF16)

    halves = (slice(0, TILE // 2), slice(TILE // 2, TILE))
    hrate = _half_decay_rate(lam_ref[0:1, :])
    for h in range(N_HEADS):
        cols = _head_cols(h)
        _gate_terms(h, v_ref[0, :, cols], vf_s[:, cols], wg_ref, bg_ref, hrate, g_s, slice(0, TILE))
        rq = halves[h % 2]
        if h < 2:
            uf_ref[0, rq, :] = jnp.dot(lhs_s[rq, :], win_ref[:, 0:D_FOURIER],
                                       preferred_element_type=F32).astype(BF16)
        else:
            silu_gate(gfs_ref, D_FOURIER, rq)

    tot_a, tot_h = _scan(_stored_terms(g_s), a_s, h_s, range(CH))
    hin, h_out = _chunk_carries(tot_a, tot_h, hc_s[0:1, :], reverse=False)
    hc_s[...] = jnp.broadcast_to(h_out, hc_s.shape)
    for s in range(CH):
        rows = slice(s * MC, (s + 1) * MC)
        hf_ref[0, rows, :] = (h_s[rows, :] + a_s[rows, :] * hin).astype(BF16)


def _fwd_pass(x, mod, g, win, cw, cb, wg, bg, lam, fin):
    bsz = x.shape[0]
    full = lambda *shape: pl.BlockSpec(shape, lambda b, j: (0,) * len(shape))
    tile = lambda c: pl.BlockSpec((1, TILE, c), lambda b, j: (b, j, 0))
    shp = lambda c: jax.ShapeDtypeStruct((bsz, SEQ, c), BF16)
    rows8 = TILE // 8
    return pl.pallas_call(
        _fwd_kernel,
        grid=(bsz, N_TILES),
        in_specs=[tile(D_MODEL),
                  pl.BlockSpec((1, 8, D_MODEL),
                               lambda b, j: (b, jnp.minimum((j + 1) * rows8, SEQ // 8 - 1), 0)),
                  full(8, 3 * D_MODEL),
                  full(1, D_MODEL), full(D_MODEL, 2 * D_MODEL),
                  full(4, MC, D_LRU), full(MC, D_LRU),
                  *_gate_specs(0),
                  full(2, D_LRU),
                  pl.BlockSpec((1, 2, D_LRU), lambda b, j: (b, 0, 0))],
        out_specs=[tile(D_FOURIER), tile(D_FOURIER), tile(D_LRU), tile(D_LRU), tile(D_LRU)],
        out_shape=[shp(D_FOURIER), shp(D_FOURIER), shp(D_LRU), shp(D_LRU), shp(D_LRU)],
        scratch_shapes=[pltpu.VMEM((LHS_ROWS, D_MODEL), BF16),
                        pltpu.VMEM((N_QL, MC * _pitch(CH), LANES), F32),
                        pltpu.VMEM((TILE + 3 * MC, D_LRU), F32),
                        pltpu.VMEM((TILE, D_LRU), F32),
                        pltpu.VMEM((TILE, 2 * D_LRU), F32),
                        pltpu.VMEM((TILE, D_LRU), F32),
                        pltpu.VMEM((TILE, D_LRU), F32),
                        pltpu.VMEM((8, D_LRU), F32),
                        pltpu.VMEM((8, D_LRU), F32)],
        compiler_params=pltpu.CompilerParams(dimension_semantics=("arbitrary", "arbitrary"),
                                             vmem_limit_bytes=VMEM_LIMIT_BYTES),
        name="fwd_pass",
    )(x, x, mod, g, win, cw, cb, wg, bg, lam, fin)


def _dft_kernel(u_ref, dftc_ref, f1_ref, m2_ref, o_ref, sp_s, sq_s, yr_s, yj_s, so_s):
    p1 = DFT_P1
    p2 = DFT_P2
    dftc = dftc_ref[...].astype(BF16)
    chunks = TILE // DFT_N2
    for blk in range(SEQ // TILE):
        pq = jnp.dot(u_ref[0, blk * TILE:(blk + 1) * TILE, :], dftc, preferred_element_type=F32)
        for m in range(chunks):
            n1 = blk * chunks + m
            rows = slice(m * DFT_N2, (m + 1) * DFT_N2)
            sp_s[n1 * p1:n1 * p1 + DFT_N2, :] = pq[rows, :GROUP_DIM]
            sq_s[n1 * p1:n1 * p1 + DFT_N2, :] = pq[rows, GROUP_DIM:]
    f1 = f1_ref[...].astype(BF16)
    for n2 in range(0, DFT_N2, DFT_NB):
        rhs = jnp.concatenate(
            [jnp.concatenate([sp_s[pl.ds(n2 + i, DFT_N1, stride=p1), :],
                              sq_s[pl.ds(n2 + i, DFT_N1, stride=p1), :]], axis=0)
             for i in range(DFT_NB)], axis=1).astype(BF16)
        y = jnp.dot(f1, rhs, preferred_element_type=F32)
        for i in range(DFT_NB):
            lanes = slice(i * LANES, (i + 1) * LANES)
            yr_s[(n2 + i) * p2:(n2 + i) * p2 + DFT_N1, :] = y[:DFT_N1, lanes]
            yj_s[(n2 + i) * p2:(n2 + i) * p2 + DFT_N1, :] = y[DFT_N1:, lanes]
    for k1 in range(DFT_N1):
        rhs = jnp.concatenate([yr_s[pl.ds(k1, DFT_N2, stride=p2), :],
                               yj_s[pl.ds(k1, DFT_N2, stride=p2), :]], axis=0).astype(BF16)
        so_s[pl.ds(k1, DFT_N2, stride=p2), :] = jnp.dot(m2_ref[k1].astype(BF16), rhs,
                                                         preferred_element_type=F32)
    for k2 in range(DFT_N2):
        o_ref[0, k2 * DFT_N1:(k2 + 1) * DFT_N1, :] = so_s[k2 * p2:k2 * p2 + DFT_N1, :].astype(BF16)


def _fourier_dft(uf, dftc, f1, m2):
    bsz = uf.shape[0]
    blk = pl.BlockSpec((1, SEQ, GROUP_DIM), lambda b, i: (b, 0, i))
    pad1 = DFT_N1 * DFT_P1
    pad2 = DFT_N2 * DFT_P2
    return pl.pallas_call(
        _dft_kernel,
        grid=(bsz, N_GROUPS),
        in_specs=[blk,
                  pl.BlockSpec((GROUP_DIM, 2 * GROUP_DIM), lambda b, i: (0, 0)),
                  pl.BlockSpec((2 * DFT_N1, 2 * DFT_N1), lambda b, i: (0, 0)),
                  pl.BlockSpec((DFT_N1, DFT_N2, 2 * DFT_N2), lambda b, i: (0, 0, 0))],
        out_specs=blk,
        out_shape=jax.ShapeDtypeStruct(uf.shape, BF16),
        scratch_shapes=[pltpu.VMEM((pad1, LANES), F32), pltpu.VMEM((pad1, LANES), F32),
                        pltpu.VMEM((pad2, LANES), F32), pltpu.VMEM((pad2, LANES), F32),
                        pltpu.VMEM((pad2, LANES), F32)],
        compiler_params=pltpu.CompilerParams(dimension_semantics=("arbitrary", "arbitrary"),
                                             vmem_limit_bytes=VMEM_LIMIT_BYTES),
        name="fourier_dft",
    )(uf, dftc, f1, m2)


def _bwd_kernel(x_ref, yf_ref, gfs_ref, gls_ref, v_ref, hf_ref, mod_ref, fg_ref, wfour_ref, wout_ref,
                wg_ref, bg_ref, lam_ref, h0_ref, o_ref,
                g_s, a_s, h_s, yp_s, lhs_s, hc_s):
    j = pl.program_id(1)

    @pl.when(j == 0)
    def _():
        hc_s[...] = jnp.broadcast_to(h0_ref[0, 1:2, :], hc_s.shape)

    for h in range(N_HEADS):
        res = jnp.dot(v_ref[0, :, _head_cols(h)], wg_ref[h], preferred_element_type=F32)
        g_s[:, _head_cols(h)] = res[:, :HEAD_DIM]
        g_s[:, _head_cols(h, D_LRU)] = res[:, HEAD_DIM:]
    yfw = jnp.dot(yf_ref[0], wfour_ref[...], preferred_element_type=F32)
    lhs_s[:, 0:D_FOURIER] = (yfw * gfs_ref[0].astype(F32)).astype(BF16)

    hrate = _half_decay_rate(lam_ref[1:2, :])
    b_r = bg_ref[:, :D_LRU]
    b_i = bg_ref[:, D_LRU:]
    terms = lambda rows: _gate_ab(g_s[rows, :D_LRU] + b_r, g_s[rows, D_LRU:] + b_i,
                                  v_ref[0, rows, :].astype(F32), hrate)
    tot_a, tot_h = _scan(terms, a_s, h_s, range(CH - 1, -1, -1))
    hin, h_out = _chunk_carries(tot_a, tot_h, hc_s[0:1, :], reverse=True)
    hc_s[...] = jnp.broadcast_to(h_out, hc_s.shape)
    pitch = _pitch(CH)
    for s in range(CH):
        rows = slice(s * MC, (s + 1) * MC)
        yl = h_s[rows, :] + a_s[rows, :] * hin + hf_ref[0, rows, :].astype(F32)
        for q in range(N_QL):
            yp_s[q, pl.ds(s, MC, stride=pitch), :] = yl[:, q * LANES:(q + 1) * LANES]
    for m in range(MC):
        rows = slice(m * CH, (m + 1) * CH)
        for q in range(N_QL):
            cols = slice(q * LANES, (q + 1) * LANES)
            yl = yp_s[q, m * pitch:m * pitch + CH, :]
            lhs_s[rows, D_FOURIER + q * LANES:D_FOURIER + (q + 1) * LANES] = (
                yl * gls_ref[0, rows, cols].astype(F32)).astype(BF16)

    res_gate = mod_ref[pl.ds(pl.program_id(0), 1), 2 * D_MODEL:3 * D_MODEL]
    fg = fg_ref[...]
    proj = jnp.dot(lhs_s[...], wout_ref[...], preferred_element_type=F32)
    for m in range(MC):
        rows = slice(m * CH, (m + 1) * CH)
        res = x_ref[0, rows, :] + res_gate * proj[rows, :]
        ms = jnp.mean(res * res, axis=-1, keepdims=True)
        o_ref[0, rows, :] = (res * lax.rsqrt(ms + EPS)) * fg


def _bwd_pass(x, yf, gfs, gls, v, hf, mod, fg, wfour, wout, wg, bg, lam, fin):
    bsz = x.shape[0]
    full = lambda *shape: pl.BlockSpec(shape, lambda b, j: (0,) * len(shape))
    tile = lambda c: pl.BlockSpec((1, TILE, c), lambda b, j: (b, N_TILES - 1 - j, 0))
    return pl.pallas_call(
        _bwd_kernel,
        grid=(bsz, N_TILES),
        in_specs=[tile(D_MODEL), tile(D_FOURIER), tile(D_FOURIER), tile(D_LRU), tile(D_LRU),
                  tile(D_LRU),
                  full(8, 3 * D_MODEL),
                  full(1, D_MODEL), full(D_FOURIER, D_FOURIER), full(D_MODEL, D_MODEL),
                  *_gate_specs(1), full(2, D_LRU),
                  pl.BlockSpec((1, 2, D_LRU), lambda b, j: (b, 0, 0))],
        out_specs=tile(D_MODEL),
        out_shape=jax.ShapeDtypeStruct(x.shape, F32),
        scratch_shapes=[pltpu.VMEM((TILE, 2 * D_LRU), F32),
                        pltpu.VMEM((TILE, D_LRU), F32),
                        pltpu.VMEM((TILE, D_LRU), F32),
                        pltpu.VMEM((N_QL, MC * _pitch(CH), LANES), F32),
                        pltpu.VMEM((TILE, D_MODEL), BF16),
                        pltpu.VMEM((8, D_LRU), F32)],
        compiler_params=pltpu.CompilerParams(dimension_semantics=("arbitrary", "arbitrary"),
                                             vmem_limit_bytes=VMEM_LIMIT_BYTES),
        name="bwd_pass",
    )(x, yf, gfs, gls, v, hf, mod, fg, wfour, wout, wg, bg, lam, fin)


def kernel(x, c, ctx, c_ctx, w_ada, b_ada, norm_gain, w_in, w_four, conv_w, conv_b, w_rg, b_rg,
           w_ig, b_ig, lam, w_out, final_gain):
    bsz = x.shape[0]
    assert x.shape == (bsz, SEQ, D_MODEL) and ctx.shape == (bsz, CTX_LEN, D_MODEL)
    assert w_ada.shape[0] == 1, "single-layer kernel"
    dftc, f1, m2 = _dft_constants()

    mod, win, wout, wfour, wg, bg, cw, cb = _prep(c, c_ctx[None, :], w_ada, b_ada, w_in, w_out,
                                                  w_four, w_rg, w_ig, b_rg, b_ig, conv_w, conv_b)
    g = norm_gain[0][None, :]
    lam0 = lam[0]

    fin = _ctx_states(ctx, mod, g, win, cw, cb, wg, bg, lam0)
    uf, gfs, gls, v, hf = _fwd_pass(x, mod, g, win, cw, cb, wg, bg, lam0, fin)

    yf = _fourier_dft(uf, dftc, f1, m2)

    return _bwd_pass(x, yf, gfs, gls, v, hf, mod, final_gain[None, :], wfour, wout, wg, bg, lam0, fin)
```

```python
import functools

import numpy as np
import jax
import jax.numpy as jnp
from jax import lax
from jax.experimental import pallas as pl
from jax.experimental.pallas import tpu as pltpu

D_MODEL = 1024
SEQ = 8192
CTX_LEN = 256
D_FOURIER = 512
D_LRU = 512
N_GROUPS = 4
GROUP_DIM = 128
N_HEADS = 4
HEAD_DIM = 128
LRU_C = 8.0
EPS = 1e-6
LOG2_E = float(np.log2(np.e))

LANES = 128
MC = 16
CH = 64
TILE = MC * CH
N_TILES = SEQ // TILE
CH_CTX = CTX_LEN // MC
N_QL = D_LRU // LANES
LHS_ROWS = TILE + MC
DFT_N1 = 128
DFT_N2 = 64
DFT_NB = 2

VMEM_LIMIT_BYTES = 56 * 1024 * 1024
F32 = jnp.float32
BF16 = jnp.bfloat16


def _pitch(n):
    return n + 4


DFT_P1 = _pitch(DFT_N2)
DFT_P2 = _pitch(DFT_N1)


def _dft_constants():
    c = np.arange(GROUP_DIM)
    ang = 2.0 * np.pi * ((c[:, None] * c[None, :]) % GROUP_DIM) / GROUP_DIM
    dftc = np.concatenate([np.cos(ang), np.sin(ang)], axis=1) * 2.0 ** -3

    n = np.arange(DFT_N1)
    ang1 = 2.0 * np.pi * ((n[:, None] * n[None, :]) % DFT_N1) / DFT_N1
    c1, s1 = np.cos(ang1), np.sin(ang1)
    f1 = np.block([[c1, -s1], [s1, c1]]) * 2.0 ** -4

    k1 = np.arange(DFT_N1)[:, None, None]
    k2 = np.arange(DFT_N2)[None, :, None]
    n2 = np.arange(DFT_N2)[None, None, :]
    ang2 = 2.0 * np.pi * ((n2 * (k1 + DFT_N1 * k2)) % SEQ) / SEQ
    m2 = np.concatenate([np.cos(ang2), -np.sin(ang2)], axis=2) * 2.0 ** -3
    return jnp.asarray(dftc, dtype=F32), jnp.asarray(f1, dtype=F32), jnp.asarray(m2, dtype=F32)


def _silu(x):
    hx = 0.5 * x
    return hx * jnp.tanh(hx) + hx


def _silu_of_half(hx):
    return hx * jnp.tanh(hx) + hx


def _half_decay_rate(lam_row):
    z = -lam_row
    return (-0.5 * LRU_C * LOG2_E) * (jnp.maximum(z, 0.0) + jnp.log1p(jnp.exp(-jnp.abs(z))))


def _modulate(xs, gmul, shift):
    ms = jnp.mean(xs * xs, axis=-1, keepdims=True)
    return (xs * lax.rsqrt(ms + EPS)) * gmul + shift


def _gate_ab(hpre_r, hpre_i, hv, hrate):
    a = jnp.exp2(jnp.tanh(hpre_r) * hrate + hrate)
    om = 1.0 - a * a
    s = om * lax.rsqrt(jnp.maximum(om, 1e-30))
    return a, s * ((jnp.tanh(hpre_i) + 1.0) * hv)


def _ds(start, size):
    if isinstance(start, int):
        return slice(start, start + size)
    return pl.ds(pl.multiple_of(start, MC), size)


def _to_slab_order(u, up_s, uext_s, ch):
    pitch = _pitch(ch)
    for m in range(MC):
        for q in range(N_QL):
            up_s[q, m * pitch:m * pitch + ch, :] = u[m * ch:(m + 1) * ch, q * LANES:(q + 1) * LANES]
    for s in range(ch):
        for q in range(N_QL):
            uext_s[(s + 2) * MC:(s + 3) * MC, q * LANES:(q + 1) * LANES] = (
                up_s[q, pl.ds(s, MC, stride=pitch), :])


def _conv_halo(uext_s, lookahead, c_prev1, c_prev2, ch):
    last = uext_s[(ch + 1) * MC:(ch + 2) * MC, :]
    last2 = uext_s[ch * MC:(ch + 1) * MC, :]
    first = uext_s[2 * MC:3 * MC, :]
    uext_s[0:MC, :] = jnp.concatenate([c_prev2, last2[:MC - 1]], axis=0)
    uext_s[MC:2 * MC, :] = jnp.concatenate([c_prev1, last[:MC - 1]], axis=0)
    uext_s[(ch + 2) * MC:(ch + 3) * MC, :] = jnp.concatenate([first[1:], lookahead], axis=0)


def _conv_slab(uext_s, s, cw_ref, cb_ref):
    acc = cb_ref[...]
    for k in range(4):
        acc = acc + cw_ref[k] * uext_s[_ds((s + k) * MC, MC), :]
    return acc


def _head_cols(h, base=0):
    return slice(base + h * HEAD_DIM, base + (h + 1) * HEAD_DIM)


def _gate_terms(h, vb, hv, wg_ref, bg_ref, hrate, g_s, rows):
    cols, icols = _head_cols(h), _head_cols(h, D_LRU)
    res = jnp.dot(vb, wg_ref[h], preferred_element_type=F32)
    a, bx = _gate_ab(res[:, :HEAD_DIM] + bg_ref[:, cols], res[:, HEAD_DIM:] + bg_ref[:, icols],
                     hv, hrate[:, cols])
    g_s[rows, cols] = a
    g_s[rows, icols] = bx


def _stored_terms(g_s):
    return lambda rows: (g_s[rows, :D_LRU], g_s[rows, D_LRU:])


def _scan(terms, a_s, h_s, steps):
    acc_a = jnp.ones((MC, D_LRU), F32)
    acc_h = jnp.zeros((MC, D_LRU), F32)
    for s in steps:
        rows = slice(s * MC, (s + 1) * MC)
        a, bx = terms(rows)
        acc_a = a * acc_a
        acc_h = a * acc_h + bx
        a_s[rows, :] = acc_a
        h_s[rows, :] = acc_h
    return acc_a, acc_h


def _chunk_carries(tot_a, tot_h, carry_in, reverse):
    row_id = lax.broadcasted_iota(jnp.int32, tot_a.shape, 0)
    hin = jnp.zeros(tot_a.shape, F32)
    h = carry_in
    order = range(MC - 1, -1, -1) if reverse else range(MC)
    for m in order:
        hin = jnp.where(row_id == m, h, hin)
        h = tot_a[m:m + 1] * h + tot_h[m:m + 1]
    return hin, h


def _prep_kernel(c_ref, cctx_ref, wada_ref, bada_ref, win_ref, wout_ref, wfour_ref, wrg_ref, wig_ref,
                 brg_ref, big_ref, cw_ref, cb_ref,
                 mod_ref, win_o, wout_o, wfour_o, wg_o, bg_o, cwr_o, cbr_o):
    i = pl.program_id(0)
    bsz = c_ref.shape[0]
    cc = jnp.concatenate([c_ref[...], cctx_ref[...], jnp.zeros((8 - bsz - 1, D_MODEL), F32)], axis=0)
    s = _silu(cc).astype(BF16)
    mod_ref[...] = (jnp.dot(s, wada_ref[0].astype(BF16), preferred_element_type=F32)
                    + bada_ref[...])

    @pl.when(i < 2)
    def _():
        win_o[:, 0:D_FOURIER] = win_ref[0, :, 0:D_FOURIER].astype(BF16)
        win_o[:, D_FOURIER:] = (0.5 * win_ref[0, :, D_FOURIER:]).astype(BF16)

    @pl.when(i == 2)
    def _():
        wout_o[...] = wout_ref[0].astype(BF16)
        wfour_o[...] = wfour_ref[0].astype(BF16)
        for d in range(2):
            for h in range(N_HEADS):
                wg_o[d, h] = jnp.concatenate([wrg_ref[0, d, h], wig_ref[0, d, h]],
                                             axis=1).astype(BF16)
            bg_o[d] = 0.5 * jnp.concatenate([brg_ref[0, d:d + 1, :], big_ref[0, d:d + 1, :]], axis=1)
        for k in range(4):
            cwr_o[k] = jnp.broadcast_to(0.5 * cw_ref[0, k:k + 1, :], (MC, D_LRU))
        cbr_o[...] = jnp.broadcast_to(0.5 * cb_ref[...], (MC, D_LRU))


def _prep(c, c_ctx, w_ada, b_ada, w_in, w_out, w_four, w_rg, w_ig, b_rg, b_ig, conv_w, conv_b):
    full = lambda *shape: pl.BlockSpec(shape, lambda i: (0,) * len(shape))
    half = lambda i: jnp.minimum(i, 1)
    gshape = (1, 2, N_HEADS, HEAD_DIM, HEAD_DIM)
    return pl.pallas_call(
        _prep_kernel,
        grid=(3,),
        in_specs=[full(*c.shape), full(1, D_MODEL),
                  pl.BlockSpec((1, D_MODEL, D_MODEL), lambda i: (0, 0, i)),
                  pl.BlockSpec((1, D_MODEL), lambda i: (0, i)),
                  pl.BlockSpec((1, D_MODEL, D_MODEL), lambda i: (0, 0, half(i))),
                  full(1, D_MODEL, D_MODEL), full(1, D_FOURIER, D_FOURIER),
                  full(*gshape), full(*gshape), full(1, 2, D_LRU), full(1, 2, D_LRU),
                  full(1, 4, D_LRU), full(1, D_LRU)],
        out_specs=[pl.BlockSpec((8, D_MODEL), lambda i: (0, i)),
                   pl.BlockSpec((D_MODEL, D_MODEL), lambda i: (0, half(i))),
                   full(D_MODEL, D_MODEL), full(D_FOURIER, D_FOURIER),
                   full(2, N_HEADS, HEAD_DIM, 2 * HEAD_DIM), full(2, 1, 2 * D_LRU),
                   full(4, MC, D_LRU), full(MC, D_LRU)],
        out_shape=[jax.ShapeDtypeStruct((8, 3 * D_MODEL), F32),
                   jax.ShapeDtypeStruct((D_MODEL, 2 * D_MODEL), BF16),
                   jax.ShapeDtypeStruct((D_MODEL, D_MODEL), BF16),
                   jax.ShapeDtypeStruct((D_FOURIER, D_FOURIER), BF16),
                   jax.ShapeDtypeStruct((2, N_HEADS, HEAD_DIM, 2 * HEAD_DIM), BF16),
                   jax.ShapeDtypeStruct((2, 1, 2 * D_LRU), F32),
                   jax.ShapeDtypeStruct((4, MC, D_LRU), F32),
                   jax.ShapeDtypeStruct((MC, D_LRU), F32)],
        compiler_params=pltpu.CompilerParams(dimension_semantics=("arbitrary",),
                                             vmem_limit_bytes=VMEM_LIMIT_BYTES),
        name="prep",
    )(c, c_ctx, w_ada, b_ada, w_in, w_out, w_four, w_rg, w_ig, b_rg, b_ig, conv_w, conv_b)


def _ctx_kernel(x_ref, mod_ref, g_ref, wlx_ref, cw_ref, cb_ref, wgf_ref, bgf_ref, wgb_ref, bgb_ref,
                lam_ref, fin_ref, up_s, uext_s, vf_s, g_s, a_s, h_s, *, mod_row):
    ch = CH_CTX
    shift = mod_ref[mod_row:mod_row + 1, 0:D_MODEL]
    scale = mod_ref[mod_row:mod_row + 1, D_MODEL:2 * D_MODEL]
    gmul = g_ref[...] * (1.0 + scale)
    lhs = _modulate(x_ref[0], gmul, shift).astype(BF16)
    u = jnp.dot(lhs, wlx_ref[...], preferred_element_type=F32)
    _to_slab_order(u, up_s, uext_s, ch)
    zero_row = jnp.zeros((1, D_LRU), F32)
    _conv_halo(uext_s, zero_row, zero_row, zero_row, ch)
    for s in range(ch):
        vf_s[s * MC:(s + 1) * MC, :] = _conv_slab(uext_s, s, cw_ref, cb_ref)
    vb = vf_s[...].astype(BF16)
    for d, (wg_ref, bg_ref) in enumerate(((wgf_ref, bgf_ref), (wgb_ref, bgb_ref))):
        rate = _half_decay_rate(lam_ref[d:d + 1, :])
        steps = range(ch - 1, -1, -1) if d == 1 else range(ch)
        for h in range(N_HEADS):
            cols = _head_cols(h)
            _gate_terms(h, vb[:, cols], vf_s[:, cols], wg_ref, bg_ref, rate, g_s,
                        slice(0, ch * MC))
        tot_a, tot_h = _scan(_stored_terms(g_s), a_s, h_s, steps)
        _, fin = _chunk_carries(tot_a, tot_h, zero_row, reverse=(d == 1))
        fin_ref[0, d:d + 1, :] = fin


def _gate_specs(direction):
    return (pl.BlockSpec((None, N_HEADS, HEAD_DIM, 2 * HEAD_DIM), lambda *_: (direction, 0, 0, 0)),
            pl.BlockSpec((None, 1, 2 * D_LRU), lambda *_: (direction, 0, 0)))


def _ctx_states(ctx, mod, g, win, cw, cb, wg, bg, lam):
    bsz = ctx.shape[0]
    n = CTX_LEN
    full = lambda *shape: pl.BlockSpec(shape, lambda b: (0,) * len(shape))
    return pl.pallas_call(
        functools.partial(_ctx_kernel, mod_row=bsz),
        grid=(bsz,),
        in_specs=[pl.BlockSpec((1, n, D_MODEL), lambda b: (b, 0, 0)),
                  full(8, 3 * D_MODEL),
                  full(1, D_MODEL),
                  pl.BlockSpec((D_MODEL, D_LRU), lambda b: (0, 2)),
                  full(4, MC, D_LRU), full(MC, D_LRU),
                  *_gate_specs(0), *_gate_specs(1),
                  full(2, D_LRU)],
        out_specs=pl.BlockSpec((1, 2, D_LRU), lambda b: (b, 0, 0)),
        out_shape=jax.ShapeDtypeStruct((bsz, 2, D_LRU), F32),
        scratch_shapes=[pltpu.VMEM((N_QL, MC * _pitch(CH_CTX), LANES), F32),
                        pltpu.VMEM((n + 3 * MC, D_LRU), F32),
                        pltpu.VMEM((n, D_LRU), F32),
                        pltpu.VMEM((n, 2 * D_LRU), F32),
                        pltpu.VMEM((n, D_LRU), F32),
                        pltpu.VMEM((n, D_LRU), F32)],
        compiler_params=pltpu.CompilerParams(dimension_semantics=("arbitrary",)),
        name="ctx_states",
    )(ctx, mod, g, win, cw, cb, wg, bg, wg, bg, lam)


def _fwd_kernel(x_ref, xh_ref, mod_ref, g_ref, win_ref, cw_ref, cb_ref, wg_ref,
                bg_ref, lam_ref, h0_ref,
                uf_ref, gfs_ref, gls_ref, v_ref, hf_ref,
                lhs_s, up_s, uext_s, vf_s, g_s, a_s, h_s, cc_s, hc_s):
    j = pl.program_id(1)

    @pl.when(j == 0)
    def _():
        cc_s[...] = jnp.zeros_like(cc_s)
        hc_s[...] = jnp.broadcast_to(h0_ref[0, 0:1, :], hc_s.shape)

    brow = pl.ds(pl.program_id(0), 1)
    shift = mod_ref[brow, 0:D_MODEL]
    scale = mod_ref[brow, D_MODEL:2 * D_MODEL]
    gmul = g_ref[...] * (1.0 + scale)
    for m in range(MC):
        rows = slice(m * CH, (m + 1) * CH)
        lhs_s[rows, :] = _modulate(x_ref[0, rows, :], gmul, shift).astype(BF16)
    hh = _modulate(xh_ref[0], gmul, shift)
    lhs_s[TILE:LHS_ROWS, :] = jnp.concatenate([hh, jnp.zeros_like(hh)], axis=0).astype(BF16)

    u = jnp.dot(lhs_s[...], win_ref[:, 2 * D_FOURIER:2 * D_FOURIER + D_LRU],
                preferred_element_type=F32)

    _to_slab_order(u, up_s, uext_s, CH)
    look = jnp.where(j == N_TILES - 1, 0.0, u[TILE:TILE + 1])
    _conv_halo(uext_s, look, cc_s[0:1, :], cc_s[1:2, :], CH)
    cc_s[0:1, :] = u[TILE - 1:TILE]
    cc_s[1:2, :] = u[TILE - 2:TILE - 1]

    uf_ref[0] = jnp.dot(lhs_s[0:TILE, :], win_ref[:, 0:D_FOURIER],
                        preferred_element_type=F32).astype(BF16)

    for s in range(CH):
        rows = slice(s * MC, (s + 1) * MC)
        v = _conv_slab(uext_s, s, cw_ref, cb_ref)
        vf_s[rows, :] = v
        v_ref[0, rows, :] = v.astype(BF16)

    halves = (slice(0, TILE // 2), slice(TILE // 2, TILE))
    hrate = _half_decay_rate(lam_ref[0:1, :])
    for h in range(N_HEADS):
        cols = _head_cols(h)
        _gate_terms(h, v_ref[0, :, cols], vf_s[:, cols], wg_ref, bg_ref, hrate, g_s, slice(0, TILE))
        o_ref, c0 = ((gfs_ref, D_FOURIER), (gls_ref, 2 * D_FOURIER + D_LRU))[h // 2]
        rq = halves[h % 2]
        o_ref[0, rq, :] = _silu_of_half(jnp.dot(lhs_s[rq, :], win_ref[:, c0:c0 + D_LRU],
                                                preferred_element_type=F32)).astype(BF16)

    tot_a, tot_h = _scan(_stored_terms(g_s), a_s, h_s, range(CH))
    hin, h_out = _chunk_carries(tot_a, tot_h, hc_s[0:1, :], reverse=False)
    hc_s[...] = jnp.broadcast_to(h_out, hc_s.shape)
    for s in range(CH):
        rows = slice(s * MC, (s + 1) * MC)
        hf_ref[0, rows, :] = (h_s[rows, :] + a_s[rows, :] * hin).astype(BF16)


def _fwd_pass(x, mod, g, win, cw, cb, wg, bg, lam, fin):
    bsz = x.shape[0]
    full = lambda *shape: pl.BlockSpec(shape, lambda b, j: (0,) * len(shape))
    tile = lambda c: pl.BlockSpec((1, TILE, c), lambda b, j: (b, j, 0))
    shp = lambda c: jax.ShapeDtypeStruct((bsz, SEQ, c), BF16)
    rows8 = TILE // 8
    return pl.pallas_call(
        _fwd_kernel,
        grid=(bsz, N_TILES),
        in_specs=[tile(D_MODEL),
                  pl.BlockSpec((1, 8, D_MODEL),
                               lambda b, j: (b, jnp.minimum((j + 1) * rows8, SEQ // 8 - 1), 0)),
                  full(8, 3 * D_MODEL),
                  full(1, D_MODEL), full(D_MODEL, 2 * D_MODEL),
                  full(4, MC, D_LRU), full(MC, D_LRU),
                  *_gate_specs(0),
                  full(2, D_LRU),
                  pl.BlockSpec((1, 2, D_LRU), lambda b, j: (b, 0, 0))],
        out_specs=[tile(D_FOURIER), tile(D_FOURIER), tile(D_LRU), tile(D_LRU), tile(D_LRU)],
        out_shape=[shp(D_FOURIER), shp(D_FOURIER), shp(D_LRU), shp(D_LRU), shp(D_LRU)],
        scratch_shapes=[pltpu.VMEM((LHS_ROWS, D_MODEL), BF16),
                        pltpu.VMEM((N_QL, MC * _pitch(CH), LANES), F32),
                        pltpu.VMEM((TILE + 3 * MC, D_LRU), F32),
                        pltpu.VMEM((TILE, D_LRU), F32),
                        pltpu.VMEM((TILE, 2 * D_LRU), F32),
                        pltpu.VMEM((TILE, D_LRU), F32),
                        pltpu.VMEM((TILE, D_LRU), F32),
                        pltpu.VMEM((8, D_LRU), F32),
                        pltpu.VMEM((8, D_LRU), F32)],
        compiler_params=pltpu.CompilerParams(dimension_semantics=("arbitrary", "arbitrary"),
                                             vmem_limit_bytes=VMEM_LIMIT_BYTES),
        name="fwd_pass",
    )(x, x, mod, g, win, cw, cb, wg, bg, lam, fin)


def _dft_kernel(u_ref, dftc_ref, f1_ref, m2_ref, o_ref, sp_s, sq_s, yr_s, yj_s, so_s):
    p1 = DFT_P1
    p2 = DFT_P2
    dftc = dftc_ref[...].astype(BF16)
    chunks = TILE // DFT_N2
    for blk in range(SEQ // TILE):
        pq = jnp.dot(u_ref[0, blk * TILE:(blk + 1) * TILE, :], dftc, preferred_element_type=F32)
        for m in range(chunks):
            n1 = blk * chunks + m
            rows = slice(m * DFT_N2, (m + 1) * DFT_N2)
            sp_s[n1 * p1:n1 * p1 + DFT_N2, :] = pq[rows, :GROUP_DIM]
            sq_s[n1 * p1:n1 * p1 + DFT_N2, :] = pq[rows, GROUP_DIM:]
    f1 = f1_ref[...].astype(BF16)
    for n2 in range(0, DFT_N2, DFT_NB):
        rhs = jnp.concatenate(
            [jnp.concatenate([sp_s[pl.ds(n2 + i, DFT_N1, stride=p1), :],
                              sq_s[pl.ds(n2 + i, DFT_N1, stride=p1), :]], axis=0)
             for i in range(DFT_NB)], axis=1).astype(BF16)
        y = jnp.dot(f1, rhs, preferred_element_type=F32)
        for i in range(DFT_NB):
            lanes = slice(i * LANES, (i + 1) * LANES)
            yr_s[(n2 + i) * p2:(n2 + i) * p2 + DFT_N1, :] = y[:DFT_N1, lanes]
            yj_s[(n2 + i) * p2:(n2 + i) * p2 + DFT_N1, :] = y[DFT_N1:, lanes]
    for k1 in range(DFT_N1):
        rhs = jnp.concatenate([yr_s[pl.ds(k1, DFT_N2, stride=p2), :],
                               yj_s[pl.ds(k1, DFT_N2, stride=p2), :]], axis=0).astype(BF16)
        so_s[pl.ds(k1, DFT_N2, stride=p2), :] = jnp.dot(m2_ref[k1].astype(BF16), rhs,
                                                         preferred_element_type=F32)
    for k2 in range(DFT_N2):
        o_ref[0, k2 * DFT_N1:(k2 + 1) * DFT_N1, :] = so_s[k2 * p2:k2 * p2 + DFT_N1, :].astype(BF16)


def _fourier_dft(uf, dftc, f1, m2):
    bsz = uf.shape[0]
    blk = pl.BlockSpec((1, SEQ, GROUP_DIM), lambda b, i: (b, 0, i))
    pad1 = DFT_N1 * DFT_P1
    pad2 = DFT_N2 * DFT_P2
    return pl.pallas_call(
        _dft_kernel,
        grid=(bsz, N_GROUPS),
        in_specs=[blk,
                  pl.BlockSpec((GROUP_DIM, 2 * GROUP_DIM), lambda b, i: (0, 0)),
                  pl.BlockSpec((2 * DFT_N1, 2 * DFT_N1), lambda b, i: (0, 0)),
                  pl.BlockSpec((DFT_N1, DFT_N2, 2 * DFT_N2), lambda b, i: (0, 0, 0))],
        out_specs=blk,
        out_shape=jax.ShapeDtypeStruct(uf.shape, BF16),
        scratch_shapes=[pltpu.VMEM((pad1, LANES), F32), pltpu.VMEM((pad1, LANES), F32),
                        pltpu.VMEM((pad2, LANES), F32), pltpu.VMEM((pad2, LANES), F32),
                        pltpu.VMEM((pad2, LANES), F32)],
        compiler_params=pltpu.CompilerParams(dimension_semantics=("arbitrary", "arbitrary"),
                                             vmem_limit_bytes=VMEM_LIMIT_BYTES),
        name="fourier_dft",
    )(uf, dftc, f1, m2)


X_SLOTS = 3


def _bwd_kernel(x_hbm, yf_ref, gfs_ref, gls_ref, v_ref, hf_ref, mod_ref, fg_ref, wfour_ref, wout_ref,
                wg_ref, bg_ref, lam_ref, h0_ref, o_ref,
                g_s, a_s, h_s, yp_s, lhs_s, hc_s, x_s, x_sem):
    j = pl.program_id(1)

    n_steps = pl.num_programs(0) * N_TILES
    step = pl.program_id(0) * N_TILES + j

    def x_copy(s):
        tile_start = (N_TILES - 1 - s % N_TILES) * TILE
        if not isinstance(s, int):
            tile_start = pl.multiple_of(tile_start, TILE)
        slot = s % X_SLOTS
        return pltpu.make_async_copy(x_hbm.at[s // N_TILES, pl.ds(tile_start, TILE), :],
                                     x_s.at[slot], x_sem.at[slot])

    @pl.when(step == 0)
    def _():
        for s in range(X_SLOTS - 1):
            x_copy(s).start()

    @pl.when(step + X_SLOTS - 1 < n_steps)
    def _():
        x_copy(step + X_SLOTS - 1).start()

    x_copy(step).wait()

    @pl.when(j == 0)
    def _():
        hc_s[...] = jnp.broadcast_to(h0_ref[0, 1:2, :], hc_s.shape)

    for h in range(N_HEADS):
        res = jnp.dot(v_ref[0, :, _head_cols(h)], wg_ref[h], preferred_element_type=F32)
        g_s[:, _head_cols(h)] = res[:, :HEAD_DIM]
        g_s[:, _head_cols(h, D_LRU)] = res[:, HEAD_DIM:]
    yfw = jnp.dot(yf_ref[0], wfour_ref[...], preferred_element_type=F32)
    lhs_s[:, 0:D_FOURIER] = (yfw * gfs_ref[0].astype(F32)).astype(BF16)

    hrate = _half_decay_rate(lam_ref[1:2, :])
    b_r = bg_ref[:, :D_LRU]
    b_i = bg_ref[:, D_LRU:]
    terms = lambda rows: _gate_ab(g_s[rows, :D_LRU] + b_r, g_s[rows, D_LRU:] + b_i,
                                  v_ref[0, rows, :].astype(F32), hrate)
    tot_a, tot_h = _scan(terms, a_s, h_s, range(CH - 1, -1, -1))
    hin, h_out = _chunk_carries(tot_a, tot_h, hc_s[0:1, :], reverse=True)
    hc_s[...] = jnp.broadcast_to(h_out, hc_s.shape)
    pitch = _pitch(CH)
    for s in range(CH):
        rows = slice(s * MC, (s + 1) * MC)
        yl = h_s[rows, :] + a_s[rows, :] * hin + hf_ref[0, rows, :].astype(F32)
        for q in range(N_QL):
            yp_s[q, pl.ds(s, MC, stride=pitch), :] = yl[:, q * LANES:(q + 1) * LANES]
    for m in range(MC):
        rows = slice(m * CH, (m + 1) * CH)
        for q in range(N_QL):
            cols = slice(q * LANES, (q + 1) * LANES)
            yl = yp_s[q, m * pitch:m * pitch + CH, :]
            lhs_s[rows, D_FOURIER + q * LANES:D_FOURIER + (q + 1) * LANES] = (
                yl * gls_ref[0, rows, cols].astype(F32)).astype(BF16)

    res_gate = mod_ref[pl.ds(pl.program_id(0), 1), 2 * D_MODEL:3 * D_MODEL]
    fg = fg_ref[...]
    proj = jnp.dot(lhs_s[...], wout_ref[...], preferred_element_type=F32)
    x_tile = x_s.at[step % X_SLOTS]
    for m in range(MC):
        rows = slice(m * CH, (m + 1) * CH)
        res = x_tile[rows, :] + res_gate * proj[rows, :]
        ms = jnp.mean(res * res, axis=-1, keepdims=True)
        o_ref[0, rows, :] = (res * lax.rsqrt(ms + EPS)) * fg


def _bwd_pass(x, yf, gfs, gls, v, hf, mod, fg, wfour, wout, wg, bg, lam, fin):
    bsz = x.shape[0]
    full = lambda *shape: pl.BlockSpec(shape, lambda b, j: (0,) * len(shape))
    tile = lambda c: pl.BlockSpec((1, TILE, c), lambda b, j: (b, N_TILES - 1 - j, 0))
    return pl.pallas_call(
        _bwd_kernel,
        grid=(bsz, N_TILES),
        in_specs=[pl.BlockSpec(memory_space=pl.ANY),
                  tile(D_FOURIER), tile(D_FOURIER), tile(D_LRU), tile(D_LRU), tile(D_LRU),
                  full(8, 3 * D_MODEL),
                  full(1, D_MODEL), full(D_FOURIER, D_FOURIER), full(D_MODEL, D_MODEL),
                  *_gate_specs(1), full(2, D_LRU),
                  pl.BlockSpec((1, 2, D_LRU), lambda b, j: (b, 0, 0))],
        out_specs=tile(D_MODEL),
        out_shape=jax.ShapeDtypeStruct(x.shape, F32),
        scratch_shapes=[pltpu.VMEM((TILE, 2 * D_LRU), F32),
                        pltpu.VMEM((TILE, D_LRU), F32),
                        pltpu.VMEM((TILE, D_LRU), F32),
                        pltpu.VMEM((N_QL, MC * _pitch(CH), LANES), F32),
                        pltpu.VMEM((TILE, D_MODEL), BF16),
                        pltpu.VMEM((8, D_LRU), F32),
                        pltpu.VMEM((X_SLOTS, TILE, D_MODEL), F32),
                        pltpu.SemaphoreType.DMA((X_SLOTS,))],
        compiler_params=pltpu.CompilerParams(dimension_semantics=("arbitrary", "arbitrary"),
                                             vmem_limit_bytes=VMEM_LIMIT_BYTES),
        name="bwd_pass",
    )(x, yf, gfs, gls, v, hf, mod, fg, wfour, wout, wg, bg, lam, fin)


def kernel(x, c, ctx, c_ctx, w_ada, b_ada, norm_gain, w_in, w_four, conv_w, conv_b, w_rg, b_rg,
           w_ig, b_ig, lam, w_out, final_gain):
    bsz = x.shape[0]
    assert x.shape == (bsz, SEQ, D_MODEL) and ctx.shape == (bsz, CTX_LEN, D_MODEL)
    assert w_ada.shape[0] == 1, "single-layer kernel"
    dftc, f1, m2 = _dft_constants()

    mod, win, wout, wfour, wg, bg, cw, cb = _prep(c, c_ctx[None, :], w_ada, b_ada, w_in, w_out,
                                                  w_four, w_rg, w_ig, b_rg, b_ig, conv_w, conv_b)
    g = norm_gain[0][None, :]
    lam0 = lam[0]

    fin = _ctx_states(ctx, mod, g, win, cw, cb, wg, bg, lam0)
    uf, gfs, gls, v, hf = _fwd_pass(x, mod, g, win, cw, cb, wg, bg, lam0, fin)

    yf = _fourier_dft(uf, dftc, f1, m2)

    return _bwd_pass(x, yf, gfs, gls, v, hf, mod, final_gain[None, :], wfour, wout, wg, bg, lam0, fin)
```

```python
import functools

import numpy as np
import jax
import jax.numpy as jnp
from jax import lax
from jax.experimental import pallas as pl
from jax.experimental.pallas import tpu as pltpu

D_MODEL = 1024
SEQ = 8192
CTX_LEN = 256
D_FOURIER = 512
D_LRU = 512
N_GROUPS = 4
GROUP_DIM = 128
N_HEADS = 4
HEAD_DIM = 128
LRU_C = 8.0
EPS = 1e-6
LOG2_E = float(np.log2(np.e))

LANES = 128
MC = 16
CH = 64
TILE = MC * CH
N_TILES = SEQ // TILE
CH_CTX = CTX_LEN // MC
N_QL = D_LRU // LANES
LHS_ROWS = TILE + MC
DFT_N1 = 128
DFT_N2 = 64
DFT_NB = 2

VMEM_LIMIT_BYTES = 56 * 1024 * 1024
F32 = jnp.float32
BF16 = jnp.bfloat16


def _pitch(n):
    return n + 4


DFT_P1 = _pitch(DFT_N2)
DFT_P2 = _pitch(DFT_N1)


def _dft_constants():
    c = np.arange(GROUP_DIM)
    ang = 2.0 * np.pi * ((c[:, None] * c[None, :]) % GROUP_DIM) / GROUP_DIM
    dftc = np.concatenate([np.cos(ang), np.sin(ang)], axis=1) * 2.0 ** -3

    n = np.arange(DFT_N1)
    ang1 = 2.0 * np.pi * ((n[:, None] * n[None, :]) % DFT_N1) / DFT_N1
    c1, s1 = np.cos(ang1), np.sin(ang1)
    f1 = np.block([[c1, -s1], [s1, c1]]) * 2.0 ** -4

    k1 = np.arange(DFT_N1)[:, None, None]
    k2 = np.arange(DFT_N2)[None, :, None]
    n2 = np.arange(DFT_N2)[None, None, :]
    ang2 = 2.0 * np.pi * ((n2 * (k1 + DFT_N1 * k2)) % SEQ) / SEQ
    m2 = np.concatenate([np.cos(ang2), -np.sin(ang2)], axis=2) * 2.0 ** -3
    return jnp.asarray(dftc, dtype=F32), jnp.asarray(f1, dtype=F32), jnp.asarray(m2, dtype=F32)


def _silu(x):
    hx = 0.5 * x
    return hx * jnp.tanh(hx) + hx


def _silu_of_half(hx):
    return hx * jnp.tanh(hx) + hx


def _half_decay_rate(lam_row):
    z = -lam_row
    return (-0.5 * LRU_C * LOG2_E) * (jnp.maximum(z, 0.0) + jnp.log1p(jnp.exp(-jnp.abs(z))))


def _modulate(xs, gmul, shift):
    ms = jnp.mean(xs * xs, axis=-1, keepdims=True)
    return (xs * lax.rsqrt(ms + EPS)) * gmul + shift


def _gate_ab(hpre_r, hpre_i, hv, hrate):
    a = jnp.exp2(jnp.tanh(hpre_r) * hrate + hrate)
    om = 1.0 - a * a
    s = om * lax.rsqrt(jnp.maximum(om, 1e-30))
    return a, s * ((jnp.tanh(hpre_i) + 1.0) * hv)


def _ds(start, size):
    if isinstance(start, int):
        return slice(start, start + size)
    return pl.ds(pl.multiple_of(start, MC), size)


def _to_slab_order(u, up_s, uext_s, ch):
    pitch = _pitch(ch)
    for m in range(MC):
        for q in range(N_QL):
            up_s[q, m * pitch:m * pitch + ch, :] = u[m * ch:(m + 1) * ch, q * LANES:(q + 1) * LANES]
    for s in range(ch):
        for q in range(N_QL):
            uext_s[(s + 2) * MC:(s + 3) * MC, q * LANES:(q + 1) * LANES] = (
                up_s[q, pl.ds(s, MC, stride=pitch), :])


def _conv_halo(uext_s, lookahead, c_prev1, c_prev2, ch):
    last = uext_s[(ch + 1) * MC:(ch + 2) * MC, :]
    last2 = uext_s[ch * MC:(ch + 1) * MC, :]
    first = uext_s[2 * MC:3 * MC, :]
    uext_s[0:MC, :] = jnp.concatenate([c_prev2, last2[:MC - 1]], axis=0)
    uext_s[MC:2 * MC, :] = jnp.concatenate([c_prev1, last[:MC - 1]], axis=0)
    uext_s[(ch + 2) * MC:(ch + 3) * MC, :] = jnp.concatenate([first[1:], lookahead], axis=0)


def _conv_slab(uext_s, s, cw_ref, cb_ref):
    acc = cb_ref[...]
    for k in range(4):
        acc = acc + cw_ref[k] * uext_s[_ds((s + k) * MC, MC), :]
    return acc


def _head_cols(h, base=0):
    return slice(base + h * HEAD_DIM, base + (h + 1) * HEAD_DIM)


def _gate_terms(h, vb, hv, wg_ref, bg_ref, hrate, g_s, rows):
    cols, icols = _head_cols(h), _head_cols(h, D_LRU)
    res = jnp.dot(vb, wg_ref[h], preferred_element_type=F32)
    a, bx = _gate_ab(res[:, :HEAD_DIM] + bg_ref[:, cols], res[:, HEAD_DIM:] + bg_ref[:, icols],
                     hv, hrate[:, cols])
    g_s[rows, cols] = a
    g_s[rows, icols] = bx


def _stored_terms(g_s):
    return lambda rows: (g_s[rows, :D_LRU], g_s[rows, D_LRU:])


def _scan(terms, a_s, h_s, steps):
    acc_a = jnp.ones((MC, D_LRU), F32)
    acc_h = jnp.zeros((MC, D_LRU), F32)
    for s in steps:
        rows = slice(s * MC, (s + 1) * MC)
        a, bx = terms(rows)
        acc_a = a * acc_a
        acc_h = a * acc_h + bx
        a_s[rows, :] = acc_a
        h_s[rows, :] = acc_h
    return acc_a, acc_h


def _chunk_carries(tot_a, tot_h, carry_in, reverse):
    row_id = lax.broadcasted_iota(jnp.int32, tot_a.shape, 0)
    hin = jnp.zeros(tot_a.shape, F32)
    h = carry_in
    order = range(MC - 1, -1, -1) if reverse else range(MC)
    for m in order:
        hin = jnp.where(row_id == m, h, hin)
        h = tot_a[m:m + 1] * h + tot_h[m:m + 1]
    return hin, h


def _prep_kernel(c_ref, cctx_ref, wada_ref, bada_ref, win_ref, wout_ref, wfour_ref, wrg_ref, wig_ref,
                 brg_ref, big_ref, cw_ref, cb_ref,
                 mod_ref, win_o, wout_o, wfour_o, wg_o, bg_o, cwr_o, cbr_o):
    i = pl.program_id(0)
    bsz = c_ref.shape[0]
    cc = jnp.concatenate([c_ref[...], cctx_ref[...], jnp.zeros((8 - bsz - 1, D_MODEL), F32)], axis=0)
    s = _silu(cc).astype(BF16)
    mod_ref[...] = (jnp.dot(s, wada_ref[0].astype(BF16), preferred_element_type=F32)
                    + bada_ref[...])

    @pl.when(i < 2)
    def _():
        win_o[:, 0:D_FOURIER] = win_ref[0, :, 0:D_FOURIER].astype(BF16)
        win_o[:, D_FOURIER:] = (0.5 * win_ref[0, :, D_FOURIER:]).astype(BF16)

    @pl.when(i == 2)
    def _():
        wout_o[...] = wout_ref[0].astype(BF16)
        wfour_o[...] = wfour_ref[0].astype(BF16)
        for d in range(2):
            for h in range(N_HEADS):
                wg_o[d, h] = jnp.concatenate([wrg_ref[0, d, h], wig_ref[0, d, h]],
                                             axis=1).astype(BF16)
            bg_o[d] = 0.5 * jnp.concatenate([brg_ref[0, d:d + 1, :], big_ref[0, d:d + 1, :]], axis=1)
        for k in range(4):
            cwr_o[k] = jnp.broadcast_to(0.5 * cw_ref[0, k:k + 1, :], (MC, D_LRU))
        cbr_o[...] = jnp.broadcast_to(0.5 * cb_ref[...], (MC, D_LRU))


def _prep(c, c_ctx, w_ada, b_ada, w_in, w_out, w_four, w_rg, w_ig, b_rg, b_ig, conv_w, conv_b):
    full = lambda *shape: pl.BlockSpec(shape, lambda i: (0,) * len(shape))
    half = lambda i: jnp.minimum(i, 1)
    gshape = (1, 2, N_HEADS, HEAD_DIM, HEAD_DIM)
    return pl.pallas_call(
        _prep_kernel,
        grid=(3,),
        in_specs=[full(*c.shape), full(1, D_MODEL),
                  pl.BlockSpec((1, D_MODEL, D_MODEL), lambda i: (0, 0, i)),
                  pl.BlockSpec((1, D_MODEL), lambda i: (0, i)),
                  pl.BlockSpec((1, D_MODEL, D_MODEL), lambda i: (0, 0, half(i))),
                  full(1, D_MODEL, D_MODEL), full(1, D_FOURIER, D_FOURIER),
                  full(*gshape), full(*gshape), full(1, 2, D_LRU), full(1, 2, D_LRU),
                  full(1, 4, D_LRU), full(1, D_LRU)],
        out_specs=[pl.BlockSpec((8, D_MODEL), lambda i: (0, i)),
                   pl.BlockSpec((D_MODEL, D_MODEL), lambda i: (0, half(i))),
                   full(D_MODEL, D_MODEL), full(D_FOURIER, D_FOURIER),
                   full(2, N_HEADS, HEAD_DIM, 2 * HEAD_DIM), full(2, 1, 2 * D_LRU),
                   full(4, MC, D_LRU), full(MC, D_LRU)],
        out_shape=[jax.ShapeDtypeStruct((8, 3 * D_MODEL), F32),
                   jax.ShapeDtypeStruct((D_MODEL, 2 * D_MODEL), BF16),
                   jax.ShapeDtypeStruct((D_MODEL, D_MODEL), BF16),
                   jax.ShapeDtypeStruct((D_FOURIER, D_FOURIER), BF16),
                   jax.ShapeDtypeStruct((2, N_HEADS, HEAD_DIM, 2 * HEAD_DIM), BF16),
                   jax.ShapeDtypeStruct((2, 1, 2 * D_LRU), F32),
                   jax.ShapeDtypeStruct((4, MC, D_LRU), F32),
                   jax.ShapeDtypeStruct((MC, D_LRU), F32)],
        compiler_params=pltpu.CompilerParams(dimension_semantics=("arbitrary",),
                                             vmem_limit_bytes=VMEM_LIMIT_BYTES),
        name="prep",
    )(c, c_ctx, w_ada, b_ada, w_in, w_out, w_four, w_rg, w_ig, b_rg, b_ig, conv_w, conv_b)


def _ctx_kernel(x_ref, mod_ref, g_ref, wlx_ref, cw_ref, cb_ref, wgf_ref, bgf_ref, wgb_ref, bgb_ref,
                lam_ref, fin_ref, up_s, uext_s, vf_s, g_s, a_s, h_s, *, mod_row):
    ch = CH_CTX
    shift = mod_ref[mod_row:mod_row + 1, 0:D_MODEL]
    scale = mod_ref[mod_row:mod_row + 1, D_MODEL:2 * D_MODEL]
    gmul = g_ref[...] * (1.0 + scale)
    lhs = _modulate(x_ref[0], gmul, shift).astype(BF16)
    u = jnp.dot(lhs, wlx_ref[...], preferred_element_type=F32)
    _to_slab_order(u, up_s, uext_s, ch)
    zero_row = jnp.zeros((1, D_LRU), F32)
    _conv_halo(uext_s, zero_row, zero_row, zero_row, ch)
    for s in range(ch):
        vf_s[s * MC:(s + 1) * MC, :] = _conv_slab(uext_s, s, cw_ref, cb_ref)
    vb = vf_s[...].astype(BF16)
    for d, (wg_ref, bg_ref) in enumerate(((wgf_ref, bgf_ref), (wgb_ref, bgb_ref))):
        rate = _half_decay_rate(lam_ref[d:d + 1, :])
        steps = range(ch - 1, -1, -1) if d == 1 else range(ch)
        for h in range(N_HEADS):
            cols = _head_cols(h)
            _gate_terms(h, vb[:, cols], vf_s[:, cols], wg_ref, bg_ref, rate, g_s,
                        slice(0, ch * MC))
        tot_a, tot_h = _scan(_stored_terms(g_s), a_s, h_s, steps)
        _, fin = _chunk_carries(tot_a, tot_h, zero_row, reverse=(d == 1))
        fin_ref[0, d:d + 1, :] = fin


def _gate_specs(direction):
    return (pl.BlockSpec((None, N_HEADS, HEAD_DIM, 2 * HEAD_DIM), lambda *_: (direction, 0, 0, 0)),
            pl.BlockSpec((None, 1, 2 * D_LRU), lambda *_: (direction, 0, 0)))


def _ctx_states(ctx, mod, g, win, cw, cb, wg, bg, lam):
    bsz = ctx.shape[0]
    n = CTX_LEN
    full = lambda *shape: pl.BlockSpec(shape, lambda b: (0,) * len(shape))
    return pl.pallas_call(
        functools.partial(_ctx_kernel, mod_row=bsz),
        grid=(bsz,),
        in_specs=[pl.BlockSpec((1, n, D_MODEL), lambda b: (b, 0, 0)),
                  full(8, 3 * D_MODEL),
                  full(1, D_MODEL),
                  pl.BlockSpec((D_MODEL, D_LRU), lambda b: (0, 2)),
                  full(4, MC, D_LRU), full(MC, D_LRU),
                  *_gate_specs(0), *_gate_specs(1),
                  full(2, D_LRU)],
        out_specs=pl.BlockSpec((1, 2, D_LRU), lambda b: (b, 0, 0)),
        out_shape=jax.ShapeDtypeStruct((bsz, 2, D_LRU), F32),
        scratch_shapes=[pltpu.VMEM((N_QL, MC * _pitch(CH_CTX), LANES), F32),
                        pltpu.VMEM((n + 3 * MC, D_LRU), F32),
                        pltpu.VMEM((n, D_LRU), F32),
                        pltpu.VMEM((n, 2 * D_LRU), F32),
                        pltpu.VMEM((n, D_LRU), F32),
                        pltpu.VMEM((n, D_LRU), F32)],
        compiler_params=pltpu.CompilerParams(dimension_semantics=("arbitrary",)),
        name="ctx_states",
    )(ctx, mod, g, win, cw, cb, wg, bg, wg, bg, lam)


def _fwd_kernel(x_ref, xh_ref, mod_ref, g_ref, win_ref, cw_ref, cb_ref, wg_ref,
                bg_ref, lam_ref, h0_ref,
                uf_ref, gfs_ref, gls_ref, v_ref, hf_ref,
                lhs_s, up_s, uext_s, vf_s, g_s, a_s, h_s, cc_s, hc_s):
    j = pl.program_id(1)

    @pl.when(j == 0)
    def _():
        cc_s[...] = jnp.zeros_like(cc_s)
        hc_s[...] = jnp.broadcast_to(h0_ref[0, 0:1, :], hc_s.shape)

    brow = pl.ds(pl.program_id(0), 1)
    shift = mod_ref[brow, 0:D_MODEL]
    scale = mod_ref[brow, D_MODEL:2 * D_MODEL]
    gmul = g_ref[...] * (1.0 + scale)
    for m in range(MC):
        rows = slice(m * CH, (m + 1) * CH)
        lhs_s[rows, :] = _modulate(x_ref[0, rows, :], gmul, shift).astype(BF16)
    hh = _modulate(xh_ref[0], gmul, shift)
    lhs_s[TILE:LHS_ROWS, :] = jnp.concatenate([hh, jnp.zeros_like(hh)], axis=0).astype(BF16)

    u = jnp.dot(lhs_s[...], win_ref[:, 2 * D_FOURIER:2 * D_FOURIER + D_LRU],
                preferred_element_type=F32)

    _to_slab_order(u, up_s, uext_s, CH)
    look = jnp.where(j == N_TILES - 1, 0.0, u[TILE:TILE + 1])
    _conv_halo(uext_s, look, cc_s[0:1, :], cc_s[1:2, :], CH)
    cc_s[0:1, :] = u[TILE - 1:TILE]
    cc_s[1:2, :] = u[TILE - 2:TILE - 1]

    uf_ref[0] = jnp.dot(lhs_s[0:TILE, :], win_ref[:, 0:D_FOURIER],
                        preferred_element_type=F32).astype(BF16)

    for s in range(CH):
        rows = slice(s * MC, (s + 1) * MC)
        v = _conv_slab(uext_s, s, cw_ref, cb_ref)
        vf_s[rows, :] = v
        v_ref[0, rows, :] = v.astype(BF16)

    halves = (slice(0, TILE // 2), slice(TILE // 2, TILE))
    hrate = _half_decay_rate(lam_ref[0:1, :])
    for h in range(N_HEADS):
        cols = _head_cols(h)
        _gate_terms(h, v_ref[0, :, cols], vf_s[:, cols], wg_ref, bg_ref, hrate, g_s, slice(0, TILE))
        o_ref, c0 = ((gfs_ref, D_FOURIER), (gls_ref, 2 * D_FOURIER + D_LRU))[h // 2]
        rq = halves[h % 2]
        o_ref[0, rq, :] = _silu_of_half(jnp.dot(lhs_s[rq, :], win_ref[:, c0:c0 + D_LRU],
                                                preferred_element_type=F32)).astype(BF16)

    tot_a, tot_h = _scan(_stored_terms(g_s), a_s, h_s, range(CH))
    hin, h_out = _chunk_carries(tot_a, tot_h, hc_s[0:1, :], reverse=False)
    hc_s[...] = jnp.broadcast_to(h_out, hc_s.shape)
    for s in range(CH):
        rows = slice(s * MC, (s + 1) * MC)
        hf_ref[0, rows, :] = (h_s[rows, :] + a_s[rows, :] * hin).astype(BF16)


def _fwd_pass(x, mod, g, win, cw, cb, wg, bg, lam, fin):
    bsz = x.shape[0]
    full = lambda *shape: pl.BlockSpec(shape, lambda b, j: (0,) * len(shape))
    tile = lambda c: pl.BlockSpec((1, TILE, c), lambda b, j: (b, j, 0))
    shp = lambda c: jax.ShapeDtypeStruct((bsz, SEQ, c), BF16)
    rows8 = TILE // 8
    return pl.pallas_call(
        _fwd_kernel,
        grid=(bsz, N_TILES),
        in_specs=[tile(D_MODEL),
                  pl.BlockSpec((1, 8, D_MODEL),
                               lambda b, j: (b, jnp.minimum((j + 1) * rows8, SEQ // 8 - 1), 0)),
                  full(8, 3 * D_MODEL),
                  full(1, D_MODEL), full(D_MODEL, 2 * D_MODEL),
                  full(4, MC, D_LRU), full(MC, D_LRU),
                  *_gate_specs(0),
                  full(2, D_LRU),
                  pl.BlockSpec((1, 2, D_LRU), lambda b, j: (b, 0, 0))],
        out_specs=[tile(D_FOURIER), tile(D_FOURIER), tile(D_LRU), tile(D_LRU), tile(D_LRU)],
        out_shape=[shp(D_FOURIER), shp(D_FOURIER), shp(D_LRU), shp(D_LRU), shp(D_LRU)],
        scratch_shapes=[pltpu.VMEM((LHS_ROWS, D_MODEL), BF16),
                        pltpu.VMEM((N_QL, MC * _pitch(CH), LANES), F32),
                        pltpu.VMEM((TILE + 3 * MC, D_LRU), F32),
                        pltpu.VMEM((TILE, D_LRU), F32),
                        pltpu.VMEM((TILE, 2 * D_LRU), F32),
                        pltpu.VMEM((TILE, D_LRU), F32),
                        pltpu.VMEM((TILE, D_LRU), F32),
                        pltpu.VMEM((8, D_LRU), F32),
                        pltpu.VMEM((8, D_LRU), F32)],
        compiler_params=pltpu.CompilerParams(dimension_semantics=("arbitrary", "arbitrary"),
                                             vmem_limit_bytes=VMEM_LIMIT_BYTES),
        name="fwd_pass",
    )(x, x, mod, g, win, cw, cb, wg, bg, lam, fin)


def _dft_kernel(u_ref, dftc_ref, f1_ref, m2_ref, o_ref, sp_s, sq_s, yr_s, yj_s, so_s):
    p1 = DFT_P1
    p2 = DFT_P2
    dftc = dftc_ref[...].astype(BF16)
    chunks = TILE // DFT_N2
    for blk in range(SEQ // TILE):
        pq = jnp.dot(u_ref[0, blk * TILE:(blk + 1) * TILE, :], dftc, preferred_element_type=F32)
        for m in range(chunks):
            n1 = blk * chunks + m
            rows = slice(m * DFT_N2, (m + 1) * DFT_N2)
            sp_s[n1 * p1:n1 * p1 + DFT_N2, :] = pq[rows, :GROUP_DIM]
            sq_s[n1 * p1:n1 * p1 + DFT_N2, :] = pq[rows, GROUP_DIM:]
    f1 = f1_ref[...].astype(BF16)
    for n2 in range(0, DFT_N2, DFT_NB):
        rhs = jnp.concatenate(
            [jnp.concatenate([sp_s[pl.ds(n2 + i, DFT_N1, stride=p1), :],
                              sq_s[pl.ds(n2 + i, DFT_N1, stride=p1), :]], axis=0)
             for i in range(DFT_NB)], axis=1).astype(BF16)
        y = jnp.dot(f1, rhs, preferred_element_type=F32)
        for i in range(DFT_NB):
            lanes = slice(i * LANES, (i + 1) * LANES)
            yr_s[(n2 + i) * p2:(n2 + i) * p2 + DFT_N1, :] = y[:DFT_N1, lanes]
            yj_s[(n2 + i) * p2:(n2 + i) * p2 + DFT_N1, :] = y[DFT_N1:, lanes]
    for k1 in range(DFT_N1):
        rhs = jnp.concatenate([yr_s[pl.ds(k1, DFT_N2, stride=p2), :],
                               yj_s[pl.ds(k1, DFT_N2, stride=p2), :]], axis=0).astype(BF16)
        so_s[pl.ds(k1, DFT_N2, stride=p2), :] = jnp.dot(m2_ref[k1].astype(BF16), rhs,
                                                         preferred_element_type=F32)
    for k2 in range(DFT_N2):
        o_ref[0, k2 * DFT_N1:(k2 + 1) * DFT_N1, :] = so_s[k2 * p2:k2 * p2 + DFT_N1, :].astype(BF16)


def _fourier_dft(uf, dftc, f1, m2):
    bsz = uf.shape[0]
    blk = pl.BlockSpec((1, SEQ, GROUP_DIM), lambda b, i: (b, 0, i))
    pad1 = DFT_N1 * DFT_P1
    pad2 = DFT_N2 * DFT_P2
    return pl.pallas_call(
        _dft_kernel,
        grid=(bsz, N_GROUPS),
        in_specs=[blk,
                  pl.BlockSpec((GROUP_DIM, 2 * GROUP_DIM), lambda b, i: (0, 0)),
                  pl.BlockSpec((2 * DFT_N1, 2 * DFT_N1), lambda b, i: (0, 0)),
                  pl.BlockSpec((DFT_N1, DFT_N2, 2 * DFT_N2), lambda b, i: (0, 0, 0))],
        out_specs=blk,
        out_shape=jax.ShapeDtypeStruct(uf.shape, BF16),
        scratch_shapes=[pltpu.VMEM((pad1, LANES), F32), pltpu.VMEM((pad1, LANES), F32),
                        pltpu.VMEM((pad2, LANES), F32), pltpu.VMEM((pad2, LANES), F32),
                        pltpu.VMEM((pad2, LANES), F32)],
        compiler_params=pltpu.CompilerParams(dimension_semantics=("arbitrary", "arbitrary"),
                                             vmem_limit_bytes=VMEM_LIMIT_BYTES),
        name="fourier_dft",
    )(uf, dftc, f1, m2)


X_SLOTS = 3


def _bwd_kernel(x_hbm, yf_ref, gfs_ref, gls_hbm, v_ref, hf_hbm, mod_ref, fg_ref, wfour_ref, wout_ref,
                wg_ref, bg_ref, lam_ref, h0_ref, o_ref,
                g_s, a_s, h_s, yp_s, lhs_s, hc_s, x_s, gls_s, hf_s, ring_sem):
    j = pl.program_id(1)

    n_steps = pl.num_programs(0) * N_TILES
    step = pl.program_id(0) * N_TILES + j
    rings = ((x_hbm, x_s), (gls_hbm, gls_s), (hf_hbm, hf_s))

    def ring_copies(s):
        tile_start = (N_TILES - 1 - s % N_TILES) * TILE
        if not isinstance(s, int):
            tile_start = pl.multiple_of(tile_start, TILE)
        slot = s % X_SLOTS
        return [pltpu.make_async_copy(hbm.at[s // N_TILES, pl.ds(tile_start, TILE), :],
                                      buf.at[slot], ring_sem.at[i, slot])
                for i, (hbm, buf) in enumerate(rings)]

    @pl.when(step == 0)
    def _():
        for s in range(X_SLOTS - 1):
            for cp in ring_copies(s):
                cp.start()

    @pl.when(step + X_SLOTS - 1 < n_steps)
    def _():
        for cp in ring_copies(step + X_SLOTS - 1):
            cp.start()

    for cp in ring_copies(step):
        cp.wait()
    gls_ref = gls_s.at[step % X_SLOTS]
    hf_ref = hf_s.at[step % X_SLOTS]

    @pl.when(j == 0)
    def _():
        hc_s[...] = jnp.broadcast_to(h0_ref[0, 1:2, :], hc_s.shape)

    for h in range(N_HEADS):
        res = jnp.dot(v_ref[0, :, _head_cols(h)], wg_ref[h], preferred_element_type=F32)
        g_s[:, _head_cols(h)] = res[:, :HEAD_DIM]
        g_s[:, _head_cols(h, D_LRU)] = res[:, HEAD_DIM:]
    yfw = jnp.dot(yf_ref[0], wfour_ref[...], preferred_element_type=F32)
    lhs_s[:, 0:D_FOURIER] = (yfw * gfs_ref[0].astype(F32)).astype(BF16)

    hrate = _half_decay_rate(lam_ref[1:2, :])
    b_r = bg_ref[:, :D_LRU]
    b_i = bg_ref[:, D_LRU:]
    terms = lambda rows: _gate_ab(g_s[rows, :D_LRU] + b_r, g_s[rows, D_LRU:] + b_i,
                                  v_ref[0, rows, :].astype(F32), hrate)
    tot_a, tot_h = _scan(terms, a_s, h_s, range(CH - 1, -1, -1))
    hin, h_out = _chunk_carries(tot_a, tot_h, hc_s[0:1, :], reverse=True)
    hc_s[...] = jnp.broadcast_to(h_out, hc_s.shape)
    pitch = _pitch(CH)
    for s in range(CH):
        rows = slice(s * MC, (s + 1) * MC)
        yl = h_s[rows, :] + a_s[rows, :] * hin + hf_ref[rows, :].astype(F32)
        for q in range(N_QL):
            yp_s[q, pl.ds(s, MC, stride=pitch), :] = yl[:, q * LANES:(q + 1) * LANES]
    for m in range(MC):
        rows = slice(m * CH, (m + 1) * CH)
        for q in range(N_QL):
            cols = slice(q * LANES, (q + 1) * LANES)
            yl = yp_s[q, m * pitch:m * pitch + CH, :]
            lhs_s[rows, D_FOURIER + q * LANES:D_FOURIER + (q + 1) * LANES] = (
                yl * gls_ref[rows, cols].astype(F32)).astype(BF16)

    res_gate = mod_ref[pl.ds(pl.program_id(0), 1), 2 * D_MODEL:3 * D_MODEL]
    fg = fg_ref[...]
    proj = jnp.dot(lhs_s[...], wout_ref[...], preferred_element_type=F32)
    x_tile = x_s.at[step % X_SLOTS]
    for m in range(MC):
        rows = slice(m * CH, (m + 1) * CH)
        res = x_tile[rows, :] + res_gate * proj[rows, :]
        ms = jnp.mean(res * res, axis=-1, keepdims=True)
        o_ref[0, rows, :] = (res * lax.rsqrt(ms + EPS)) * fg


def _bwd_pass(x, yf, gfs, gls, v, hf, mod, fg, wfour, wout, wg, bg, lam, fin):
    bsz = x.shape[0]
    full = lambda *shape: pl.BlockSpec(shape, lambda b, j: (0,) * len(shape))
    tile = lambda c: pl.BlockSpec((1, TILE, c), lambda b, j: (b, N_TILES - 1 - j, 0))
    return pl.pallas_call(
        _bwd_kernel,
        grid=(bsz, N_TILES),
        in_specs=[pl.BlockSpec(memory_space=pl.ANY),
                  tile(D_FOURIER), tile(D_FOURIER), pl.BlockSpec(memory_space=pl.ANY), tile(D_LRU),
                  pl.BlockSpec(memory_space=pl.ANY),
                  full(8, 3 * D_MODEL),
                  full(1, D_MODEL), full(D_FOURIER, D_FOURIER), full(D_MODEL, D_MODEL),
                  *_gate_specs(1), full(2, D_LRU),
                  pl.BlockSpec((1, 2, D_LRU), lambda b, j: (b, 0, 0))],
        out_specs=tile(D_MODEL),
        out_shape=jax.ShapeDtypeStruct(x.shape, F32),
        scratch_shapes=[pltpu.VMEM((TILE, 2 * D_LRU), F32),
                        pltpu.VMEM((TILE, D_LRU), F32),
                        pltpu.VMEM((TILE, D_LRU), F32),
                        pltpu.VMEM((N_QL, MC * _pitch(CH), LANES), F32),
                        pltpu.VMEM((TILE, D_MODEL), BF16),
                        pltpu.VMEM((8, D_LRU), F32),
                        pltpu.VMEM((X_SLOTS, TILE, D_MODEL), F32),
                        pltpu.VMEM((X_SLOTS, TILE, D_LRU), BF16),
                        pltpu.VMEM((X_SLOTS, TILE, D_LRU), BF16),
                        pltpu.SemaphoreType.DMA((3, X_SLOTS))],
        compiler_params=pltpu.CompilerParams(dimension_semantics=("arbitrary", "arbitrary"),
                                             vmem_limit_bytes=VMEM_LIMIT_BYTES),
        name="bwd_pass",
    )(x, yf, gfs, gls, v, hf, mod, fg, wfour, wout, wg, bg, lam, fin)


def kernel(x, c, ctx, c_ctx, w_ada, b_ada, norm_gain, w_in, w_four, conv_w, conv_b, w_rg, b_rg,
           w_ig, b_ig, lam, w_out, final_gain):
    bsz = x.shape[0]
    assert x.shape == (bsz, SEQ, D_MODEL) and ctx.shape == (bsz, CTX_LEN, D_MODEL)
    assert w_ada.shape[0] == 1, "single-layer kernel"
    dftc, f1, m2 = _dft_constants()

    mod, win, wout, wfour, wg, bg, cw, cb = _prep(c, c_ctx[None, :], w_ada, b_ada, w_in, w_out,
                                                  w_four, w_rg, w_ig, b_rg, b_ig, conv_w, conv_b)
    g = norm_gain[0][None, :]
    lam0 = lam[0]

    fin = _ctx_states(ctx, mod, g, win, cw, cb, wg, bg, lam0)
    uf, gfs, gls, v, hf = _fwd_pass(x, mod, g, win, cw, cb, wg, bg, lam0, fin)

    yf = _fourier_dft(uf, dftc, f1, m2)

    return _bwd_pass(x, yf, gfs, gls, v, hf, mod, final_gain[None, :], wfour, wout, wg, bg, lam0, fin)
```
